```python
import jax, jax.numpy as jnp
from jax import lax
import numpy as np

D_MODEL = 1024
BATCH = 4
SEQ = 4096
DEPTH = 1

N_META = 16
GRID_W = 64
MIX_WIDTH = D_MODEL
NA_WIDTH = MIX_WIDTH // 2
NA_HEAD_DIM = 64
NA_HEADS = NA_WIDTH // NA_HEAD_DIM
NA_KH_MAX = 8
NA_KW = 16
HG_WIDTH = MIX_WIDTH - NA_WIDTH
HG_HEAD_DIM = 128
HG_HEADS = HG_WIDTH // HG_HEAD_DIM
HG_CHUNK = 64
IN_WIDTHS = [NA_WIDTH] * 3 + [HG_WIDTH] * 5
IN_COLS = sum(IN_WIDTHS)
N_EXPERTS = 32
TOP_K = 4
D_FF = D_MODEL
SWIGLU_LIMIT = 7.0
SWIGLU_ALPHA = 1.702
MOE_BLOCK = 128
RMS_EPS = 1e-6

kernel_name = "hymba_natten_hgrn2_moe_encoder"


def rms_norm(x, g):
    x32 = x.astype(jnp.float32)
    y = x32 * lax.rsqrt(jnp.mean(x32 * x32, axis=-1, keepdims=True) + RMS_EPS)
    return (y * g.astype(jnp.float32)).astype(x.dtype)


def head_rms_norm(y, g, n_heads):
    B, L, W = y.shape
    y32 = y.astype(jnp.float32).reshape(B, L, n_heads, W // n_heads)
    y32 = y32 * lax.rsqrt(jnp.mean(y32 * y32, axis=-1, keepdims=True) + RMS_EPS)
    return (y32.reshape(B, L, W) * g.astype(jnp.float32)).astype(y.dtype)


def neighbourhood_attention(q_raw, k_raw, v_raw, rpb):
    B, L, _ = q_raw.shape
    T = L - N_META
    ROWS = T // GRID_W
    KH = min(NA_KH_MAX, ROWS)
    H, dh = NA_HEADS, NA_HEAD_DIM
    scale = dh ** -0.5

    def heads(t):
        return t.reshape(B, L, H, dh).transpose(0, 2, 1, 3)

    q, k, v = heads(q_raw) * scale, heads(k_raw), heads(v_raw)
    qm, km, vm = q[:, :, :N_META], k[:, :, :N_META], v[:, :, :N_META]
    q_grid = q[:, :, N_META:].reshape(B, H, ROWS, GRID_W, dh)
    k_grid = k[:, :, N_META:].reshape(B, H, ROWS, GRID_W, dh)
    v_grid = v[:, :, N_META:].reshape(B, H, ROWS, GRID_W, dh)

    cols = np.arange(GRID_W)
    col_start = np.clip(cols - NA_KW // 2, 0, GRID_W - NA_KW)
    col_idx = col_start[:, None] + np.arange(NA_KW)[None, :]
    dc_idx = col_idx - cols[:, None] + (NA_KW - 1)
    rpb_cols = rpb.astype(jnp.float32)[:, :, dc_idx]

    def row_block(r):
        rs = jnp.clip(r - KH // 2, 0, ROWS - KH)
        qb = lax.dynamic_index_in_dim(q_grid, r, axis=2, keepdims=False)
        kb = lax.dynamic_slice_in_dim(k_grid, rs, KH, axis=2)[:, :, :, col_idx]
        vb = lax.dynamic_slice_in_dim(v_grid, rs, KH, axis=2)[:, :, :, col_idx]
        s_win = jnp.einsum('bhcd,bhicjd->bhcij', qb, kb).astype(jnp.float32)
        dr = rs - r + jnp.arange(KH) + (NA_KH_MAX - 1)
        bias = jnp.take(rpb_cols, dr, axis=1).transpose(0, 2, 1, 3)
        s_win = s_win + bias[None]
        s_meta = jnp.einsum('bhcd,bhmd->bhcm', qb, km).astype(jnp.float32)
        s = jnp.concatenate([s_win.reshape(B, H, GRID_W, KH * NA_KW), s_meta], axis=-1)
        p = jax.nn.softmax(s, axis=-1)
        p_win = p[..., :KH * NA_KW].reshape(B, H, GRID_W, KH, NA_KW)
        p_meta = p[..., KH * NA_KW:]
        return (jnp.einsum('bhcij,bhicjd->bhcd', p_win, vb.astype(jnp.float32))
                + jnp.einsum('bhcm,bhmd->bhcd', p_meta, vm.astype(jnp.float32)))

    o_rows = lax.map(row_block, jnp.arange(ROWS))
    o_real = o_rows.transpose(1, 2, 0, 3, 4).reshape(B, H, T, dh)
    p_mm = jax.nn.softmax(jnp.einsum('bhmd,bhnd->bhmn', qm, km).astype(jnp.float32), axis=-1)
    o_meta = jnp.einsum('bhmn,bhnd->bhmd', p_mm, vm.astype(jnp.float32))
    o = jnp.concatenate([o_meta, o_real], axis=2).transpose(0, 2, 1, 3)
    return o.reshape(B, L, NA_WIDTH).astype(q_raw.dtype)


def gla_chunk(S, q, k, v, g):
    C = q.shape[2]
    b = jnp.cumsum(g, axis=2)
    mask = jnp.tril(jnp.ones((C, C), dtype=bool))[:, :, None]
    diff = b[:, :, :, None, :] - b[:, :, None, :, :]
    decay = jnp.exp(jnp.where(mask, diff, -jnp.inf))
    attn = jnp.einsum('bhtk,bhtsk,bhsk->bhts', q, decay, k)
    o = (jnp.einsum('bhts,bhsv->bhtv', attn, v)
         + jnp.einsum('bhtk,bhkv->bhtv', q * jnp.exp(b), S))
    b_last = b[:, :, -1:, :]
    S_new = (jnp.exp(b_last[:, :, 0, :])[..., None] * S
             + jnp.einsum('bhsk,bhsv->bhkv', k * jnp.exp(b_last - b), v))
    return S_new, o


def gla_scan(q, k, v, g, reverse):
    B, H, L, dk = q.shape
    dv = v.shape[-1]
    T = L - N_META
    NC = T // HG_CHUNK

    def real(t):
        t = t[:, :, N_META:]
        return jnp.flip(t, axis=2) if reverse else t

    def to_chunks(t):
        return jnp.moveaxis(t.reshape(B, H, NC, HG_CHUNK, t.shape[-1]), 2, 0)

    S0 = jnp.zeros((B, H, dk, dv), jnp.float32)
    S, o_meta = gla_chunk(S0, q[:, :, :N_META], k[:, :, :N_META], v[:, :, :N_META], g[:, :, :N_META])

    def step(S, xs):
        return gla_chunk(S, *xs)

    _, o_real = lax.scan(step, S, (to_chunks(real(q)), to_chunks(real(k)),
                                   to_chunks(real(v)), to_chunks(real(g))))
    o_real = jnp.moveaxis(o_real, 0, 2).reshape(B, H, T, dv)
    if reverse:
        o_real = jnp.flip(o_real, axis=2)
    return jnp.concatenate([o_meta, o_real], axis=2)


def hgrn2_bidirectional(q_raw, i_raw, f_fwd_raw, f_bwd_raw, gate_raw, lb, g_norm):
    B, L, _ = q_raw.shape

    def heads(t):
        return t.reshape(B, L, HG_HEADS, HG_HEAD_DIM).transpose(0, 2, 1, 3)

    q = heads(jax.nn.silu(q_raw.astype(jnp.float32)))
    v = heads(i_raw.astype(jnp.float32))

    def gates(f_raw, lb_dir):
        f = lb_dir + (1.0 - lb_dir) * jax.nn.sigmoid(f_raw.astype(jnp.float32))
        return heads(1.0 - f), heads(jnp.log(f))

    k_f, g_f = gates(f_fwd_raw, lb[0])
    k_b, g_b = gates(f_bwd_raw, lb[1])
    o = gla_scan(q, k_f, v, g_f, reverse=False) + gla_scan(q, k_b, v, g_b, reverse=True)
    o = o.transpose(0, 2, 1, 3)
    o = o * lax.rsqrt(jnp.mean(o * o, axis=-1, keepdims=True) + RMS_EPS)
    o = o.reshape(B, L, HG_WIDTH) * g_norm.astype(jnp.float32)
    o = o * jax.nn.silu(gate_raw.astype(jnp.float32))
    return o.astype(q_raw.dtype)


def clamped_swiglu(gate, up):
    gate = jnp.minimum(gate, SWIGLU_LIMIT)
    up = jnp.clip(up, -SWIGLU_LIMIT, SWIGLU_LIMIT)
    return (up + 1.0) * gate * jax.nn.sigmoid(SWIGLU_ALPHA * gate)


def moe_ffn(x, router_w, router_b, w_gu, b_gu, w_down, b_down):
    B, L, D = x.shape
    xf = x.reshape(-1, D)
    N = xf.shape[0]
    NK = N * TOP_K
    logits = (xf @ router_w + router_b).astype(jnp.float32)
    top_vals, top_idx = lax.top_k(logits, TOP_K)
    top_w = jax.nn.softmax(top_vals, axis=-1)

    flat_e = top_idx.reshape(-1)
    flat_tok = jnp.arange(NK, dtype=jnp.int32) // TOP_K
    order = jnp.argsort(flat_e, stable=True)
    sorted_e = flat_e[order]
    sorted_tok = flat_tok[order]
    sorted_w = top_w.reshape(-1)[order]

    counts = jnp.bincount(flat_e, length=N_EXPERTS)
    padded = (counts + MOE_BLOCK - 1) // MOE_BLOCK * MOE_BLOCK
    pend = jnp.cumsum(padded)
    pstart = pend - padded
    start = jnp.cumsum(counts) - counts
    dest = pstart[sorted_e] + jnp.arange(NK, dtype=jnp.int32) - start[sorted_e]

    n_blocks = (NK + N_EXPERTS * (MOE_BLOCK - 1) + MOE_BLOCK - 1) // MOE_BLOCK
    x_buf = jnp.zeros((n_blocks * MOE_BLOCK, D), xf.dtype).at[dest].set(xf[sorted_tok])
    block_e = jnp.minimum(
        jnp.searchsorted(pend, jnp.arange(n_blocks, dtype=pend.dtype) * MOE_BLOCK, side='right'),
        N_EXPERTS - 1)

    def expert_block(args):
        xb, e = args
        gu = xb @ w_gu[e] + b_gu[e]
        act = clamped_swiglu(gu[:, :D_FF].astype(jnp.float32), gu[:, D_FF:].astype(jnp.float32))
        return act.astype(xb.dtype) @ w_down[e] + b_down[e]

    y_buf = lax.map(expert_block, (x_buf.reshape(n_blocks, MOE_BLOCK, D), block_e)).reshape(-1, D)
    y = jax.ops.segment_sum(y_buf[dest].astype(jnp.float32) * sorted_w[:, None], sorted_tok,
                            num_segments=N)
    return y.reshape(B, L, D).astype(x.dtype)


def setup_inputs(seed: int = 0) -> dict:
    key = jax.random.key(seed)
    ks = jax.random.split(key, 17)
    f32 = jnp.float32
    nrm = lambda k, shape, s: jax.random.normal(k, shape, f32) * s
    return {
        "x": nrm(ks[0], (BATCH, SEQ, D_MODEL), 1.0),
        "meta_tokens": nrm(ks[1], (N_META, D_MODEL), 1.0),
        "attn_norm_g": 1.0 + nrm(ks[2], (DEPTH, D_MODEL), 0.02),
        "w_in": nrm(ks[3], (DEPTH, D_MODEL, IN_COLS), D_MODEL ** -0.5),
        "na_rpb": nrm(ks[4], (DEPTH, NA_HEADS, 2 * NA_KH_MAX - 1, 2 * NA_KW - 1), 0.02),
        "na_norm_g": 1.0 + nrm(ks[5], (DEPTH, NA_WIDTH), 0.02),
        "hgrn_lb_logits": nrm(ks[6], (2, DEPTH + 1, HG_WIDTH), 0.5),
        "hgrn_norm_g": 1.0 + nrm(ks[7], (DEPTH, HG_WIDTH), 0.02),
        "w_out": nrm(ks[8], (DEPTH, MIX_WIDTH, D_MODEL), MIX_WIDTH ** -0.5),
        "ffn_norm_g": 1.0 + nrm(ks[9], (DEPTH, D_MODEL), 0.02),
        "router_w": nrm(ks[10], (DEPTH, D_MODEL, N_EXPERTS), D_MODEL ** -0.5),
        "router_b": nrm(ks[11], (DEPTH, N_EXPERTS), 0.01),
        "expert_w_gu": nrm(ks[12], (DEPTH, N_EXPERTS, D_MODEL, 2 * D_FF), D_MODEL ** -0.5),
        "expert_b_gu": nrm(ks[13], (DEPTH, N_EXPERTS, 2 * D_FF), 0.01),
        "expert_w_down": nrm(ks[14], (DEPTH, N_EXPERTS, D_FF, D_MODEL), D_FF ** -0.5),
        "expert_b_down": nrm(ks[15], (DEPTH, N_EXPERTS, D_MODEL), 0.01),
        "final_norm_g": 1.0 + nrm(ks[16], (D_MODEL,), 0.02),
    }


def reference(x, meta_tokens, attn_norm_g, w_in, na_rpb, na_norm_g, hgrn_lb_logits, hgrn_norm_g,
              w_out, ffn_norm_g, router_w, router_b, expert_w_gu, expert_b_gu, expert_w_down,
              expert_b_down, final_norm_g):
    B, T, D = x.shape
    meta = jnp.broadcast_to(meta_tokens.astype(x.dtype)[None], (B, N_META, D))
    h = jnp.concatenate([meta, x], axis=1)
    lower_bounds = jnp.cumsum(jax.nn.softmax(hgrn_lb_logits.astype(jnp.float32), axis=1), axis=1)
    split_at = [int(s) for s in np.cumsum(IN_WIDTHS)[:-1]]
    for l in range(DEPTH):
        n = rms_norm(h, attn_norm_g[l])
        proj = jnp.einsum('bld,de->ble', n, w_in[l])
        qa, ka, va, qh, ih, ff, fb, gh = jnp.split(proj, split_at, axis=-1)
        y_na = head_rms_norm(neighbourhood_attention(qa, ka, va, na_rpb[l]), na_norm_g[l], NA_HEADS)
        y_hg = hgrn2_bidirectional(qh, ih, ff, fb, gh, lower_bounds[:, l], hgrn_norm_g[l])
        mix = jnp.einsum('ble,ed->bld', jnp.concatenate([y_na, y_hg], axis=-1), w_out[l])
        h = h + mix.astype(h.dtype)
        h = h + moe_ffn(rms_norm(h, ffn_norm_g[l]), router_w[l], router_b[l], expert_w_gu[l],
                        expert_b_gu[l], expert_w_down[l], expert_b_down[l])
    return rms_norm(h[:, N_META:], final_norm_g)
```

```python
import functools

import numpy as np
import jax
import jax.numpy as jnp
from jax import lax
from jax.experimental import pallas as pl
from jax.experimental.pallas import tpu as pltpu

F32 = jnp.float32
BF16 = jnp.bfloat16

D_MODEL = 1024
N_META = 16
GRID_W = 64
NA_WIDTH = 512
NA_HEAD_DIM = 64
NA_HEADS = 8
NA_KH = 8
NA_KW = 16
HG_WIDTH = 512
HG_HEAD_DIM = 128
HG_HEADS = 4
HG_CHUNK = 64
IN_COLS = 3 * NA_WIDTH + 5 * HG_WIDTH
N_EXPERTS = 32
TOP_K = 4
D_FF = 1024
SWIGLU_LIMIT = 7.0
SWIGLU_ALPHA = 1.702
RMS_EPS = 1e-6
NEG_BIG = -1e30

LANES = 128
VMEM_LIMIT_BYTES = 56 * 1024 * 1024

TOK_TILE = 256
EXPERT_BLOCK = 256
SUBLANES = 8
STAGE_ROWS = TOK_TILE * TOP_K + N_EXPERTS * SUBLANES
N_HG_LEVELS = 6


def _dot(a, b):
    return jnp.dot(a, b, preferred_element_type=F32)


def _dot_nt(a, b):
    return lax.dot_general(a, b, (((1,), (1,)), ((), ())), preferred_element_type=F32)


def _dot_tn(a, b):
    return lax.dot_general(a, b, (((0,), (0,)), ((), ())), preferred_element_type=F32)


def _split_bf16(x):
    hi = x.astype(BF16)
    lo = (x - hi.astype(F32)).astype(BF16)
    return hi, lo


def _pack_cols(cols):
    lane = lax.broadcasted_iota(jnp.int32, (cols[0].shape[0], len(cols)), 1)
    out = jnp.broadcast_to(cols[-1], lane.shape)
    for k in reversed(range(len(cols) - 1)):
        out = jnp.where(lane == k, cols[k], out)
    return out


def _params(*sem):
    return pltpu.CompilerParams(dimension_semantics=sem, vmem_limit_bytes=VMEM_LIMIT_BYTES)


def _inproj_body(x_ref, ng_ref, w_ref, lbl_ref, att_ref, hg_ref):
    x = x_ref[...]
    ms = jnp.mean(x * x, axis=-1, keepdims=True)
    n = (x * lax.rsqrt(ms + RMS_EPS) * ng_ref[...]).astype(BF16)

    def proj(lo, hi):
        return _dot(n, w_ref[:, lo:hi])

    pq = proj(0, NA_WIDTH)
    att_ref[:, 0:NA_WIDTH] = (pq * (NA_HEAD_DIM ** -0.5)).astype(BF16)
    att_ref[:, NA_WIDTH:3 * NA_WIDTH] = proj(NA_WIDTH, 3 * NA_WIDTH).astype(BF16)

    base = 3 * NA_WIDTH
    W = HG_WIDTH
    qh = proj(base, base + W)
    hg_ref[:, 0:W] = qh * jax.nn.sigmoid(qh)
    hg_ref[:, W:2 * W] = proj(base + W, base + 2 * W)
    lbl = lbl_ref[...]
    for d in range(2):
        a0 = lbl[2 * d:2 * d + 1, :]
        a1 = lbl[2 * d + 1:2 * d + 2, :]
        m = jnp.maximum(a0, a1)
        e0 = jnp.exp(a0 - m)
        e1 = jnp.exp(a1 - m)
        lb = e0 / (e0 + e1)
        raw = proj(base + (2 + d) * W, base + (3 + d) * W)
        f = lb + (1.0 - lb) * jax.nn.sigmoid(raw)
        hg_ref[:, (2 + d) * W:(3 + d) * W] = jnp.log(f)
    gt = proj(base + 4 * W, base + 5 * W)
    hg_ref[:, 4 * W:5 * W] = gt * jax.nn.sigmoid(gt)


def _inproj(x2d, norm_g, w_bf16, lb_logits4, tm):
    n = x2d.shape[0]
    return pl.pallas_call(
        _inproj_body,
        grid=(n // tm,),
        in_specs=[
            pl.BlockSpec((tm, D_MODEL), lambda i: (i, 0)),
            pl.BlockSpec((1, D_MODEL), lambda i: (0, 0)),
            pl.BlockSpec((D_MODEL, IN_COLS), lambda i: (0, 0)),
            pl.BlockSpec((4, HG_WIDTH), lambda i: (0, 0)),
        ],
        out_specs=[
            pl.BlockSpec((tm, 3 * NA_WIDTH), lambda i: (i, 0)),
            pl.BlockSpec((tm, 5 * HG_WIDTH), lambda i: (i, 0)),
        ],
        out_shape=[
            jax.ShapeDtypeStruct((n, 3 * NA_WIDTH), BF16),
            jax.ShapeDtypeStruct((n, 5 * HG_WIDTH), F32),
        ],
        compiler_params=_params("arbitrary"),
        name="inproj",
    )(x2d, norm_g, w_bf16, lb_logits4)


N_WIN_KEYS = NA_KH * GRID_W
META_PAD = LANES


def _natten_body(q_ref, k_ref, v_ref, km_ref, vm_ref, bias_ref, ng_ref, o_ref, *, rows):
    r = pl.program_id(1)
    rs = jnp.clip(r - NA_KH // 2, 0, rows - NA_KH)
    start = pl.multiple_of(rs * GRID_W, GRID_W)
    lane = lax.broadcasted_iota(jnp.int32, (GRID_W, LANES), 1)
    first = lane < NA_HEAD_DIM
    meta_bias = jnp.where(lane < N_META, 0.0, NEG_BIG).astype(F32)
    for p in range(NA_HEADS // 2):
        sl = slice(p * LANES, (p + 1) * LANES)
        q2 = q_ref[0, :, sl]
        kw = k_ref[0, pl.ds(start, N_WIN_KEYS), sl]
        vw = v_ref[0, pl.ds(start, N_WIN_KEYS), sl]
        km = km_ref[:, sl]
        vm = vm_ref[:, sl]
        outs = []
        for hh in range(2):
            sel = first if hh == 0 else jnp.logical_not(first)
            qh = jnp.where(sel, q2, jnp.zeros_like(q2))
            s = _dot_nt(qh, kw) + bias_ref[0, 2 * p + hh]
            sm = _dot_nt(qh, km) + meta_bias
            m = jnp.maximum(jnp.max(s, axis=-1, keepdims=True), jnp.max(sm, axis=-1, keepdims=True))
            e = jnp.exp(s - m)
            em = jnp.exp(sm - m)
            den = jnp.sum(e, axis=-1, keepdims=True) + jnp.sum(em, axis=-1, keepdims=True)
            o = _dot(e.astype(BF16), vw) + _dot(em.astype(BF16), vm)
            outs.append(o / den)
        o2 = jnp.where(first, outs[0], outs[1])
        sq = o2 * o2
        ms0 = jnp.sum(jnp.where(first, sq, 0.0), axis=-1, keepdims=True) * (1.0 / NA_HEAD_DIM)
        ms1 = jnp.sum(jnp.where(first, 0.0, sq), axis=-1, keepdims=True) * (1.0 / NA_HEAD_DIM)
        inv = jnp.where(first, lax.rsqrt(ms0 + RMS_EPS), lax.rsqrt(ms1 + RMS_EPS))
        o_ref[0, :, sl] = (o2 * inv * ng_ref[:, sl]).astype(o_ref.dtype)


def _natten_bias_tables(rpb):
    c = np.arange(GRID_W)[:, None]
    kc = np.arange(GRID_W)[None, :]
    cs = np.clip(c - NA_KW // 2, 0, GRID_W - NA_KW)
    valid = (kc >= cs) & (kc < cs + NA_KW)
    dc = np.clip(kc - c + (NA_KW - 1), 0, 2 * NA_KW - 2)
    t1 = jnp.where(valid[None, None], rpb.astype(F32)[:, :, dc], NEG_BIG)
    tabs = []
    for d0 in range(NA_KH):
        t = t1[:, d0:d0 + NA_KH]
        tabs.append(t.transpose(0, 2, 1, 3).reshape(NA_HEADS, GRID_W, N_WIN_KEYS))
    return jnp.stack(tabs)


def _natten(att, att_meta, bias_tabs, norm_g, batch, seq):
    rows = seq // GRID_W
    att3 = att.reshape(batch, seq, 3 * NA_WIDTH)

    def d0_of(r):
        return jnp.clip(r - NA_KH // 2, 0, rows - NA_KH) - r + (NA_KH - 1)

    return pl.pallas_call(
        functools.partial(_natten_body, rows=rows),
        grid=(batch, rows),
        in_specs=[
            pl.BlockSpec((1, GRID_W, NA_WIDTH), lambda b, r: (b, r, 0)),
            pl.BlockSpec((1, seq, NA_WIDTH), lambda b, r: (b, 0, 1)),
            pl.BlockSpec((1, seq, NA_WIDTH), lambda b, r: (b, 0, 2)),
            pl.BlockSpec((META_PAD, NA_WIDTH), lambda b, r: (0, 1)),
            pl.BlockSpec((META_PAD, NA_WIDTH), lambda b, r: (0, 2)),
            pl.BlockSpec((1, NA_HEADS, GRID_W, N_WIN_KEYS), lambda b, r: (d0_of(r), 0, 0, 0)),
            pl.BlockSpec((1, NA_WIDTH), lambda b, r: (0, 0)),
        ],
        out_specs=pl.BlockSpec((1, GRID_W, NA_WIDTH), lambda b, r: (b, r, 0)),
        out_shape=jax.ShapeDtypeStruct((batch, seq, NA_WIDTH), BF16),
        compiler_params=_params("arbitrary", "arbitrary"),
        name="natten",
    )(att3, att3, att3, att_meta, att_meta, bias_tabs, norm_g)


ROW_B, ROW_BL, ROW_Q, ROW_K = 0, 1, 2, 2 + N_HG_LEVELS
N_EXP_BLOCKS = 2 + 2 * N_HG_LEVELS


def _hgrn_constants():
    C = HG_CHUNK
    t = np.arange(C)
    u = t[None, :]
    mats = [u <= t[:, None], u > t[:, None]]
    qm, km, masks = [], [], [np.eye(C, dtype=bool)]
    for lv in range(N_HG_LEVELS):
        m = 1 << lv
        blk = t // (2 * m)
        upper = (t // m) % 2 == 1
        p = blk * 2 * m + m - 1
        qm.append(upper[:, None] & (u > p[:, None]) & (u <= t[:, None]))
        km.append((~upper)[:, None] & (u > t[:, None]) & (u <= p[:, None]))
        masks.append((blk[:, None] == blk[None, :]) & upper[:, None] & (~upper)[None, :])
    fwd = np.concatenate(mats + qm + km, axis=0).astype(np.float32)
    fwd_mask = np.stack(masks).astype(np.float32)
    blocks = fwd.reshape(N_EXP_BLOCKS, C, C)
    bwd = blocks[:, ::-1, ::-1].reshape(N_EXP_BLOCKS * C, C)
    bwd_mask = fwd_mask[:, ::-1, ::-1]
    return (jnp.asarray(np.stack([fwd, bwd]), BF16), jnp.asarray(np.stack([fwd_mask, bwd_mask]), F32))


def _hgrn_chunk(blk, g_col, mat, masks, st_ref, total_row):
    C, W = HG_CHUNK, HG_WIDTH
    q = blk[:, 0:W]
    v = blk[:, W:2 * W]
    g = blk[:, g_col:g_col + W]
    k = 1.0 - jnp.exp(g)
    g_hi, g_lo = _split_bf16(g)
    ex = jnp.exp(_dot(mat, g_hi) + _dot(mat, g_lo))

    def rows(i):
        return ex[i * C:(i + 1) * C]

    outs = []
    for h in range(HG_HEADS):
        sl = slice(h * HG_HEAD_DIM, (h + 1) * HG_HEAD_DIM)
        qh, kh, vh = q[:, sl], k[:, sl], v[:, sl].astype(BF16)
        a = jnp.where(masks[0] > 0, _dot_nt(qh.astype(BF16), kh.astype(BF16)), 0.0)
        for lv in range(N_HG_LEVELS):
            ql = (qh * rows(ROW_Q + lv)[:, sl]).astype(BF16)
            kl = (kh * rows(ROW_K + lv)[:, sl]).astype(BF16)
            a = a + jnp.where(masks[1 + lv] > 0, _dot_nt(ql, kl), 0.0)
        qb = (qh * rows(ROW_B)[:, sl]).astype(BF16)
        kb = (kh * rows(ROW_BL)[:, sl]).astype(BF16)
        st = st_ref[h]
        outs.append(_dot(a.astype(BF16), vh) + _dot_nt(qb, st.astype(BF16)))
        decay = ex[ROW_B * C + total_row:ROW_B * C + total_row + 1, sl]
        st_ref[h] = decay * st + _dot_tn(vh, kb)
    return jnp.concatenate(outs, axis=-1)


def _hgrn_body(hf_ref, hb_ref, meta_ref, mat_ref, mask_ref, of_ref, ob_ref, stf_ref, stb_ref):
    C, W = HG_CHUNK, HG_WIDTH
    c = pl.program_id(1)

    @pl.when(c == 0)
    def _init():
        mb = meta_ref[...]
        v = mb[:, W:2 * W].astype(BF16)
        suffix = mat_ref[0, ROW_BL * C:(ROW_BL + 1) * C, :]
        for d, st_ref in ((0, stf_ref), (1, stb_ref)):
            g = mb[:, (2 + d) * W:(3 + d) * W]
            g_hi, g_lo = _split_bf16(g)
            kb = ((1.0 - jnp.exp(g)) * jnp.exp(_dot(suffix, g_hi) + _dot(suffix, g_lo))).astype(BF16)
            for h in range(HG_HEADS):
                sl = slice(h * HG_HEAD_DIM, (h + 1) * HG_HEAD_DIM)
                st_ref[h] = _dot_tn(v[:, sl], kb[:, sl])

    of_ref[0] = _hgrn_chunk(hf_ref[0], 2 * W, mat_ref[0], mask_ref[0], stf_ref, C - 1)
    ob_ref[0] = _hgrn_chunk(hb_ref[0], 3 * W, mat_ref[1], mask_ref[1], stb_ref, 0)


def _hgrn(hg, hg_meta_pad, batch, seq):
    nc = seq // HG_CHUNK
    hg3 = hg.reshape(batch, seq, 5 * HG_WIDTH)
    mats, masks = _hgrn_constants()
    blk = (1, HG_CHUNK, 5 * HG_WIDTH)
    oblk = (1, HG_CHUNK, HG_WIDTH)
    return pl.pallas_call(
        _hgrn_body,
        grid=(batch, nc),
        in_specs=[
            pl.BlockSpec(blk, lambda b, c: (b, c, 0)),
            pl.BlockSpec(blk, lambda b, c: (b, nc - 1 - c, 0)),
            pl.BlockSpec((HG_CHUNK, 5 * HG_WIDTH), lambda b, c: (0, 0)),
            pl.BlockSpec(mats.shape, lambda b, c: (0, 0, 0)),
            pl.BlockSpec(masks.shape, lambda b, c: (0, 0, 0, 0)),
        ],
        out_specs=[
            pl.BlockSpec(oblk, lambda b, c: (b, c, 0)),
            pl.BlockSpec(oblk, lambda b, c: (b, nc - 1 - c, 0)),
        ],
        out_shape=[jax.ShapeDtypeStruct((batch, seq, HG_WIDTH), F32)] * 2,
        scratch_shapes=[pltpu.VMEM((HG_HEADS, HG_HEAD_DIM, HG_HEAD_DIM), F32)] * 2,
        compiler_params=_params("arbitrary", "arbitrary"),
        name="hgrn",
    )(hg3, hg3, hg_meta_pad, mats, masks)


def _mix_route_body(yna_ref, of_ref, ob_ref, gate_ref, x_ref, wout_ref, hgn_ref, ffg_ref,
                    rwh_ref, rwl_ref, rb_ref, tri_ref, upe_ref,
                    h1_ref, xn_ref, lpos_ref, tw_ref, cnt_ref):
    o = of_ref[...] + ob_ref[...]
    parts = []
    for h in range(HG_HEADS):
        seg = o[:, h * HG_HEAD_DIM:(h + 1) * HG_HEAD_DIM]
        ms = jnp.mean(seg * seg, axis=-1, keepdims=True)
        parts.append(seg * lax.rsqrt(ms + RMS_EPS))
    yhg = jnp.concatenate(parts, axis=-1) * hgn_ref[...] * gate_ref[...]
    mix = _dot(yna_ref[...], wout_ref[0:NA_WIDTH, :]) + _dot(yhg.astype(BF16), wout_ref[NA_WIDTH:, :])
    h1 = x_ref[...] + mix
    h1_ref[...] = h1
    ms = jnp.mean(h1 * h1, axis=-1, keepdims=True)
    xn = h1 * lax.rsqrt(ms + RMS_EPS) * ffg_ref[...]
    x_hi, x_lo = _split_bf16(xn)
    xn_ref[...] = x_hi

    logits = (_dot(x_hi, rwh_ref[...]) + _dot(x_lo, rwh_ref[...]) + _dot(x_hi, rwl_ref[...])) + rb_ref[...]

    T, E = TOK_TILE, N_EXPERTS
    lane = lax.broadcasted_iota(jnp.int32, (T, E), 1).astype(F32)
    cur = logits
    sels, vals = [], []
    for _ in range(TOP_K):
        m = jnp.max(cur, axis=-1, keepdims=True)
        first = jnp.min(jnp.where(cur == m, lane, float(E)), axis=-1, keepdims=True)
        sel = lane == first
        sels.append(sel)
        vals.append(m)
        cur = jnp.where(sel, -jnp.inf, cur)
    es = [jnp.exp(vk - vals[0]) for vk in vals]
    den = es[0] + es[1] + es[2] + es[3]
    tw_ref[...] = _pack_cols([e / den for e in es])

    onehot = jnp.zeros((T, E), F32)
    for sel in sels:
        onehot = onehot + jnp.where(sel, 1.0, 0.0)
    lrank = _dot(tri_ref[...], onehot.astype(BF16))
    cnt = lrank[T - 1:T, :] + onehot[T - 1:T, :]
    cnt = jnp.floor((cnt + (SUBLANES - 1)) * (1.0 / SUBLANES)) * SUBLANES
    off = _dot(jnp.broadcast_to(cnt, (8, E)).astype(BF16), upe_ref[...])[0:1, :]
    base = off + lrank
    lpos_ref[...] = _pack_cols(
        [jnp.sum(jnp.where(sel, base, 0.0), axis=-1, keepdims=True) for sel in sels]).astype(jnp.int32)
    cnt_ref[0] = cnt.astype(jnp.int32)


def _mix_route(yna, o_f, o_b, hg, x2d, wout_bf16, hgn, ffg, router_w, router_b):
    n = x2d.shape[0]
    T = TOK_TILE
    nt = n // T
    rwh = router_w.astype(BF16)
    rwl = (router_w - rwh.astype(F32)).astype(BF16)
    tri = jnp.asarray(np.tril(np.ones((T, T), np.float32), -1), BF16)
    upe = jnp.asarray(np.triu(np.ones((N_EXPERTS, N_EXPERTS), np.float32), 1), BF16)
    row = lambda i: (i, 0)
    const = lambda i: (0, 0)
    return pl.pallas_call(
        _mix_route_body,
        grid=(nt,),
        in_specs=[
            pl.BlockSpec((T, NA_WIDTH), row),
            pl.BlockSpec((T, HG_WIDTH), row),
            pl.BlockSpec((T, HG_WIDTH), row),
            pl.BlockSpec((T, HG_WIDTH), lambda i: (i, 4)),
            pl.BlockSpec((T, D_MODEL), row),
            pl.BlockSpec((D_MODEL, D_MODEL), const),
            pl.BlockSpec((1, HG_WIDTH), const),
            pl.BlockSpec((1, D_MODEL), const),
            pl.BlockSpec((D_MODEL, N_EXPERTS), const),
            pl.BlockSpec((D_MODEL, N_EXPERTS), const),
            pl.BlockSpec((1, N_EXPERTS), const),
            pl.BlockSpec((T, T), const),
            pl.BlockSpec((N_EXPERTS, N_EXPERTS), const),
        ],
        out_specs=[
            pl.BlockSpec((T, D_MODEL), row),
            pl.BlockSpec((T, D_MODEL), row),
            pl.BlockSpec((T, TOP_K), row),
            pl.BlockSpec((T, TOP_K), row),
            pl.BlockSpec((1, 1, N_EXPERTS), lambda i: (i, 0, 0)),
        ],
        out_shape=[
            jax.ShapeDtypeStruct((n, D_MODEL), F32),
            jax.ShapeDtypeStruct((n, D_MODEL), BF16),
            jax.ShapeDtypeStruct((n, TOP_K), jnp.int32),
            jax.ShapeDtypeStruct((n, TOP_K), F32),
            jax.ShapeDtypeStruct((nt, 1, N_EXPERTS), jnp.int32),
        ],
        compiler_params=_params("arbitrary"),
        name="mix_route",
    )(yna, o_f, o_b, hg, x2d, wout_bf16, hgn, ffg, rwh, rwl, router_b, tri, upe)


def _copy_runs(i, cnt_ref, off_ref, goff_ref, make_copy):
    def per_expert(e, carry):
        n = pl.multiple_of(cnt_ref[i * N_EXPERTS + e], SUBLANES)
        lo = pl.multiple_of(off_ref[i * N_EXPERTS + e], SUBLANES)
        go = pl.multiple_of(goff_ref[i * N_EXPERTS + e], SUBLANES)

        @pl.when(n > 0)
        def _start():
            make_copy(lo, go, n).start()

        return carry

    lax.fori_loop(0, N_EXPERTS, per_expert, 0)


def _slot_matrix(lpos, weights=None):
    lane = lax.broadcasted_iota(jnp.int32, (TOK_TILE, STAGE_ROWS), 1)
    acc = jnp.zeros((TOK_TILE, STAGE_ROWS), F32)
    for k in range(TOP_K):
        hit = lpos[:, k:k + 1] == lane
        val = 1.0 if weights is None else weights[:, k:k + 1]
        acc = jnp.where(hit, val, acc)
    return acc


def _dispatch_body(cnt_ref, off_ref, goff_ref, tot_ref, xn_ref, lpos_ref, xbuf_in_ref, xbuf_ref, stage_ref, sem):
    del xbuf_in_ref
    i = pl.program_id(0)
    pt = _slot_matrix(lpos_ref[...]).astype(BF16)
    stage_ref[...] = _dot_tn(pt, xn_ref[...])

    def make_copy(lo, go, size):
        return pltpu.make_async_copy(stage_ref.at[pl.ds(lo, size)], xbuf_ref.at[pl.ds(go, size)], sem)

    _copy_runs(i, cnt_ref, off_ref, goff_ref, make_copy)
    tot = pl.multiple_of(tot_ref[i], SUBLANES)
    make_copy(0, 0, tot).wait()


def _dispatch(cnt, off, goff, tot, xn, lpos, n_rows):
    n = xn.shape[0]
    nt = n // TOK_TILE
    xbuf0 = jnp.zeros((n_rows, D_MODEL), F32)
    return pl.pallas_call(
        _dispatch_body,
        grid_spec=pltpu.PrefetchScalarGridSpec(
            num_scalar_prefetch=4,
            grid=(nt,),
            in_specs=[
                pl.BlockSpec((TOK_TILE, D_MODEL), lambda i, *_: (i, 0)),
                pl.BlockSpec((TOK_TILE, TOP_K), lambda i, *_: (i, 0)),
                pl.BlockSpec(memory_space=pl.ANY),
            ],
            out_specs=pl.BlockSpec(memory_space=pl.ANY),
            scratch_shapes=[pltpu.VMEM((STAGE_ROWS, D_MODEL), F32), pltpu.SemaphoreType.DMA(())],
        ),
        out_shape=jax.ShapeDtypeStruct((n_rows, D_MODEL), F32),
        input_output_aliases={6: 0},
        compiler_params=_params("arbitrary"),
        name="moe_dispatch",
    )(cnt, off, goff, tot, xn, lpos, xbuf0)


def _experts_body(blk_e_ref, nused_ref, x_ref, wgu_ref, bgu_ref, wd_ref, bd_ref, y_ref):
    del blk_e_ref

    @pl.when(pl.program_id(0) < nused_ref[0])
    def _():
        x = x_ref[...].astype(BF16)
        gu = _dot(x, wgu_ref[0]) + bgu_ref[0]
        gate = jnp.minimum(gu[:, :D_FF], SWIGLU_LIMIT)
        up = jnp.clip(gu[:, D_FF:], -SWIGLU_LIMIT, SWIGLU_LIMIT)
        act = (up + 1.0) * gate * jax.nn.sigmoid(SWIGLU_ALPHA * gate)
        y_ref[...] = _dot(act.astype(BF16), wd_ref[0]) + bd_ref[0]


def _experts(blk_e, nused, xbuf, wgu, bgu, wd, bd):
    n_rows = xbuf.shape[0]
    nblk = n_rows // EXPERT_BLOCK

    def rowblk(j, be, nu):
        return (jnp.minimum(j, nu[0] - 1), 0)

    def expert(j, be, nu):
        return (be[jnp.minimum(j, nu[0] - 1)], 0, 0)

    return pl.pallas_call(
        _experts_body,
        grid_spec=pltpu.PrefetchScalarGridSpec(
            num_scalar_prefetch=2,
            grid=(nblk,),
            in_specs=[
                pl.BlockSpec((EXPERT_BLOCK, D_MODEL), rowblk),
                pl.BlockSpec((1, D_MODEL, 2 * D_FF), expert),
                pl.BlockSpec((1, 1, 2 * D_FF), expert),
                pl.BlockSpec((1, D_FF, D_MODEL), expert),
                pl.BlockSpec((1, 1, D_MODEL), expert),
            ],
            out_specs=pl.BlockSpec((EXPERT_BLOCK, D_MODEL), rowblk),
        ),
        out_shape=jax.ShapeDtypeStruct((n_rows, D_MODEL), F32),
        input_output_aliases={2: 0},
        compiler_params=_params("arbitrary"),
        name="moe_experts",
    )(blk_e, nused, xbuf, wgu, bgu, wd, bd)


def _combine_body(cnt_ref, off_ref, goff_ref, tot_ref, ybuf_ref, lpos_ref, tw_ref, h1_ref, fg_ref, out_ref,
                  stage_ref, sem):
    i = pl.program_id(0)

    @pl.when(i == 0)
    def _init():
        stage_ref[...] = jnp.zeros_like(stage_ref)

    def make_copy(lo, go, size):
        return pltpu.make_async_copy(ybuf_ref.at[pl.ds(go, size)], stage_ref.at[pl.ds(lo, size)], sem)

    _copy_runs(i, cnt_ref, off_ref, goff_ref, make_copy)
    w_hi, w_lo = _split_bf16(_slot_matrix(lpos_ref[...], tw_ref[...]))
    tot = pl.multiple_of(tot_ref[i], SUBLANES)
    make_copy(0, 0, tot).wait()
    ys = stage_ref[...].astype(BF16)
    h2 = h1_ref[...] + (_dot(w_hi, ys) + _dot(w_lo, ys))
    ms = jnp.mean(h2 * h2, axis=-1, keepdims=True)
    out_ref[...] = h2 * lax.rsqrt(ms + RMS_EPS) * fg_ref[...]


def _combine(cnt, off, goff, tot, ybuf, lpos, tw, h1, final_g):
    n = h1.shape[0]
    nt = n // TOK_TILE
    row = lambda i, *_: (i, 0)
    return pl.pallas_call(
        _combine_body,
        grid_spec=pltpu.PrefetchScalarGridSpec(
            num_scalar_prefetch=4,
            grid=(nt,),
            in_specs=[
                pl.BlockSpec(memory_space=pl.ANY),
                pl.BlockSpec((TOK_TILE, TOP_K), row),
                pl.BlockSpec((TOK_TILE, TOP_K), row),
                pl.BlockSpec((TOK_TILE, D_MODEL), row),
                pl.BlockSpec((1, D_MODEL), lambda i, *_: (0, 0)),
            ],
            out_specs=pl.BlockSpec((TOK_TILE, D_MODEL), row),
            scratch_shapes=[pltpu.VMEM((STAGE_ROWS, D_MODEL), F32), pltpu.SemaphoreType.DMA(())],
        ),
        out_shape=jax.ShapeDtypeStruct((n, D_MODEL), F32),
        compiler_params=_params("arbitrary"),
        name="moe_combine",
    )(cnt, off, goff, tot, ybuf, lpos, tw, h1, final_g)


def _moe_layout(cnt_tiles):
    nt = cnt_tiles.shape[0]
    total = jnp.sum(cnt_tiles, axis=0)
    nblk_e = (total + EXPERT_BLOCK - 1) // EXPERT_BLOCK
    blk_end = jnp.cumsum(nblk_e)
    pstart = (blk_end - nblk_e) * EXPERT_BLOCK
    prefix = jnp.cumsum(cnt_tiles, axis=0) - cnt_tiles
    goff = pstart[None, :] + prefix
    off = jnp.cumsum(cnt_tiles, axis=1) - cnt_tiles
    tot = jnp.sum(cnt_tiles, axis=1)
    max_rows = nt * (TOK_TILE * TOP_K + N_EXPERTS * (SUBLANES - 1)) + N_EXPERTS * (EXPERT_BLOCK - 1)
    max_blocks = (max_rows + EXPERT_BLOCK - 1) // EXPERT_BLOCK
    blk_id = jnp.arange(max_blocks, dtype=blk_end.dtype)
    blk_e = jnp.minimum(jnp.sum(blk_end[None, :] <= blk_id[:, None], axis=1), N_EXPERTS - 1).astype(jnp.int32)
    nused = blk_end[-1:].astype(jnp.int32)
    flat = lambda a: a.reshape(-1).astype(jnp.int32)
    return flat(cnt_tiles), flat(off), flat(goff), flat(tot), blk_e, nused, max_blocks * EXPERT_BLOCK


def kernel(x, meta_tokens, attn_norm_g, w_in, na_rpb, na_norm_g, hgrn_lb_logits, hgrn_norm_g, w_out, ffn_norm_g,
           router_w, router_b, expert_w_gu, expert_b_gu, expert_w_down, expert_b_down, final_norm_g):
    B, T, D = x.shape
    x2d = x.reshape(B * T, D)
    w_in_b = w_in[0].astype(BF16)
    ng = attn_norm_g[0].reshape(1, D)
    lbl = hgrn_lb_logits[:, :, :].reshape(4, HG_WIDTH)

    att, hg = _inproj(x2d, ng, w_in_b, lbl, 512)
    att_m, hg_m = _inproj(meta_tokens.astype(F32), ng, w_in_b, lbl, N_META)
    att_m = jnp.pad(att_m, ((0, META_PAD - N_META), (0, 0)))
    hg_m = jnp.pad(hg_m, ((0, HG_CHUNK - N_META), (0, 0)))

    yna = _natten(att, att_m, _natten_bias_tables(na_rpb[0]), na_norm_g[0].reshape(1, NA_WIDTH), B, T)
    o_f, o_b = _hgrn(hg, hg_m, B, T)

    h1, xn, lpos, tw, cnt_tiles = _mix_route(
        yna.reshape(B * T, NA_WIDTH), o_f.reshape(B * T, HG_WIDTH), o_b.reshape(B * T, HG_WIDTH), hg, x2d,
        w_out[0].astype(BF16), hgrn_norm_g[0].reshape(1, HG_WIDTH), ffn_norm_g[0].reshape(1, D),
        router_w[0], router_b[0].reshape(1, N_EXPERTS))

    cnt, off, goff, tot, blk_e, nused, n_rows = _moe_layout(cnt_tiles[:, 0, :])
    xbuf = _dispatch(cnt, off, goff, tot, xn, lpos, n_rows)
    ybuf = _experts(blk_e, nused, xbuf, expert_w_gu[0].astype(BF16), expert_b_gu[0][:, None, :],
                    expert_w_down[0].astype(BF16), expert_b_down[0][:, None, :])
    out = _combine(cnt, off, goff, tot, ybuf, lpos, tw, h1, final_norm_g.reshape(1, D))
    return out.reshape(B, T, D)
```

```python
import functools

import numpy as np
import jax
import jax.numpy as jnp
from jax import lax
from jax.experimental import pallas as pl
from jax.experimental.pallas import tpu as pltpu

F32 = jnp.float32
BF16 = jnp.bfloat16

D_MODEL = 1024
N_META = 16
GRID_W = 64
NA_WIDTH = 512
NA_HEAD_DIM = 64
NA_HEADS = 8
NA_KH = 8
NA_KW = 16
HG_WIDTH = 512
HG_HEAD_DIM = 128
HG_HEADS = 4
HG_CHUNK = 64
IN_COLS = 3 * NA_WIDTH + 5 * HG_WIDTH
N_EXPERTS = 32
TOP_K = 4
D_FF = 1024
SWIGLU_LIMIT = 7.0
SWIGLU_ALPHA = 1.702
RMS_EPS = 1e-6
NEG_BIG = -1e30
LOG2_E = 1.4426950408889634

LANES = 128
VMEM_LIMIT_BYTES = 56 * 1024 * 1024

TOK_TILE = 256
EXPERT_BLOCK = 256
SUBLANES = 8
STAGE_ROWS = TOK_TILE * TOP_K + N_EXPERTS * SUBLANES
N_HG_LEVELS = 6


def _dot(a, b):
    return jnp.dot(a, b, preferred_element_type=F32)


def _dot_nt(a, b):
    return lax.dot_general(a, b, (((1,), (1,)), ((), ())), preferred_element_type=F32)


def _dot_tn(a, b):
    return lax.dot_general(a, b, (((0,), (0,)), ((), ())), preferred_element_type=F32)


def _split_bf16(x):
    hi = x.astype(BF16)
    lo = (x - hi.astype(F32)).astype(BF16)
    return hi, lo


def _pack_cols(cols):
    lane = lax.broadcasted_iota(jnp.int32, (cols[0].shape[0], len(cols)), 1)
    out = jnp.broadcast_to(cols[-1], lane.shape)
    for k in reversed(range(len(cols) - 1)):
        out = jnp.where(lane == k, cols[k], out)
    return out


def _params(*sem):
    return pltpu.CompilerParams(dimension_semantics=sem, vmem_limit_bytes=VMEM_LIMIT_BYTES)


def _inproj_body(x_ref, ng_ref, w_ref, lbl_ref, att_ref, hg_ref):
    x = x_ref[...]
    ms = jnp.mean(x * x, axis=-1, keepdims=True)
    n = (x * lax.rsqrt(ms + RMS_EPS) * ng_ref[...]).astype(BF16)

    def proj(lo, hi):
        return _dot(n, w_ref[:, lo:hi])

    pq = proj(0, NA_WIDTH)
    att_ref[:, 0:NA_WIDTH] = (pq * (NA_HEAD_DIM ** -0.5 * LOG2_E)).astype(BF16)
    att_ref[:, NA_WIDTH:3 * NA_WIDTH] = proj(NA_WIDTH, 3 * NA_WIDTH).astype(BF16)

    base = 3 * NA_WIDTH
    W = HG_WIDTH
    qh = proj(base, base + W)
    hg_ref[:, 0:W] = qh * jax.nn.sigmoid(qh)
    hg_ref[:, W:2 * W] = proj(base + W, base + 2 * W)
    lbl = lbl_ref[...]
    for d in range(2):
        a0 = lbl[2 * d:2 * d + 1, :]
        a1 = lbl[2 * d + 1:2 * d + 2, :]
        m = jnp.maximum(a0, a1)
        e0 = jnp.exp(a0 - m)
        e1 = jnp.exp(a1 - m)
        lb = e0 / (e0 + e1)
        raw = proj(base + (2 + d) * W, base + (3 + d) * W)
        f = lb + (1.0 - lb) * jax.nn.sigmoid(raw)
        hg_ref[:, (2 + d) * W:(3 + d) * W] = jnp.log(f)
    gt = proj(base + 4 * W, base + 5 * W)
    hg_ref[:, 4 * W:5 * W] = gt * jax.nn.sigmoid(gt)


def _inproj(x2d, norm_g, w_bf16, lb_logits4, tm):
    n = x2d.shape[0]
    return pl.pallas_call(
        _inproj_body,
        grid=(n // tm,),
        in_specs=[
            pl.BlockSpec((tm, D_MODEL), lambda i: (i, 0)),
            pl.BlockSpec((1, D_MODEL), lambda i: (0, 0)),
            pl.BlockSpec((D_MODEL, IN_COLS), lambda i: (0, 0)),
            pl.BlockSpec((4, HG_WIDTH), lambda i: (0, 0)),
        ],
        out_specs=[
            pl.BlockSpec((tm, 3 * NA_WIDTH), lambda i: (i, 0)),
            pl.BlockSpec((tm, 5 * HG_WIDTH), lambda i: (i, 0)),
        ],
        out_shape=[
            jax.ShapeDtypeStruct((n, 3 * NA_WIDTH), BF16),
            jax.ShapeDtypeStruct((n, 5 * HG_WIDTH), F32),
        ],
        compiler_params=_params("arbitrary"),
        name="inproj",
    )(x2d, norm_g, w_bf16, lb_logits4)


N_WIN_KEYS = NA_KH * GRID_W
META_PAD = LANES


NA_GROUP = 4
GROUP_LANES = NA_GROUP * NA_HEAD_DIM


NA_ROWS_PER_STEP = 4


def _natten_row(rr, r, q_ref, k_ref, v_ref, km_ref, vm_ref, bias_ref, *, rows):
    rs = jnp.clip(r - NA_KH // 2, 0, rows - NA_KH)
    d0 = rs - r + (NA_KH - 1)
    start = pl.multiple_of(rs * GRID_W, GRID_W)
    lane_head = lax.broadcasted_iota(jnp.int32, (GRID_W, GROUP_LANES), 1) // NA_HEAD_DIM
    meta_col = lax.broadcasted_iota(jnp.int32, (1, META_PAD), 1)
    meta_bias = jnp.where(meta_col < N_META, 0.0, NEG_BIG).astype(F32)
    folded = []
    for g in range(NA_HEADS // NA_GROUP):
        sl = slice(g * GROUP_LANES, (g + 1) * GROUP_LANES)
        q4 = q_ref[0, rr * GRID_W:(rr + 1) * GRID_W, sl]
        zero = jnp.zeros_like(q4)
        qm = jnp.concatenate([jnp.where(lane_head == h, q4, zero) for h in range(NA_GROUP)], axis=0)
        kw = k_ref[0, pl.ds(start, N_WIN_KEYS), sl]
        vw = v_ref[0, pl.ds(start, N_WIN_KEYS), sl]
        s = _dot_nt(qm, kw) + bias_ref[d0, g]
        sm = _dot_nt(qm, km_ref[:, sl]) + meta_bias
        m = jnp.maximum(jnp.max(s, axis=-1, keepdims=True), jnp.max(sm, axis=-1, keepdims=True))
        e = jnp.exp2(s - m)
        em = jnp.exp2(sm - m)
        den = jnp.sum(e, axis=-1, keepdims=True) + jnp.sum(em, axis=-1, keepdims=True)
        o = (_dot(e.astype(BF16), vw) + _dot(em.astype(BF16), vm_ref[:, sl])) * (1.0 / den)
        acc = jnp.where(lane_head == 0, o[0:GRID_W], 0.0)
        for h in range(1, NA_GROUP):
            acc = jnp.where(lane_head == h, o[h * GRID_W:(h + 1) * GRID_W], acc)
        folded.append(acc)
    return jnp.concatenate(folded, axis=-1)


def _natten_body(q_ref, k_ref, v_ref, km_ref, vm_ref, bias_ref, ng_ref, bd_ref, o_ref, *, rows):
    r0 = pl.program_id(1) * NA_ROWS_PER_STEP
    o2 = jnp.concatenate(
        [_natten_row(rr, r0 + rr, q_ref, k_ref, v_ref, km_ref, vm_ref, bias_ref, rows=rows)
         for rr in range(NA_ROWS_PER_STEP)], axis=0)
    sq_hi, sq_lo = _split_bf16(o2 * o2)
    ms = (_dot(sq_hi, bd_ref[...]) + _dot(sq_lo, bd_ref[...])) * (1.0 / NA_HEAD_DIM)
    o_ref[0] = (o2 * lax.rsqrt(ms + RMS_EPS) * ng_ref[...]).astype(o_ref.dtype)


def _natten_bias_tables(rpb):
    c = np.arange(GRID_W)[:, None]
    kc = np.arange(GRID_W)[None, :]
    cs = np.clip(c - NA_KW // 2, 0, GRID_W - NA_KW)
    valid = (kc >= cs) & (kc < cs + NA_KW)
    dc = np.clip(kc - c + (NA_KW - 1), 0, 2 * NA_KW - 2)
    t1 = jnp.where(valid[None, None], rpb.astype(F32)[:, :, dc] * LOG2_E, NEG_BIG)
    tabs = []
    for d0 in range(NA_KH):
        t = t1[:, d0:d0 + NA_KH]
        tabs.append(t.transpose(0, 2, 1, 3).reshape(NA_HEADS // NA_GROUP, NA_GROUP * GRID_W, N_WIN_KEYS))
    return jnp.stack(tabs)


def _natten(att, att_meta, bias_tabs, norm_g, batch, seq):
    rows = seq // GRID_W
    att3 = att.reshape(batch, seq, 3 * NA_WIDTH)
    head_of = np.arange(NA_WIDTH) // NA_HEAD_DIM
    same_head = jnp.asarray(head_of[:, None] == head_of[None, :], BF16)

    qrows = NA_ROWS_PER_STEP * GRID_W
    return pl.pallas_call(
        functools.partial(_natten_body, rows=rows),
        grid=(batch, rows // NA_ROWS_PER_STEP),
        in_specs=[
            pl.BlockSpec((1, qrows, NA_WIDTH), lambda b, r: (b, r, 0)),
            pl.BlockSpec((1, seq, NA_WIDTH), lambda b, r: (b, 0, 1)),
            pl.BlockSpec((1, seq, NA_WIDTH), lambda b, r: (b, 0, 2)),
            pl.BlockSpec((META_PAD, NA_WIDTH), lambda b, r: (0, 1)),
            pl.BlockSpec((META_PAD, NA_WIDTH), lambda b, r: (0, 2)),
            pl.BlockSpec(bias_tabs.shape, lambda b, r: (0, 0, 0, 0), pipeline_mode=pl.Buffered(1)),
            pl.BlockSpec((1, NA_WIDTH), lambda b, r: (0, 0)),
            pl.BlockSpec((NA_WIDTH, NA_WIDTH), lambda b, r: (0, 0)),
        ],
        out_specs=pl.BlockSpec((1, qrows, NA_WIDTH), lambda b, r: (b, r, 0)),
        out_shape=jax.ShapeDtypeStruct((batch, seq, NA_WIDTH), BF16),
        compiler_params=_params("arbitrary", "arbitrary"),
        name="natten",
    )(att3, att3, att3, att_meta, att_meta, bias_tabs, norm_g, same_head)


HG_CHUNKS_PER_STEP = 4
ROW_B, ROW_LV = 0, 1
N_EXP_BLOCKS = 1 + N_HG_LEVELS


def _hgrn_constants():
    C = HG_CHUNK
    t = np.arange(C)
    u = t[None, :]
    mats = [u <= t[:, None]]
    masks = [np.eye(C, dtype=bool)]
    for lv in range(N_HG_LEVELS):
        m = 1 << lv
        blk = t // (2 * m)
        upper = (t // m) % 2 == 1
        p = blk * 2 * m + m - 1
        q_rows = upper[:, None] & (u > p[:, None]) & (u <= t[:, None])
        k_rows = (~upper)[:, None] & (u > t[:, None]) & (u <= p[:, None])
        mats.append(q_rows | k_rows)
        masks.append((blk[:, None] == blk[None, :]) & upper[:, None] & (~upper)[None, :])
    fwd = np.concatenate(mats, axis=0).astype(np.float32)
    fwd_mask = np.stack(masks).astype(np.float32)
    blocks = fwd.reshape(N_EXP_BLOCKS, C, C)
    bwd = blocks[:, ::-1, ::-1].reshape(N_EXP_BLOCKS * C, C)
    bwd_mask = fwd_mask[:, ::-1, ::-1]
    both = np.stack([fwd, bwd])
    both = np.concatenate([both, both], axis=-1)
    return (jnp.asarray(both, BF16), jnp.asarray(np.stack([fwd_mask, bwd_mask]), F32))


def _hgrn_exponents(mat, g):
    g_hi, g_lo = _split_bf16(g)
    return _dot(mat, jnp.concatenate([g_hi, g_lo], axis=0))


def _hgrn_body(hf_ref, hb_ref, meta_ref, mat_ref, mask_ref, of_ref, ob_ref, stf_ref, stb_ref):
    C, W = HG_CHUNK, HG_WIDTH
    c = pl.program_id(1)

    @pl.when(c == 0)
    def _init():
        mb = meta_ref[...]
        v = mb[:, W:2 * W].astype(BF16)
        prefix = mat_ref[0, ROW_B * C:(ROW_B + 1) * C, :]
        for d, st_ref in ((0, stf_ref), (1, stb_ref)):
            g = mb[:, (2 + d) * W:(3 + d) * W]
            b = _hgrn_exponents(prefix, g)
            kb = ((1.0 - jnp.exp(g)) * jnp.exp(b[C - 1:C] - b)).astype(BF16)
            for h in range(HG_HEADS):
                sl = slice(h * HG_HEAD_DIM, (h + 1) * HG_HEAD_DIM)
                st_ref[h] = _dot_tn(v[:, sl], kb[:, sl])

    n = HG_CHUNKS_PER_STEP
    dir_refs = ((hf_ref, 2 * W, stf_ref, C - 1, of_ref), (hb_ref, 3 * W, stb_ref, 0, ob_ref))
    tasks = [(0, j * C) for j in range(n)] + [(1, (n - 1 - j) * C) for j in range(n)]
    heads = [slice(h * HG_HEAD_DIM, (h + 1) * HG_HEAD_DIM) for h in range(HG_HEADS)]
    qs, ks, vs, exs, ex_ends = {}, {}, {}, {}, {}
    for d, r0 in tasks:
        ref, g_col, total_row = dir_refs[d][0], dir_refs[d][1], dir_refs[d][3]
        g = ref[0, r0:r0 + C, g_col:g_col + W]
        sums = _hgrn_exponents(mat_ref[d], g)
        exs[d, r0] = jnp.exp(sums)
        ex_ends[d, r0] = jnp.exp(sums[total_row:total_row + 1] - sums[0:C])
        qs[d, r0] = ref[0, r0:r0 + C, 0:W]
        ks[d, r0] = 1.0 - jnp.exp(g)
        vs[d, r0] = ref[0, r0:r0 + C, W:2 * W].astype(BF16)

    def ex_rows(task, i, sl):
        return exs[task][i * C:(i + 1) * C, sl]

    prods = {}
    for task in tasks:
        for h, sl in enumerate(heads):
            qh, kh = qs[task][:, sl], ks[task][:, sl]
            ps = [_dot_nt(qh.astype(BF16), kh.astype(BF16))]
            for lv in range(N_HG_LEVELS):
                scale = ex_rows(task, ROW_LV + lv, sl)
                ps.append(_dot_nt((qh * scale).astype(BF16), (kh * scale).astype(BF16)))
            prods[task, h] = ps
    attn, qbs, kbs = {}, {}, {}
    for task in tasks:
        d = task[0]
        for h, sl in enumerate(heads):
            a = prods[task, h][0] * mask_ref[d, 0]
            for lv in range(N_HG_LEVELS):
                a = a + prods[task, h][1 + lv] * mask_ref[d, 1 + lv]
            attn[task, h] = _dot(a.astype(BF16), vs[task][:, sl])
            qbs[task, h] = (qs[task][:, sl] * ex_rows(task, ROW_B, sl)).astype(BF16)
            kbs[task, h] = _dot_tn(vs[task][:, sl], (ks[task][:, sl] * ex_ends[task][:, sl]).astype(BF16))
    states = {(d, h): dir_refs[d][2][h] for d in range(2) for h in range(HG_HEADS)}
    for task in tasks:
        d, r0 = task
        total_row, o_ref = dir_refs[d][3], dir_refs[d][4]
        outs = []
        for h, sl in enumerate(heads):
            st = states[d, h]
            outs.append(attn[task, h] + _dot_nt(qbs[task, h], st.astype(BF16)))
            decay = exs[task][ROW_B * C + total_row:ROW_B * C + total_row + 1, sl]
            states[d, h] = decay * st + kbs[task, h]
        o_ref[0, r0:r0 + C, :] = jnp.concatenate(outs, axis=-1)
    for d in range(2):
        for h in range(HG_HEADS):
            dir_refs[d][2][h] = states[d, h]


def _hgrn(hg, hg_meta_pad, batch, seq):
    nc = seq // (HG_CHUNK * HG_CHUNKS_PER_STEP)
    hg3 = hg.reshape(batch, seq, 5 * HG_WIDTH)
    mats, masks = _hgrn_constants()
    blk = (1, HG_CHUNK * HG_CHUNKS_PER_STEP, 5 * HG_WIDTH)
    oblk = (1, HG_CHUNK * HG_CHUNKS_PER_STEP, HG_WIDTH)
    return pl.pallas_call(
        _hgrn_body,
        grid=(batch, nc),
        in_specs=[
            pl.BlockSpec(blk, lambda b, c: (b, c, 0)),
            pl.BlockSpec(blk, lambda b, c: (b, nc - 1 - c, 0)),
            pl.BlockSpec((HG_CHUNK, 5 * HG_WIDTH), lambda b, c: (0, 0)),
            pl.BlockSpec(mats.shape, lambda b, c: (0, 0, 0)),
            pl.BlockSpec(masks.shape, lambda b, c: (0, 0, 0, 0)),
        ],
        out_specs=[
            pl.BlockSpec(oblk, lambda b, c: (b, c, 0)),
            pl.BlockSpec(oblk, lambda b, c: (b, nc - 1 - c, 0)),
        ],
        out_shape=[jax.ShapeDtypeStruct((batch, seq, HG_WIDTH), F32)] * 2,
        scratch_shapes=[pltpu.VMEM((HG_HEADS, HG_HEAD_DIM, HG_HEAD_DIM), F32)] * 2,
        compiler_params=_params("arbitrary", "arbitrary"),
        name="hgrn",
    )(hg3, hg3, hg_meta_pad, mats, masks)


def _mix_route_body(yna_ref, of_ref, ob_ref, gate_ref, x_ref, wout_ref, hgn_ref, ffg_ref,
                    rwh_ref, rwl_ref, rb_ref, tri_ref, upe_ref,
                    h1_ref, xn_ref, lpos_ref, tw_ref, cnt_ref):
    o = of_ref[...] + ob_ref[...]
    parts = []
    for h in range(HG_HEADS):
        seg = o[:, h * HG_HEAD_DIM:(h + 1) * HG_HEAD_DIM]
        ms = jnp.mean(seg * seg, axis=-1, keepdims=True)
        parts.append(seg * lax.rsqrt(ms + RMS_EPS))
    yhg = jnp.concatenate(parts, axis=-1) * hgn_ref[...] * gate_ref[...]
    mix = _dot(yna_ref[...], wout_ref[0:NA_WIDTH, :]) + _dot(yhg.astype(BF16), wout_ref[NA_WIDTH:, :])
    h1 = x_ref[...] + mix
    h1_ref[...] = h1
    ms = jnp.mean(h1 * h1, axis=-1, keepdims=True)
    xn = h1 * lax.rsqrt(ms + RMS_EPS) * ffg_ref[...]
    x_hi, x_lo = _split_bf16(xn)
    xn_ref[...] = x_hi

    logits = (_dot(x_hi, rwh_ref[...]) + _dot(x_lo, rwh_ref[...]) + _dot(x_hi, rwl_ref[...])) + rb_ref[...]

    T, E = TOK_TILE, N_EXPERTS
    lane = lax.broadcasted_iota(jnp.int32, (T, E), 1).astype(F32)
    cur = logits
    sels, vals = [], []
    for _ in range(TOP_K):
        m = jnp.max(cur, axis=-1, keepdims=True)
        first = jnp.min(jnp.where(cur == m, lane, float(E)), axis=-1, keepdims=True)
        sel = lane == first
        sels.append(sel)
        vals.append(m)
        cur = jnp.where(sel, -jnp.inf, cur)
    es = [jnp.exp(vk - vals[0]) for vk in vals]
    den = es[0] + es[1] + es[2] + es[3]
    tw_ref[...] = _pack_cols([e / den for e in es])

    onehot = jnp.zeros((T, E), F32)
    for sel in sels:
        onehot = onehot + jnp.where(sel, 1.0, 0.0)
    lrank = _dot(tri_ref[...], onehot.astype(BF16))
    cnt = lrank[T - 1:T, :] + onehot[T - 1:T, :]
    cnt = jnp.floor((cnt + (SUBLANES - 1)) * (1.0 / SUBLANES)) * SUBLANES
    off = _dot(jnp.broadcast_to(cnt, (8, E)).astype(BF16), upe_ref[...])[0:1, :]
    base = off + lrank
    lpos_ref[...] = _pack_cols(
        [jnp.sum(jnp.where(sel, base, 0.0), axis=-1, keepdims=True) for sel in sels]).astype(jnp.int32)
    cnt_ref[0] = cnt.astype(jnp.int32)


def _mix_route(yna, o_f, o_b, hg, x2d, wout_bf16, hgn, ffg, router_w, router_b):
    n = x2d.shape[0]
    T = TOK_TILE
    nt = n // T
    rwh = router_w.astype(BF16)
    rwl = (router_w - rwh.astype(F32)).astype(BF16)
    tri = jnp.asarray(np.tril(np.ones((T, T), np.float32), -1), BF16)
    upe = jnp.asarray(np.triu(np.ones((N_EXPERTS, N_EXPERTS), np.float32), 1), BF16)
    row = lambda i: (i, 0)
    const = lambda i: (0, 0)
    return pl.pallas_call(
        _mix_route_body,
        grid=(nt,),
        in_specs=[
            pl.BlockSpec((T, NA_WIDTH), row),
            pl.BlockSpec((T, HG_WIDTH), row),
            pl.BlockSpec((T, HG_WIDTH), row),
            pl.BlockSpec((T, HG_WIDTH), lambda i: (i, 4)),
            pl.BlockSpec((T, D_MODEL), row),
            pl.BlockSpec((D_MODEL, D_MODEL), const),
            pl.BlockSpec((1, HG_WIDTH), const),
            pl.BlockSpec((1, D_MODEL), const),
            pl.BlockSpec((D_MODEL, N_EXPERTS), const),
            pl.BlockSpec((D_MODEL, N_EXPERTS), const),
            pl.BlockSpec((1, N_EXPERTS), const),
            pl.BlockSpec((T, T), const),
            pl.BlockSpec((N_EXPERTS, N_EXPERTS), const),
        ],
        out_specs=[
            pl.BlockSpec((T, D_MODEL), row),
            pl.BlockSpec((T, D_MODEL), row),
            pl.BlockSpec((T, TOP_K), row),
            pl.BlockSpec((T, TOP_K), row),
            pl.BlockSpec((1, 1, N_EXPERTS), lambda i: (i, 0, 0)),
        ],
        out_shape=[
            jax.ShapeDtypeStruct((n, D_MODEL), F32),
            jax.ShapeDtypeStruct((n, D_MODEL), BF16),
            jax.ShapeDtypeStruct((n, TOP_K), jnp.int32),
            jax.ShapeDtypeStruct((n, TOP_K), F32),
            jax.ShapeDtypeStruct((nt, 1, N_EXPERTS), jnp.int32),
        ],
        compiler_params=_params("arbitrary"),
        name="mix_route",
    )(yna, o_f, o_b, hg, x2d, wout_bf16, hgn, ffg, rwh, rwl, router_b, tri, upe)


def _copy_runs(i, cnt_ref, off_ref, goff_ref, make_copy):
    def per_expert(e, carry):
        n = pl.multiple_of(cnt_ref[i * N_EXPERTS + e], SUBLANES)
        lo = pl.multiple_of(off_ref[i * N_EXPERTS + e], SUBLANES)
        go = pl.multiple_of(goff_ref[i * N_EXPERTS + e], SUBLANES)

        @pl.when(n > 0)
        def _start():
            make_copy(lo, go, n).start()

        return carry

    lax.fori_loop(0, N_EXPERTS, per_expert, 0)


def _slot_matrix(lpos, weights=None):
    lane = lax.broadcasted_iota(jnp.int32, (TOK_TILE, STAGE_ROWS), 1)
    acc = jnp.zeros((TOK_TILE, STAGE_ROWS), F32)
    for k in range(TOP_K):
        hit = lpos[:, k:k + 1] == lane
        val = 1.0 if weights is None else weights[:, k:k + 1]
        acc = jnp.where(hit, val, acc)
    return acc


def _dispatch_body(cnt_ref, off_ref, goff_ref, tot_ref, xn_ref, lpos_ref, xbuf_in_ref, xbuf_ref, stage_ref, sem):
    del xbuf_in_ref
    i = pl.program_id(0)
    pt = _slot_matrix(lpos_ref[...]).astype(BF16)
    stage_ref[...] = _dot_tn(pt, xn_ref[...])

    def make_copy(lo, go, size):
        return pltpu.make_async_copy(stage_ref.at[pl.ds(lo, size)], xbuf_ref.at[pl.ds(go, size)], sem)

    _copy_runs(i, cnt_ref, off_ref, goff_ref, make_copy)
    tot = pl.multiple_of(tot_ref[i], SUBLANES)
    make_copy(0, 0, tot).wait()


def _dispatch(cnt, off, goff, tot, xn, lpos, n_rows):
    n = xn.shape[0]
    nt = n // TOK_TILE
    xbuf0 = jnp.zeros((n_rows, D_MODEL), F32)
    return pl.pallas_call(
        _dispatch_body,
        grid_spec=pltpu.PrefetchScalarGridSpec(
            num_scalar_prefetch=4,
            grid=(nt,),
            in_specs=[
                pl.BlockSpec((TOK_TILE, D_MODEL), lambda i, *_: (i, 0)),
                pl.BlockSpec((TOK_TILE, TOP_K), lambda i, *_: (i, 0)),
                pl.BlockSpec(memory_space=pl.ANY),
            ],
            out_specs=pl.BlockSpec(memory_space=pl.ANY),
            scratch_shapes=[pltpu.VMEM((STAGE_ROWS, D_MODEL), F32), pltpu.SemaphoreType.DMA(())],
        ),
        out_shape=jax.ShapeDtypeStruct((n_rows, D_MODEL), F32),
        input_output_aliases={6: 0},
        compiler_params=_params("arbitrary"),
        name="moe_dispatch",
    )(cnt, off, goff, tot, xn, lpos, xbuf0)


def _experts_body(blk_e_ref, nused_ref, x_ref, wgu_ref, bgu_ref, wd_ref, bd_ref, y_ref):
    del blk_e_ref

    @pl.when(pl.program_id(0) < nused_ref[0])
    def _():
        x = x_ref[...].astype(BF16)
        gu = _dot(x, wgu_ref[0]) + bgu_ref[0]
        gate = jnp.minimum(gu[:, :D_FF], SWIGLU_LIMIT)
        up = jnp.clip(gu[:, D_FF:], -SWIGLU_LIMIT, SWIGLU_LIMIT)
        act = (up + 1.0) * gate * jax.nn.sigmoid(SWIGLU_ALPHA * gate)
        y_ref[...] = _dot(act.astype(BF16), wd_ref[0]) + bd_ref[0]


def _experts(blk_e, nused, xbuf, wgu, bgu, wd, bd):
    n_rows = xbuf.shape[0]
    nblk = n_rows // EXPERT_BLOCK

    def rowblk(j, be, nu):
        return (jnp.minimum(j, nu[0] - 1), 0)

    def expert(j, be, nu):
        return (be[jnp.minimum(j, nu[0] - 1)], 0, 0)

    return pl.pallas_call(
        _experts_body,
        grid_spec=pltpu.PrefetchScalarGridSpec(
            num_scalar_prefetch=2,
            grid=(nblk,),
            in_specs=[
                pl.BlockSpec((EXPERT_BLOCK, D_MODEL), rowblk),
                pl.BlockSpec((1, D_MODEL, 2 * D_FF), expert),
                pl.BlockSpec((1, 1, 2 * D_FF), expert),
                pl.BlockSpec((1, D_FF, D_MODEL), expert),
                pl.BlockSpec((1, 1, D_MODEL), expert),
            ],
            out_specs=pl.BlockSpec((EXPERT_BLOCK, D_MODEL), rowblk),
        ),
        out_shape=jax.ShapeDtypeStruct((n_rows, D_MODEL), F32),
        input_output_aliases={2: 0},
        compiler_params=_params("arbitrary"),
        name="moe_experts",
    )(blk_e, nused, xbuf, wgu, bgu, wd, bd)


def _combine_body(cnt_ref, off_ref, goff_ref, tot_ref, ybuf_ref, lpos_ref, tw_ref, h1_ref, fg_ref, out_ref,
                  stage_ref, sem):
    i = pl.program_id(0)

    @pl.when(i == 0)
    def _init():
        stage_ref[...] = jnp.zeros_like(stage_ref)

    def make_copy(lo, go, size):
        return pltpu.make_async_copy(ybuf_ref.at[pl.ds(go, size)], stage_ref.at[pl.ds(lo, size)], sem)

    _copy_runs(i, cnt_ref, off_ref, goff_ref, make_copy)
    w_hi, w_lo = _split_bf16(_slot_matrix(lpos_ref[...], tw_ref[...]))
    tot = pl.multiple_of(tot_ref[i], SUBLANES)
    make_copy(0, 0, tot).wait()
    ys = stage_ref[...].astype(BF16)
    h2 = h1_ref[...] + (_dot(w_hi, ys) + _dot(w_lo, ys))
    ms = jnp.mean(h2 * h2, axis=-1, keepdims=True)
    out_ref[...] = h2 * lax.rsqrt(ms + RMS_EPS) * fg_ref[...]


def _combine(cnt, off, goff, tot, ybuf, lpos, tw, h1, final_g):
    n = h1.shape[0]
    nt = n // TOK_TILE
    row = lambda i, *_: (i, 0)
    return pl.pallas_call(
        _combine_body,
        grid_spec=pltpu.PrefetchScalarGridSpec(
            num_scalar_prefetch=4,
            grid=(nt,),
            in_specs=[
                pl.BlockSpec(memory_space=pl.ANY),
                pl.BlockSpec((TOK_TILE, TOP_K), row),
                pl.BlockSpec((TOK_TILE, TOP_K), row),
                pl.BlockSpec((TOK_TILE, D_MODEL), row),
                pl.BlockSpec((1, D_MODEL), lambda i, *_: (0, 0)),
            ],
            out_specs=pl.BlockSpec((TOK_TILE, D_MODEL), row),
            scratch_shapes=[pltpu.VMEM((STAGE_ROWS, D_MODEL), F32), pltpu.SemaphoreType.DMA(())],
        ),
        out_shape=jax.ShapeDtypeStruct((n, D_MODEL), F32),
        compiler_params=_params("arbitrary"),
        name="moe_combine",
    )(cnt, off, goff, tot, ybuf, lpos, tw, h1, final_g)


def _moe_layout(cnt_tiles):
    nt = cnt_tiles.shape[0]
    total = jnp.sum(cnt_tiles, axis=0)
    nblk_e = (total + EXPERT_BLOCK - 1) // EXPERT_BLOCK
    blk_end = jnp.cumsum(nblk_e)
    pstart = (blk_end - nblk_e) * EXPERT_BLOCK
    prefix = jnp.cumsum(cnt_tiles, axis=0) - cnt_tiles
    goff = pstart[None, :] + prefix
    off = jnp.cumsum(cnt_tiles, axis=1) - cnt_tiles
    tot = jnp.sum(cnt_tiles, axis=1)
    max_rows = nt * (TOK_TILE * TOP_K + N_EXPERTS * (SUBLANES - 1)) + N_EXPERTS * (EXPERT_BLOCK - 1)
    max_blocks = (max_rows + EXPERT_BLOCK - 1) // EXPERT_BLOCK
    blk_id = jnp.arange(max_blocks, dtype=blk_end.dtype)
    blk_e = jnp.minimum(jnp.sum(blk_end[None, :] <= blk_id[:, None], axis=1), N_EXPERTS - 1).astype(jnp.int32)
    nused = blk_end[-1:].astype(jnp.int32)
    flat = lambda a: a.reshape(-1).astype(jnp.int32)
    return flat(cnt_tiles), flat(off), flat(goff), flat(tot), blk_e, nused, max_blocks * EXPERT_BLOCK


def kernel(x, meta_tokens, attn_norm_g, w_in, na_rpb, na_norm_g, hgrn_lb_logits, hgrn_norm_g, w_out, ffn_norm_g,
           router_w, router_b, expert_w_gu, expert_b_gu, expert_w_down, expert_b_down, final_norm_g):
    B, T, D = x.shape
    x2d = x.reshape(B * T, D)
    w_in_b = w_in[0].astype(BF16)
    ng = attn_norm_g[0].reshape(1, D)
    lbl = hgrn_lb_logits[:, :, :].reshape(4, HG_WIDTH)

    att, hg = _inproj(x2d, ng, w_in_b, lbl, 512)
    att_m, hg_m = _inproj(meta_tokens.astype(F32), ng, w_in_b, lbl, N_META)
    att_m = jnp.pad(att_m, ((0, META_PAD - N_META), (0, 0)))
    hg_m = jnp.pad(hg_m, ((0, HG_CHUNK - N_META), (0, 0)))

    yna = _natten(att, att_m, _natten_bias_tables(na_rpb[0]), na_norm_g[0].reshape(1, NA_WIDTH), B, T)
    o_f, o_b = _hgrn(hg, hg_m, B, T)

    h1, xn, lpos, tw, cnt_tiles = _mix_route(
        yna.reshape(B * T, NA_WIDTH), o_f.reshape(B * T, HG_WIDTH), o_b.reshape(B * T, HG_WIDTH), hg, x2d,
        w_out[0].astype(BF16), hgrn_norm_g[0].reshape(1, HG_WIDTH), ffn_norm_g[0].reshape(1, D),
        router_w[0], router_b[0].reshape(1, N_EXPERTS))

    cnt, off, goff, tot, blk_e, nused, n_rows = _moe_layout(cnt_tiles[:, 0, :])
    xbuf = _dispatch(cnt, off, goff, tot, xn, lpos, n_rows)
    ybuf = _experts(blk_e, nused, xbuf, expert_w_gu[0].astype(BF16), expert_b_gu[0][:, None, :],
                    expert_w_down[0].astype(BF16), expert_b_down[0][:, None, :])
    out = _combine(cnt, off, goff, tot, ybuf, lpos, tw, h1, final_norm_g.reshape(1, D))
    return out.reshape(B, T, D)
```

```python
import functools

import numpy as np
import jax
import jax.numpy as jnp
from jax import lax
from jax.experimental import pallas as pl
from jax.experimental.pallas import tpu as pltpu

F32 = jnp.float32
BF16 = jnp.bfloat16

D_MODEL = 1024
N_META = 16
GRID_W = 64
NA_WIDTH = 512
NA_HEAD_DIM = 64
NA_HEADS = 8
NA_KH = 8
NA_KW = 16
HG_WIDTH = 512
HG_HEAD_DIM = 128
HG_HEADS = 4
HG_CHUNK = 64
IN_COLS = 3 * NA_WIDTH + 5 * HG_WIDTH
N_EXPERTS = 32
TOP_K = 4
D_FF = 1024
SWIGLU_LIMIT = 7.0
SWIGLU_ALPHA = 1.702
RMS_EPS = 1e-6
NEG_BIG = -1e30
LOG2_E = 1.4426950408889634

LANES = 128
VMEM_LIMIT_BYTES = 56 * 1024 * 1024

TOK_TILE = 256
EXPERT_BLOCK = 256
SUBLANES = 8
STAGE_ROWS = TOK_TILE * TOP_K + N_EXPERTS * SUBLANES
N_HG_LEVELS = 6


def _dot(a, b):
    return jnp.dot(a, b, preferred_element_type=F32)


def _dot_nt(a, b):
    return lax.dot_general(a, b, (((1,), (1,)), ((), ())), preferred_element_type=F32)


def _dot_tn(a, b):
    return lax.dot_general(a, b, (((0,), (0,)), ((), ())), preferred_element_type=F32)


def _split_bf16(x):
    hi = x.astype(BF16)
    lo = (x - hi.astype(F32)).astype(BF16)
    return hi, lo


def _pack_cols(cols):
    lane = lax.broadcasted_iota(jnp.int32, (cols[0].shape[0], len(cols)), 1)
    out = jnp.broadcast_to(cols[-1], lane.shape)
    for k in reversed(range(len(cols) - 1)):
        out = jnp.where(lane == k, cols[k], out)
    return out


def _params(*sem):
    return pltpu.CompilerParams(dimension_semantics=sem, vmem_limit_bytes=VMEM_LIMIT_BYTES)


def _inproj_body(x_ref, ng_ref, w_ref, lbl_ref, att_ref, hg_ref):
    x = x_ref[...]
    ms = jnp.mean(x * x, axis=-1, keepdims=True)
    n = (x * lax.rsqrt(ms + RMS_EPS) * ng_ref[...]).astype(BF16)

    def proj(lo, hi):
        return _dot(n, w_ref[:, lo:hi])

    pq = proj(0, NA_WIDTH)
    att_ref[:, 0:NA_WIDTH] = (pq * (NA_HEAD_DIM ** -0.5 * LOG2_E)).astype(BF16)
    att_ref[:, NA_WIDTH:3 * NA_WIDTH] = proj(NA_WIDTH, 3 * NA_WIDTH).astype(BF16)

    base = 3 * NA_WIDTH
    W = HG_WIDTH
    qh = proj(base, base + W)
    hg_ref[:, 0:W] = qh * jax.nn.sigmoid(qh)
    hg_ref[:, W:2 * W] = proj(base + W, base + 2 * W)
    lbl = lbl_ref[...]
    for d in range(2):
        a0 = lbl[2 * d:2 * d + 1, :]
        a1 = lbl[2 * d + 1:2 * d + 2, :]
        m = jnp.maximum(a0, a1)
        e0 = jnp.exp(a0 - m)
        e1 = jnp.exp(a1 - m)
        lb = e0 / (e0 + e1)
        raw = proj(base + (2 + d) * W, base + (3 + d) * W)
        f = lb + (1.0 - lb) * jax.nn.sigmoid(raw)
        hg_ref[:, (2 + d) * W:(3 + d) * W] = jnp.log(f)
    gt = proj(base + 4 * W, base + 5 * W)
    hg_ref[:, 4 * W:5 * W] = gt * jax.nn.sigmoid(gt)


def _inproj(x2d, norm_g, w_bf16, lb_logits4, tm):
    n = x2d.shape[0]
    return pl.pallas_call(
        _inproj_body,
        grid=(n // tm,),
        in_specs=[
            pl.BlockSpec((tm, D_MODEL), lambda i: (i, 0)),
            pl.BlockSpec((1, D_MODEL), lambda i: (0, 0)),
            pl.BlockSpec((D_MODEL, IN_COLS), lambda i: (0, 0)),
            pl.BlockSpec((4, HG_WIDTH), lambda i: (0, 0)),
        ],
        out_specs=[
            pl.BlockSpec((tm, 3 * NA_WIDTH), lambda i: (i, 0)),
            pl.BlockSpec((tm, 5 * HG_WIDTH), lambda i: (i, 0)),
        ],
        out_shape=[
            jax.ShapeDtypeStruct((n, 3 * NA_WIDTH), BF16),
            jax.ShapeDtypeStruct((n, 5 * HG_WIDTH), F32),
        ],
        compiler_params=_params("arbitrary"),
        name="inproj",
    )(x2d, norm_g, w_bf16, lb_logits4)


N_WIN_KEYS = NA_KH * GRID_W
META_PAD = LANES


NA_GROUP = 4
GROUP_LANES = NA_GROUP * NA_HEAD_DIM


NA_ROWS_PER_STEP = 4


def _natten_row(rr, r, q_ref, k_ref, v_ref, km_ref, vm_ref, bias_ref, *, rows):
    rs = jnp.clip(r - NA_KH // 2, 0, rows - NA_KH)
    d0 = rs - r + (NA_KH - 1)
    start = pl.multiple_of(rs * GRID_W, GRID_W)
    lane_head = lax.broadcasted_iota(jnp.int32, (GRID_W, GROUP_LANES), 1) // NA_HEAD_DIM
    meta_col = lax.broadcasted_iota(jnp.int32, (1, META_PAD), 1)
    meta_bias = jnp.where(meta_col < N_META, 0.0, NEG_BIG).astype(F32)
    folded = []
    for g in range(NA_HEADS // NA_GROUP):
        sl = slice(g * GROUP_LANES, (g + 1) * GROUP_LANES)
        q4 = q_ref[0, rr * GRID_W:(rr + 1) * GRID_W, sl]
        zero = jnp.zeros_like(q4)
        qm = jnp.concatenate([jnp.where(lane_head == h, q4, zero) for h in range(NA_GROUP)], axis=0)
        kw = k_ref[0, pl.ds(start, N_WIN_KEYS), sl]
        vw = v_ref[0, pl.ds(start, N_WIN_KEYS), sl]
        s = _dot_nt(qm, kw) + bias_ref[d0, g]
        sm = _dot_nt(qm, km_ref[:, sl]) + meta_bias
        m = jnp.maximum(jnp.max(s, axis=-1, keepdims=True), jnp.max(sm, axis=-1, keepdims=True))
        e = jnp.exp2(s - m)
        em = jnp.exp2(sm - m)
        den = jnp.sum(e, axis=-1, keepdims=True) + jnp.sum(em, axis=-1, keepdims=True)
        o = (_dot(e.astype(BF16), vw) + _dot(em.astype(BF16), vm_ref[:, sl])) * (1.0 / den)
        acc = jnp.where(lane_head == 0, o[0:GRID_W], 0.0)
        for h in range(1, NA_GROUP):
            acc = jnp.where(lane_head == h, o[h * GRID_W:(h + 1) * GRID_W], acc)
        folded.append(acc)
    return jnp.concatenate(folded, axis=-1)


def _natten_body(q_ref, k_ref, v_ref, km_ref, vm_ref, bias_ref, ng_ref, bd_ref, o_ref, *, rows):
    r0 = pl.program_id(1) * NA_ROWS_PER_STEP
    o2 = jnp.concatenate(
        [_natten_row(rr, r0 + rr, q_ref, k_ref, v_ref, km_ref, vm_ref, bias_ref, rows=rows)
         for rr in range(NA_ROWS_PER_STEP)], axis=0)
    sq_hi, sq_lo = _split_bf16(o2 * o2)
    ms = (_dot(sq_hi, bd_ref[...]) + _dot(sq_lo, bd_ref[...])) * (1.0 / NA_HEAD_DIM)
    o_ref[0] = (o2 * lax.rsqrt(ms + RMS_EPS) * ng_ref[...]).astype(o_ref.dtype)


def _natten_bias_tables(rpb):
    c = np.arange(GRID_W)[:, None]
    kc = np.arange(GRID_W)[None, :]
    cs = np.clip(c - NA_KW // 2, 0, GRID_W - NA_KW)
    valid = (kc >= cs) & (kc < cs + NA_KW)
    dc = np.clip(kc - c + (NA_KW - 1), 0, 2 * NA_KW - 2)
    t1 = jnp.where(valid[None, None], rpb.astype(F32)[:, :, dc] * LOG2_E, NEG_BIG)
    tabs = []
    for d0 in range(NA_KH):
        t = t1[:, d0:d0 + NA_KH]
        tabs.append(t.transpose(0, 2, 1, 3).reshape(NA_HEADS // NA_GROUP, NA_GROUP * GRID_W, N_WIN_KEYS))
    return jnp.stack(tabs)


def _natten(att, att_meta, bias_tabs, norm_g, batch, seq):
    rows = seq // GRID_W
    att3 = att.reshape(batch, seq, 3 * NA_WIDTH)
    head_of = np.arange(NA_WIDTH) // NA_HEAD_DIM
    same_head = jnp.asarray(head_of[:, None] == head_of[None, :], BF16)

    qrows = NA_ROWS_PER_STEP * GRID_W
    return pl.pallas_call(
        functools.partial(_natten_body, rows=rows),
        grid=(batch, rows // NA_ROWS_PER_STEP),
        in_specs=[
            pl.BlockSpec((1, qrows, NA_WIDTH), lambda b, r: (b, r, 0)),
            pl.BlockSpec((1, seq, NA_WIDTH), lambda b, r: (b, 0, 1)),
            pl.BlockSpec((1, seq, NA_WIDTH), lambda b, r: (b, 0, 2)),
            pl.BlockSpec((META_PAD, NA_WIDTH), lambda b, r: (0, 1)),
            pl.BlockSpec((META_PAD, NA_WIDTH), lambda b, r: (0, 2)),
            pl.BlockSpec(bias_tabs.shape, lambda b, r: (0, 0, 0, 0), pipeline_mode=pl.Buffered(1)),
            pl.BlockSpec((1, NA_WIDTH), lambda b, r: (0, 0)),
            pl.BlockSpec((NA_WIDTH, NA_WIDTH), lambda b, r: (0, 0)),
        ],
        out_specs=pl.BlockSpec((1, qrows, NA_WIDTH), lambda b, r: (b, r, 0)),
        out_shape=jax.ShapeDtypeStruct((batch, seq, NA_WIDTH), BF16),
        compiler_params=_params("arbitrary", "arbitrary"),
        name="natten",
    )(att3, att3, att3, att_meta, att_meta, bias_tabs, norm_g, same_head)


HG_CHUNKS_PER_STEP = 4
ROW_B, ROW_LV = 0, 1
N_EXP_BLOCKS = 1 + N_HG_LEVELS


def _hgrn_constants():
    C = HG_CHUNK
    t = np.arange(C)
    u = t[None, :]
    mats = [u <= t[:, None]]
    masks = [np.eye(C, dtype=bool)]
    for lv in range(N_HG_LEVELS):
        m = 1 << lv
        blk = t // (2 * m)
        upper = (t // m) % 2 == 1
        p = blk * 2 * m + m - 1
        q_rows = upper[:, None] & (u > p[:, None]) & (u <= t[:, None])
        k_rows = (~upper)[:, None] & (u > t[:, None]) & (u <= p[:, None])
        mats.append(q_rows | k_rows)
        masks.append((blk[:, None] == blk[None, :]) & upper[:, None] & (~upper)[None, :])
    fwd = np.concatenate(mats, axis=0).astype(np.float32)
    fwd_mask = np.stack(masks).astype(np.float32)
    blocks = fwd.reshape(N_EXP_BLOCKS, C, C)
    bwd = blocks[:, ::-1, ::-1].reshape(N_EXP_BLOCKS * C, C)
    bwd_mask = fwd_mask[:, ::-1, ::-1]
    both = np.stack([fwd, bwd])
    both = np.concatenate([both, both], axis=-1)
    return (jnp.asarray(both, BF16), jnp.asarray(np.stack([fwd_mask, bwd_mask]), F32))


def _hgrn_exponents(mat, g):
    g_hi, g_lo = _split_bf16(g)
    return _dot(mat, jnp.concatenate([g_hi, g_lo], axis=0))


def _hgrn_body(hf_ref, hb_ref, meta_ref, mat_ref, mask_ref, of_ref, ob_ref, stf_ref, stb_ref):
    C, W = HG_CHUNK, HG_WIDTH
    c = pl.program_id(1)

    @pl.when(c == 0)
    def _init():
        mb = meta_ref[...]
        v = mb[:, W:2 * W].astype(BF16)
        prefix = mat_ref[0, ROW_B * C:(ROW_B + 1) * C, :]
        for d, st_ref in ((0, stf_ref), (1, stb_ref)):
            g = mb[:, (2 + d) * W:(3 + d) * W]
            b = _hgrn_exponents(prefix, g)
            kb = ((1.0 - jnp.exp(g)) * jnp.exp(b[C - 1:C] - b)).astype(BF16)
            for h in range(HG_HEADS):
                sl = slice(h * HG_HEAD_DIM, (h + 1) * HG_HEAD_DIM)
                st_ref[h] = _dot_tn(v[:, sl], kb[:, sl])

    n = HG_CHUNKS_PER_STEP
    dir_refs = ((hf_ref, 2 * W, stf_ref, C - 1, of_ref), (hb_ref, 3 * W, stb_ref, 0, ob_ref))
    tasks = [(0, j * C) for j in range(n)] + [(1, (n - 1 - j) * C) for j in range(n)]
    heads = [slice(h * HG_HEAD_DIM, (h + 1) * HG_HEAD_DIM) for h in range(HG_HEADS)]
    qs, ks, vs, exs, ex_ends = {}, {}, {}, {}, {}
    for d, r0 in tasks:
        ref, g_col, total_row = dir_refs[d][0], dir_refs[d][1], dir_refs[d][3]
        g = ref[0, r0:r0 + C, g_col:g_col + W]
        sums = _hgrn_exponents(mat_ref[d], g)
        exs[d, r0] = jnp.exp(sums)
        ex_ends[d, r0] = jnp.exp(sums[total_row:total_row + 1] - sums[0:C])
        qs[d, r0] = ref[0, r0:r0 + C, 0:W]
        ks[d, r0] = 1.0 - jnp.exp(g)
        vs[d, r0] = ref[0, r0:r0 + C, W:2 * W].astype(BF16)

    def ex_rows(task, i, sl):
        return exs[task][i * C:(i + 1) * C, sl]

    prods = {}
    for task in tasks:
        for h, sl in enumerate(heads):
            qh, kh = qs[task][:, sl], ks[task][:, sl]
            ps = [_dot_nt(qh.astype(BF16), kh.astype(BF16))]
            for lv in range(N_HG_LEVELS):
                scale = ex_rows(task, ROW_LV + lv, sl)
                ps.append(_dot_nt((qh * scale).astype(BF16), (kh * scale).astype(BF16)))
            prods[task, h] = ps
    attn, qbs, kbs = {}, {}, {}
    for task in tasks:
        d = task[0]
        for h, sl in enumerate(heads):
            a = prods[task, h][0] * mask_ref[d, 0]
            for lv in range(N_HG_LEVELS):
                a = a + prods[task, h][1 + lv] * mask_ref[d, 1 + lv]
            attn[task, h] = _dot(a.astype(BF16), vs[task][:, sl])
            qbs[task, h] = (qs[task][:, sl] * ex_rows(task, ROW_B, sl)).astype(BF16)
            kbs[task, h] = _dot_tn(vs[task][:, sl], (ks[task][:, sl] * ex_ends[task][:, sl]).astype(BF16))
    states = {(d, h): dir_refs[d][2][h] for d in range(2) for h in range(HG_HEADS)}
    for task in tasks:
        d, r0 = task
        total_row, o_ref = dir_refs[d][3], dir_refs[d][4]
        outs = []
        for h, sl in enumerate(heads):
            st = states[d, h]
            outs.append(attn[task, h] + _dot_nt(qbs[task, h], st.astype(BF16)))
            decay = exs[task][ROW_B * C + total_row:ROW_B * C + total_row + 1, sl]
            states[d, h] = decay * st + kbs[task, h]
        o_ref[0, r0:r0 + C, :] = jnp.concatenate(outs, axis=-1)
    for d in range(2):
        for h in range(HG_HEADS):
            dir_refs[d][2][h] = states[d, h]


def _hgrn(hg, hg_meta_pad, batch, seq):
    nc = seq // (HG_CHUNK * HG_CHUNKS_PER_STEP)
    hg3 = hg.reshape(batch, seq, 5 * HG_WIDTH)
    mats, masks = _hgrn_constants()
    blk = (1, HG_CHUNK * HG_CHUNKS_PER_STEP, 5 * HG_WIDTH)
    oblk = (1, HG_CHUNK * HG_CHUNKS_PER_STEP, HG_WIDTH)
    return pl.pallas_call(
        _hgrn_body,
        grid=(batch, nc),
        in_specs=[
            pl.BlockSpec(blk, lambda b, c: (b, c, 0)),
            pl.BlockSpec(blk, lambda b, c: (b, nc - 1 - c, 0)),
            pl.BlockSpec((HG_CHUNK, 5 * HG_WIDTH), lambda b, c: (0, 0)),
            pl.BlockSpec(mats.shape, lambda b, c: (0, 0, 0)),
            pl.BlockSpec(masks.shape, lambda b, c: (0, 0, 0, 0)),
        ],
        out_specs=[
            pl.BlockSpec(oblk, lambda b, c: (b, c, 0)),
            pl.BlockSpec(oblk, lambda b, c: (b, nc - 1 - c, 0)),
        ],
        out_shape=[jax.ShapeDtypeStruct((batch, seq, HG_WIDTH), F32)] * 2,
        scratch_shapes=[pltpu.VMEM((HG_HEADS, HG_HEAD_DIM, HG_HEAD_DIM), F32)] * 2,
        compiler_params=_params("arbitrary", "arbitrary"),
        name="hgrn",
    )(hg3, hg3, hg_meta_pad, mats, masks)


def _mix_route_body(yna_ref, of_ref, ob_ref, gate_ref, x_ref, wout_ref, hgn_ref, ffg_ref,
                    rwh_ref, rwl_ref, rb_ref, tri_ref, upe_ref,
                    h1_ref, xn_ref, lpos_ref, tw_ref, cnt_ref):
    o = of_ref[...] + ob_ref[...]
    parts = []
    for h in range(HG_HEADS):
        seg = o[:, h * HG_HEAD_DIM:(h + 1) * HG_HEAD_DIM]
        ms = jnp.mean(seg * seg, axis=-1, keepdims=True)
        parts.append(seg * lax.rsqrt(ms + RMS_EPS))
    yhg = jnp.concatenate(parts, axis=-1) * hgn_ref[...] * gate_ref[...]
    mix = _dot(yna_ref[...], wout_ref[0:NA_WIDTH, :]) + _dot(yhg.astype(BF16), wout_ref[NA_WIDTH:, :])
    h1 = x_ref[...] + mix
    h1_ref[...] = h1
    ms = jnp.mean(h1 * h1, axis=-1, keepdims=True)
    xn = h1 * lax.rsqrt(ms + RMS_EPS) * ffg_ref[...]
    x_hi, x_lo = _split_bf16(xn)
    xn_ref[...] = x_hi

    logits = (_dot(x_hi, rwh_ref[...]) + _dot(x_lo, rwh_ref[...]) + _dot(x_hi, rwl_ref[...])) + rb_ref[...]

    T, E = TOK_TILE, N_EXPERTS
    lane = lax.broadcasted_iota(jnp.int32, (T, E), 1).astype(F32)
    cur = logits
    sels, vals = [], []
    for _ in range(TOP_K):
        m = jnp.max(cur, axis=-1, keepdims=True)
        first = jnp.min(jnp.where(cur == m, lane, float(E)), axis=-1, keepdims=True)
        sel = lane == first
        sels.append(sel)
        vals.append(m)
        cur = jnp.where(sel, -jnp.inf, cur)
    es = [jnp.exp(vk - vals[0]) for vk in vals]
    den = es[0] + es[1] + es[2] + es[3]
    tw_ref[...] = _pack_cols([e / den for e in es])

    onehot = jnp.zeros((T, E), F32)
    for sel in sels:
        onehot = onehot + jnp.where(sel, 1.0, 0.0)
    lrank = _dot(tri_ref[...], onehot.astype(BF16))
    cnt = lrank[T - 1:T, :] + onehot[T - 1:T, :]
    cnt = jnp.floor((cnt + (SUBLANES - 1)) * (1.0 / SUBLANES)) * SUBLANES
    off = _dot(jnp.broadcast_to(cnt, (8, E)).astype(BF16), upe_ref[...])[0:1, :]
    base = off + lrank
    lpos_ref[...] = _pack_cols(
        [jnp.sum(jnp.where(sel, base, 0.0), axis=-1, keepdims=True) for sel in sels]).astype(jnp.int32)
    cnt_ref[0] = cnt.astype(jnp.int32)


def _mix_route(yna, o_f, o_b, hg, x2d, wout_bf16, hgn, ffg, router_w, router_b):
    n = x2d.shape[0]
    T = TOK_TILE
    nt = n // T
    rwh = router_w.astype(BF16)
    rwl = (router_w - rwh.astype(F32)).astype(BF16)
    tri = jnp.asarray(np.tril(np.ones((T, T), np.float32), -1), BF16)
    upe = jnp.asarray(np.triu(np.ones((N_EXPERTS, N_EXPERTS), np.float32), 1), BF16)
    row = lambda i: (i, 0)
    const = lambda i: (0, 0)
    return pl.pallas_call(
        _mix_route_body,
        grid=(nt,),
        in_specs=[
            pl.BlockSpec((T, NA_WIDTH), row),
            pl.BlockSpec((T, HG_WIDTH), row),
            pl.BlockSpec((T, HG_WIDTH), row),
            pl.BlockSpec((T, HG_WIDTH), lambda i: (i, 4)),
            pl.BlockSpec((T, D_MODEL), row),
            pl.BlockSpec((D_MODEL, D_MODEL), const),
            pl.BlockSpec((1, HG_WIDTH), const),
            pl.BlockSpec((1, D_MODEL), const),
            pl.BlockSpec((D_MODEL, N_EXPERTS), const),
            pl.BlockSpec((D_MODEL, N_EXPERTS), const),
            pl.BlockSpec((1, N_EXPERTS), const),
            pl.BlockSpec((T, T), const),
            pl.BlockSpec((N_EXPERTS, N_EXPERTS), const),
        ],
        out_specs=[
            pl.BlockSpec((T, D_MODEL), row),
            pl.BlockSpec((T, D_MODEL), row),
            pl.BlockSpec((T, TOP_K), row),
            pl.BlockSpec((T, TOP_K), row),
            pl.BlockSpec((1, 1, N_EXPERTS), lambda i: (i, 0, 0)),
        ],
        out_shape=[
            jax.ShapeDtypeStruct((n, D_MODEL), F32),
            jax.ShapeDtypeStruct((n, D_MODEL), BF16),
            jax.ShapeDtypeStruct((n, TOP_K), jnp.int32),
            jax.ShapeDtypeStruct((n, TOP_K), F32),
            jax.ShapeDtypeStruct((nt, 1, N_EXPERTS), jnp.int32),
        ],
        compiler_params=_params("arbitrary"),
        name="mix_route",
    )(yna, o_f, o_b, hg, x2d, wout_bf16, hgn, ffg, rwh, rwl, router_b, tri, upe)


TILES_PER_STEP = 2


def _wait_rows(make_copy, rows):
    make_copy(0, 0, pl.multiple_of(rows, SUBLANES)).wait()


def _copy_runs(i, cnt_ref, off_ref, goff_ref, make_copy):
    def per_expert(e, carry):
        n = pl.multiple_of(cnt_ref[i * N_EXPERTS + e], SUBLANES)
        lo = pl.multiple_of(off_ref[i * N_EXPERTS + e], SUBLANES)
        go = pl.multiple_of(goff_ref[i * N_EXPERTS + e], SUBLANES)

        @pl.when(n > 0)
        def _start():
            make_copy(lo, go, n).start()

        return carry

    lax.fori_loop(0, N_EXPERTS, per_expert, 0)


def _slot_matrix(lpos, weights=None):
    lane = lax.broadcasted_iota(jnp.int32, (TOK_TILE, STAGE_ROWS), 1)
    acc = jnp.zeros((TOK_TILE, STAGE_ROWS), F32)
    for k in range(TOP_K):
        hit = lpos[:, k:k + 1] == lane
        val = 1.0 if weights is None else weights[:, k:k + 1]
        acc = jnp.where(hit, val, acc)
    return acc


def _dispatch_body(cnt_ref, off_ref, goff_ref, tot_ref, zoff_ref, zlen_ref, blkrange_ref,
                   xn_ref, lpos_ref, xbuf_ref, stage_ref, zero_ref, sems, zsem):
    i = pl.program_id(0)
    last = pl.num_programs(0) - 1

    def zero_fill(act):
        def tail(e, carry):
            n = pl.multiple_of(zlen_ref[e], SUBLANES)

            @pl.when(n > 0)
            def _():
                go = pl.multiple_of(zoff_ref[e], SUBLANES)
                act(pltpu.make_async_copy(zero_ref.at[pl.ds(0, n)], xbuf_ref.at[pl.ds(go, n)], zsem))

            return carry

        def unused(j, carry):
            go = pl.multiple_of(j * EXPERT_BLOCK, EXPERT_BLOCK)
            act(pltpu.make_async_copy(zero_ref, xbuf_ref.at[pl.ds(go, EXPERT_BLOCK)], zsem))
            return carry

        lax.fori_loop(0, N_EXPERTS, tail, 0)
        lax.fori_loop(blkrange_ref[0], blkrange_ref[1], unused, 0)

    @pl.when(i == 0)
    def _zero_start():
        zero_ref[...] = jnp.zeros_like(zero_ref)
        zero_fill(lambda c: c.start())

    for s in range(TILES_PER_STEP):
        t = i * TILES_PER_STEP + s

        def make_copy(lo, go, size, s=s):
            return pltpu.make_async_copy(stage_ref.at[s, pl.ds(lo, size)], xbuf_ref.at[pl.ds(go, size)], sems.at[s])

        @pl.when(i > 0)
        def _drain():
            _wait_rows(make_copy, tot_ref[t - TILES_PER_STEP])

        rows = slice(s * TOK_TILE, (s + 1) * TOK_TILE)
        pt = _slot_matrix(lpos_ref[rows, :]).astype(BF16)
        stage_ref[s] = _dot_tn(pt, xn_ref[rows, :])
        _copy_runs(t, cnt_ref, off_ref, goff_ref, make_copy)

        @pl.when(i == last)
        def _finish():
            _wait_rows(make_copy, tot_ref[t])

    @pl.when(i == last)
    def _zero_finish():
        zero_fill(lambda c: c.wait())


def _dispatch(cnt, off, goff, tot, zoff, zlen, blkrange, xn, lpos, n_rows):
    n = xn.shape[0]
    rows = TILES_PER_STEP * TOK_TILE
    return pl.pallas_call(
        _dispatch_body,
        grid_spec=pltpu.PrefetchScalarGridSpec(
            num_scalar_prefetch=7,
            grid=(n // rows,),
            in_specs=[
                pl.BlockSpec((rows, D_MODEL), lambda i, *_: (i, 0)),
                pl.BlockSpec((rows, TOP_K), lambda i, *_: (i, 0)),
            ],
            out_specs=pl.BlockSpec(memory_space=pl.ANY),
            scratch_shapes=[
                pltpu.VMEM((TILES_PER_STEP, STAGE_ROWS, D_MODEL), F32),
                pltpu.VMEM((EXPERT_BLOCK, D_MODEL), F32),
                pltpu.SemaphoreType.DMA((TILES_PER_STEP,)),
                pltpu.SemaphoreType.DMA(()),
            ],
        ),
        out_shape=jax.ShapeDtypeStruct((n_rows, D_MODEL), F32),
        compiler_params=_params("arbitrary"),
        name="moe_dispatch",
    )(cnt, off, goff, tot, zoff, zlen, blkrange, xn, lpos)


CAST_ROWS = 128


def _experts_body(blk_e_ref, first_ref, slot_ref, next_e_ref, nused_ref,
                  x_ref, bgu_ref, bd_ref, wgu_hbm, wd_hbm, y_ref,
                  wgu_f32, wd_f32, wgu_bf, wd_bf, sem_gu, sem_d):
    j = pl.program_id(0)

    def weight_copies(e, slot):
        return (pltpu.make_async_copy(wgu_hbm.at[e], wgu_f32.at[slot], sem_gu.at[slot]),
                pltpu.make_async_copy(wd_hbm.at[e], wd_f32.at[slot], sem_d.at[slot]))

    @pl.when(j < nused_ref[0])
    def _():
        e = blk_e_ref[j]
        slot = slot_ref[j]

        @pl.when(first_ref[j] == 1)
        def _new_expert():
            @pl.when(j == 0)
            def _():
                for c in weight_copies(e, slot):
                    c.start()

            for c in weight_copies(e, slot):
                c.wait()
            nxt = next_e_ref[j]

            @pl.when(nxt >= 0)
            def _():
                for c in weight_copies(nxt, 1 - slot):
                    c.start()

            def cast(src, dst):
                def rows(r, carry):
                    sl = pl.ds(pl.multiple_of(r * CAST_ROWS, CAST_ROWS), CAST_ROWS)
                    dst[sl, :] = src[slot, sl, :].astype(BF16)
                    return carry
                lax.fori_loop(0, src.shape[1] // CAST_ROWS, rows, 0)

            cast(wgu_f32, wgu_bf)
            cast(wd_f32, wd_bf)

        x = x_ref[...].astype(BF16)
        gu = _dot(x, wgu_bf[...]) + bgu_ref[0]
        gate = jnp.minimum(gu[:, :D_FF], SWIGLU_LIMIT)
        up = jnp.clip(gu[:, D_FF:], -SWIGLU_LIMIT, SWIGLU_LIMIT)
        act = (up + 1.0) * gate * jax.nn.sigmoid(SWIGLU_ALPHA * gate)
        y_ref[...] = _dot(act.astype(BF16), wd_bf[...]) + bd_ref[0]


def _experts(blk_e, first, slot, next_e, nused, xbuf, wgu, bgu, wd, bd):
    n_rows = xbuf.shape[0]
    nblk = n_rows // EXPERT_BLOCK

    def rowblk(j, be, fi, sl, ne, nu):
        return (jnp.minimum(j, nu[0] - 1), 0)

    def expert(j, be, fi, sl, ne, nu):
        return (be[jnp.minimum(j, nu[0] - 1)], 0, 0)

    return pl.pallas_call(
        _experts_body,
        grid_spec=pltpu.PrefetchScalarGridSpec(
            num_scalar_prefetch=5,
            grid=(nblk,),
            in_specs=[
                pl.BlockSpec((EXPERT_BLOCK, D_MODEL), rowblk),
                pl.BlockSpec((1, 1, 2 * D_FF), expert),
                pl.BlockSpec((1, 1, D_MODEL), expert),
                pl.BlockSpec(memory_space=pl.ANY),
                pl.BlockSpec(memory_space=pl.ANY),
            ],
            out_specs=pl.BlockSpec((EXPERT_BLOCK, D_MODEL), rowblk),
            scratch_shapes=[
                pltpu.VMEM((2, D_MODEL, 2 * D_FF), F32),
                pltpu.VMEM((2, D_FF, D_MODEL), F32),
                pltpu.VMEM((D_MODEL, 2 * D_FF), BF16),
                pltpu.VMEM((D_FF, D_MODEL), BF16),
                pltpu.SemaphoreType.DMA((2,)),
                pltpu.SemaphoreType.DMA((2,)),
            ],
        ),
        out_shape=jax.ShapeDtypeStruct((n_rows, D_MODEL), F32),
        input_output_aliases={5: 0},
        compiler_params=_params("arbitrary"),
        name="moe_experts",
    )(blk_e, first, slot, next_e, nused, xbuf, bgu, bd, wgu, wd)


def _combine_body(cnt_ref, off_ref, goff_ref, tot_ref, ybuf_ref, lpos_ref, tw_ref, h1_ref, fg_ref, out_ref,
                  stage_ref, sems):
    i = pl.program_id(0)
    n_tiles = pl.num_programs(0) * TILES_PER_STEP

    def copier(s):
        def make_copy(lo, go, size):
            return pltpu.make_async_copy(ybuf_ref.at[pl.ds(go, size)], stage_ref.at[s, pl.ds(lo, size)], sems.at[s])
        return make_copy

    @pl.when(i == 0)
    def _init():
        stage_ref[...] = jnp.zeros_like(stage_ref)
        _copy_runs(0, cnt_ref, off_ref, goff_ref, copier(0))

    for s in range(TILES_PER_STEP):
        t = i * TILES_PER_STEP + s

        @pl.when(t + 1 < n_tiles)
        def _prefetch():
            _copy_runs(t + 1, cnt_ref, off_ref, goff_ref, copier((s + 1) % TILES_PER_STEP))

        rows = slice(s * TOK_TILE, (s + 1) * TOK_TILE)
        w_hi, w_lo = _split_bf16(_slot_matrix(lpos_ref[rows, :], tw_ref[rows, :]))
        _wait_rows(copier(s), tot_ref[t])
        ys = stage_ref[s].astype(BF16)
        h2 = h1_ref[rows, :] + (_dot(w_hi, ys) + _dot(w_lo, ys))
        ms = jnp.mean(h2 * h2, axis=-1, keepdims=True)
        out_ref[rows, :] = h2 * lax.rsqrt(ms + RMS_EPS) * fg_ref[...]


def _combine(cnt, off, goff, tot, ybuf, lpos, tw, h1, final_g):
    n = h1.shape[0]
    rows = TILES_PER_STEP * TOK_TILE
    row = lambda i, *_: (i, 0)
    return pl.pallas_call(
        _combine_body,
        grid_spec=pltpu.PrefetchScalarGridSpec(
            num_scalar_prefetch=4,
            grid=(n // rows,),
            in_specs=[
                pl.BlockSpec(memory_space=pl.ANY),
                pl.BlockSpec((rows, TOP_K), row),
                pl.BlockSpec((rows, TOP_K), row),
                pl.BlockSpec((rows, D_MODEL), row),
                pl.BlockSpec((1, D_MODEL), lambda i, *_: (0, 0)),
            ],
            out_specs=pl.BlockSpec((rows, D_MODEL), row),
            scratch_shapes=[pltpu.VMEM((TILES_PER_STEP, STAGE_ROWS, D_MODEL), F32),
                            pltpu.SemaphoreType.DMA((TILES_PER_STEP,))],
        ),
        out_shape=jax.ShapeDtypeStruct((n, D_MODEL), F32),
        compiler_params=_params("arbitrary"),
        name="moe_combine",
    )(cnt, off, goff, tot, ybuf, lpos, tw, h1, final_g)


def _moe_layout(cnt_tiles):
    nt = cnt_tiles.shape[0]
    total = jnp.sum(cnt_tiles, axis=0)
    nblk_e = (total + EXPERT_BLOCK - 1) // EXPERT_BLOCK
    blk_end = jnp.cumsum(nblk_e)
    pstart = (blk_end - nblk_e) * EXPERT_BLOCK
    prefix = jnp.cumsum(cnt_tiles, axis=0) - cnt_tiles
    goff = pstart[None, :] + prefix
    off = jnp.cumsum(cnt_tiles, axis=1) - cnt_tiles
    tot = jnp.sum(cnt_tiles, axis=1)
    max_rows = nt * (TOK_TILE * TOP_K + N_EXPERTS * (SUBLANES - 1)) + N_EXPERTS * (EXPERT_BLOCK - 1)
    max_blocks = (max_rows + EXPERT_BLOCK - 1) // EXPERT_BLOCK
    blk_id = jnp.arange(max_blocks, dtype=blk_end.dtype)
    blk_e = jnp.minimum(jnp.sum(blk_end[None, :] <= blk_id[:, None], axis=1), N_EXPERTS - 1).astype(jnp.int32)
    nused = blk_end[-1:].astype(jnp.int32)
    flat = lambda a: a.reshape(-1).astype(jnp.int32)
    zoff = pstart + total
    zlen = nblk_e * EXPERT_BLOCK - total
    blkrange = jnp.stack([nused[0], jnp.int32(max_blocks)])
    used = blk_id < nused[0]
    first = used & jnp.concatenate([jnp.ones((1,), bool), blk_e[1:] != blk_e[:-1]])
    slot = (jnp.cumsum(first.astype(jnp.int32)) - 1) % 2
    eid = jnp.arange(N_EXPERTS)
    later = (eid[None, :] > eid[:, None]) & (nblk_e > 0)[None, :]
    next_of = jnp.min(jnp.where(later, eid[None, :], N_EXPERTS), axis=1)
    next_of = jnp.where(next_of < N_EXPERTS, next_of, -1)
    experts_plan = (blk_e, first.astype(jnp.int32), slot.astype(jnp.int32), next_of[blk_e].astype(jnp.int32), nused)
    dispatch_plan = (flat(cnt_tiles), flat(off), flat(goff), flat(tot))
    zero_plan = (flat(zoff), flat(zlen), blkrange.astype(jnp.int32))
    return dispatch_plan, zero_plan, experts_plan, max_blocks * EXPERT_BLOCK


def kernel(x, meta_tokens, attn_norm_g, w_in, na_rpb, na_norm_g, hgrn_lb_logits, hgrn_norm_g, w_out, ffn_norm_g,
           router_w, router_b, expert_w_gu, expert_b_gu, expert_w_down, expert_b_down, final_norm_g):
    B, T, D = x.shape
    x2d = x.reshape(B * T, D)
    w_in_b = w_in[0].astype(BF16)
    ng = attn_norm_g[0].reshape(1, D)
    lbl = hgrn_lb_logits[:, :, :].reshape(4, HG_WIDTH)

    att, hg = _inproj(x2d, ng, w_in_b, lbl, 512)
    att_m, hg_m = _inproj(meta_tokens.astype(F32), ng, w_in_b, lbl, N_META)
    att_m = jnp.pad(att_m, ((0, META_PAD - N_META), (0, 0)))
    hg_m = jnp.pad(hg_m, ((0, HG_CHUNK - N_META), (0, 0)))

    yna = _natten(att, att_m, _natten_bias_tables(na_rpb[0]), na_norm_g[0].reshape(1, NA_WIDTH), B, T)
    o_f, o_b = _hgrn(hg, hg_m, B, T)

    h1, xn, lpos, tw, cnt_tiles = _mix_route(
        yna.reshape(B * T, NA_WIDTH), o_f.reshape(B * T, HG_WIDTH), o_b.reshape(B * T, HG_WIDTH), hg, x2d,
        w_out[0].astype(BF16), hgrn_norm_g[0].reshape(1, HG_WIDTH), ffn_norm_g[0].reshape(1, D),
        router_w[0], router_b[0].reshape(1, N_EXPERTS))

    dispatch_plan, zero_plan, experts_plan, n_rows = _moe_layout(cnt_tiles[:, 0, :])
    xbuf = _dispatch(*dispatch_plan, *zero_plan, xn, lpos, n_rows)
    ybuf = _experts(*experts_plan, xbuf, expert_w_gu[0], expert_b_gu[0][:, None, :],
                    expert_w_down[0], expert_b_down[0][:, None, :])
    out = _combine(*dispatch_plan, ybuf, lpos, tw, h1, final_norm_g.reshape(1, D))
    return out.reshape(B, T, D)
```

```python
import functools

import numpy as np
import jax
import jax.numpy as jnp
from jax import lax
from jax.experimental import pallas as pl
from jax.experimental.pallas import tpu as pltpu

F32 = jnp.float32
BF16 = jnp.bfloat16

D_MODEL = 1024
N_META = 16
GRID_W = 64
NA_WIDTH = 512
NA_HEAD_DIM = 64
NA_HEADS = 8
NA_KH = 8
NA_KW = 16
HG_WIDTH = 512
HG_HEAD_DIM = 128
HG_HEADS = 4
HG_CHUNK = 64
IN_COLS = 3 * NA_WIDTH + 5 * HG_WIDTH
N_EXPERTS = 32
TOP_K = 4
D_FF = 1024
SWIGLU_LIMIT = 7.0
SWIGLU_ALPHA = 1.702
RMS_EPS = 1e-6
NEG_BIG = -1e30
LOG2_E = 1.4426950408889634

LANES = 128
VMEM_LIMIT_BYTES = 56 * 1024 * 1024

TOK_TILE = 256
EXPERT_BLOCK = 256
SUBLANES = 8
STAGE_ROWS = TOK_TILE * TOP_K + N_EXPERTS * SUBLANES
N_HG_LEVELS = 6


def _dot(a, b):
    return jnp.dot(a, b, preferred_element_type=F32)


def _dot_nt(a, b):
    return lax.dot_general(a, b, (((1,), (1,)), ((), ())), preferred_element_type=F32)


def _dot_tn(a, b):
    return lax.dot_general(a, b, (((0,), (0,)), ((), ())), preferred_element_type=F32)


def _split_bf16(x):
    hi = x.astype(BF16)
    lo = (x - hi.astype(F32)).astype(BF16)
    return hi, lo


def _pack_rows(rows, n_rows):
    idx = lax.broadcasted_iota(jnp.int32, (n_rows, rows[0].shape[1]), 0)
    out = jnp.zeros(idx.shape, rows[0].dtype)
    for k, r in enumerate(rows):
        out = jnp.where(idx == k, r, out)
    return out


def _params(*sem):
    return pltpu.CompilerParams(dimension_semantics=sem, vmem_limit_bytes=VMEM_LIMIT_BYTES)


def _inproj_body(x_ref, ng_ref, w_ref, lbl_ref, att_ref, hg_ref):
    x = x_ref[...]
    ms = jnp.mean(x * x, axis=-1, keepdims=True)
    n = (x * lax.rsqrt(ms + RMS_EPS) * ng_ref[...]).astype(BF16)

    def proj(lo, hi):
        return _dot(n, w_ref[:, lo:hi])

    pq = proj(0, NA_WIDTH)
    att_ref[:, 0:NA_WIDTH] = (pq * (NA_HEAD_DIM ** -0.5 * LOG2_E)).astype(BF16)
    att_ref[:, NA_WIDTH:3 * NA_WIDTH] = proj(NA_WIDTH, 3 * NA_WIDTH).astype(BF16)

    base = 3 * NA_WIDTH
    W = HG_WIDTH
    qh = proj(base, base + W)
    hg_ref[:, 0:W] = qh * jax.nn.sigmoid(qh)
    hg_ref[:, W:2 * W] = proj(base + W, base + 2 * W)
    lbl = lbl_ref[...]
    for d in range(2):
        a0 = lbl[2 * d:2 * d + 1, :]
        a1 = lbl[2 * d + 1:2 * d + 2, :]
        m = jnp.maximum(a0, a1)
        e0 = jnp.exp(a0 - m)
        e1 = jnp.exp(a1 - m)
        lb = e0 / (e0 + e1)
        raw = proj(base + (2 + d) * W, base + (3 + d) * W)
        f = lb + (1.0 - lb) * jax.nn.sigmoid(raw)
        hg_ref[:, (2 + d) * W:(3 + d) * W] = jnp.log(f)
    gt = proj(base + 4 * W, base + 5 * W)
    hg_ref[:, 4 * W:5 * W] = gt * jax.nn.sigmoid(gt)


def _inproj(x2d, norm_g, w_bf16, lb_logits4, tm):
    n = x2d.shape[0]
    return pl.pallas_call(
        _inproj_body,
        grid=(n // tm,),
        in_specs=[
            pl.BlockSpec((tm, D_MODEL), lambda i: (i, 0)),
            pl.BlockSpec((1, D_MODEL), lambda i: (0, 0)),
            pl.BlockSpec((D_MODEL, IN_COLS), lambda i: (0, 0)),
            pl.BlockSpec((4, HG_WIDTH), lambda i: (0, 0)),
        ],
        out_specs=[
            pl.BlockSpec((tm, 3 * NA_WIDTH), lambda i: (i, 0)),
            pl.BlockSpec((tm, 5 * HG_WIDTH), lambda i: (i, 0)),
        ],
        out_shape=[
            jax.ShapeDtypeStruct((n, 3 * NA_WIDTH), BF16),
            jax.ShapeDtypeStruct((n, 5 * HG_WIDTH), F32),
        ],
        compiler_params=_params("arbitrary"),
        name="inproj",
    )(x2d, norm_g, w_bf16, lb_logits4)


N_WIN_KEYS = NA_KH * GRID_W
META_PAD = LANES


NA_GROUP = 4
GROUP_LANES = NA_GROUP * NA_HEAD_DIM


NA_ROWS_PER_STEP = 4


def _natten_row(rr, r, q_ref, k_ref, v_ref, km_ref, vm_ref, bias_ref, *, rows):
    rs = jnp.clip(r - NA_KH // 2, 0, rows - NA_KH)
    d0 = rs - r + (NA_KH - 1)
    start = pl.multiple_of(rs * GRID_W, GRID_W)
    lane_head = lax.broadcasted_iota(jnp.int32, (GRID_W, GROUP_LANES), 1) // NA_HEAD_DIM
    meta_col = lax.broadcasted_iota(jnp.int32, (1, META_PAD), 1)
    meta_bias = jnp.where(meta_col < N_META, 0.0, NEG_BIG).astype(F32)
    folded = []
    for g in range(NA_HEADS // NA_GROUP):
        sl = slice(g * GROUP_LANES, (g + 1) * GROUP_LANES)
        q4 = q_ref[0, rr * GRID_W:(rr + 1) * GRID_W, sl]
        zero = jnp.zeros_like(q4)
        qm = jnp.concatenate([jnp.where(lane_head == h, q4, zero) for h in range(NA_GROUP)], axis=0)
        kw = k_ref[0, pl.ds(start, N_WIN_KEYS), sl]
        vw = v_ref[0, pl.ds(start, N_WIN_KEYS), sl]
        s = _dot_nt(qm, kw) + bias_ref[d0, g]
        sm = _dot_nt(qm, km_ref[:, sl]) + meta_bias
        m = jnp.maximum(jnp.max(s, axis=-1, keepdims=True), jnp.max(sm, axis=-1, keepdims=True))
        e = jnp.exp2(s - m)
        em = jnp.exp2(sm - m)
        den = jnp.sum(e, axis=-1, keepdims=True) + jnp.sum(em, axis=-1, keepdims=True)
        o = (_dot(e.astype(BF16), vw) + _dot(em.astype(BF16), vm_ref[:, sl])) * (1.0 / den)
        acc = jnp.where(lane_head == 0, o[0:GRID_W], 0.0)
        for h in range(1, NA_GROUP):
            acc = jnp.where(lane_head == h, o[h * GRID_W:(h + 1) * GRID_W], acc)
        folded.append(acc)
    return jnp.concatenate(folded, axis=-1)


def _natten_body(q_ref, k_ref, v_ref, km_ref, vm_ref, bias_ref, ng_ref, bd_ref, o_ref, *, rows):
    r0 = pl.program_id(1) * NA_ROWS_PER_STEP
    o2 = jnp.concatenate(
        [_natten_row(rr, r0 + rr, q_ref, k_ref, v_ref, km_ref, vm_ref, bias_ref, rows=rows)
         for rr in range(NA_ROWS_PER_STEP)], axis=0)
    sq_hi, sq_lo = _split_bf16(o2 * o2)
    ms = (_dot(sq_hi, bd_ref[...]) + _dot(sq_lo, bd_ref[...])) * (1.0 / NA_HEAD_DIM)
    o_ref[0] = (o2 * lax.rsqrt(ms + RMS_EPS) * ng_ref[...]).astype(o_ref.dtype)


def _natten_bias_tables(rpb):
    c = np.arange(GRID_W)[:, None]
    kc = np.arange(GRID_W)[None, :]
    cs = np.clip(c - NA_KW // 2, 0, GRID_W - NA_KW)
    valid = (kc >= cs) & (kc < cs + NA_KW)
    dc = np.clip(kc - c + (NA_KW - 1), 0, 2 * NA_KW - 2)
    t1 = jnp.where(valid[None, None], rpb.astype(F32)[:, :, dc] * LOG2_E, NEG_BIG)
    tabs = []
    for d0 in range(NA_KH):
        t = t1[:, d0:d0 + NA_KH]
        tabs.append(t.transpose(0, 2, 1, 3).reshape(NA_HEADS // NA_GROUP, NA_GROUP * GRID_W, N_WIN_KEYS))
    return jnp.stack(tabs)


def _natten(att, att_meta, bias_tabs, norm_g, batch, seq):
    rows = seq // GRID_W
    att3 = att.reshape(batch, seq, 3 * NA_WIDTH)
    head_of = np.arange(NA_WIDTH) // NA_HEAD_DIM
    same_head = jnp.asarray(head_of[:, None] == head_of[None, :], BF16)

    qrows = NA_ROWS_PER_STEP * GRID_W
    return pl.pallas_call(
        functools.partial(_natten_body, rows=rows),
        grid=(batch, rows // NA_ROWS_PER_STEP),
        in_specs=[
            pl.BlockSpec((1, qrows, NA_WIDTH), lambda b, r: (b, r, 0)),
            pl.BlockSpec((1, seq, NA_WIDTH), lambda b, r: (b, 0, 1)),
            pl.BlockSpec((1, seq, NA_WIDTH), lambda b, r: (b, 0, 2)),
            pl.BlockSpec((META_PAD, NA_WIDTH), lambda b, r: (0, 1)),
            pl.BlockSpec((META_PAD, NA_WIDTH), lambda b, r: (0, 2)),
            pl.BlockSpec(bias_tabs.shape, lambda b, r: (0, 0, 0, 0), pipeline_mode=pl.Buffered(1)),
            pl.BlockSpec((1, NA_WIDTH), lambda b, r: (0, 0)),
            pl.BlockSpec((NA_WIDTH, NA_WIDTH), lambda b, r: (0, 0)),
        ],
        out_specs=pl.BlockSpec((1, qrows, NA_WIDTH), lambda b, r: (b, r, 0)),
        out_shape=jax.ShapeDtypeStruct((batch, seq, NA_WIDTH), BF16),
        compiler_params=_params("arbitrary", "arbitrary"),
        name="natten",
    )(att3, att3, att3, att_meta, att_meta, bias_tabs, norm_g, same_head)


HG_CHUNKS_PER_STEP = 4
ROW_B, ROW_LV = 0, 1
N_EXP_BLOCKS = 1 + N_HG_LEVELS


def _hgrn_constants():
    C = HG_CHUNK
    t = np.arange(C)
    u = t[None, :]
    mats = [u <= t[:, None]]
    masks = [np.eye(C, dtype=bool)]
    for lv in range(N_HG_LEVELS):
        m = 1 << lv
        blk = t // (2 * m)
        upper = (t // m) % 2 == 1
        p = blk * 2 * m + m - 1
        q_rows = upper[:, None] & (u > p[:, None]) & (u <= t[:, None])
        k_rows = (~upper)[:, None] & (u > t[:, None]) & (u <= p[:, None])
        mats.append(q_rows | k_rows)
        masks.append((blk[:, None] == blk[None, :]) & upper[:, None] & (~upper)[None, :])
    fwd = np.concatenate(mats, axis=0).astype(np.float32)
    fwd_mask = np.stack(masks).astype(np.float32)
    blocks = fwd.reshape(N_EXP_BLOCKS, C, C)
    bwd = blocks[:, ::-1, ::-1].reshape(N_EXP_BLOCKS * C, C)
    bwd_mask = fwd_mask[:, ::-1, ::-1]
    both = np.stack([fwd, bwd])
    both = np.concatenate([both, both], axis=-1)
    return (jnp.asarray(both, BF16), jnp.asarray(np.stack([fwd_mask, bwd_mask]), F32))


def _hgrn_exponents(mat, g):
    g_hi, g_lo = _split_bf16(g)
    return _dot(mat, jnp.concatenate([g_hi, g_lo], axis=0))


def _hgrn_body(hf_ref, hb_ref, meta_ref, mat_ref, mask_ref, of_ref, ob_ref, stf_ref, stb_ref):
    C, W = HG_CHUNK, HG_WIDTH
    c = pl.program_id(1)

    @pl.when(c == 0)
    def _init():
        mb = meta_ref[...]
        v = mb[:, W:2 * W].astype(BF16)
        prefix = mat_ref[0, ROW_B * C:(ROW_B + 1) * C, :]
        for d, st_ref in ((0, stf_ref), (1, stb_ref)):
            g = mb[:, (2 + d) * W:(3 + d) * W]
            b = _hgrn_exponents(prefix, g)
            kb = ((1.0 - jnp.exp(g)) * jnp.exp(b[C - 1:C] - b)).astype(BF16)
            for h in range(HG_HEADS):
                sl = slice(h * HG_HEAD_DIM, (h + 1) * HG_HEAD_DIM)
                st_ref[h] = _dot_tn(v[:, sl], kb[:, sl])

    n = HG_CHUNKS_PER_STEP
    dir_refs = ((hf_ref, 2 * W, stf_ref, C - 1, of_ref), (hb_ref, 3 * W, stb_ref, 0, ob_ref))
    tasks = [(0, j * C) for j in range(n)] + [(1, (n - 1 - j) * C) for j in range(n)]
    heads = [slice(h * HG_HEAD_DIM, (h + 1) * HG_HEAD_DIM) for h in range(HG_HEADS)]
    qs, ks, vs, exs, ex_ends = {}, {}, {}, {}, {}
    for d, r0 in tasks:
        ref, g_col, total_row = dir_refs[d][0], dir_refs[d][1], dir_refs[d][3]
        g = ref[0, r0:r0 + C, g_col:g_col + W]
        sums = _hgrn_exponents(mat_ref[d], g)
        exs[d, r0] = jnp.exp(sums)
        ex_ends[d, r0] = jnp.exp(sums[total_row:total_row + 1] - sums[0:C])
        qs[d, r0] = ref[0, r0:r0 + C, 0:W]
        ks[d, r0] = 1.0 - jnp.exp(g)
        vs[d, r0] = ref[0, r0:r0 + C, W:2 * W].astype(BF16)

    def ex_rows(task, i, sl):
        return exs[task][i * C:(i + 1) * C, sl]

    prods = {}
    for task in tasks:
        for h, sl in enumerate(heads):
            qh, kh = qs[task][:, sl], ks[task][:, sl]
            ps = [_dot_nt(qh.astype(BF16), kh.astype(BF16))]
            for lv in range(N_HG_LEVELS):
                scale = ex_rows(task, ROW_LV + lv, sl)
                ps.append(_dot_nt((qh * scale).astype(BF16), (kh * scale).astype(BF16)))
            prods[task, h] = ps
    attn, qbs, kbs = {}, {}, {}
    for task in tasks:
        d = task[0]
        for h, sl in enumerate(heads):
            a = prods[task, h][0] * mask_ref[d, 0]
            for lv in range(N_HG_LEVELS):
                a = a + prods[task, h][1 + lv] * mask_ref[d, 1 + lv]
            attn[task, h] = _dot(a.astype(BF16), vs[task][:, sl])
            qbs[task, h] = (qs[task][:, sl] * ex_rows(task, ROW_B, sl)).astype(BF16)
            kbs[task, h] = _dot_tn(vs[task][:, sl], (ks[task][:, sl] * ex_ends[task][:, sl]).astype(BF16))
    states = {(d, h): dir_refs[d][2][h] for d in range(2) for h in range(HG_HEADS)}
    for task in tasks:
        d, r0 = task
        total_row, o_ref = dir_refs[d][3], dir_refs[d][4]
        outs = []
        for h, sl in enumerate(heads):
            st = states[d, h]
            outs.append(attn[task, h] + _dot_nt(qbs[task, h], st.astype(BF16)))
            decay = exs[task][ROW_B * C + total_row:ROW_B * C + total_row + 1, sl]
            states[d, h] = decay * st + kbs[task, h]
        o_ref[0, r0:r0 + C, :] = jnp.concatenate(outs, axis=-1)
    for d in range(2):
        for h in range(HG_HEADS):
            dir_refs[d][2][h] = states[d, h]


def _hgrn(hg, hg_meta_pad, batch, seq):
    nc = seq // (HG_CHUNK * HG_CHUNKS_PER_STEP)
    hg3 = hg.reshape(batch, seq, 5 * HG_WIDTH)
    mats, masks = _hgrn_constants()
    blk = (1, HG_CHUNK * HG_CHUNKS_PER_STEP, 5 * HG_WIDTH)
    oblk = (1, HG_CHUNK * HG_CHUNKS_PER_STEP, HG_WIDTH)
    return pl.pallas_call(
        _hgrn_body,
        grid=(batch, nc),
        in_specs=[
            pl.BlockSpec(blk, lambda b, c: (b, c, 0)),
            pl.BlockSpec(blk, lambda b, c: (b, nc - 1 - c, 0)),
            pl.BlockSpec((HG_CHUNK, 5 * HG_WIDTH), lambda b, c: (0, 0)),
            pl.BlockSpec(mats.shape, lambda b, c: (0, 0, 0)),
            pl.BlockSpec(masks.shape, lambda b, c: (0, 0, 0, 0)),
        ],
        out_specs=[
            pl.BlockSpec(oblk, lambda b, c: (b, c, 0)),
            pl.BlockSpec(oblk, lambda b, c: (b, nc - 1 - c, 0)),
        ],
        out_shape=[jax.ShapeDtypeStruct((batch, seq, HG_WIDTH), F32)] * 2,
        scratch_shapes=[pltpu.VMEM((HG_HEADS, HG_HEAD_DIM, HG_HEAD_DIM), F32)] * 2,
        compiler_params=_params("arbitrary", "arbitrary"),
        name="hgrn",
    )(hg3, hg3, hg_meta_pad, mats, masks)


MIX_TILES_PER_STEP = 4


def _mix_route_body(yna_ref, of_ref, ob_ref, gate_ref, x_ref, wout_ref, hgn_ref, ffg_ref,
                    rwh_ref, rwl_ref, rb_ref, tri_ref, lowe_ref,
                    h1_ref, xn_ref, lpos_ref, tw_ref, cnt_ref):
    T, E = TOK_TILE, N_EXPERTS
    tiles = [slice(s * T, (s + 1) * T) for s in range(MIX_TILES_PER_STEP)]
    x_his, x_los = [], []
    for rows in tiles:
        o = of_ref[rows, :] + ob_ref[rows, :]
        parts = []
        for h in range(HG_HEADS):
            seg = o[:, h * HG_HEAD_DIM:(h + 1) * HG_HEAD_DIM]
            ms = jnp.mean(seg * seg, axis=-1, keepdims=True)
            parts.append(seg * lax.rsqrt(ms + RMS_EPS))
        yhg = jnp.concatenate(parts, axis=-1) * hgn_ref[...] * gate_ref[rows, :]
        mix = _dot(yna_ref[rows, :], wout_ref[0:NA_WIDTH, :]) + _dot(yhg.astype(BF16), wout_ref[NA_WIDTH:, :])
        h1 = x_ref[rows, :] + mix
        h1_ref[rows, :] = h1
        ms = jnp.mean(h1 * h1, axis=-1, keepdims=True)
        xn = h1 * lax.rsqrt(ms + RMS_EPS) * ffg_ref[...]
        x_hi, x_lo = _split_bf16(xn)
        xn_ref[rows, :] = x_hi
        x_his.append(x_hi)
        x_los.append(x_lo)

    curs = [(_dot_nt(rwh_ref[...], x_hi) + _dot_nt(rwh_ref[...], x_lo) + _dot_nt(rwl_ref[...], x_hi)) + rb_ref[...]
            for x_hi, x_lo in zip(x_his, x_los)]
    row = lax.broadcasted_iota(jnp.int32, (E, T), 0).astype(F32)
    sels = [[] for _ in tiles]
    vals = [[] for _ in tiles]
    for _ in range(TOP_K):
        for s in range(len(tiles)):
            m = jnp.max(curs[s], axis=0, keepdims=True)
            first = jnp.min(jnp.where(curs[s] == m, row, float(E)), axis=0, keepdims=True)
            sel = row == first
            sels[s].append(sel)
            vals[s].append(m)
            curs[s] = jnp.where(sel, -jnp.inf, curs[s])
    for s in range(len(tiles)):
        es = [jnp.exp(vk - vals[s][0]) for vk in vals[s]]
        den = es[0] + es[1] + es[2] + es[3]
        tw_ref[s] = _pack_rows([e / den for e in es], SUBLANES)

        onehot = jnp.zeros((E, T), F32)
        for sel in sels[s]:
            onehot = onehot + jnp.where(sel, 1.0, 0.0)
        lrank = _dot(onehot.astype(BF16), tri_ref[...])
        cnt = jnp.sum(onehot, axis=1, keepdims=True)
        cnt = jnp.floor((cnt + (SUBLANES - 1)) * (1.0 / SUBLANES)) * SUBLANES
        off = _dot(lowe_ref[...], jnp.broadcast_to(cnt, (E, LANES)).astype(BF16))
        base = jnp.concatenate([off] * (T // LANES), axis=1) + lrank
        lpos_ref[s] = _pack_rows(
            [jnp.sum(jnp.where(sel, base, 0.0), axis=0, keepdims=True) for sel in sels[s]],
            SUBLANES).astype(jnp.int32)
        cnt_ref[s] = cnt.astype(jnp.int32)


def _mix_route(yna, o_f, o_b, hg, x2d, wout_bf16, hgn, ffg, router_w, router_b):
    n = x2d.shape[0]
    T = TOK_TILE
    nt = n // T
    rw_t = router_w.T
    rwh = rw_t.astype(BF16)
    rwl = (rw_t - rwh.astype(F32)).astype(BF16)
    router_b = router_b.reshape(N_EXPERTS, 1)
    tri = jnp.asarray(np.triu(np.ones((T, T), np.float32), 1), BF16)
    lowe = jnp.asarray(np.tril(np.ones((N_EXPERTS, N_EXPERTS), np.float32), -1), BF16)
    row = lambda i: (i, 0)
    const = lambda i: (0, 0)
    tile3 = lambda i: (i, 0, 0)
    K = MIX_TILES_PER_STEP
    R = K * T
    return pl.pallas_call(
        _mix_route_body,
        grid=(nt // K,),
        in_specs=[
            pl.BlockSpec((R, NA_WIDTH), row),
            pl.BlockSpec((R, HG_WIDTH), row),
            pl.BlockSpec((R, HG_WIDTH), row),
            pl.BlockSpec((R, HG_WIDTH), lambda i: (i, 4)),
            pl.BlockSpec((R, D_MODEL), row),
            pl.BlockSpec((D_MODEL, D_MODEL), const),
            pl.BlockSpec((1, HG_WIDTH), const),
            pl.BlockSpec((1, D_MODEL), const),
            pl.BlockSpec((N_EXPERTS, D_MODEL), const),
            pl.BlockSpec((N_EXPERTS, D_MODEL), const),
            pl.BlockSpec((N_EXPERTS, 1), const),
            pl.BlockSpec((T, T), const),
            pl.BlockSpec((N_EXPERTS, N_EXPERTS), const),
        ],
        out_specs=[
            pl.BlockSpec((R, D_MODEL), row),
            pl.BlockSpec((R, D_MODEL), row),
            pl.BlockSpec((K, SUBLANES, T), tile3),
            pl.BlockSpec((K, SUBLANES, T), tile3),
            pl.BlockSpec((K, N_EXPERTS, 1), tile3),
        ],
        out_shape=[
            jax.ShapeDtypeStruct((n, D_MODEL), F32),
            jax.ShapeDtypeStruct((n, D_MODEL), BF16),
            jax.ShapeDtypeStruct((nt, SUBLANES, T), jnp.int32),
            jax.ShapeDtypeStruct((nt, SUBLANES, T), F32),
            jax.ShapeDtypeStruct((nt, N_EXPERTS, 1), jnp.int32),
        ],
        compiler_params=_params("arbitrary"),
        name="mix_route",
    )(yna, o_f, o_b, hg, x2d, wout_bf16, hgn, ffg, rwh, rwl, router_b, tri, lowe)


TILES_PER_STEP = 2


def _wait_rows(make_copy, rows):
    make_copy(0, 0, pl.multiple_of(rows, SUBLANES)).wait()


def _copy_runs(i, cnt_ref, off_ref, goff_ref, make_copy):
    def per_expert(e, carry):
        n = pl.multiple_of(cnt_ref[i * N_EXPERTS + e], SUBLANES)
        lo = pl.multiple_of(off_ref[i * N_EXPERTS + e], SUBLANES)
        go = pl.multiple_of(goff_ref[i * N_EXPERTS + e], SUBLANES)

        @pl.when(n > 0)
        def _start():
            make_copy(lo, go, n).start()

        return carry

    lax.fori_loop(0, N_EXPERTS, per_expert, 0)


def _slot_matrix(lpos, weights=None):
    row = lax.broadcasted_iota(jnp.int32, (STAGE_ROWS, TOK_TILE), 0)
    acc = jnp.zeros((STAGE_ROWS, TOK_TILE), F32)
    for k in range(TOP_K):
        hit = lpos[k:k + 1, :] == row
        val = 1.0 if weights is None else weights[k:k + 1, :]
        acc = jnp.where(hit, val, acc)
    return acc


def _dispatch_body(cnt_ref, off_ref, goff_ref, tot_ref, zoff_ref, zlen_ref, blkrange_ref,
                   xn_ref, lpos_ref, xbuf_ref, stage_ref, zero_ref, sems, zsem):
    i = pl.program_id(0)
    last = pl.num_programs(0) - 1

    def zero_fill(act):
        def tail(e, carry):
            n = pl.multiple_of(zlen_ref[e], SUBLANES)

            @pl.when(n > 0)
            def _():
                go = pl.multiple_of(zoff_ref[e], SUBLANES)
                act(pltpu.make_async_copy(zero_ref.at[pl.ds(0, n)], xbuf_ref.at[pl.ds(go, n)], zsem))

            return carry

        def unused(j, carry):
            go = pl.multiple_of(j * EXPERT_BLOCK, EXPERT_BLOCK)
            act(pltpu.make_async_copy(zero_ref, xbuf_ref.at[pl.ds(go, EXPERT_BLOCK)], zsem))
            return carry

        lax.fori_loop(0, N_EXPERTS, tail, 0)
        lax.fori_loop(blkrange_ref[0], blkrange_ref[1], unused, 0)

    @pl.when(i == 0)
    def _zero_start():
        zero_ref[...] = jnp.zeros_like(zero_ref)
        zero_fill(lambda c: c.start())

    for s in range(TILES_PER_STEP):
        t = i * TILES_PER_STEP + s

        def make_copy(lo, go, size, s=s):
            return pltpu.make_async_copy(stage_ref.at[s, pl.ds(lo, size)], xbuf_ref.at[pl.ds(go, size)], sems.at[s])

        @pl.when(i > 0)
        def _drain():
            _wait_rows(make_copy, tot_ref[t - TILES_PER_STEP])

        rows = slice(s * TOK_TILE, (s + 1) * TOK_TILE)
        perm = _slot_matrix(lpos_ref[s]).astype(BF16)
        stage_ref[s] = _dot(perm, xn_ref[rows, :])
        _copy_runs(t, cnt_ref, off_ref, goff_ref, make_copy)

        @pl.when(i == last)
        def _finish():
            _wait_rows(make_copy, tot_ref[t])

    @pl.when(i == last)
    def _zero_finish():
        zero_fill(lambda c: c.wait())


def _dispatch(cnt, off, goff, tot, zoff, zlen, blkrange, xn, lpos, n_rows):
    n = xn.shape[0]
    rows = TILES_PER_STEP * TOK_TILE
    return pl.pallas_call(
        _dispatch_body,
        grid_spec=pltpu.PrefetchScalarGridSpec(
            num_scalar_prefetch=7,
            grid=(n // rows,),
            in_specs=[
                pl.BlockSpec((rows, D_MODEL), lambda i, *_: (i, 0)),
                pl.BlockSpec((TILES_PER_STEP, SUBLANES, TOK_TILE), lambda i, *_: (i, 0, 0)),
            ],
            out_specs=pl.BlockSpec(memory_space=pl.ANY),
            scratch_shapes=[
                pltpu.VMEM((TILES_PER_STEP, STAGE_ROWS, D_MODEL), F32),
                pltpu.VMEM((EXPERT_BLOCK, D_MODEL), F32),
                pltpu.SemaphoreType.DMA((TILES_PER_STEP,)),
                pltpu.SemaphoreType.DMA(()),
            ],
        ),
        out_shape=jax.ShapeDtypeStruct((n_rows, D_MODEL), F32),
        compiler_params=_params("arbitrary"),
        name="moe_dispatch",
    )(cnt, off, goff, tot, zoff, zlen, blkrange, xn, lpos)


CAST_ROWS = 128


def _experts_body(blk_e_ref, first_ref, slot_ref, next_e_ref, nused_ref,
                  x_ref, bgu_ref, bd_ref, wgu_hbm, wd_hbm, y_ref,
                  wgu_f32, wd_f32, wgu_bf, wd_bf, sem_gu, sem_d):
    j = pl.program_id(0)

    def weight_copies(e, slot):
        return (pltpu.make_async_copy(wgu_hbm.at[e], wgu_f32.at[slot], sem_gu.at[slot]),
                pltpu.make_async_copy(wd_hbm.at[e], wd_f32.at[slot], sem_d.at[slot]))

    @pl.when(j < nused_ref[0])
    def _():
        e = blk_e_ref[j]
        slot = slot_ref[j]

        @pl.when(first_ref[j] == 1)
        def _new_expert():
            @pl.when(j == 0)
            def _():
                for c in weight_copies(e, slot):
                    c.start()

            for c in weight_copies(e, slot):
                c.wait()
            nxt = next_e_ref[j]

            @pl.when(nxt >= 0)
            def _():
                for c in weight_copies(nxt, 1 - slot):
                    c.start()

            def cast(src, dst):
                def rows(r, carry):
                    sl = pl.ds(pl.multiple_of(r * CAST_ROWS, CAST_ROWS), CAST_ROWS)
                    dst[sl, :] = src[slot, sl, :].astype(BF16)
                    return carry
                lax.fori_loop(0, src.shape[1] // CAST_ROWS, rows, 0)

            cast(wgu_f32, wgu_bf)
            cast(wd_f32, wd_bf)

        x = x_ref[...].astype(BF16)
        gu = _dot(x, wgu_bf[...]) + bgu_ref[0]
        gate = jnp.minimum(gu[:, :D_FF], SWIGLU_LIMIT)
        up = jnp.clip(gu[:, D_FF:], -SWIGLU_LIMIT, SWIGLU_LIMIT)
        act = (up + 1.0) * gate * jax.nn.sigmoid(SWIGLU_ALPHA * gate)
        y_ref[...] = _dot(act.astype(BF16), wd_bf[...]) + bd_ref[0]


def _experts(blk_e, first, slot, next_e, nused, xbuf, wgu, bgu, wd, bd):
    n_rows = xbuf.shape[0]
    nblk = n_rows // EXPERT_BLOCK

    def rowblk(j, be, fi, sl, ne, nu):
        return (jnp.minimum(j, nu[0] - 1), 0)

    def expert(j, be, fi, sl, ne, nu):
        return (be[jnp.minimum(j, nu[0] - 1)], 0, 0)

    return pl.pallas_call(
        _experts_body,
        grid_spec=pltpu.PrefetchScalarGridSpec(
            num_scalar_prefetch=5,
            grid=(nblk,),
            in_specs=[
                pl.BlockSpec((EXPERT_BLOCK, D_MODEL), rowblk),
                pl.BlockSpec((1, 1, 2 * D_FF), expert),
                pl.BlockSpec((1, 1, D_MODEL), expert),
                pl.BlockSpec(memory_space=pl.ANY),
                pl.BlockSpec(memory_space=pl.ANY),
            ],
            out_specs=pl.BlockSpec((EXPERT_BLOCK, D_MODEL), rowblk),
            scratch_shapes=[
                pltpu.VMEM((2, D_MODEL, 2 * D_FF), F32),
                pltpu.VMEM((2, D_FF, D_MODEL), F32),
                pltpu.VMEM((D_MODEL, 2 * D_FF), BF16),
                pltpu.VMEM((D_FF, D_MODEL), BF16),
                pltpu.SemaphoreType.DMA((2,)),
                pltpu.SemaphoreType.DMA((2,)),
            ],
        ),
        out_shape=jax.ShapeDtypeStruct((n_rows, D_MODEL), F32),
        input_output_aliases={5: 0},
        compiler_params=_params("arbitrary"),
        name="moe_experts",
    )(blk_e, first, slot, next_e, nused, xbuf, bgu, bd, wgu, wd)


def _combine_body(cnt_ref, off_ref, goff_ref, tot_ref, ybuf_ref, lpos_ref, tw_ref, h1_ref, fg_ref, out_ref,
                  stage_ref, sems):
    i = pl.program_id(0)
    n_tiles = pl.num_programs(0) * TILES_PER_STEP

    def copier(s):
        def make_copy(lo, go, size):
            return pltpu.make_async_copy(ybuf_ref.at[pl.ds(go, size)], stage_ref.at[s, pl.ds(lo, size)], sems.at[s])
        return make_copy

    @pl.when(i == 0)
    def _init():
        stage_ref[...] = jnp.zeros_like(stage_ref)
        _copy_runs(0, cnt_ref, off_ref, goff_ref, copier(0))

    for s in range(TILES_PER_STEP):
        t = i * TILES_PER_STEP + s

        @pl.when(t + 1 < n_tiles)
        def _prefetch():
            _copy_runs(t + 1, cnt_ref, off_ref, goff_ref, copier((s + 1) % TILES_PER_STEP))

        rows = slice(s * TOK_TILE, (s + 1) * TOK_TILE)
        w_hi, w_lo = _split_bf16(_slot_matrix(lpos_ref[s], tw_ref[s]))
        w2 = jnp.concatenate([w_hi, w_lo], axis=1)
        _wait_rows(copier(s), tot_ref[t])
        y2 = _dot_tn(w2, stage_ref[s].astype(BF16))
        h2 = h1_ref[rows, :] + (y2[0:TOK_TILE] + y2[TOK_TILE:2 * TOK_TILE])
        ms = jnp.mean(h2 * h2, axis=-1, keepdims=True)
        out_ref[rows, :] = h2 * lax.rsqrt(ms + RMS_EPS) * fg_ref[...]


def _combine(cnt, off, goff, tot, ybuf, lpos, tw, h1, final_g):
    n = h1.shape[0]
    rows = TILES_PER_STEP * TOK_TILE
    row = lambda i, *_: (i, 0)
    return pl.pallas_call(
        _combine_body,
        grid_spec=pltpu.PrefetchScalarGridSpec(
            num_scalar_prefetch=4,
            grid=(n // rows,),
            in_specs=[
                pl.BlockSpec(memory_space=pl.ANY),
                pl.BlockSpec((TILES_PER_STEP, SUBLANES, TOK_TILE), lambda i, *_: (i, 0, 0)),
                pl.BlockSpec((TILES_PER_STEP, SUBLANES, TOK_TILE), lambda i, *_: (i, 0, 0)),
                pl.BlockSpec((rows, D_MODEL), row),
                pl.BlockSpec((1, D_MODEL), lambda i, *_: (0, 0)),
            ],
            out_specs=pl.BlockSpec((rows, D_MODEL), row),
            scratch_shapes=[pltpu.VMEM((TILES_PER_STEP, STAGE_ROWS, D_MODEL), F32),
                            pltpu.SemaphoreType.DMA((TILES_PER_STEP,))],
        ),
        out_shape=jax.ShapeDtypeStruct((n, D_MODEL), F32),
        compiler_params=_params("arbitrary"),
        name="moe_combine",
    )(cnt, off, goff, tot, ybuf, lpos, tw, h1, final_g)


def _moe_layout(cnt_tiles):
    nt = cnt_tiles.shape[0]
    total = jnp.sum(cnt_tiles, axis=0)
    nblk_e = (total + EXPERT_BLOCK - 1) // EXPERT_BLOCK
    blk_end = jnp.cumsum(nblk_e)
    pstart = (blk_end - nblk_e) * EXPERT_BLOCK
    prefix = jnp.cumsum(cnt_tiles, axis=0) - cnt_tiles
    goff = pstart[None, :] + prefix
    off = jnp.cumsum(cnt_tiles, axis=1) - cnt_tiles
    tot = jnp.sum(cnt_tiles, axis=1)
    max_rows = nt * (TOK_TILE * TOP_K + N_EXPERTS * (SUBLANES - 1)) + N_EXPERTS * (EXPERT_BLOCK - 1)
    max_blocks = (max_rows + EXPERT_BLOCK - 1) // EXPERT_BLOCK
    blk_id = jnp.arange(max_blocks, dtype=blk_end.dtype)
    blk_e = jnp.minimum(jnp.sum(blk_end[None, :] <= blk_id[:, None], axis=1), N_EXPERTS - 1).astype(jnp.int32)
    nused = blk_end[-1:].astype(jnp.int32)
    flat = lambda a: a.reshape(-1).astype(jnp.int32)
    zoff = pstart + total
    zlen = nblk_e * EXPERT_BLOCK - total
    blkrange = jnp.stack([nused[0], jnp.int32(max_blocks)])
    used = blk_id < nused[0]
    first = used & jnp.concatenate([jnp.ones((1,), bool), blk_e[1:] != blk_e[:-1]])
    slot = (jnp.cumsum(first.astype(jnp.int32)) - 1) % 2
    eid = jnp.arange(N_EXPERTS)
    later = (eid[None, :] > eid[:, None]) & (nblk_e > 0)[None, :]
    next_of = jnp.min(jnp.where(later, eid[None, :], N_EXPERTS), axis=1)
    next_of = jnp.where(next_of < N_EXPERTS, next_of, -1)
    experts_plan = (blk_e, first.astype(jnp.int32), slot.astype(jnp.int32), next_of[blk_e].astype(jnp.int32), nused)
    dispatch_plan = (flat(cnt_tiles), flat(off), flat(goff), flat(tot))
    zero_plan = (flat(zoff), flat(zlen), blkrange.astype(jnp.int32))
    return dispatch_plan, zero_plan, experts_plan, max_blocks * EXPERT_BLOCK


def kernel(x, meta_tokens, attn_norm_g, w_in, na_rpb, na_norm_g, hgrn_lb_logits, hgrn_norm_g, w_out, ffn_norm_g,
           router_w, router_b, expert_w_gu, expert_b_gu, expert_w_down, expert_b_down, final_norm_g):
    B, T, D = x.shape
    x2d = x.reshape(B * T, D)
    w_in_b = w_in[0].astype(BF16)
    ng = attn_norm_g[0].reshape(1, D)
    lbl = hgrn_lb_logits[:, :, :].reshape(4, HG_WIDTH)

    att, hg = _inproj(x2d, ng, w_in_b, lbl, 512)
    att_m, hg_m = _inproj(meta_tokens.astype(F32), ng, w_in_b, lbl, N_META)
    att_m = jnp.pad(att_m, ((0, META_PAD - N_META), (0, 0)))
    hg_m = jnp.pad(hg_m, ((0, HG_CHUNK - N_META), (0, 0)))

    yna = _natten(att, att_m, _natten_bias_tables(na_rpb[0]), na_norm_g[0].reshape(1, NA_WIDTH), B, T)
    o_f, o_b = _hgrn(hg, hg_m, B, T)

    h1, xn, lpos, tw, cnt_tiles = _mix_route(
        yna.reshape(B * T, NA_WIDTH), o_f.reshape(B * T, HG_WIDTH), o_b.reshape(B * T, HG_WIDTH), hg, x2d,
        w_out[0].astype(BF16), hgrn_norm_g[0].reshape(1, HG_WIDTH), ffn_norm_g[0].reshape(1, D),
        router_w[0], router_b[0].reshape(1, N_EXPERTS))

    dispatch_plan, zero_plan, experts_plan, n_rows = _moe_layout(cnt_tiles[:, :, 0])
    xbuf = _dispatch(*dispatch_plan, *zero_plan, xn, lpos, n_rows)
    ybuf = _experts(*experts_plan, xbuf, expert_w_gu[0], expert_b_gu[0][:, None, :],
                    expert_w_down[0], expert_b_down[0][:, None, :])
    out = _combine(*dispatch_plan, ybuf, lpos, tw, h1, final_norm_g.reshape(1, D))
    return out.reshape(B, T, D)
```

```python
import functools

import numpy as np
import jax
import jax.numpy as jnp
from jax import lax
from jax.experimental import pallas as pl
from jax.experimental.pallas import tpu as pltpu

F32 = jnp.float32
BF16 = jnp.bfloat16

D_MODEL = 1024
N_META = 16
GRID_W = 64
NA_WIDTH = 512
NA_HEAD_DIM = 64
NA_HEADS = 8
NA_KH = 8
NA_KW = 16
HG_WIDTH = 512
HG_HEAD_DIM = 128
HG_HEADS = 4
HG_CHUNK = 64
IN_COLS = 3 * NA_WIDTH + 5 * HG_WIDTH
N_EXPERTS = 32
TOP_K = 4
D_FF = 1024
SWIGLU_LIMIT = 7.0
SWIGLU_ALPHA = 1.702
RMS_EPS = 1e-6
NEG_BIG = -1e30
LOG2_E = 1.4426950408889634

LANES = 128
VMEM_LIMIT_BYTES = 56 * 1024 * 1024

TOK_TILE = 256
EXPERT_BLOCK = 256
SUBLANES = 8
STAGE_ROWS = TOK_TILE * TOP_K + N_EXPERTS * SUBLANES
N_HG_LEVELS = 6


def _dot(a, b):
    return jnp.dot(a, b, preferred_element_type=F32)


def _dot_nt(a, b):
    return lax.dot_general(a, b, (((1,), (1,)), ((), ())), preferred_element_type=F32)


def _dot_tn(a, b):
    return lax.dot_general(a, b, (((0,), (0,)), ((), ())), preferred_element_type=F32)


def _split_bf16(x):
    hi = x.astype(BF16)
    lo = (x - hi.astype(F32)).astype(BF16)
    return hi, lo


def _pack_rows(rows, n_rows):
    idx = lax.broadcasted_iota(jnp.int32, (n_rows, rows[0].shape[1]), 0)
    out = jnp.zeros(idx.shape, rows[0].dtype)
    for k, r in enumerate(rows):
        out = jnp.where(idx == k, r, out)
    return out


def _params(*sem):
    return pltpu.CompilerParams(dimension_semantics=sem, vmem_limit_bytes=VMEM_LIMIT_BYTES)


def _inproj_body(x_ref, ng_ref, w_ref, lbl_ref, att_ref, hg_ref):
    x = x_ref[...]
    ms = jnp.mean(x * x, axis=-1, keepdims=True)
    n = (x * lax.rsqrt(ms + RMS_EPS) * ng_ref[...]).astype(BF16)

    def proj(lo, hi):
        return _dot(n, w_ref[:, lo:hi])

    pq = proj(0, NA_WIDTH)
    att_ref[:, 0:NA_WIDTH] = (pq * (NA_HEAD_DIM ** -0.5 * LOG2_E)).astype(BF16)
    att_ref[:, NA_WIDTH:3 * NA_WIDTH] = proj(NA_WIDTH, 3 * NA_WIDTH).astype(BF16)

    base = 3 * NA_WIDTH
    W = HG_WIDTH
    qh = proj(base, base + W)
    hg_ref[:, 0:W] = qh * jax.nn.sigmoid(qh)
    hg_ref[:, W:2 * W] = proj(base + W, base + 2 * W)
    lbl = lbl_ref[...]
    for d in range(2):
        a0 = lbl[2 * d:2 * d + 1, :]
        a1 = lbl[2 * d + 1:2 * d + 2, :]
        m = jnp.maximum(a0, a1)
        e0 = jnp.exp(a0 - m)
        e1 = jnp.exp(a1 - m)
        lb = e0 / (e0 + e1)
        raw = proj(base + (2 + d) * W, base + (3 + d) * W)
        f = lb + (1.0 - lb) * jax.nn.sigmoid(raw)
        hg_ref[:, (2 + d) * W:(3 + d) * W] = jnp.log(f)
    gt = proj(base + 4 * W, base + 5 * W)
    hg_ref[:, 4 * W:5 * W] = gt * jax.nn.sigmoid(gt)


def _inproj(x2d, norm_g, w_bf16, lb_logits4, tm):
    n = x2d.shape[0]
    return pl.pallas_call(
        _inproj_body,
        grid=(n // tm,),
        in_specs=[
            pl.BlockSpec((tm, D_MODEL), lambda i: (i, 0)),
            pl.BlockSpec((1, D_MODEL), lambda i: (0, 0)),
            pl.BlockSpec((D_MODEL, IN_COLS), lambda i: (0, 0)),
            pl.BlockSpec((4, HG_WIDTH), lambda i: (0, 0)),
        ],
        out_specs=[
            pl.BlockSpec((tm, 3 * NA_WIDTH), lambda i: (i, 0)),
            pl.BlockSpec((tm, 5 * HG_WIDTH), lambda i: (i, 0)),
        ],
        out_shape=[
            jax.ShapeDtypeStruct((n, 3 * NA_WIDTH), BF16),
            jax.ShapeDtypeStruct((n, 5 * HG_WIDTH), F32),
        ],
        compiler_params=_params("arbitrary"),
        name="inproj",
    )(x2d, norm_g, w_bf16, lb_logits4)


N_WIN_KEYS = NA_KH * GRID_W
META_PAD = LANES


NA_GROUP = 4
GROUP_LANES = NA_GROUP * NA_HEAD_DIM


NA_ROWS_PER_STEP = 8


def _natten_row(rr, r, q_ref, k_ref, v_ref, km_ref, vm_ref, bias_ref, *, rows):
    rs = jnp.clip(r - NA_KH // 2, 0, rows - NA_KH)
    d0 = rs - r + (NA_KH - 1)
    start = pl.multiple_of(rs * GRID_W, GRID_W)
    lane_head = lax.broadcasted_iota(jnp.int32, (GRID_W, GROUP_LANES), 1) // NA_HEAD_DIM
    meta_col = lax.broadcasted_iota(jnp.int32, (1, META_PAD), 1)
    meta_bias = jnp.where(meta_col < N_META, 0.0, NEG_BIG).astype(F32)
    folded = []
    for g in range(NA_HEADS // NA_GROUP):
        sl = slice(g * GROUP_LANES, (g + 1) * GROUP_LANES)
        q4 = q_ref[0, rr * GRID_W:(rr + 1) * GRID_W, sl]
        zero = jnp.zeros_like(q4)
        qm = jnp.concatenate([jnp.where(lane_head == h, q4, zero) for h in range(NA_GROUP)], axis=0)
        kw = k_ref[0, pl.ds(start, N_WIN_KEYS), sl]
        vw = v_ref[0, pl.ds(start, N_WIN_KEYS), sl]
        hs = slice(g * NA_GROUP, (g + 1) * NA_GROUP)
        bias = jnp.concatenate(
            [bias_ref[d0 + 2 * j, hs].reshape(NA_GROUP * GRID_W, 2 * GRID_W) for j in range(NA_KH // 2)], axis=-1)
        s = _dot_nt(qm, kw) + bias
        sm = _dot_nt(qm, km_ref[:, sl]) + meta_bias
        m = jnp.maximum(jnp.max(s, axis=-1, keepdims=True), jnp.max(sm, axis=-1, keepdims=True))
        e = jnp.exp2(s - m)
        em = jnp.exp2(sm - m)
        den = jnp.sum(e, axis=-1, keepdims=True) + jnp.sum(em, axis=-1, keepdims=True)
        o = (_dot(e.astype(BF16), vw) + _dot(em.astype(BF16), vm_ref[:, sl])) * (1.0 / den)
        acc = jnp.where(lane_head == 0, o[0:GRID_W], 0.0)
        for h in range(1, NA_GROUP):
            acc = jnp.where(lane_head == h, o[h * GRID_W:(h + 1) * GRID_W], acc)
        folded.append(acc)
    return jnp.concatenate(folded, axis=-1)


def _natten_body(q_ref, k_ref, v_ref, km_ref, vm_ref, bias_ref, ng_ref, bd_ref, o_ref, *, rows):
    r0 = pl.program_id(1) * NA_ROWS_PER_STEP
    o2 = jnp.concatenate(
        [_natten_row(rr, r0 + rr, q_ref, k_ref, v_ref, km_ref, vm_ref, bias_ref, rows=rows)
         for rr in range(NA_ROWS_PER_STEP)], axis=0)
    sq_hi, sq_lo = _split_bf16(o2 * o2)
    ms = (_dot(sq_hi, bd_ref[...]) + _dot(sq_lo, bd_ref[...])) * (1.0 / NA_HEAD_DIM)
    o_ref[0] = (o2 * lax.rsqrt(ms + RMS_EPS) * ng_ref[...]).astype(o_ref.dtype)


def _natten_bias_tables(rpb):
    c = np.arange(GRID_W)[:, None]
    kc = np.arange(GRID_W)[None, :]
    cs = np.clip(c - NA_KW // 2, 0, GRID_W - NA_KW)
    valid = (kc >= cs) & (kc < cs + NA_KW)
    dc = kc - c + (NA_KW - 1)
    pick = np.asarray(dc[None] == np.arange(2 * NA_KW - 1)[:, None, None], np.float32)
    t1 = jnp.einsum("hdj,jck->hdck", rpb.astype(F32) * LOG2_E, pick, precision=lax.Precision.HIGHEST)
    t1 = jnp.where(valid[None, None], t1, NEG_BIG)
    pairs = jnp.concatenate([t1[:, :-1], t1[:, 1:]], axis=-1)
    return pairs.transpose(1, 0, 2, 3)


def _natten(att, att_meta, bias_tabs, norm_g, batch, seq):
    rows = seq // GRID_W
    att3 = att.reshape(batch, seq, 3 * NA_WIDTH)
    head_of = np.arange(NA_WIDTH) // NA_HEAD_DIM
    same_head = jnp.asarray(head_of[:, None] == head_of[None, :], BF16)

    qrows = NA_ROWS_PER_STEP * GRID_W
    return pl.pallas_call(
        functools.partial(_natten_body, rows=rows),
        grid=(batch, rows // NA_ROWS_PER_STEP),
        in_specs=[
            pl.BlockSpec((1, qrows, NA_WIDTH), lambda b, r: (b, r, 0)),
            pl.BlockSpec((1, seq, NA_WIDTH), lambda b, r: (b, 0, 1)),
            pl.BlockSpec((1, seq, NA_WIDTH), lambda b, r: (b, 0, 2)),
            pl.BlockSpec((META_PAD, NA_WIDTH), lambda b, r: (0, 1)),
            pl.BlockSpec((META_PAD, NA_WIDTH), lambda b, r: (0, 2)),
            pl.BlockSpec(bias_tabs.shape, lambda b, r: (0, 0, 0, 0), pipeline_mode=pl.Buffered(1)),
            pl.BlockSpec((1, NA_WIDTH), lambda b, r: (0, 0)),
            pl.BlockSpec((NA_WIDTH, NA_WIDTH), lambda b, r: (0, 0)),
        ],
        out_specs=pl.BlockSpec((1, qrows, NA_WIDTH), lambda b, r: (b, r, 0)),
        out_shape=jax.ShapeDtypeStruct((batch, seq, NA_WIDTH), BF16),
        compiler_params=_params("arbitrary", "arbitrary"),
        name="natten",
    )(att3, att3, att3, att_meta, att_meta, bias_tabs, norm_g, same_head)


HG_CHUNKS_PER_STEP = 4
ROW_B, ROW_LV = 0, 1
N_EXP_BLOCKS = 1 + N_HG_LEVELS


def _hgrn_constants():
    C = HG_CHUNK
    t = np.arange(C)
    u = t[None, :]
    mats = [u <= t[:, None]]
    masks = [np.eye(C, dtype=bool)]
    for lv in range(N_HG_LEVELS):
        m = 1 << lv
        blk = t // (2 * m)
        upper = (t // m) % 2 == 1
        p = blk * 2 * m + m - 1
        q_rows = upper[:, None] & (u > p[:, None]) & (u <= t[:, None])
        k_rows = (~upper)[:, None] & (u > t[:, None]) & (u <= p[:, None])
        mats.append(q_rows | k_rows)
        masks.append((blk[:, None] == blk[None, :]) & upper[:, None] & (~upper)[None, :])
    fwd = np.concatenate(mats, axis=0).astype(np.float32)
    fwd_mask = np.stack(masks).astype(np.float32)
    blocks = fwd.reshape(N_EXP_BLOCKS, C, C)
    bwd = blocks[:, ::-1, ::-1].reshape(N_EXP_BLOCKS * C, C)
    bwd_mask = fwd_mask[:, ::-1, ::-1]
    both = np.stack([fwd, bwd])
    both = np.concatenate([both, both], axis=-1)
    both_mask = np.stack([fwd_mask, bwd_mask])
    both_mask = np.concatenate([both_mask, both_mask], axis=-1)
    return (jnp.asarray(both, BF16), jnp.asarray(both_mask, F32))


def _hgrn_exponents(mat, g):
    g_hi, g_lo = _split_bf16(g)
    return _dot(mat, jnp.concatenate([g_hi, g_lo], axis=0))


def _hgrn_body(hf_ref, hb_ref, meta_ref, mat_ref, mask_ref, of_ref, ob_ref, stf_ref, stb_ref):
    C, W = HG_CHUNK, HG_WIDTH
    c = pl.program_id(1)

    @pl.when(c == 0)
    def _init():
        mb = meta_ref[...]
        v = mb[:, W:2 * W].astype(BF16)
        prefix = mat_ref[0, ROW_B * C:(ROW_B + 1) * C, :]
        for d, st_ref in ((0, stf_ref), (1, stb_ref)):
            g = mb[:, (2 + d) * W:(3 + d) * W]
            b = _hgrn_exponents(prefix, g)
            kb = ((1.0 - jnp.exp(g)) * jnp.exp(b[C - 1:C] - b)).astype(BF16)
            for h in range(HG_HEADS):
                sl = slice(h * HG_HEAD_DIM, (h + 1) * HG_HEAD_DIM)
                st_ref[h] = _dot_tn(v[:, sl], kb[:, sl])

    n = HG_CHUNKS_PER_STEP
    dir_refs = ((hf_ref, 2 * W, stf_ref, C - 1, of_ref), (hb_ref, 3 * W, stb_ref, 0, ob_ref))
    tasks = [(0, j * C) for j in range(n)] + [(1, (n - 1 - j) * C) for j in range(n)]
    heads = [slice(h * HG_HEAD_DIM, (h + 1) * HG_HEAD_DIM) for h in range(HG_HEADS)]
    qs, ks, vs, exs, ex_ends = {}, {}, {}, {}, {}
    for d, r0 in tasks:
        ref, g_col, total_row = dir_refs[d][0], dir_refs[d][1], dir_refs[d][3]
        g = ref[0, r0:r0 + C, g_col:g_col + W]
        sums = _hgrn_exponents(mat_ref[d], g)
        exs[d, r0] = jnp.exp(sums)
        ex_ends[d, r0] = jnp.exp(sums[total_row:total_row + 1] - sums[0:C])
        qs[d, r0] = ref[0, r0:r0 + C, 0:W]
        ks[d, r0] = 1.0 - jnp.exp(g)
        vs[d, r0] = ref[0, r0:r0 + C, W:2 * W].astype(BF16)

    def ex_rows(task, i, sl):
        return exs[task][i * C:(i + 1) * C, sl]

    D2 = 2 * HG_HEAD_DIM
    pairs = [slice(p * D2, (p + 1) * D2) for p in range(HG_HEADS // 2)]

    def block_diag(x):
        zero = jnp.zeros((C, HG_HEAD_DIM), x.dtype)
        return jnp.concatenate([jnp.concatenate([x[:, :HG_HEAD_DIM], zero], axis=1),
                                jnp.concatenate([zero, x[:, HG_HEAD_DIM:]], axis=1)], axis=0)

    prods = {}
    for task in tasks:
        for p, sl in enumerate(pairs):
            q2, k2 = qs[task][:, sl], ks[task][:, sl]
            ps = [_dot_nt(q2.astype(BF16), block_diag(k2.astype(BF16)))]
            for lv in range(N_HG_LEVELS):
                scale = ex_rows(task, ROW_LV + lv, sl)
                ps.append(_dot_nt((q2 * scale).astype(BF16), block_diag((k2 * scale).astype(BF16))))
            prods[task, p] = ps
    attn, qbs, kbs = {}, {}, {}
    for task in tasks:
        d = task[0]
        for p, sl in enumerate(pairs):
            a = prods[task, p][0] * mask_ref[d, 0]
            for lv in range(N_HG_LEVELS):
                a = a + prods[task, p][1 + lv] * mask_ref[d, 1 + lv]
            o2 = _dot(a.astype(BF16), block_diag(vs[task][:, sl]))
            attn[task, 2 * p] = o2[:, :HG_HEAD_DIM]
            attn[task, 2 * p + 1] = o2[:, HG_HEAD_DIM:]
        for h, sl in enumerate(heads):
            qbs[task, h] = (qs[task][:, sl] * ex_rows(task, ROW_B, sl)).astype(BF16)
            kbs[task, h] = _dot_tn(vs[task][:, sl], (ks[task][:, sl] * ex_ends[task][:, sl]).astype(BF16))
    states = {(d, h): dir_refs[d][2][h] for d in range(2) for h in range(HG_HEADS)}
    for task in tasks:
        d, r0 = task
        total_row, o_ref = dir_refs[d][3], dir_refs[d][4]
        outs = []
        for h, sl in enumerate(heads):
            st = states[d, h]
            outs.append(attn[task, h] + _dot_nt(qbs[task, h], st.astype(BF16)))
            decay = exs[task][ROW_B * C + total_row:ROW_B * C + total_row + 1, sl]
            states[d, h] = decay * st + kbs[task, h]
        o_ref[0, r0:r0 + C, :] = jnp.concatenate(outs, axis=-1)
    for d in range(2):
        for h in range(HG_HEADS):
            dir_refs[d][2][h] = states[d, h]


def _hgrn(hg, hg_meta_pad, batch, seq):
    nc = seq // (HG_CHUNK * HG_CHUNKS_PER_STEP)
    hg3 = hg.reshape(batch, seq, 5 * HG_WIDTH)
    mats, masks = _hgrn_constants()
    blk = (1, HG_CHUNK * HG_CHUNKS_PER_STEP, 5 * HG_WIDTH)
    oblk = (1, HG_CHUNK * HG_CHUNKS_PER_STEP, HG_WIDTH)
    return pl.pallas_call(
        _hgrn_body,
        grid=(batch, nc),
        in_specs=[
            pl.BlockSpec(blk, lambda b, c: (b, c, 0)),
            pl.BlockSpec(blk, lambda b, c: (b, nc - 1 - c, 0)),
            pl.BlockSpec((HG_CHUNK, 5 * HG_WIDTH), lambda b, c: (0, 0)),
            pl.BlockSpec(mats.shape, lambda b, c: (0, 0, 0)),
            pl.BlockSpec(masks.shape, lambda b, c: (0, 0, 0, 0)),
        ],
        out_specs=[
            pl.BlockSpec(oblk, lambda b, c: (b, c, 0)),
            pl.BlockSpec(oblk, lambda b, c: (b, nc - 1 - c, 0)),
        ],
        out_shape=[jax.ShapeDtypeStruct((batch, seq, HG_WIDTH), F32)] * 2,
        scratch_shapes=[pltpu.VMEM((HG_HEADS, HG_HEAD_DIM, HG_HEAD_DIM), F32)] * 2,
        compiler_params=_params("arbitrary", "arbitrary"),
        name="hgrn",
    )(hg3, hg3, hg_meta_pad, mats, masks)


MIX_TILES_PER_STEP = 4


def _mix_route_body(yna_ref, of_ref, ob_ref, gate_ref, x_ref, wout_ref, hgn_ref, ffg_ref,
                    rwh_ref, rwl_ref, rb_ref, tri_ref, lowe_ref,
                    h1_ref, xn_ref, lpos_ref, tw_ref, cnt_ref):
    T, E = TOK_TILE, N_EXPERTS
    tiles = [slice(s * T, (s + 1) * T) for s in range(MIX_TILES_PER_STEP)]
    x_his, x_los = [], []
    for rows in tiles:
        o = of_ref[rows, :] + ob_ref[rows, :]
        parts = []
        for h in range(HG_HEADS):
            seg = o[:, h * HG_HEAD_DIM:(h + 1) * HG_HEAD_DIM]
            ms = jnp.mean(seg * seg, axis=-1, keepdims=True)
            parts.append(seg * lax.rsqrt(ms + RMS_EPS))
        yhg = jnp.concatenate(parts, axis=-1) * hgn_ref[...] * gate_ref[rows, :]
        mix = _dot(yna_ref[rows, :], wout_ref[0:NA_WIDTH, :]) + _dot(yhg.astype(BF16), wout_ref[NA_WIDTH:, :])
        h1 = x_ref[rows, :] + mix
        h1_ref[rows, :] = h1
        ms = jnp.mean(h1 * h1, axis=-1, keepdims=True)
        xn = h1 * lax.rsqrt(ms + RMS_EPS) * ffg_ref[...]
        x_hi, x_lo = _split_bf16(xn)
        xn_ref[rows, :] = x_hi
        x_his.append(x_hi)
        x_los.append(x_lo)

    curs = [(_dot_nt(rwh_ref[...], x_hi) + _dot_nt(rwh_ref[...], x_lo) + _dot_nt(rwl_ref[...], x_hi)) + rb_ref[...]
            for x_hi, x_lo in zip(x_his, x_los)]
    row = lax.broadcasted_iota(jnp.int32, (E, T), 0).astype(F32)
    sels = [[] for _ in tiles]
    vals = [[] for _ in tiles]
    for _ in range(TOP_K):
        for s in range(len(tiles)):
            m = jnp.max(curs[s], axis=0, keepdims=True)
            first = jnp.min(jnp.where(curs[s] == m, row, float(E)), axis=0, keepdims=True)
            sel = row == first
            sels[s].append(sel)
            vals[s].append(m)
            curs[s] = jnp.where(sel, -jnp.inf, curs[s])
    for s in range(len(tiles)):
        es = [jnp.exp(vk - vals[s][0]) for vk in vals[s]]
        den = es[0] + es[1] + es[2] + es[3]
        tw_ref[s] = _pack_rows([e / den for e in es], SUBLANES)

        onehot = jnp.zeros((E, T), F32)
        for sel in sels[s]:
            onehot = onehot + jnp.where(sel, 1.0, 0.0)
        lrank = _dot(onehot.astype(BF16), tri_ref[...])
        cnt = jnp.sum(onehot, axis=1, keepdims=True)
        cnt = jnp.floor((cnt + (SUBLANES - 1)) * (1.0 / SUBLANES)) * SUBLANES
        off = _dot(lowe_ref[...], jnp.broadcast_to(cnt, (E, LANES)).astype(BF16))
        base = jnp.concatenate([off] * (T // LANES), axis=1) + lrank
        lpos_ref[s] = _pack_rows(
            [jnp.sum(jnp.where(sel, base, 0.0), axis=0, keepdims=True) for sel in sels[s]],
            SUBLANES).astype(jnp.int32)
        cnt_ref[s] = cnt.astype(jnp.int32)


def _mix_route(yna, o_f, o_b, hg, x2d, wout_bf16, hgn, ffg, router_w, router_b):
    n = x2d.shape[0]
    T = TOK_TILE
    nt = n // T
    rw_t = router_w.T
    rwh = rw_t.astype(BF16)
    rwl = (rw_t - rwh.astype(F32)).astype(BF16)
    router_b = router_b.reshape(N_EXPERTS, 1)
    tri = jnp.asarray(np.triu(np.ones((T, T), np.float32), 1), BF16)
    lowe = jnp.asarray(np.tril(np.ones((N_EXPERTS, N_EXPERTS), np.float32), -1), BF16)
    row = lambda i: (i, 0)
    const = lambda i: (0, 0)
    tile3 = lambda i: (i, 0, 0)
    K = MIX_TILES_PER_STEP
    R = K * T
    return pl.pallas_call(
        _mix_route_body,
        grid=(nt // K,),
        in_specs=[
            pl.BlockSpec((R, NA_WIDTH), row),
            pl.BlockSpec((R, HG_WIDTH), row),
            pl.BlockSpec((R, HG_WIDTH), row),
            pl.BlockSpec((R, HG_WIDTH), lambda i: (i, 4)),
            pl.BlockSpec((R, D_MODEL), row),
            pl.BlockSpec((D_MODEL, D_MODEL), const),
            pl.BlockSpec((1, HG_WIDTH), const),
            pl.BlockSpec((1, D_MODEL), const),
            pl.BlockSpec((N_EXPERTS, D_MODEL), const),
            pl.BlockSpec((N_EXPERTS, D_MODEL), const),
            pl.BlockSpec((N_EXPERTS, 1), const),
            pl.BlockSpec((T, T), const),
            pl.BlockSpec((N_EXPERTS, N_EXPERTS), const),
        ],
        out_specs=[
            pl.BlockSpec((R, D_MODEL), row),
            pl.BlockSpec((R, D_MODEL), row),
            pl.BlockSpec((K, SUBLANES, T), tile3),
            pl.BlockSpec((K, SUBLANES, T), tile3),
            pl.BlockSpec((K, N_EXPERTS, 1), tile3),
        ],
        out_shape=[
            jax.ShapeDtypeStruct((n, D_MODEL), F32),
            jax.ShapeDtypeStruct((n, D_MODEL), BF16),
            jax.ShapeDtypeStruct((nt, SUBLANES, T), jnp.int32),
            jax.ShapeDtypeStruct((nt, SUBLANES, T), F32),
            jax.ShapeDtypeStruct((nt, N_EXPERTS, 1), jnp.int32),
        ],
        compiler_params=_params("arbitrary"),
        name="mix_route",
    )(yna, o_f, o_b, hg, x2d, wout_bf16, hgn, ffg, rwh, rwl, router_b, tri, lowe)


TILES_PER_STEP = 2


def _wait_rows(make_copy, rows):
    make_copy(0, 0, pl.multiple_of(rows, SUBLANES)).wait()


def _copy_runs(i, cnt_ref, off_ref, goff_ref, make_copy):
    def per_expert(e, carry):
        n = pl.multiple_of(cnt_ref[i * N_EXPERTS + e], SUBLANES)
        lo = pl.multiple_of(off_ref[i * N_EXPERTS + e], SUBLANES)
        go = pl.multiple_of(goff_ref[i * N_EXPERTS + e], SUBLANES)

        @pl.when(n > 0)
        def _start():
            make_copy(lo, go, n).start()

        return carry

    lax.fori_loop(0, N_EXPERTS, per_expert, 0)


def _slot_matrix(lpos, weights=None):
    row = lax.broadcasted_iota(jnp.int32, (STAGE_ROWS, TOK_TILE), 0)
    acc = jnp.zeros((STAGE_ROWS, TOK_TILE), F32)
    for k in range(TOP_K):
        hit = lpos[k:k + 1, :] == row
        val = 1.0 if weights is None else weights[k:k + 1, :]
        acc = jnp.where(hit, val, acc)
    return acc


def _dispatch_body(cnt_ref, off_ref, goff_ref, tot_ref, zoff_ref, zlen_ref, blkrange_ref,
                   xn_ref, lpos_ref, xbuf_ref, stage_ref, zero_ref, sems, zsem):
    i = pl.program_id(0)
    last = pl.num_programs(0) - 1

    def zero_fill(act):
        def tail(e, carry):
            n = pl.multiple_of(zlen_ref[e], SUBLANES)

            @pl.when(n > 0)
            def _():
                go = pl.multiple_of(zoff_ref[e], SUBLANES)
                act(pltpu.make_async_copy(zero_ref.at[pl.ds(0, n)], xbuf_ref.at[pl.ds(go, n)], zsem))

            return carry

        def unused(j, carry):
            go = pl.multiple_of(j * EXPERT_BLOCK, EXPERT_BLOCK)
            act(pltpu.make_async_copy(zero_ref, xbuf_ref.at[pl.ds(go, EXPERT_BLOCK)], zsem))
            return carry

        lax.fori_loop(0, N_EXPERTS, tail, 0)
        lax.fori_loop(blkrange_ref[0], blkrange_ref[1], unused, 0)

    @pl.when(i == 0)
    def _zero_start():
        zero_ref[...] = jnp.zeros_like(zero_ref)
        zero_fill(lambda c: c.start())

    for s in range(TILES_PER_STEP):
        t = i * TILES_PER_STEP + s

        def make_copy(lo, go, size, s=s):
            return pltpu.make_async_copy(stage_ref.at[s, pl.ds(lo, size)], xbuf_ref.at[pl.ds(go, size)], sems.at[s])

        @pl.when(i > 0)
        def _drain():
            _wait_rows(make_copy, tot_ref[t - TILES_PER_STEP])

        rows = slice(s * TOK_TILE, (s + 1) * TOK_TILE)
        perm = _slot_matrix(lpos_ref[s]).astype(BF16)
        stage_ref[s] = _dot(perm, xn_ref[rows, :])
        _copy_runs(t, cnt_ref, off_ref, goff_ref, make_copy)

        @pl.when(i == last)
        def _finish():
            _wait_rows(make_copy, tot_ref[t])

    @pl.when(i == last)
    def _zero_finish():
        zero_fill(lambda c: c.wait())


def _dispatch(cnt, off, goff, tot, zoff, zlen, blkrange, xn, lpos, n_rows):
    n = xn.shape[0]
    rows = TILES_PER_STEP * TOK_TILE
    return pl.pallas_call(
        _dispatch_body,
        grid_spec=pltpu.PrefetchScalarGridSpec(
            num_scalar_prefetch=7,
            grid=(n // rows,),
            in_specs=[
                pl.BlockSpec((rows, D_MODEL), lambda i, *_: (i, 0)),
                pl.BlockSpec((TILES_PER_STEP, SUBLANES, TOK_TILE), lambda i, *_: (i, 0, 0)),
            ],
            out_specs=pl.BlockSpec(memory_space=pl.ANY),
            scratch_shapes=[
                pltpu.VMEM((TILES_PER_STEP, STAGE_ROWS, D_MODEL), F32),
                pltpu.VMEM((EXPERT_BLOCK, D_MODEL), F32),
                pltpu.SemaphoreType.DMA((TILES_PER_STEP,)),
                pltpu.SemaphoreType.DMA(()),
            ],
        ),
        out_shape=jax.ShapeDtypeStruct((n_rows, D_MODEL), F32),
        compiler_params=_params("arbitrary"),
        name="moe_dispatch",
    )(cnt, off, goff, tot, zoff, zlen, blkrange, xn, lpos)


CAST_ROWS = 128


def _experts_body(blk_e_ref, first_ref, slot_ref, next_e_ref, nused_ref,
                  x_ref, bgu_ref, bd_ref, wgu_hbm, wd_hbm, y_ref,
                  wgu_f32, wd_f32, wgu_bf, wd_bf, sem_gu, sem_d):
    j = pl.program_id(0)

    def weight_copies(e, slot):
        return (pltpu.make_async_copy(wgu_hbm.at[e], wgu_f32.at[slot], sem_gu.at[slot]),
                pltpu.make_async_copy(wd_hbm.at[e], wd_f32.at[slot], sem_d.at[slot]))

    @pl.when(j < nused_ref[0])
    def _():
        e = blk_e_ref[j]
        slot = slot_ref[j]

        @pl.when(first_ref[j] == 1)
        def _new_expert():
            @pl.when(j == 0)
            def _():
                for c in weight_copies(e, slot):
                    c.start()

            for c in weight_copies(e, slot):
                c.wait()
            nxt = next_e_ref[j]

            @pl.when(nxt >= 0)
            def _():
                for c in weight_copies(nxt, 1 - slot):
                    c.start(priority=1)

            def cast(src, dst):
                def rows(r, carry):
                    sl = pl.ds(pl.multiple_of(r * CAST_ROWS, CAST_ROWS), CAST_ROWS)
                    dst[sl, :] = src[slot, sl, :].astype(BF16)
                    return carry
                lax.fori_loop(0, src.shape[1] // CAST_ROWS, rows, 0)

            cast(wgu_f32, wgu_bf)
            cast(wd_f32, wd_bf)

        x = x_ref[...].astype(BF16)
        gu = _dot(x, wgu_bf[...]) + bgu_ref[0]
        gate = jnp.minimum(gu[:, :D_FF], SWIGLU_LIMIT)
        up = jnp.clip(gu[:, D_FF:], -SWIGLU_LIMIT, SWIGLU_LIMIT)
        act = (up + 1.0) * gate * jax.nn.sigmoid(SWIGLU_ALPHA * gate)
        y_ref[...] = _dot(act.astype(BF16), wd_bf[...]) + bd_ref[0]


def _experts(blk_e, first, slot, next_e, nused, xbuf, wgu, bgu, wd, bd):
    n_rows = xbuf.shape[0]
    nblk = n_rows // EXPERT_BLOCK

    def rowblk(j, be, fi, sl, ne, nu):
        return (jnp.minimum(j, nu[0] - 1), 0)

    def expert(j, be, fi, sl, ne, nu):
        return (be[jnp.minimum(j, nu[0] - 1)], 0, 0)

    return pl.pallas_call(
        _experts_body,
        grid_spec=pltpu.PrefetchScalarGridSpec(
            num_scalar_prefetch=5,
            grid=(nblk,),
            in_specs=[
                pl.BlockSpec((EXPERT_BLOCK, D_MODEL), rowblk),
                pl.BlockSpec((1, 1, 2 * D_FF), expert),
                pl.BlockSpec((1, 1, D_MODEL), expert),
                pl.BlockSpec(memory_space=pl.ANY),
                pl.BlockSpec(memory_space=pl.ANY),
            ],
            out_specs=pl.BlockSpec((EXPERT_BLOCK, D_MODEL), rowblk),
            scratch_shapes=[
                pltpu.VMEM((2, D_MODEL, 2 * D_FF), F32),
                pltpu.VMEM((2, D_FF, D_MODEL), F32),
                pltpu.VMEM((D_MODEL, 2 * D_FF), BF16),
                pltpu.VMEM((D_FF, D_MODEL), BF16),
                pltpu.SemaphoreType.DMA((2,)),
                pltpu.SemaphoreType.DMA((2,)),
            ],
        ),
        out_shape=jax.ShapeDtypeStruct((n_rows, D_MODEL), F32),
        input_output_aliases={5: 0},
        compiler_params=_params("arbitrary"),
        name="moe_experts",
    )(blk_e, first, slot, next_e, nused, xbuf, bgu, bd, wgu, wd)


def _combine_body(cnt_ref, off_ref, goff_ref, tot_ref, ybuf_ref, lpos_ref, tw_ref, h1_ref, fg_ref, out_ref,
                  stage_ref, sems):
    i = pl.program_id(0)
    n_tiles = pl.num_programs(0) * TILES_PER_STEP

    def copier(s):
        def make_copy(lo, go, size):
            return pltpu.make_async_copy(ybuf_ref.at[pl.ds(go, size)], stage_ref.at[s, pl.ds(lo, size)], sems.at[s])
        return make_copy

    @pl.when(i == 0)
    def _init():
        stage_ref[...] = jnp.zeros_like(stage_ref)
        _copy_runs(0, cnt_ref, off_ref, goff_ref, copier(0))

    for s in range(TILES_PER_STEP):
        t = i * TILES_PER_STEP + s

        @pl.when(t + 1 < n_tiles)
        def _prefetch():
            _copy_runs(t + 1, cnt_ref, off_ref, goff_ref, copier((s + 1) % TILES_PER_STEP))

        rows = slice(s * TOK_TILE, (s + 1) * TOK_TILE)
        w_hi, w_lo = _split_bf16(_slot_matrix(lpos_ref[s], tw_ref[s]))
        w2 = jnp.concatenate([w_hi, w_lo], axis=1)
        _wait_rows(copier(s), tot_ref[t])
        y2 = _dot_tn(w2, stage_ref[s].astype(BF16))
        h2 = h1_ref[rows, :] + (y2[0:TOK_TILE] + y2[TOK_TILE:2 * TOK_TILE])
        ms = jnp.mean(h2 * h2, axis=-1, keepdims=True)
        out_ref[rows, :] = h2 * lax.rsqrt(ms + RMS_EPS) * fg_ref[...]


def _combine(cnt, off, goff, tot, ybuf, lpos, tw, h1, final_g):
    n = h1.shape[0]
    rows = TILES_PER_STEP * TOK_TILE
    row = lambda i, *_: (i, 0)
    return pl.pallas_call(
        _combine_body,
        grid_spec=pltpu.PrefetchScalarGridSpec(
            num_scalar_prefetch=4,
            grid=(n // rows,),
            in_specs=[
                pl.BlockSpec(memory_space=pl.ANY),
                pl.BlockSpec((TILES_PER_STEP, SUBLANES, TOK_TILE), lambda i, *_: (i, 0, 0)),
                pl.BlockSpec((TILES_PER_STEP, SUBLANES, TOK_TILE), lambda i, *_: (i, 0, 0)),
                pl.BlockSpec((rows, D_MODEL), row),
                pl.BlockSpec((1, D_MODEL), lambda i, *_: (0, 0)),
            ],
            out_specs=pl.BlockSpec((rows, D_MODEL), row),
            scratch_shapes=[pltpu.VMEM((TILES_PER_STEP, STAGE_ROWS, D_MODEL), F32),
                            pltpu.SemaphoreType.DMA((TILES_PER_STEP,))],
        ),
        out_shape=jax.ShapeDtypeStruct((n, D_MODEL), F32),
        compiler_params=_params("arbitrary"),
        name="moe_combine",
    )(cnt, off, goff, tot, ybuf, lpos, tw, h1, final_g)


def _moe_layout(cnt_tiles):
    nt = cnt_tiles.shape[0]
    total = jnp.sum(cnt_tiles, axis=0)
    nblk_e = (total + EXPERT_BLOCK - 1) // EXPERT_BLOCK
    blk_end = jnp.cumsum(nblk_e)
    pstart = (blk_end - nblk_e) * EXPERT_BLOCK
    prefix = jnp.cumsum(cnt_tiles, axis=0) - cnt_tiles
    goff = pstart[None, :] + prefix
    off = jnp.cumsum(cnt_tiles, axis=1) - cnt_tiles
    tot = jnp.sum(cnt_tiles, axis=1)
    max_rows = nt * (TOK_TILE * TOP_K + N_EXPERTS * (SUBLANES - 1)) + N_EXPERTS * (EXPERT_BLOCK - 1)
    max_blocks = (max_rows + EXPERT_BLOCK - 1) // EXPERT_BLOCK
    blk_id = jnp.arange(max_blocks, dtype=blk_end.dtype)
    blk_e = jnp.minimum(jnp.sum(blk_end[None, :] <= blk_id[:, None], axis=1), N_EXPERTS - 1).astype(jnp.int32)
    nused = blk_end[-1:].astype(jnp.int32)
    flat = lambda a: a.reshape(-1).astype(jnp.int32)
    zoff = pstart + total
    zlen = nblk_e * EXPERT_BLOCK - total
    blkrange = jnp.stack([nused[0], jnp.int32(max_blocks)])
    used = blk_id < nused[0]
    first = used & jnp.concatenate([jnp.ones((1,), bool), blk_e[1:] != blk_e[:-1]])
    slot = (jnp.cumsum(first.astype(jnp.int32)) - 1) % 2
    eid = jnp.arange(N_EXPERTS)
    later = (eid[None, :] > eid[:, None]) & (nblk_e > 0)[None, :]
    next_of = jnp.min(jnp.where(later, eid[None, :], N_EXPERTS), axis=1)
    next_of = jnp.where(next_of < N_EXPERTS, next_of, -1)
    experts_plan = (blk_e, first.astype(jnp.int32), slot.astype(jnp.int32), next_of[blk_e].astype(jnp.int32), nused)
    dispatch_plan = (flat(cnt_tiles), flat(off), flat(goff), flat(tot))
    zero_plan = (flat(zoff), flat(zlen), blkrange.astype(jnp.int32))
    return dispatch_plan, zero_plan, experts_plan, max_blocks * EXPERT_BLOCK


def kernel(x, meta_tokens, attn_norm_g, w_in, na_rpb, na_norm_g, hgrn_lb_logits, hgrn_norm_g, w_out, ffn_norm_g,
           router_w, router_b, expert_w_gu, expert_b_gu, expert_w_down, expert_b_down, final_norm_g):
    B, T, D = x.shape
    x2d = x.reshape(B * T, D)
    w_in_b = w_in[0].astype(BF16)
    ng = attn_norm_g[0].reshape(1, D)
    lbl = hgrn_lb_logits[:, :, :].reshape(4, HG_WIDTH)

    att, hg = _inproj(x2d, ng, w_in_b, lbl, 512)
    att_m, hg_m = _inproj(meta_tokens.astype(F32), ng, w_in_b, lbl, N_META)
    att_m = jnp.pad(att_m, ((0, META_PAD - N_META), (0, 0)))
    hg_m = jnp.pad(hg_m, ((0, HG_CHUNK - N_META), (0, 0)))

    yna = _natten(att, att_m, _natten_bias_tables(na_rpb[0]), na_norm_g[0].reshape(1, NA_WIDTH), B, T)
    o_f, o_b = _hgrn(hg, hg_m, B, T)

    h1, xn, lpos, tw, cnt_tiles = _mix_route(
        yna.reshape(B * T, NA_WIDTH), o_f.reshape(B * T, HG_WIDTH), o_b.reshape(B * T, HG_WIDTH), hg, x2d,
        w_out[0].astype(BF16), hgrn_norm_g[0].reshape(1, HG_WIDTH), ffn_norm_g[0].reshape(1, D),
        router_w[0], router_b[0].reshape(1, N_EXPERTS))

    dispatch_plan, zero_plan, experts_plan, n_rows = _moe_layout(cnt_tiles[:, :, 0])
    xbuf = _dispatch(*dispatch_plan, *zero_plan, xn, lpos, n_rows)
    ybuf = _experts(*experts_plan, xbuf, expert_w_gu[0], expert_b_gu[0][:, None, :],
                    expert_w_down[0], expert_b_down[0][:, None, :])
    out = _combine(*dispatch_plan, ybuf, lpos, tw, h1, final_norm_g.reshape(1, D))
    return out.reshape(B, T, D)
```

```python
import functools

import numpy as np
import jax
import jax.numpy as jnp
from jax import lax
from jax.experimental import pallas as pl
from jax.experimental.pallas import tpu as pltpu

F32 = jnp.float32
BF16 = jnp.bfloat16

D_MODEL = 1024
N_META = 16
GRID_W = 64
NA_WIDTH = 512
NA_HEAD_DIM = 64
NA_HEADS = 8
NA_KH = 8
NA_KW = 16
HG_WIDTH = 512
HG_HEAD_DIM = 128
HG_HEADS = 4
HG_CHUNK = 64
IN_COLS = 3 * NA_WIDTH + 5 * HG_WIDTH
N_EXPERTS = 32
TOP_K = 4
D_FF = 1024
SWIGLU_LIMIT = 7.0
SWIGLU_ALPHA = 1.702
RMS_EPS = 1e-6
NEG_BIG = -1e30
LOG2_E = 1.4426950408889634

LANES = 128
VMEM_LIMIT_BYTES = 56 * 1024 * 1024

TOK_TILE = 256
EXPERT_BLOCK = 512
SUBLANES = 8
STAGE_ROWS = TOK_TILE * TOP_K + N_EXPERTS * SUBLANES
N_HG_LEVELS = 6


def _dot(a, b):
    return jnp.dot(a, b, preferred_element_type=F32)


def _dot_nt(a, b):
    return lax.dot_general(a, b, (((1,), (1,)), ((), ())), preferred_element_type=F32)


def _dot_tn(a, b):
    return lax.dot_general(a, b, (((0,), (0,)), ((), ())), preferred_element_type=F32)


def _split_bf16(x):
    hi = x.astype(BF16)
    lo = (x - hi.astype(F32)).astype(BF16)
    return hi, lo


def _pack_rows(rows, n_rows):
    idx = lax.broadcasted_iota(jnp.int32, (n_rows, rows[0].shape[1]), 0)
    out = jnp.zeros(idx.shape, rows[0].dtype)
    for k, r in enumerate(rows):
        out = jnp.where(idx == k, r, out)
    return out


def _params(*sem):
    return pltpu.CompilerParams(dimension_semantics=sem, vmem_limit_bytes=VMEM_LIMIT_BYTES)


def _inproj_body(x_ref, ng_ref, w_ref, lbl_ref, att_ref, hg_ref):
    x = x_ref[...]
    ms = jnp.mean(x * x, axis=-1, keepdims=True)
    n = (x * lax.rsqrt(ms + RMS_EPS) * ng_ref[...]).astype(BF16)

    def proj(lo, hi):
        return _dot(n, w_ref[:, lo:hi])

    pq = proj(0, NA_WIDTH)
    att_ref[:, 0:NA_WIDTH] = (pq * (NA_HEAD_DIM ** -0.5 * LOG2_E)).astype(BF16)
    att_ref[:, NA_WIDTH:3 * NA_WIDTH] = proj(NA_WIDTH, 3 * NA_WIDTH).astype(BF16)

    base = 3 * NA_WIDTH
    W = HG_WIDTH
    qh = proj(base, base + W)
    hg_ref[:, 0:W] = qh * jax.nn.sigmoid(qh)
    hg_ref[:, W:2 * W] = proj(base + W, base + 2 * W)
    lbl = lbl_ref[...]
    for d in range(2):
        a0 = lbl[2 * d:2 * d + 1, :]
        a1 = lbl[2 * d + 1:2 * d + 2, :]
        m = jnp.maximum(a0, a1)
        e0 = jnp.exp(a0 - m)
        e1 = jnp.exp(a1 - m)
        lb = e0 / (e0 + e1)
        raw = proj(base + (2 + d) * W, base + (3 + d) * W)
        f = lb + (1.0 - lb) * jax.nn.sigmoid(raw)
        hg_ref[:, (2 + d) * W:(3 + d) * W] = jnp.log(f)
    gt = proj(base + 4 * W, base + 5 * W)
    hg_ref[:, 4 * W:5 * W] = gt * jax.nn.sigmoid(gt)


def _inproj(x2d, norm_g, w_bf16, lb_logits4, tm):
    n = x2d.shape[0]
    return pl.pallas_call(
        _inproj_body,
        grid=(n // tm,),
        in_specs=[
            pl.BlockSpec((tm, D_MODEL), lambda i: (i, 0)),
            pl.BlockSpec((1, D_MODEL), lambda i: (0, 0)),
            pl.BlockSpec((D_MODEL, IN_COLS), lambda i: (0, 0)),
            pl.BlockSpec((4, HG_WIDTH), lambda i: (0, 0)),
        ],
        out_specs=[
            pl.BlockSpec((tm, 3 * NA_WIDTH), lambda i: (i, 0)),
            pl.BlockSpec((tm, 5 * HG_WIDTH), lambda i: (i, 0)),
        ],
        out_shape=[
            jax.ShapeDtypeStruct((n, 3 * NA_WIDTH), BF16),
            jax.ShapeDtypeStruct((n, 5 * HG_WIDTH), F32),
        ],
        compiler_params=_params("arbitrary"),
        name="inproj",
    )(x2d, norm_g, w_bf16, lb_logits4)


N_WIN_KEYS = NA_KH * GRID_W
META_PAD = LANES


NA_GROUP = 4
GROUP_LANES = NA_GROUP * NA_HEAD_DIM


NA_ROWS_PER_STEP = 8


def _natten_row(rr, r, q_ref, k_ref, v_ref, km_ref, vm_ref, bias_ref, *, rows):
    rs = jnp.clip(r - NA_KH // 2, 0, rows - NA_KH)
    d0 = rs - r + (NA_KH - 1)
    start = pl.multiple_of(rs * GRID_W, GRID_W)
    lane_head = lax.broadcasted_iota(jnp.int32, (GRID_W, GROUP_LANES), 1) // NA_HEAD_DIM
    meta_col = lax.broadcasted_iota(jnp.int32, (1, META_PAD), 1)
    meta_bias = jnp.where(meta_col < N_META, 0.0, NEG_BIG).astype(F32)
    folded = []
    for g in range(NA_HEADS // NA_GROUP):
        sl = slice(g * GROUP_LANES, (g + 1) * GROUP_LANES)
        q4 = q_ref[0, rr * GRID_W:(rr + 1) * GRID_W, sl]
        zero = jnp.zeros_like(q4)
        qm = jnp.concatenate([jnp.where(lane_head == h, q4, zero) for h in range(NA_GROUP)], axis=0)
        kw = k_ref[0, pl.ds(start, N_WIN_KEYS), sl]
        vw = v_ref[0, pl.ds(start, N_WIN_KEYS), sl]
        hs = slice(g * NA_GROUP, (g + 1) * NA_GROUP)
        bias = jnp.concatenate(
            [bias_ref[d0 + 2 * j, hs].reshape(NA_GROUP * GRID_W, 2 * GRID_W) for j in range(NA_KH // 2)], axis=-1)
        s = _dot_nt(qm, kw) + bias
        sm = _dot_nt(qm, km_ref[:, sl]) + meta_bias
        m = jnp.maximum(jnp.max(s, axis=-1, keepdims=True), jnp.max(sm, axis=-1, keepdims=True))
        e = jnp.exp2(s - m)
        em = jnp.exp2(sm - m)
        den = jnp.sum(e, axis=-1, keepdims=True) + jnp.sum(em, axis=-1, keepdims=True)
        o = (_dot(e.astype(BF16), vw) + _dot(em.astype(BF16), vm_ref[:, sl])) * (1.0 / den)
        acc = jnp.where(lane_head == 0, o[0:GRID_W], 0.0)
        for h in range(1, NA_GROUP):
            acc = jnp.where(lane_head == h, o[h * GRID_W:(h + 1) * GRID_W], acc)
        folded.append(acc)
    return jnp.concatenate(folded, axis=-1)


def _natten_body(q_ref, k_ref, v_ref, km_ref, vm_ref, bias_ref, ng_ref, bd_ref, o_ref, *, rows):
    r0 = pl.program_id(1) * NA_ROWS_PER_STEP
    o2 = jnp.concatenate(
        [_natten_row(rr, r0 + rr, q_ref, k_ref, v_ref, km_ref, vm_ref, bias_ref, rows=rows)
         for rr in range(NA_ROWS_PER_STEP)], axis=0)
    sq_hi, sq_lo = _split_bf16(o2 * o2)
    ms = (_dot(sq_hi, bd_ref[...]) + _dot(sq_lo, bd_ref[...])) * (1.0 / NA_HEAD_DIM)
    o_ref[0] = (o2 * lax.rsqrt(ms + RMS_EPS) * ng_ref[...]).astype(o_ref.dtype)


def _natten_bias_tables(rpb):
    c = np.arange(GRID_W)[:, None]
    kc = np.arange(GRID_W)[None, :]
    cs = np.clip(c - NA_KW // 2, 0, GRID_W - NA_KW)
    valid = (kc >= cs) & (kc < cs + NA_KW)
    dc = kc - c + (NA_KW - 1)
    pick = np.asarray(dc[None] == np.arange(2 * NA_KW - 1)[:, None, None], np.float32)
    t1 = jnp.einsum("hdj,jck->hdck", rpb.astype(F32) * LOG2_E, pick, precision=lax.Precision.HIGHEST)
    t1 = jnp.where(valid[None, None], t1, NEG_BIG)
    pairs = jnp.concatenate([t1[:, :-1], t1[:, 1:]], axis=-1)
    return pairs.transpose(1, 0, 2, 3)


def _natten(att, att_meta, bias_tabs, norm_g, batch, seq):
    rows = seq // GRID_W
    att3 = att.reshape(batch, seq, 3 * NA_WIDTH)
    head_of = np.arange(NA_WIDTH) // NA_HEAD_DIM
    same_head = jnp.asarray(head_of[:, None] == head_of[None, :], BF16)

    qrows = NA_ROWS_PER_STEP * GRID_W
    return pl.pallas_call(
        functools.partial(_natten_body, rows=rows),
        grid=(batch, rows // NA_ROWS_PER_STEP),
        in_specs=[
            pl.BlockSpec((1, qrows, NA_WIDTH), lambda b, r: (b, r, 0)),
            pl.BlockSpec((1, seq, NA_WIDTH), lambda b, r: (b, 0, 1)),
            pl.BlockSpec((1, seq, NA_WIDTH), lambda b, r: (b, 0, 2)),
            pl.BlockSpec((META_PAD, NA_WIDTH), lambda b, r: (0, 1)),
            pl.BlockSpec((META_PAD, NA_WIDTH), lambda b, r: (0, 2)),
            pl.BlockSpec(bias_tabs.shape, lambda b, r: (0, 0, 0, 0), pipeline_mode=pl.Buffered(1)),
            pl.BlockSpec((1, NA_WIDTH), lambda b, r: (0, 0)),
            pl.BlockSpec((NA_WIDTH, NA_WIDTH), lambda b, r: (0, 0)),
        ],
        out_specs=pl.BlockSpec((1, qrows, NA_WIDTH), lambda b, r: (b, r, 0)),
        out_shape=jax.ShapeDtypeStruct((batch, seq, NA_WIDTH), BF16),
        compiler_params=_params("arbitrary", "arbitrary"),
        name="natten",
    )(att3, att3, att3, att_meta, att_meta, bias_tabs, norm_g, same_head)


HG_CHUNKS_PER_STEP = 4
ROW_B, ROW_LV = 0, 1
N_EXP_BLOCKS = 1 + N_HG_LEVELS


def _hgrn_constants():
    C = HG_CHUNK
    t = np.arange(C)
    u = t[None, :]
    mats = [u <= t[:, None]]
    masks = [np.eye(C, dtype=bool)]
    for lv in range(N_HG_LEVELS):
        m = 1 << lv
        blk = t // (2 * m)
        upper = (t // m) % 2 == 1
        p = blk * 2 * m + m - 1
        q_rows = upper[:, None] & (u > p[:, None]) & (u <= t[:, None])
        k_rows = (~upper)[:, None] & (u > t[:, None]) & (u <= p[:, None])
        mats.append(q_rows | k_rows)
        masks.append((blk[:, None] == blk[None, :]) & upper[:, None] & (~upper)[None, :])
    fwd = np.concatenate(mats, axis=0).astype(np.float32)
    fwd_mask = np.stack(masks).astype(np.float32)
    blocks = fwd.reshape(N_EXP_BLOCKS, C, C)
    bwd = blocks[:, ::-1, ::-1].reshape(N_EXP_BLOCKS * C, C)
    bwd_mask = fwd_mask[:, ::-1, ::-1]
    both = np.stack([fwd, bwd])
    both = np.concatenate([both, both], axis=-1)
    both_mask = np.stack([fwd_mask, bwd_mask])
    both_mask = np.concatenate([both_mask, both_mask], axis=-1)
    return (jnp.asarray(both, BF16), jnp.asarray(both_mask, F32))


def _hgrn_exponents(mat, g):
    g_hi, g_lo = _split_bf16(g)
    return _dot(mat, jnp.concatenate([g_hi, g_lo], axis=0))


def _hgrn_body(hf_ref, hb_ref, meta_ref, mat_ref, mask_ref, of_ref, ob_ref, stf_ref, stb_ref):
    C, W = HG_CHUNK, HG_WIDTH
    c = pl.program_id(1)

    @pl.when(c == 0)
    def _init():
        mb = meta_ref[...]
        v = mb[:, W:2 * W].astype(BF16)
        prefix = mat_ref[0, ROW_B * C:(ROW_B + 1) * C, :]
        for d, st_ref in ((0, stf_ref), (1, stb_ref)):
            g = mb[:, (2 + d) * W:(3 + d) * W]
            b = _hgrn_exponents(prefix, g)
            kb = ((1.0 - jnp.exp(g)) * jnp.exp(b[C - 1:C] - b)).astype(BF16)
            for h in range(HG_HEADS):
                sl = slice(h * HG_HEAD_DIM, (h + 1) * HG_HEAD_DIM)
                st_ref[h] = _dot_tn(v[:, sl], kb[:, sl])

    n = HG_CHUNKS_PER_STEP
    dir_refs = ((hf_ref, 2 * W, stf_ref, C - 1, of_ref), (hb_ref, 3 * W, stb_ref, 0, ob_ref))
    tasks = [(0, j * C) for j in range(n)] + [(1, (n - 1 - j) * C) for j in range(n)]
    heads = [slice(h * HG_HEAD_DIM, (h + 1) * HG_HEAD_DIM) for h in range(HG_HEADS)]
    qs, ks, vs, exs, ex_ends = {}, {}, {}, {}, {}
    for d, r0 in tasks:
        ref, g_col, total_row = dir_refs[d][0], dir_refs[d][1], dir_refs[d][3]
        g = ref[0, r0:r0 + C, g_col:g_col + W]
        sums = _hgrn_exponents(mat_ref[d], g)
        exs[d, r0] = jnp.exp(sums)
        ex_ends[d, r0] = jnp.exp(sums[total_row:total_row + 1] - sums[0:C])
        qs[d, r0] = ref[0, r0:r0 + C, 0:W]
        ks[d, r0] = 1.0 - jnp.exp(g)
        vs[d, r0] = ref[0, r0:r0 + C, W:2 * W].astype(BF16)

    def ex_rows(task, i, sl):
        return exs[task][i * C:(i + 1) * C, sl]

    D2 = 2 * HG_HEAD_DIM
    pairs = [slice(p * D2, (p + 1) * D2) for p in range(HG_HEADS // 2)]

    def block_diag(x):
        zero = jnp.zeros((C, HG_HEAD_DIM), x.dtype)
        return jnp.concatenate([jnp.concatenate([x[:, :HG_HEAD_DIM], zero], axis=1),
                                jnp.concatenate([zero, x[:, HG_HEAD_DIM:]], axis=1)], axis=0)

    prods = {}
    for task in tasks:
        for p, sl in enumerate(pairs):
            q2, k2 = qs[task][:, sl], ks[task][:, sl]
            ps = [_dot_nt(q2.astype(BF16), block_diag(k2.astype(BF16)))]
            for lv in range(N_HG_LEVELS):
                scale = ex_rows(task, ROW_LV + lv, sl)
                ps.append(_dot_nt((q2 * scale).astype(BF16), block_diag((k2 * scale).astype(BF16))))
            prods[task, p] = ps
    attn, qbs, kbs = {}, {}, {}
    for task in tasks:
        d = task[0]
        for p, sl in enumerate(pairs):
            a = prods[task, p][0] * mask_ref[d, 0]
            for lv in range(N_HG_LEVELS):
                a = a + prods[task, p][1 + lv] * mask_ref[d, 1 + lv]
            o2 = _dot(a.astype(BF16), block_diag(vs[task][:, sl]))
            attn[task, 2 * p] = o2[:, :HG_HEAD_DIM]
            attn[task, 2 * p + 1] = o2[:, HG_HEAD_DIM:]
        for h, sl in enumerate(heads):
            qbs[task, h] = (qs[task][:, sl] * ex_rows(task, ROW_B, sl)).astype(BF16)
            kbs[task, h] = _dot_tn(vs[task][:, sl], (ks[task][:, sl] * ex_ends[task][:, sl]).astype(BF16))
    states = {(d, h): dir_refs[d][2][h] for d in range(2) for h in range(HG_HEADS)}
    for task in tasks:
        d, r0 = task
        total_row, o_ref = dir_refs[d][3], dir_refs[d][4]
        outs = []
        for h, sl in enumerate(heads):
            st = states[d, h]
            outs.append(attn[task, h] + _dot_nt(qbs[task, h], st.astype(BF16)))
            decay = exs[task][ROW_B * C + total_row:ROW_B * C + total_row + 1, sl]
            states[d, h] = decay * st + kbs[task, h]
        o_ref[0, r0:r0 + C, :] = jnp.concatenate(outs, axis=-1)
    for d in range(2):
        for h in range(HG_HEADS):
            dir_refs[d][2][h] = states[d, h]


def _hgrn(hg, hg_meta_pad, batch, seq):
    nc = seq // (HG_CHUNK * HG_CHUNKS_PER_STEP)
    hg3 = hg.reshape(batch, seq, 5 * HG_WIDTH)
    mats, masks = _hgrn_constants()
    blk = (1, HG_CHUNK * HG_CHUNKS_PER_STEP, 5 * HG_WIDTH)
    oblk = (1, HG_CHUNK * HG_CHUNKS_PER_STEP, HG_WIDTH)
    return pl.pallas_call(
        _hgrn_body,
        grid=(batch, nc),
        in_specs=[
            pl.BlockSpec(blk, lambda b, c: (b, c, 0)),
            pl.BlockSpec(blk, lambda b, c: (b, nc - 1 - c, 0)),
            pl.BlockSpec((HG_CHUNK, 5 * HG_WIDTH), lambda b, c: (0, 0)),
            pl.BlockSpec(mats.shape, lambda b, c: (0, 0, 0)),
            pl.BlockSpec(masks.shape, lambda b, c: (0, 0, 0, 0)),
        ],
        out_specs=[
            pl.BlockSpec(oblk, lambda b, c: (b, c, 0)),
            pl.BlockSpec(oblk, lambda b, c: (b, nc - 1 - c, 0)),
        ],
        out_shape=[jax.ShapeDtypeStruct((batch, seq, HG_WIDTH), F32)] * 2,
        scratch_shapes=[pltpu.VMEM((HG_HEADS, HG_HEAD_DIM, HG_HEAD_DIM), F32)] * 2,
        compiler_params=_params("arbitrary", "arbitrary"),
        name="hgrn",
    )(hg3, hg3, hg_meta_pad, mats, masks)


MIX_TILES_PER_STEP = 4


def _mix_route_body(yna_ref, of_ref, ob_ref, gate_ref, x_ref, wout_ref, hgn_ref, ffg_ref,
                    rwh_ref, rwl_ref, rb_ref, tri_ref, lowe_ref,
                    h1_ref, xn_ref, lpos_ref, tw_ref, cnt_ref):
    T, E = TOK_TILE, N_EXPERTS
    tiles = [slice(s * T, (s + 1) * T) for s in range(MIX_TILES_PER_STEP)]
    x_his, x_los = [], []
    for rows in tiles:
        o = of_ref[rows, :] + ob_ref[rows, :]
        parts = []
        for h in range(HG_HEADS):
            seg = o[:, h * HG_HEAD_DIM:(h + 1) * HG_HEAD_DIM]
            ms = jnp.mean(seg * seg, axis=-1, keepdims=True)
            parts.append(seg * lax.rsqrt(ms + RMS_EPS))
        yhg = jnp.concatenate(parts, axis=-1) * hgn_ref[...] * gate_ref[rows, :]
        mix = _dot(yna_ref[rows, :], wout_ref[0:NA_WIDTH, :]) + _dot(yhg.astype(BF16), wout_ref[NA_WIDTH:, :])
        h1 = x_ref[rows, :] + mix
        h1_ref[rows, :] = h1
        ms = jnp.mean(h1 * h1, axis=-1, keepdims=True)
        xn = h1 * lax.rsqrt(ms + RMS_EPS) * ffg_ref[...]
        x_hi, x_lo = _split_bf16(xn)
        xn_ref[rows, :] = x_hi
        x_his.append(x_hi)
        x_los.append(x_lo)

    curs = [(_dot_nt(rwh_ref[...], x_hi) + _dot_nt(rwh_ref[...], x_lo) + _dot_nt(rwl_ref[...], x_hi)) + rb_ref[...]
            for x_hi, x_lo in zip(x_his, x_los)]
    row = lax.broadcasted_iota(jnp.int32, (E, T), 0).astype(F32)
    sels = [[] for _ in tiles]
    vals = [[] for _ in tiles]
    for _ in range(TOP_K):
        for s in range(len(tiles)):
            m = jnp.max(curs[s], axis=0, keepdims=True)
            first = jnp.min(jnp.where(curs[s] == m, row, float(E)), axis=0, keepdims=True)
            sel = row == first
            sels[s].append(sel)
            vals[s].append(m)
            curs[s] = jnp.where(sel, -jnp.inf, curs[s])
    for s in range(len(tiles)):
        es = [jnp.exp(vk - vals[s][0]) for vk in vals[s]]
        den = es[0] + es[1] + es[2] + es[3]
        tw_ref[s] = _pack_rows([e / den for e in es], SUBLANES)

        onehot = jnp.zeros((E, T), F32)
        for sel in sels[s]:
            onehot = onehot + jnp.where(sel, 1.0, 0.0)
        lrank = _dot(onehot.astype(BF16), tri_ref[...])
        cnt = jnp.sum(onehot, axis=1, keepdims=True)
        cnt = jnp.floor((cnt + (SUBLANES - 1)) * (1.0 / SUBLANES)) * SUBLANES
        off = _dot(lowe_ref[...], jnp.broadcast_to(cnt, (E, LANES)).astype(BF16))
        base = jnp.concatenate([off] * (T // LANES), axis=1) + lrank
        lpos_ref[s] = _pack_rows(
            [jnp.sum(jnp.where(sel, base, 0.0), axis=0, keepdims=True) for sel in sels[s]],
            SUBLANES).astype(jnp.int32)
        cnt_ref[s] = cnt.astype(jnp.int32)


def _mix_route(yna, o_f, o_b, hg, x2d, wout_bf16, hgn, ffg, router_w, router_b):
    n = x2d.shape[0]
    T = TOK_TILE
    nt = n // T
    rw_t = router_w.T
    rwh = rw_t.astype(BF16)
    rwl = (rw_t - rwh.astype(F32)).astype(BF16)
    router_b = router_b.reshape(N_EXPERTS, 1)
    tri = jnp.asarray(np.triu(np.ones((T, T), np.float32), 1), BF16)
    lowe = jnp.asarray(np.tril(np.ones((N_EXPERTS, N_EXPERTS), np.float32), -1), BF16)
    row = lambda i: (i, 0)
    const = lambda i: (0, 0)
    tile3 = lambda i: (i, 0, 0)
    K = MIX_TILES_PER_STEP
    R = K * T
    return pl.pallas_call(
        _mix_route_body,
        grid=(nt // K,),
        in_specs=[
            pl.BlockSpec((R, NA_WIDTH), row),
            pl.BlockSpec((R, HG_WIDTH), row),
            pl.BlockSpec((R, HG_WIDTH), row),
            pl.BlockSpec((R, HG_WIDTH), lambda i: (i, 4)),
            pl.BlockSpec((R, D_MODEL), row),
            pl.BlockSpec((D_MODEL, D_MODEL), const),
            pl.BlockSpec((1, HG_WIDTH), const),
            pl.BlockSpec((1, D_MODEL), const),
            pl.BlockSpec((N_EXPERTS, D_MODEL), const),
            pl.BlockSpec((N_EXPERTS, D_MODEL), const),
            pl.BlockSpec((N_EXPERTS, 1), const),
            pl.BlockSpec((T, T), const),
            pl.BlockSpec((N_EXPERTS, N_EXPERTS), const),
        ],
        out_specs=[
            pl.BlockSpec((R, D_MODEL), row),
            pl.BlockSpec((R, D_MODEL), row),
            pl.BlockSpec((K, SUBLANES, T), tile3),
            pl.BlockSpec((K, SUBLANES, T), tile3),
            pl.BlockSpec((K, N_EXPERTS, 1), tile3),
        ],
        out_shape=[
            jax.ShapeDtypeStruct((n, D_MODEL), F32),
            jax.ShapeDtypeStruct((n, D_MODEL), BF16),
            jax.ShapeDtypeStruct((nt, SUBLANES, T), jnp.int32),
            jax.ShapeDtypeStruct((nt, SUBLANES, T), F32),
            jax.ShapeDtypeStruct((nt, N_EXPERTS, 1), jnp.int32),
        ],
        compiler_params=_params("arbitrary"),
        name="mix_route",
    )(yna, o_f, o_b, hg, x2d, wout_bf16, hgn, ffg, rwh, rwl, router_b, tri, lowe)


TILES_PER_STEP = 2


def _wait_rows(make_copy, rows):
    make_copy(0, 0, pl.multiple_of(rows, SUBLANES)).wait()


def _copy_runs(i, cnt_ref, off_ref, goff_ref, make_copy):
    def per_expert(e, carry):
        n = pl.multiple_of(cnt_ref[i * N_EXPERTS + e], SUBLANES)
        lo = pl.multiple_of(off_ref[i * N_EXPERTS + e], SUBLANES)
        go = pl.multiple_of(goff_ref[i * N_EXPERTS + e], SUBLANES)

        @pl.when(n > 0)
        def _start():
            make_copy(lo, go, n).start()

        return carry

    lax.fori_loop(0, N_EXPERTS, per_expert, 0)


def _slot_matrix(lpos, weights=None):
    row = lax.broadcasted_iota(jnp.int32, (STAGE_ROWS, TOK_TILE), 0)
    acc = jnp.zeros((STAGE_ROWS, TOK_TILE), F32)
    for k in range(TOP_K):
        hit = lpos[k:k + 1, :] == row
        val = 1.0 if weights is None else weights[k:k + 1, :]
        acc = jnp.where(hit, val, acc)
    return acc


def _dispatch_body(cnt_ref, off_ref, goff_ref, tot_ref, zoff_ref, zlen_ref, blkrange_ref,
                   xn_ref, lpos_ref, xbuf_ref, stage_ref, zero_ref, sems, zsem):
    i = pl.program_id(0)
    last = pl.num_programs(0) - 1

    def zero_fill(act):
        def tail(e, carry):
            n = pl.multiple_of(zlen_ref[e], SUBLANES)

            @pl.when(n > 0)
            def _():
                go = pl.multiple_of(zoff_ref[e], SUBLANES)
                act(pltpu.make_async_copy(zero_ref.at[pl.ds(0, n)], xbuf_ref.at[pl.ds(go, n)], zsem))

            return carry

        def unused(j, carry):
            go = pl.multiple_of(j * EXPERT_BLOCK, EXPERT_BLOCK)
            act(pltpu.make_async_copy(zero_ref, xbuf_ref.at[pl.ds(go, EXPERT_BLOCK)], zsem))
            return carry

        lax.fori_loop(0, N_EXPERTS, tail, 0)
        lax.fori_loop(blkrange_ref[0], blkrange_ref[1], unused, 0)

    @pl.when(i == 0)
    def _zero_start():
        zero_ref[...] = jnp.zeros_like(zero_ref)
        zero_fill(lambda c: c.start())

    for s in range(TILES_PER_STEP):
        t = i * TILES_PER_STEP + s

        def make_copy(lo, go, size, s=s):
            return pltpu.make_async_copy(stage_ref.at[s, pl.ds(lo, size)], xbuf_ref.at[pl.ds(go, size)], sems.at[s])

        @pl.when(i > 0)
        def _drain():
            _wait_rows(make_copy, tot_ref[t - TILES_PER_STEP])

        rows = slice(s * TOK_TILE, (s + 1) * TOK_TILE)
        perm = _slot_matrix(lpos_ref[s]).astype(BF16)
        stage_ref[s] = _dot(perm, xn_ref[rows, :])
        _copy_runs(t, cnt_ref, off_ref, goff_ref, make_copy)

        @pl.when(i == last)
        def _finish():
            _wait_rows(make_copy, tot_ref[t])

    @pl.when(i == last)
    def _zero_finish():
        zero_fill(lambda c: c.wait())


def _dispatch(cnt, off, goff, tot, zoff, zlen, blkrange, xn, lpos, n_rows):
    n = xn.shape[0]
    rows = TILES_PER_STEP * TOK_TILE
    return pl.pallas_call(
        _dispatch_body,
        grid_spec=pltpu.PrefetchScalarGridSpec(
            num_scalar_prefetch=7,
            grid=(n // rows,),
            in_specs=[
                pl.BlockSpec((rows, D_MODEL), lambda i, *_: (i, 0)),
                pl.BlockSpec((TILES_PER_STEP, SUBLANES, TOK_TILE), lambda i, *_: (i, 0, 0)),
            ],
            out_specs=pl.BlockSpec(memory_space=pl.ANY),
            scratch_shapes=[
                pltpu.VMEM((TILES_PER_STEP, STAGE_ROWS, D_MODEL), F32),
                pltpu.VMEM((EXPERT_BLOCK, D_MODEL), F32),
                pltpu.SemaphoreType.DMA((TILES_PER_STEP,)),
                pltpu.SemaphoreType.DMA(()),
            ],
        ),
        out_shape=jax.ShapeDtypeStruct((n_rows, D_MODEL), F32),
        compiler_params=_params("arbitrary"),
        name="moe_dispatch",
    )(cnt, off, goff, tot, zoff, zlen, blkrange, xn, lpos)


CAST_ROWS = 128


def _experts_body(blk_e_ref, first_ref, slot_ref, next_e_ref, nused_ref,
                  x_ref, bgu_ref, bd_ref, wgu_hbm, wd_hbm, y_ref,
                  wgu_f32, wd_f32, wgu_bf, wd_bf, sem_gu, sem_d):
    j = pl.program_id(0)

    def weight_copies(e, slot):
        return (pltpu.make_async_copy(wgu_hbm.at[e], wgu_f32.at[slot], sem_gu.at[slot]),
                pltpu.make_async_copy(wd_hbm.at[e], wd_f32.at[slot], sem_d.at[slot]))

    @pl.when(j < nused_ref[0])
    def _():
        e = blk_e_ref[j]
        slot = slot_ref[j]

        @pl.when(first_ref[j] == 1)
        def _new_expert():
            @pl.when(j == 0)
            def _():
                for c in weight_copies(e, slot):
                    c.start()

            for c in weight_copies(e, slot):
                c.wait()
            nxt = next_e_ref[j]

            @pl.when(nxt >= 0)
            def _():
                for c in weight_copies(nxt, 1 - slot):
                    c.start(priority=1)

            def cast(src, dst):
                def rows(r, carry):
                    sl = pl.ds(pl.multiple_of(r * CAST_ROWS, CAST_ROWS), CAST_ROWS)
                    dst[sl, :] = src[slot, sl, :].astype(BF16)
                    return carry
                lax.fori_loop(0, src.shape[1] // CAST_ROWS, rows, 0)

            cast(wgu_f32, wgu_bf)
            cast(wd_f32, wd_bf)

        x = x_ref[...].astype(BF16)
        gu = _dot(x, wgu_bf[...]) + bgu_ref[0]
        gate = jnp.minimum(gu[:, :D_FF], SWIGLU_LIMIT)
        up = jnp.clip(gu[:, D_FF:], -SWIGLU_LIMIT, SWIGLU_LIMIT)
        act = (up + 1.0) * gate * jax.nn.sigmoid(SWIGLU_ALPHA * gate)
        y_ref[...] = _dot(act.astype(BF16), wd_bf[...]) + bd_ref[0]


def _experts(blk_e, first, slot, next_e, nused, xbuf, wgu, bgu, wd, bd):
    n_rows = xbuf.shape[0]
    nblk = n_rows // EXPERT_BLOCK

    def rowblk(j, be, fi, sl, ne, nu):
        return (jnp.minimum(j, nu[0] - 1), 0)

    def expert(j, be, fi, sl, ne, nu):
        return (be[jnp.minimum(j, nu[0] - 1)], 0, 0)

    return pl.pallas_call(
        _experts_body,
        grid_spec=pltpu.PrefetchScalarGridSpec(
            num_scalar_prefetch=5,
            grid=(nblk,),
            in_specs=[
                pl.BlockSpec((EXPERT_BLOCK, D_MODEL), rowblk),
                pl.BlockSpec((1, 1, 2 * D_FF), expert),
                pl.BlockSpec((1, 1, D_MODEL), expert),
                pl.BlockSpec(memory_space=pl.ANY),
                pl.BlockSpec(memory_space=pl.ANY),
            ],
            out_specs=pl.BlockSpec((EXPERT_BLOCK, D_MODEL), rowblk),
            scratch_shapes=[
                pltpu.VMEM((2, D_MODEL, 2 * D_FF), F32),
                pltpu.VMEM((2, D_FF, D_MODEL), F32),
                pltpu.VMEM((D_MODEL, 2 * D_FF), BF16),
                pltpu.VMEM((D_FF, D_MODEL), BF16),
                pltpu.SemaphoreType.DMA((2,)),
                pltpu.SemaphoreType.DMA((2,)),
            ],
        ),
        out_shape=jax.ShapeDtypeStruct((n_rows, D_MODEL), F32),
        input_output_aliases={5: 0},
        compiler_params=_params("arbitrary"),
        name="moe_experts",
    )(blk_e, first, slot, next_e, nused, xbuf, bgu, bd, wgu, wd)


def _combine_body(cnt_ref, off_ref, goff_ref, tot_ref, ybuf_ref, lpos_ref, tw_ref, h1_ref, fg_ref, out_ref,
                  stage_ref, sems):
    i = pl.program_id(0)
    n_tiles = pl.num_programs(0) * TILES_PER_STEP

    def copier(s):
        def make_copy(lo, go, size):
            return pltpu.make_async_copy(ybuf_ref.at[pl.ds(go, size)], stage_ref.at[s, pl.ds(lo, size)], sems.at[s])
        return make_copy

    @pl.when(i == 0)
    def _init():
        stage_ref[...] = jnp.zeros_like(stage_ref)
        _copy_runs(0, cnt_ref, off_ref, goff_ref, copier(0))

    for s in range(TILES_PER_STEP):
        t = i * TILES_PER_STEP + s

        @pl.when(t + 1 < n_tiles)
        def _prefetch():
            _copy_runs(t + 1, cnt_ref, off_ref, goff_ref, copier((s + 1) % TILES_PER_STEP))

        rows = slice(s * TOK_TILE, (s + 1) * TOK_TILE)
        w_hi, w_lo = _split_bf16(_slot_matrix(lpos_ref[s], tw_ref[s]))
        w2 = jnp.concatenate([w_hi, w_lo], axis=1)
        _wait_rows(copier(s), tot_ref[t])
        y2 = _dot_tn(w2, stage_ref[s].astype(BF16))
        h2 = h1_ref[rows, :] + (y2[0:TOK_TILE] + y2[TOK_TILE:2 * TOK_TILE])
        ms = jnp.mean(h2 * h2, axis=-1, keepdims=True)
        out_ref[rows, :] = h2 * lax.rsqrt(ms + RMS_EPS) * fg_ref[...]


def _combine(cnt, off, goff, tot, ybuf, lpos, tw, h1, final_g):
    n = h1.shape[0]
    rows = TILES_PER_STEP * TOK_TILE
    row = lambda i, *_: (i, 0)
    return pl.pallas_call(
        _combine_body,
        grid_spec=pltpu.PrefetchScalarGridSpec(
            num_scalar_prefetch=4,
            grid=(n // rows,),
            in_specs=[
                pl.BlockSpec(memory_space=pl.ANY),
                pl.BlockSpec((TILES_PER_STEP, SUBLANES, TOK_TILE), lambda i, *_: (i, 0, 0)),
                pl.BlockSpec((TILES_PER_STEP, SUBLANES, TOK_TILE), lambda i, *_: (i, 0, 0)),
                pl.BlockSpec((rows, D_MODEL), row),
                pl.BlockSpec((1, D_MODEL), lambda i, *_: (0, 0)),
            ],
            out_specs=pl.BlockSpec((rows, D_MODEL), row),
            scratch_shapes=[pltpu.VMEM((TILES_PER_STEP, STAGE_ROWS, D_MODEL), F32),
                            pltpu.SemaphoreType.DMA((TILES_PER_STEP,))],
        ),
        out_shape=jax.ShapeDtypeStruct((n, D_MODEL), F32),
        compiler_params=_params("arbitrary"),
        name="moe_combine",
    )(cnt, off, goff, tot, ybuf, lpos, tw, h1, final_g)


def _moe_layout(cnt_tiles):
    nt = cnt_tiles.shape[0]
    total = jnp.sum(cnt_tiles, axis=0)
    nblk_e = (total + EXPERT_BLOCK - 1) // EXPERT_BLOCK
    blk_end = jnp.cumsum(nblk_e)
    pstart = (blk_end - nblk_e) * EXPERT_BLOCK
    prefix = jnp.cumsum(cnt_tiles, axis=0) - cnt_tiles
    goff = pstart[None, :] + prefix
    off = jnp.cumsum(cnt_tiles, axis=1) - cnt_tiles
    tot = jnp.sum(cnt_tiles, axis=1)
    max_rows = nt * (TOK_TILE * TOP_K + N_EXPERTS * (SUBLANES - 1)) + N_EXPERTS * (EXPERT_BLOCK - 1)
    max_blocks = (max_rows + EXPERT_BLOCK - 1) // EXPERT_BLOCK
    blk_id = jnp.arange(max_blocks, dtype=blk_end.dtype)
    blk_e = jnp.minimum(jnp.sum(blk_end[None, :] <= blk_id[:, None], axis=1), N_EXPERTS - 1).astype(jnp.int32)
    nused = blk_end[-1:].astype(jnp.int32)
    flat = lambda a: a.reshape(-1).astype(jnp.int32)
    zoff = pstart + total
    zlen = nblk_e * EXPERT_BLOCK - total
    blkrange = jnp.stack([nused[0], jnp.int32(max_blocks)])
    used = blk_id < nused[0]
    first = used & jnp.concatenate([jnp.ones((1,), bool), blk_e[1:] != blk_e[:-1]])
    slot = (jnp.cumsum(first.astype(jnp.int32)) - 1) % 2
    eid = jnp.arange(N_EXPERTS)
    later = (eid[None, :] > eid[:, None]) & (nblk_e > 0)[None, :]
    next_of = jnp.min(jnp.where(later, eid[None, :], N_EXPERTS), axis=1)
    next_of = jnp.where(next_of < N_EXPERTS, next_of, -1)
    experts_plan = (blk_e, first.astype(jnp.int32), slot.astype(jnp.int32), next_of[blk_e].astype(jnp.int32), nused)
    dispatch_plan = (flat(cnt_tiles), flat(off), flat(goff), flat(tot))
    zero_plan = (flat(zoff), flat(zlen), blkrange.astype(jnp.int32))
    return dispatch_plan, zero_plan, experts_plan, max_blocks * EXPERT_BLOCK


def kernel(x, meta_tokens, attn_norm_g, w_in, na_rpb, na_norm_g, hgrn_lb_logits, hgrn_norm_g, w_out, ffn_norm_g,
           router_w, router_b, expert_w_gu, expert_b_gu, expert_w_down, expert_b_down, final_norm_g):
    B, T, D = x.shape
    x2d = x.reshape(B * T, D)
    w_in_b = w_in[0].astype(BF16)
    ng = attn_norm_g[0].reshape(1, D)
    lbl = hgrn_lb_logits[:, :, :].reshape(4, HG_WIDTH)

    att, hg = _inproj(x2d, ng, w_in_b, lbl, 512)
    att_m, hg_m = _inproj(meta_tokens.astype(F32), ng, w_in_b, lbl, N_META)
    att_m = jnp.pad(att_m, ((0, META_PAD - N_META), (0, 0)))
    hg_m = jnp.pad(hg_m, ((0, HG_CHUNK - N_META), (0, 0)))

    yna = _natten(att, att_m, _natten_bias_tables(na_rpb[0]), na_norm_g[0].reshape(1, NA_WIDTH), B, T)
    o_f, o_b = _hgrn(hg, hg_m, B, T)

    h1, xn, lpos, tw, cnt_tiles = _mix_route(
        yna.reshape(B * T, NA_WIDTH), o_f.reshape(B * T, HG_WIDTH), o_b.reshape(B * T, HG_WIDTH), hg, x2d,
        w_out[0].astype(BF16), hgrn_norm_g[0].reshape(1, HG_WIDTH), ffn_norm_g[0].reshape(1, D),
        router_w[0], router_b[0].reshape(1, N_EXPERTS))

    dispatch_plan, zero_plan, experts_plan, n_rows = _moe_layout(cnt_tiles[:, :, 0])
    xbuf = _dispatch(*dispatch_plan, *zero_plan, xn, lpos, n_rows)
    ybuf = _experts(*experts_plan, xbuf, expert_w_gu[0], expert_b_gu[0][:, None, :],
                    expert_w_down[0], expert_b_down[0][:, None, :])
    out = _combine(*dispatch_plan, ybuf, lpos, tw, h1, final_norm_g.reshape(1, D))
    return out.reshape(B, T, D)
```

```python
import functools

import numpy as np
import jax
import jax.numpy as jnp
from jax import lax
from jax.experimental import pallas as pl
from jax.experimental.pallas import tpu as pltpu

F32 = jnp.float32
BF16 = jnp.bfloat16

D_MODEL = 1024
N_META = 16
GRID_W = 64
NA_WIDTH = 512
NA_HEAD_DIM = 64
NA_HEADS = 8
NA_KH = 8
NA_KW = 16
HG_WIDTH = 512
HG_HEAD_DIM = 128
HG_HEADS = 4
HG_CHUNK = 64
IN_COLS = 3 * NA_WIDTH + 5 * HG_WIDTH
N_EXPERTS = 32
TOP_K = 4
D_FF = 1024
SWIGLU_LIMIT = 7.0
SWIGLU_ALPHA = 1.702
RMS_EPS = 1e-6
NEG_BIG = -1e30
LOG2_E = 1.4426950408889634

LANES = 128
VMEM_LIMIT_BYTES = 56 * 1024 * 1024

TOK_TILE = 256
EXPERT_BLOCK = 512
SUBLANES = 8
STAGE_ROWS = TOK_TILE * TOP_K + N_EXPERTS * SUBLANES
N_HG_LEVELS = 6


def _dot(a, b):
    return jnp.dot(a, b, preferred_element_type=F32)


def _dot_nt(a, b):
    return lax.dot_general(a, b, (((1,), (1,)), ((), ())), preferred_element_type=F32)


def _dot_tn(a, b):
    return lax.dot_general(a, b, (((0,), (0,)), ((), ())), preferred_element_type=F32)


def _split_bf16(x):
    hi = x.astype(BF16)
    lo = (x - hi.astype(F32)).astype(BF16)
    return hi, lo


def _pack_rows(rows, n_rows):
    idx = lax.broadcasted_iota(jnp.int32, (n_rows, rows[0].shape[1]), 0)
    out = jnp.zeros(idx.shape, rows[0].dtype)
    for k, r in enumerate(rows):
        out = jnp.where(idx == k, r, out)
    return out


def _params(*sem):
    return pltpu.CompilerParams(dimension_semantics=sem, vmem_limit_bytes=VMEM_LIMIT_BYTES)


def _inproj_body(x_ref, ng_ref, w_ref, lbl_ref, att_ref, hg_ref):
    x = x_ref[...]
    ms = jnp.mean(x * x, axis=-1, keepdims=True)
    n = (x * lax.rsqrt(ms + RMS_EPS) * ng_ref[...]).astype(BF16)

    def proj(lo, hi):
        return _dot(n, w_ref[:, lo:hi])

    pq = proj(0, NA_WIDTH)
    att_ref[:, 0:NA_WIDTH] = (pq * (NA_HEAD_DIM ** -0.5 * LOG2_E)).astype(BF16)
    att_ref[:, NA_WIDTH:3 * NA_WIDTH] = proj(NA_WIDTH, 3 * NA_WIDTH).astype(BF16)

    base = 3 * NA_WIDTH
    W = HG_WIDTH
    qh = proj(base, base + W)
    hg_ref[:, 0:W] = qh * jax.nn.sigmoid(qh)
    hg_ref[:, W:2 * W] = proj(base + W, base + 2 * W)
    lbl = lbl_ref[...]
    for d in range(2):
        a0 = lbl[2 * d:2 * d + 1, :]
        a1 = lbl[2 * d + 1:2 * d + 2, :]
        m = jnp.maximum(a0, a1)
        e0 = jnp.exp(a0 - m)
        e1 = jnp.exp(a1 - m)
        lb = e0 / (e0 + e1)
        raw = proj(base + (2 + d) * W, base + (3 + d) * W)
        f = lb + (1.0 - lb) * jax.nn.sigmoid(raw)
        hg_ref[:, (2 + d) * W:(3 + d) * W] = jnp.log(f)
    gt = proj(base + 4 * W, base + 5 * W)
    hg_ref[:, 4 * W:5 * W] = gt * jax.nn.sigmoid(gt)


def _inproj(x2d, norm_g, w_bf16, lb_logits4, tm):
    n = x2d.shape[0]
    return pl.pallas_call(
        _inproj_body,
        grid=(n // tm,),
        in_specs=[
            pl.BlockSpec((tm, D_MODEL), lambda i: (i, 0)),
            pl.BlockSpec((1, D_MODEL), lambda i: (0, 0)),
            pl.BlockSpec((D_MODEL, IN_COLS), lambda i: (0, 0)),
            pl.BlockSpec((4, HG_WIDTH), lambda i: (0, 0)),
        ],
        out_specs=[
            pl.BlockSpec((tm, 3 * NA_WIDTH), lambda i: (i, 0)),
            pl.BlockSpec((tm, 5 * HG_WIDTH), lambda i: (i, 0)),
        ],
        out_shape=[
            jax.ShapeDtypeStruct((n, 3 * NA_WIDTH), BF16),
            jax.ShapeDtypeStruct((n, 5 * HG_WIDTH), F32),
        ],
        compiler_params=_params("arbitrary"),
        name="inproj",
    )(x2d, norm_g, w_bf16, lb_logits4)


N_WIN_KEYS = NA_KH * GRID_W
META_PAD = LANES


NA_GROUP = 4
GROUP_LANES = NA_GROUP * NA_HEAD_DIM


NA_ROWS_PER_STEP = 8


def _natten_row(rr, r, q_ref, k_ref, v_ref, km_ref, vm_ref, bias_ref, *, rows):
    rs = jnp.clip(r - NA_KH // 2, 0, rows - NA_KH)
    d0 = rs - r + (NA_KH - 1)
    start = pl.multiple_of(rs * GRID_W, GRID_W)
    lane_head = lax.broadcasted_iota(jnp.int32, (GRID_W, GROUP_LANES), 1) // NA_HEAD_DIM
    meta_col = lax.broadcasted_iota(jnp.int32, (1, META_PAD), 1)
    meta_bias = jnp.where(meta_col < N_META, 0.0, NEG_BIG).astype(F32)
    folded = []
    for g in range(NA_HEADS // NA_GROUP):
        sl = slice(g * GROUP_LANES, (g + 1) * GROUP_LANES)
        q4 = q_ref[0, rr * GRID_W:(rr + 1) * GRID_W, sl]
        zero = jnp.zeros_like(q4)
        qm = jnp.concatenate([jnp.where(lane_head == h, q4, zero) for h in range(NA_GROUP)], axis=0)
        kw = k_ref[0, pl.ds(start, N_WIN_KEYS), sl]
        vw = v_ref[0, pl.ds(start, N_WIN_KEYS), sl]
        hs = slice(g * NA_GROUP, (g + 1) * NA_GROUP)
        bias = jnp.concatenate(
            [bias_ref[d0 + 2 * j, hs].reshape(NA_GROUP * GRID_W, 2 * GRID_W) for j in range(NA_KH // 2)], axis=-1)
        s = _dot_nt(qm, kw) + bias
        sm = _dot_nt(qm, km_ref[:, sl]) + meta_bias
        m = jnp.maximum(jnp.max(s, axis=-1, keepdims=True), jnp.max(sm, axis=-1, keepdims=True))
        e = jnp.exp2(s - m)
        em = jnp.exp2(sm - m)
        den = jnp.sum(e, axis=-1, keepdims=True) + jnp.sum(em, axis=-1, keepdims=True)
        o = (_dot(e.astype(BF16), vw) + _dot(em.astype(BF16), vm_ref[:, sl])) * (1.0 / den)
        acc = jnp.where(lane_head == 0, o[0:GRID_W], 0.0)
        for h in range(1, NA_GROUP):
            acc = jnp.where(lane_head == h, o[h * GRID_W:(h + 1) * GRID_W], acc)
        folded.append(acc)
    return jnp.concatenate(folded, axis=-1)


def _natten_body(q_ref, k_ref, v_ref, km_ref, vm_ref, bias_ref, ng_ref, bd_ref, o_ref, *, rows):
    r0 = pl.program_id(1) * NA_ROWS_PER_STEP
    o2 = jnp.concatenate(
        [_natten_row(rr, r0 + rr, q_ref, k_ref, v_ref, km_ref, vm_ref, bias_ref, rows=rows)
         for rr in range(NA_ROWS_PER_STEP)], axis=0)
    sq_hi, sq_lo = _split_bf16(o2 * o2)
    ms = (_dot(sq_hi, bd_ref[...]) + _dot(sq_lo, bd_ref[...])) * (1.0 / NA_HEAD_DIM)
    o_ref[0] = (o2 * lax.rsqrt(ms + RMS_EPS) * ng_ref[...]).astype(o_ref.dtype)


def _natten_bias_tables(rpb):
    c = np.arange(GRID_W)[:, None]
    kc = np.arange(GRID_W)[None, :]
    cs = np.clip(c - NA_KW // 2, 0, GRID_W - NA_KW)
    valid = (kc >= cs) & (kc < cs + NA_KW)
    dc = kc - c + (NA_KW - 1)
    pick = np.asarray(dc[None] == np.arange(2 * NA_KW - 1)[:, None, None], np.float32)
    t1 = jnp.einsum("hdj,jck->hdck", rpb.astype(F32) * LOG2_E, pick, precision=lax.Precision.HIGHEST)
    t1 = jnp.where(valid[None, None], t1, NEG_BIG)
    pairs = jnp.concatenate([t1[:, :-1], t1[:, 1:]], axis=-1)
    return pairs.transpose(1, 0, 2, 3)


def _natten(att, att_meta, bias_tabs, norm_g, batch, seq):
    rows = seq // GRID_W
    att3 = att.reshape(batch, seq, 3 * NA_WIDTH)
    head_of = np.arange(NA_WIDTH) // NA_HEAD_DIM
    same_head = jnp.asarray(head_of[:, None] == head_of[None, :], BF16)

    qrows = NA_ROWS_PER_STEP * GRID_W
    return pl.pallas_call(
        functools.partial(_natten_body, rows=rows),
        grid=(batch, rows // NA_ROWS_PER_STEP),
        in_specs=[
            pl.BlockSpec((1, qrows, NA_WIDTH), lambda b, r: (b, r, 0)),
            pl.BlockSpec((1, seq, NA_WIDTH), lambda b, r: (b, 0, 1)),
            pl.BlockSpec((1, seq, NA_WIDTH), lambda b, r: (b, 0, 2)),
            pl.BlockSpec((META_PAD, NA_WIDTH), lambda b, r: (0, 1)),
            pl.BlockSpec((META_PAD, NA_WIDTH), lambda b, r: (0, 2)),
            pl.BlockSpec(bias_tabs.shape, lambda b, r: (0, 0, 0, 0), pipeline_mode=pl.Buffered(1)),
            pl.BlockSpec((1, NA_WIDTH), lambda b, r: (0, 0)),
            pl.BlockSpec((NA_WIDTH, NA_WIDTH), lambda b, r: (0, 0)),
        ],
        out_specs=pl.BlockSpec((1, qrows, NA_WIDTH), lambda b, r: (b, r, 0)),
        out_shape=jax.ShapeDtypeStruct((batch, seq, NA_WIDTH), BF16),
        compiler_params=_params("arbitrary", "arbitrary"),
        name="natten",
    )(att3, att3, att3, att_meta, att_meta, bias_tabs, norm_g, same_head)


HG_CHUNKS_PER_STEP = 8
HG_CHUNKS_PER_GROUP = 1
ROW_B, ROW_LV = 0, 1
N_EXP_BLOCKS = 1 + N_HG_LEVELS


def _hgrn_constants():
    C = HG_CHUNK
    t = np.arange(C)
    u = t[None, :]
    mats = [u <= t[:, None]]
    masks = [np.eye(C, dtype=bool)]
    for lv in range(N_HG_LEVELS):
        m = 1 << lv
        blk = t // (2 * m)
        upper = (t // m) % 2 == 1
        p = blk * 2 * m + m - 1
        q_rows = upper[:, None] & (u > p[:, None]) & (u <= t[:, None])
        k_rows = (~upper)[:, None] & (u > t[:, None]) & (u <= p[:, None])
        mats.append(q_rows | k_rows)
        masks.append((blk[:, None] == blk[None, :]) & upper[:, None] & (~upper)[None, :])
    fwd = np.concatenate(mats, axis=0).astype(np.float32)
    fwd_mask = np.stack(masks).astype(np.float32)
    blocks = fwd.reshape(N_EXP_BLOCKS, C, C)
    bwd = blocks[:, ::-1, ::-1].reshape(N_EXP_BLOCKS * C, C)
    bwd_mask = fwd_mask[:, ::-1, ::-1]
    both = np.stack([fwd, bwd])
    both = np.concatenate([both, both], axis=-1)
    both_mask = np.stack([fwd_mask, bwd_mask])
    both_mask = np.concatenate([both_mask, both_mask], axis=-1)
    return (jnp.asarray(both, BF16), jnp.asarray(both_mask, F32))


def _hgrn_exponents(mat, g):
    g_hi, g_lo = _split_bf16(g)
    return _dot(mat, jnp.concatenate([g_hi, g_lo], axis=0))


def _hgrn_body(hf_ref, hb_ref, meta_ref, mat_ref, mask_ref, of_ref, ob_ref, stf_ref, stb_ref):
    C, W = HG_CHUNK, HG_WIDTH
    c = pl.program_id(1)

    @pl.when(c == 0)
    def _init():
        mb = meta_ref[...]
        v = mb[:, W:2 * W].astype(BF16)
        prefix = mat_ref[0, ROW_B * C:(ROW_B + 1) * C, :]
        for d, st_ref in ((0, stf_ref), (1, stb_ref)):
            g = mb[:, (2 + d) * W:(3 + d) * W]
            b = _hgrn_exponents(prefix, g)
            kb = ((1.0 - jnp.exp(g)) * jnp.exp(b[C - 1:C] - b)).astype(BF16)
            for h in range(HG_HEADS):
                sl = slice(h * HG_HEAD_DIM, (h + 1) * HG_HEAD_DIM)
                st_ref[h] = _dot_tn(v[:, sl], kb[:, sl])

    n = HG_CHUNKS_PER_STEP
    dir_refs = ((hf_ref, 2 * W, stf_ref, C - 1, of_ref), (hb_ref, 3 * W, stb_ref, 0, ob_ref))
    heads = [slice(h * HG_HEAD_DIM, (h + 1) * HG_HEAD_DIM) for h in range(HG_HEADS)]
    states = {(d, h): dir_refs[d][2][h] for d in range(2) for h in range(HG_HEADS)}
    for j0 in range(0, n, HG_CHUNKS_PER_GROUP):
        js = range(j0, j0 + HG_CHUNKS_PER_GROUP)
        tasks = [(0, j * C) for j in js] + [(1, (n - 1 - j) * C) for j in js]
        _hgrn_tasks(tasks, dir_refs, heads, states, mat_ref, mask_ref)
    for d in range(2):
        for h in range(HG_HEADS):
            dir_refs[d][2][h] = states[d, h]


def _hgrn_tasks(tasks, dir_refs, heads, states, mat_ref, mask_ref):
    C, W = HG_CHUNK, HG_WIDTH
    qs, ks, vs, ex_bs, ex_lvs, ex_ends = {}, {}, {}, {}, {}, {}
    for d, r0 in tasks:
        ref, g_col, total_row = dir_refs[d][0], dir_refs[d][1], dir_refs[d][3]
        g = ref[0, r0:r0 + C, g_col:g_col + W]
        sums = _hgrn_exponents(mat_ref[d], g)
        ex_bs[d, r0] = jnp.exp(sums[0:C])
        ex_lvs[d, r0] = jnp.exp(sums[C:]).astype(BF16)
        ex_ends[d, r0] = jnp.exp(sums[total_row:total_row + 1] - sums[0:C])
        qs[d, r0] = ref[0, r0:r0 + C, 0:W]
        ks[d, r0] = 1.0 - jnp.exp(g)
        vs[d, r0] = ref[0, r0:r0 + C, W:2 * W].astype(BF16)

    D2 = 2 * HG_HEAD_DIM
    pairs = [slice(p * D2, (p + 1) * D2) for p in range(HG_HEADS // 2)]

    def block_diag(x):
        zero = jnp.zeros((C, HG_HEAD_DIM), x.dtype)
        return jnp.concatenate([jnp.concatenate([x[:, :HG_HEAD_DIM], zero], axis=1),
                                jnp.concatenate([zero, x[:, HG_HEAD_DIM:]], axis=1)], axis=0)

    prods = {}
    for task in tasks:
        for p, sl in enumerate(pairs):
            q2, k2 = qs[task][:, sl].astype(BF16), ks[task][:, sl].astype(BF16)
            ps = [_dot_nt(q2, block_diag(k2))]
            for lv in range(N_HG_LEVELS):
                scale = ex_lvs[task][lv * C:(lv + 1) * C, sl]
                ps.append(_dot_nt(q2 * scale, block_diag(k2 * scale)))
            prods[task, p] = ps
    attn, qbs, kbs = {}, {}, {}
    for task in tasks:
        d = task[0]
        for p, sl in enumerate(pairs):
            a = prods[task, p][0] * mask_ref[d, 0]
            for lv in range(N_HG_LEVELS):
                a = a + prods[task, p][1 + lv] * mask_ref[d, 1 + lv]
            o2 = _dot(a.astype(BF16), block_diag(vs[task][:, sl]))
            attn[task, 2 * p] = o2[:, :HG_HEAD_DIM]
            attn[task, 2 * p + 1] = o2[:, HG_HEAD_DIM:]
        for h, sl in enumerate(heads):
            qbs[task, h] = (qs[task][:, sl] * ex_bs[task][:, sl]).astype(BF16)
            kbs[task, h] = _dot_tn(vs[task][:, sl], (ks[task][:, sl] * ex_ends[task][:, sl]).astype(BF16))
    for task in tasks:
        d, r0 = task
        total_row, o_ref = dir_refs[d][3], dir_refs[d][4]
        outs = []
        for h, sl in enumerate(heads):
            st = states[d, h]
            outs.append(attn[task, h] + _dot_nt(qbs[task, h], st.astype(BF16)))
            decay = ex_bs[task][total_row:total_row + 1, sl]
            states[d, h] = decay * st + kbs[task, h]
        o_ref[0, r0:r0 + C, :] = jnp.concatenate(outs, axis=-1)


def _hgrn(hg, hg_meta_pad, batch, seq):
    nc = seq // (HG_CHUNK * HG_CHUNKS_PER_STEP)
    hg3 = hg.reshape(batch, seq, 5 * HG_WIDTH)
    mats, masks = _hgrn_constants()
    blk = (1, HG_CHUNK * HG_CHUNKS_PER_STEP, 5 * HG_WIDTH)
    oblk = (1, HG_CHUNK * HG_CHUNKS_PER_STEP, HG_WIDTH)
    return pl.pallas_call(
        _hgrn_body,
        grid=(batch, nc),
        in_specs=[
            pl.BlockSpec(blk, lambda b, c: (b, c, 0)),
            pl.BlockSpec(blk, lambda b, c: (b, nc - 1 - c, 0)),
            pl.BlockSpec((HG_CHUNK, 5 * HG_WIDTH), lambda b, c: (0, 0)),
            pl.BlockSpec(mats.shape, lambda b, c: (0, 0, 0)),
            pl.BlockSpec(masks.shape, lambda b, c: (0, 0, 0, 0)),
        ],
        out_specs=[
            pl.BlockSpec(oblk, lambda b, c: (b, c, 0)),
            pl.BlockSpec(oblk, lambda b, c: (b, nc - 1 - c, 0)),
        ],
        out_shape=[jax.ShapeDtypeStruct((batch, seq, HG_WIDTH), F32)] * 2,
        scratch_shapes=[pltpu.VMEM((HG_HEADS, HG_HEAD_DIM, HG_HEAD_DIM), F32)] * 2,
        compiler_params=_params("arbitrary", "arbitrary"),
        name="hgrn",
    )(hg3, hg3, hg_meta_pad, mats, masks)


MIX_TILES_PER_STEP = 4


def _mix_route_body(yna_ref, of_ref, ob_ref, gate_ref, x_ref, wout_ref, hgn_ref, ffg_ref,
                    rwh_ref, rwl_ref, rb_ref, tri_ref, lowe_ref,
                    h1_ref, xn_ref, lpos_ref, tw_ref, cnt_ref):
    T, E = TOK_TILE, N_EXPERTS
    tiles = [slice(s * T, (s + 1) * T) for s in range(MIX_TILES_PER_STEP)]
    x_his, x_los = [], []
    for rows in tiles:
        o = of_ref[rows, :] + ob_ref[rows, :]
        parts = []
        for h in range(HG_HEADS):
            seg = o[:, h * HG_HEAD_DIM:(h + 1) * HG_HEAD_DIM]
            ms = jnp.mean(seg * seg, axis=-1, keepdims=True)
            parts.append(seg * lax.rsqrt(ms + RMS_EPS))
        yhg = jnp.concatenate(parts, axis=-1) * hgn_ref[...] * gate_ref[rows, :]
        mix = _dot(yna_ref[rows, :], wout_ref[0:NA_WIDTH, :]) + _dot(yhg.astype(BF16), wout_ref[NA_WIDTH:, :])
        h1 = x_ref[rows, :] + mix
        h1_ref[rows, :] = h1
        ms = jnp.mean(h1 * h1, axis=-1, keepdims=True)
        xn = h1 * lax.rsqrt(ms + RMS_EPS) * ffg_ref[...]
        x_hi, x_lo = _split_bf16(xn)
        xn_ref[rows, :] = x_hi
        x_his.append(x_hi)
        x_los.append(x_lo)

    curs = [(_dot_nt(rwh_ref[...], x_hi) + _dot_nt(rwh_ref[...], x_lo) + _dot_nt(rwl_ref[...], x_hi)) + rb_ref[...]
            for x_hi, x_lo in zip(x_his, x_los)]
    row = lax.broadcasted_iota(jnp.int32, (E, T), 0).astype(F32)
    sels = [[] for _ in tiles]
    vals = [[] for _ in tiles]
    for _ in range(TOP_K):
        for s in range(len(tiles)):
            m = jnp.max(curs[s], axis=0, keepdims=True)
            first = jnp.min(jnp.where(curs[s] == m, row, float(E)), axis=0, keepdims=True)
            sel = row == first
            sels[s].append(sel)
            vals[s].append(m)
            curs[s] = jnp.where(sel, -jnp.inf, curs[s])
    for s in range(len(tiles)):
        es = [jnp.exp(vk - vals[s][0]) for vk in vals[s]]
        den = es[0] + es[1] + es[2] + es[3]
        tw_ref[s] = _pack_rows([e / den for e in es], SUBLANES)

        onehot = jnp.zeros((E, T), F32)
        for sel in sels[s]:
            onehot = onehot + jnp.where(sel, 1.0, 0.0)
        lrank = _dot(onehot.astype(BF16), tri_ref[...])
        cnt = jnp.sum(onehot, axis=1, keepdims=True)
        cnt = jnp.floor((cnt + (SUBLANES - 1)) * (1.0 / SUBLANES)) * SUBLANES
        off = _dot(lowe_ref[...], jnp.broadcast_to(cnt, (E, LANES)).astype(BF16))
        base = jnp.concatenate([off] * (T // LANES), axis=1) + lrank
        lpos_ref[s] = _pack_rows(
            [jnp.sum(jnp.where(sel, base, 0.0), axis=0, keepdims=True) for sel in sels[s]],
            SUBLANES).astype(jnp.int32)
        cnt_ref[s] = cnt.astype(jnp.int32)


def _mix_route(yna, o_f, o_b, hg, x2d, wout_bf16, hgn, ffg, router_w, router_b):
    n = x2d.shape[0]
    T = TOK_TILE
    nt = n // T
    rw_t = router_w.T
    rwh = rw_t.astype(BF16)
    rwl = (rw_t - rwh.astype(F32)).astype(BF16)
    router_b = router_b.reshape(N_EXPERTS, 1)
    tri = jnp.asarray(np.triu(np.ones((T, T), np.float32), 1), BF16)
    lowe = jnp.asarray(np.tril(np.ones((N_EXPERTS, N_EXPERTS), np.float32), -1), BF16)
    row = lambda i: (i, 0)
    const = lambda i: (0, 0)
    tile3 = lambda i: (i, 0, 0)
    K = MIX_TILES_PER_STEP
    R = K * T
    return pl.pallas_call(
        _mix_route_body,
        grid=(nt // K,),
        in_specs=[
            pl.BlockSpec((R, NA_WIDTH), row),
            pl.BlockSpec((R, HG_WIDTH), row),
            pl.BlockSpec((R, HG_WIDTH), row),
            pl.BlockSpec((R, HG_WIDTH), lambda i: (i, 4)),
            pl.BlockSpec((R, D_MODEL), row),
            pl.BlockSpec((D_MODEL, D_MODEL), const),
            pl.BlockSpec((1, HG_WIDTH), const),
            pl.BlockSpec((1, D_MODEL), const),
            pl.BlockSpec((N_EXPERTS, D_MODEL), const),
            pl.BlockSpec((N_EXPERTS, D_MODEL), const),
            pl.BlockSpec((N_EXPERTS, 1), const),
            pl.BlockSpec((T, T), const),
            pl.BlockSpec((N_EXPERTS, N_EXPERTS), const),
        ],
        out_specs=[
            pl.BlockSpec((R, D_MODEL), row),
            pl.BlockSpec((R, D_MODEL), row),
            pl.BlockSpec((K, SUBLANES, T), tile3),
            pl.BlockSpec((K, SUBLANES, T), tile3),
            pl.BlockSpec((K, N_EXPERTS, 1), tile3),
        ],
        out_shape=[
            jax.ShapeDtypeStruct((n, D_MODEL), F32),
            jax.ShapeDtypeStruct((n, D_MODEL), BF16),
            jax.ShapeDtypeStruct((nt, SUBLANES, T), jnp.int32),
            jax.ShapeDtypeStruct((nt, SUBLANES, T), F32),
            jax.ShapeDtypeStruct((nt, N_EXPERTS, 1), jnp.int32),
        ],
        compiler_params=_params("arbitrary"),
        name="mix_route",
    )(yna, o_f, o_b, hg, x2d, wout_bf16, hgn, ffg, rwh, rwl, router_b, tri, lowe)


TILES_PER_STEP = 2


def _wait_rows(make_copy, rows):
    make_copy(0, 0, pl.multiple_of(rows, SUBLANES)).wait()


def _copy_runs(i, cnt_ref, off_ref, goff_ref, make_copy):
    def per_expert(e, carry):
        n = pl.multiple_of(cnt_ref[i * N_EXPERTS + e], SUBLANES)
        lo = pl.multiple_of(off_ref[i * N_EXPERTS + e], SUBLANES)
        go = pl.multiple_of(goff_ref[i * N_EXPERTS + e], SUBLANES)

        @pl.when(n > 0)
        def _start():
            make_copy(lo, go, n).start()

        return carry

    lax.fori_loop(0, N_EXPERTS, per_expert, 0)


def _slot_matrix(lpos, weights=None):
    row = lax.broadcasted_iota(jnp.int32, (STAGE_ROWS, TOK_TILE), 0)
    acc = jnp.zeros((STAGE_ROWS, TOK_TILE), F32)
    for k in range(TOP_K):
        hit = lpos[k:k + 1, :] == row
        val = 1.0 if weights is None else weights[k:k + 1, :]
        acc = jnp.where(hit, val, acc)
    return acc


def _dispatch_body(cnt_ref, off_ref, goff_ref, tot_ref, zoff_ref, zlen_ref, blkrange_ref,
                   xn_ref, lpos_ref, xbuf_ref, stage_ref, zero_ref, sems, zsem):
    i = pl.program_id(0)
    last = pl.num_programs(0) - 1

    def zero_fill(act):
        def tail(e, carry):
            n = pl.multiple_of(zlen_ref[e], SUBLANES)

            @pl.when(n > 0)
            def _():
                go = pl.multiple_of(zoff_ref[e], SUBLANES)
                act(pltpu.make_async_copy(zero_ref.at[pl.ds(0, n)], xbuf_ref.at[pl.ds(go, n)], zsem))

            return carry

        def unused(j, carry):
            go = pl.multiple_of(j * EXPERT_BLOCK, EXPERT_BLOCK)
            act(pltpu.make_async_copy(zero_ref, xbuf_ref.at[pl.ds(go, EXPERT_BLOCK)], zsem))
            return carry

        lax.fori_loop(0, N_EXPERTS, tail, 0)
        lax.fori_loop(blkrange_ref[0], blkrange_ref[1], unused, 0)

    @pl.when(i == 0)
    def _zero_start():
        zero_ref[...] = jnp.zeros_like(zero_ref)
        zero_fill(lambda c: c.start())

    for s in range(TILES_PER_STEP):
        t = i * TILES_PER_STEP + s

        def make_copy(lo, go, size, s=s):
            return pltpu.make_async_copy(stage_ref.at[s, pl.ds(lo, size)], xbuf_ref.at[pl.ds(go, size)], sems.at[s])

        @pl.when(i > 0)
        def _drain():
            _wait_rows(make_copy, tot_ref[t - TILES_PER_STEP])

        rows = slice(s * TOK_TILE, (s + 1) * TOK_TILE)
        perm = _slot_matrix(lpos_ref[s]).astype(BF16)
        stage_ref[s] = _dot(perm, xn_ref[rows, :])
        _copy_runs(t, cnt_ref, off_ref, goff_ref, make_copy)

        @pl.when(i == last)
        def _finish():
            _wait_rows(make_copy, tot_ref[t])

    @pl.when(i == last)
    def _zero_finish():
        zero_fill(lambda c: c.wait())


def _dispatch(cnt, off, goff, tot, zoff, zlen, blkrange, xn, lpos, n_rows):
    n = xn.shape[0]
    rows = TILES_PER_STEP * TOK_TILE
    return pl.pallas_call(
        _dispatch_body,
        grid_spec=pltpu.PrefetchScalarGridSpec(
            num_scalar_prefetch=7,
            grid=(n // rows,),
            in_specs=[
                pl.BlockSpec((rows, D_MODEL), lambda i, *_: (i, 0)),
                pl.BlockSpec((TILES_PER_STEP, SUBLANES, TOK_TILE), lambda i, *_: (i, 0, 0)),
            ],
            out_specs=pl.BlockSpec(memory_space=pl.ANY),
            scratch_shapes=[
                pltpu.VMEM((TILES_PER_STEP, STAGE_ROWS, D_MODEL), F32),
                pltpu.VMEM((EXPERT_BLOCK, D_MODEL), F32),
                pltpu.SemaphoreType.DMA((TILES_PER_STEP,)),
                pltpu.SemaphoreType.DMA(()),
            ],
        ),
        out_shape=jax.ShapeDtypeStruct((n_rows, D_MODEL), F32),
        compiler_params=_params("arbitrary"),
        name="moe_dispatch",
    )(cnt, off, goff, tot, zoff, zlen, blkrange, xn, lpos)


CAST_ROWS = 128


def _experts_body(blk_e_ref, first_ref, slot_ref, next_e_ref, nused_ref,
                  x_ref, bgu_ref, bd_ref, wgu_hbm, wd_hbm, y_ref,
                  wgu_f32, wd_f32, wgu_bf, wd_bf, sem_gu, sem_d):
    j = pl.program_id(0)

    def weight_copies(e, slot):
        return (pltpu.make_async_copy(wgu_hbm.at[e], wgu_f32.at[slot], sem_gu.at[slot]),
                pltpu.make_async_copy(wd_hbm.at[e], wd_f32.at[slot], sem_d.at[slot]))

    @pl.when(j < nused_ref[0])
    def _():
        e = blk_e_ref[j]
        slot = slot_ref[j]

        @pl.when(first_ref[j] == 1)
        def _new_expert():
            @pl.when(j == 0)
            def _():
                for c in weight_copies(e, slot):
                    c.start()

            for c in weight_copies(e, slot):
                c.wait()
            nxt = next_e_ref[j]

            @pl.when(nxt >= 0)
            def _():
                for c in weight_copies(nxt, 1 - slot):
                    c.start(priority=1)

            def cast(src, dst):
                def rows(r, carry):
                    sl = pl.ds(pl.multiple_of(r * CAST_ROWS, CAST_ROWS), CAST_ROWS)
                    dst[sl, :] = src[slot, sl, :].astype(BF16)
                    return carry
                lax.fori_loop(0, src.shape[1] // CAST_ROWS, rows, 0)

            cast(wgu_f32, wgu_bf)
            cast(wd_f32, wd_bf)

        x = x_ref[...].astype(BF16)
        gu = _dot(x, wgu_bf[...]) + bgu_ref[0]
        gate = jnp.minimum(gu[:, :D_FF], SWIGLU_LIMIT)
        up = jnp.clip(gu[:, D_FF:], -SWIGLU_LIMIT, SWIGLU_LIMIT)
        act = (up + 1.0) * gate * jax.nn.sigmoid(SWIGLU_ALPHA * gate)
        y_ref[...] = _dot(act.astype(BF16), wd_bf[...]) + bd_ref[0]


def _experts(blk_e, first, slot, next_e, nused, xbuf, wgu, bgu, wd, bd):
    n_rows = xbuf.shape[0]
    nblk = n_rows // EXPERT_BLOCK

    def rowblk(j, be, fi, sl, ne, nu):
        return (jnp.minimum(j, nu[0] - 1), 0)

    def expert(j, be, fi, sl, ne, nu):
        return (be[jnp.minimum(j, nu[0] - 1)], 0, 0)

    return pl.pallas_call(
        _experts_body,
        grid_spec=pltpu.PrefetchScalarGridSpec(
            num_scalar_prefetch=5,
            grid=(nblk,),
            in_specs=[
                pl.BlockSpec((EXPERT_BLOCK, D_MODEL), rowblk),
                pl.BlockSpec((1, 1, 2 * D_FF), expert),
                pl.BlockSpec((1, 1, D_MODEL), expert),
                pl.BlockSpec(memory_space=pl.ANY),
                pl.BlockSpec(memory_space=pl.ANY),
            ],
            out_specs=pl.BlockSpec((EXPERT_BLOCK, D_MODEL), rowblk),
            scratch_shapes=[
                pltpu.VMEM((2, D_MODEL, 2 * D_FF), F32),
                pltpu.VMEM((2, D_FF, D_MODEL), F32),
                pltpu.VMEM((D_MODEL, 2 * D_FF), BF16),
                pltpu.VMEM((D_FF, D_MODEL), BF16),
                pltpu.SemaphoreType.DMA((2,)),
                pltpu.SemaphoreType.DMA((2,)),
            ],
        ),
        out_shape=jax.ShapeDtypeStruct((n_rows, D_MODEL), F32),
        input_output_aliases={5: 0},
        compiler_params=_params("arbitrary"),
        name="moe_experts",
    )(blk_e, first, slot, next_e, nused, xbuf, bgu, bd, wgu, wd)


def _combine_body(cnt_ref, off_ref, goff_ref, tot_ref, ybuf_ref, lpos_ref, tw_ref, h1_ref, fg_ref, out_ref,
                  stage_ref, sems):
    i = pl.program_id(0)
    n_tiles = pl.num_programs(0) * TILES_PER_STEP

    def copier(s):
        def make_copy(lo, go, size):
            return pltpu.make_async_copy(ybuf_ref.at[pl.ds(go, size)], stage_ref.at[s, pl.ds(lo, size)], sems.at[s])
        return make_copy

    @pl.when(i == 0)
    def _init():
        stage_ref[...] = jnp.zeros_like(stage_ref)
        _copy_runs(0, cnt_ref, off_ref, goff_ref, copier(0))

    for s in range(TILES_PER_STEP):
        t = i * TILES_PER_STEP + s

        @pl.when(t + 1 < n_tiles)
        def _prefetch():
            _copy_runs(t + 1, cnt_ref, off_ref, goff_ref, copier((s + 1) % TILES_PER_STEP))

        rows = slice(s * TOK_TILE, (s + 1) * TOK_TILE)
        w_hi, w_lo = _split_bf16(_slot_matrix(lpos_ref[s], tw_ref[s]))
        w2 = jnp.concatenate([w_hi, w_lo], axis=1)
        _wait_rows(copier(s), tot_ref[t])
        y2 = _dot_tn(w2, stage_ref[s].astype(BF16))
        h2 = h1_ref[rows, :] + (y2[0:TOK_TILE] + y2[TOK_TILE:2 * TOK_TILE])
        ms = jnp.mean(h2 * h2, axis=-1, keepdims=True)
        out_ref[rows, :] = h2 * lax.rsqrt(ms + RMS_EPS) * fg_ref[...]


def _combine(cnt, off, goff, tot, ybuf, lpos, tw, h1, final_g):
    n = h1.shape[0]
    rows = TILES_PER_STEP * TOK_TILE
    row = lambda i, *_: (i, 0)
    return pl.pallas_call(
        _combine_body,
        grid_spec=pltpu.PrefetchScalarGridSpec(
            num_scalar_prefetch=4,
            grid=(n // rows,),
            in_specs=[
                pl.BlockSpec(memory_space=pl.ANY),
                pl.BlockSpec((TILES_PER_STEP, SUBLANES, TOK_TILE), lambda i, *_: (i, 0, 0)),
                pl.BlockSpec((TILES_PER_STEP, SUBLANES, TOK_TILE), lambda i, *_: (i, 0, 0)),
                pl.BlockSpec((rows, D_MODEL), row),
                pl.BlockSpec((1, D_MODEL), lambda i, *_: (0, 0)),
            ],
            out_specs=pl.BlockSpec((rows, D_MODEL), row),
            scratch_shapes=[pltpu.VMEM((TILES_PER_STEP, STAGE_ROWS, D_MODEL), F32),
                            pltpu.SemaphoreType.DMA((TILES_PER_STEP,))],
        ),
        out_shape=jax.ShapeDtypeStruct((n, D_MODEL), F32),
        compiler_params=_params("arbitrary"),
        name="moe_combine",
    )(cnt, off, goff, tot, ybuf, lpos, tw, h1, final_g)


def _moe_layout(cnt_tiles):
    nt = cnt_tiles.shape[0]
    total = jnp.sum(cnt_tiles, axis=0)
    nblk_e = (total + EXPERT_BLOCK - 1) // EXPERT_BLOCK
    blk_end = jnp.cumsum(nblk_e)
    pstart = (blk_end - nblk_e) * EXPERT_BLOCK
    prefix = jnp.cumsum(cnt_tiles, axis=0) - cnt_tiles
    goff = pstart[None, :] + prefix
    off = jnp.cumsum(cnt_tiles, axis=1) - cnt_tiles
    tot = jnp.sum(cnt_tiles, axis=1)
    max_rows = nt * (TOK_TILE * TOP_K + N_EXPERTS * (SUBLANES - 1)) + N_EXPERTS * (EXPERT_BLOCK - 1)
    max_blocks = (max_rows + EXPERT_BLOCK - 1) // EXPERT_BLOCK
    blk_id = jnp.arange(max_blocks, dtype=blk_end.dtype)
    blk_e = jnp.minimum(jnp.sum(blk_end[None, :] <= blk_id[:, None], axis=1), N_EXPERTS - 1).astype(jnp.int32)
    nused = blk_end[-1:].astype(jnp.int32)
    flat = lambda a: a.reshape(-1).astype(jnp.int32)
    zoff = pstart + total
    zlen = nblk_e * EXPERT_BLOCK - total
    blkrange = jnp.stack([nused[0], jnp.int32(max_blocks)])
    used = blk_id < nused[0]
    first = used & jnp.concatenate([jnp.ones((1,), bool), blk_e[1:] != blk_e[:-1]])
    slot = (jnp.cumsum(first.astype(jnp.int32)) - 1) % 2
    eid = jnp.arange(N_EXPERTS)
    later = (eid[None, :] > eid[:, None]) & (nblk_e > 0)[None, :]
    next_of = jnp.min(jnp.where(later, eid[None, :], N_EXPERTS), axis=1)
    next_of = jnp.where(next_of < N_EXPERTS, next_of, -1)
    experts_plan = (blk_e, first.astype(jnp.int32), slot.astype(jnp.int32), next_of[blk_e].astype(jnp.int32), nused)
    dispatch_plan = (flat(cnt_tiles), flat(off), flat(goff), flat(tot))
    zero_plan = (flat(zoff), flat(zlen), blkrange.astype(jnp.int32))
    return dispatch_plan, zero_plan, experts_plan, max_blocks * EXPERT_BLOCK


def kernel(x, meta_tokens, attn_norm_g, w_in, na_rpb, na_norm_g, hgrn_lb_logits, hgrn_norm_g, w_out, ffn_norm_g,
           router_w, router_b, expert_w_gu, expert_b_gu, expert_w_down, expert_b_down, final_norm_g):
    B, T, D = x.shape
    x2d = x.reshape(B * T, D)
    w_in_b = w_in[0].astype(BF16)
    ng = attn_norm_g[0].reshape(1, D)
    lbl = hgrn_lb_logits[:, :, :].reshape(4, HG_WIDTH)

    att, hg = _inproj(x2d, ng, w_in_b, lbl, 512)
    att_m, hg_m = _inproj(meta_tokens.astype(F32), ng, w_in_b, lbl, N_META)
    att_m = jnp.pad(att_m, ((0, META_PAD - N_META), (0, 0)))
    hg_m = jnp.pad(hg_m, ((0, HG_CHUNK - N_META), (0, 0)))

    yna = _natten(att, att_m, _natten_bias_tables(na_rpb[0]), na_norm_g[0].reshape(1, NA_WIDTH), B, T)
    o_f, o_b = _hgrn(hg, hg_m, B, T)

    h1, xn, lpos, tw, cnt_tiles = _mix_route(
        yna.reshape(B * T, NA_WIDTH), o_f.reshape(B * T, HG_WIDTH), o_b.reshape(B * T, HG_WIDTH), hg, x2d,
        w_out[0].astype(BF16), hgrn_norm_g[0].reshape(1, HG_WIDTH), ffn_norm_g[0].reshape(1, D),
        router_w[0], router_b[0].reshape(1, N_EXPERTS))

    dispatch_plan, zero_plan, experts_plan, n_rows = _moe_layout(cnt_tiles[:, :, 0])
    xbuf = _dispatch(*dispatch_plan, *zero_plan, xn, lpos, n_rows)
    ybuf = _experts(*experts_plan, xbuf, expert_w_gu[0], expert_b_gu[0][:, None, :],
                    expert_w_down[0], expert_b_down[0][:, None, :])
    out = _combine(*dispatch_plan, ybuf, lpos, tw, h1, final_norm_g.reshape(1, D))
    return out.reshape(B, T, D)
```

```python
import functools

import numpy as np
import jax
import jax.numpy as jnp
from jax import lax
from jax.experimental import pallas as pl
from jax.experimental.pallas import tpu as pltpu

F32 = jnp.float32
BF16 = jnp.bfloat16

D_MODEL = 1024
N_META = 16
GRID_W = 64
NA_WIDTH = 512
NA_HEAD_DIM = 64
NA_HEADS = 8
NA_KH = 8
NA_KW = 16
HG_WIDTH = 512
HG_HEAD_DIM = 128
HG_HEADS = 4
HG_CHUNK = 64
IN_COLS = 3 * NA_WIDTH + 5 * HG_WIDTH
N_EXPERTS = 32
TOP_K = 4
D_FF = 1024
SWIGLU_LIMIT = 7.0
SWIGLU_ALPHA = 1.702
RMS_EPS = 1e-6
NEG_BIG = -1e30
LOG2_E = 1.4426950408889634

LANES = 128
VMEM_LIMIT_BYTES = 56 * 1024 * 1024

TOK_TILE = 256
EXPERT_BLOCK = 512
SUBLANES = 8
STAGE_ROWS = TOK_TILE * TOP_K + N_EXPERTS * SUBLANES
N_HG_LEVELS = 6


def _dot(a, b):
    return jnp.dot(a, b, preferred_element_type=F32)


def _dot_nt(a, b):
    return lax.dot_general(a, b, (((1,), (1,)), ((), ())), preferred_element_type=F32)


def _dot_tn(a, b):
    return lax.dot_general(a, b, (((0,), (0,)), ((), ())), preferred_element_type=F32)


def _split_bf16(x):
    hi = x.astype(BF16)
    lo = (x - hi.astype(F32)).astype(BF16)
    return hi, lo


def _pack_rows(rows, n_rows):
    idx = lax.broadcasted_iota(jnp.int32, (n_rows, rows[0].shape[1]), 0)
    out = jnp.zeros(idx.shape, rows[0].dtype)
    for k, r in enumerate(rows):
        out = jnp.where(idx == k, r, out)
    return out


def _params(*sem):
    return pltpu.CompilerParams(dimension_semantics=sem, vmem_limit_bytes=VMEM_LIMIT_BYTES)


def _inproj_body(x_ref, ng_ref, w_ref, lbl_ref, att_ref, hg_ref):
    x = x_ref[...]
    ms = jnp.mean(x * x, axis=-1, keepdims=True)
    n = (x * lax.rsqrt(ms + RMS_EPS) * ng_ref[...]).astype(BF16)

    def proj(lo, hi):
        return _dot(n, w_ref[:, lo:hi])

    pq = proj(0, NA_WIDTH)
    att_ref[:, 0:NA_WIDTH] = (pq * (NA_HEAD_DIM ** -0.5 * LOG2_E)).astype(BF16)
    att_ref[:, NA_WIDTH:3 * NA_WIDTH] = proj(NA_WIDTH, 3 * NA_WIDTH).astype(BF16)

    base = 3 * NA_WIDTH
    W = HG_WIDTH
    qh = proj(base, base + W)
    hg_ref[:, 0:W] = qh * jax.nn.sigmoid(qh)
    hg_ref[:, W:2 * W] = proj(base + W, base + 2 * W)
    lbl = lbl_ref[...]
    for d in range(2):
        a0 = lbl[2 * d:2 * d + 1, :]
        a1 = lbl[2 * d + 1:2 * d + 2, :]
        m = jnp.maximum(a0, a1)
        e0 = jnp.exp(a0 - m)
        e1 = jnp.exp(a1 - m)
        lb = e0 / (e0 + e1)
        raw = proj(base + (2 + d) * W, base + (3 + d) * W)
        f = lb + (1.0 - lb) * jax.nn.sigmoid(raw)
        hg_ref[:, (2 + d) * W:(3 + d) * W] = jnp.log(f)
    gt = proj(base + 4 * W, base + 5 * W)
    hg_ref[:, 4 * W:5 * W] = gt * jax.nn.sigmoid(gt)


def _inproj(x2d, norm_g, w_bf16, lb_logits4, tm):
    n = x2d.shape[0]
    return pl.pallas_call(
        _inproj_body,
        grid=(n // tm,),
        in_specs=[
            pl.BlockSpec((tm, D_MODEL), lambda i: (i, 0)),
            pl.BlockSpec((1, D_MODEL), lambda i: (0, 0)),
            pl.BlockSpec((D_MODEL, IN_COLS), lambda i: (0, 0)),
            pl.BlockSpec((4, HG_WIDTH), lambda i: (0, 0)),
        ],
        out_specs=[
            pl.BlockSpec((tm, 3 * NA_WIDTH), lambda i: (i, 0)),
            pl.BlockSpec((tm, 5 * HG_WIDTH), lambda i: (i, 0)),
        ],
        out_shape=[
            jax.ShapeDtypeStruct((n, 3 * NA_WIDTH), BF16),
            jax.ShapeDtypeStruct((n, 5 * HG_WIDTH), F32),
        ],
        compiler_params=_params("arbitrary"),
        name="inproj",
    )(x2d, norm_g, w_bf16, lb_logits4)


N_WIN_KEYS = NA_KH * GRID_W
META_PAD = LANES


NA_GROUP = 4
GROUP_LANES = NA_GROUP * NA_HEAD_DIM


NA_ROWS_PER_STEP = 8


def _natten_row(rr, r, q_ref, k_ref, v_ref, km_ref, vm_ref, bias_ref, *, rows):
    rs = jnp.clip(r - NA_KH // 2, 0, rows - NA_KH)
    d0 = rs - r + (NA_KH - 1)
    start = pl.multiple_of(rs * GRID_W, GRID_W)
    lane_head = lax.broadcasted_iota(jnp.int32, (GRID_W, GROUP_LANES), 1) // NA_HEAD_DIM
    meta_col = lax.broadcasted_iota(jnp.int32, (1, META_PAD), 1)
    meta_bias = jnp.where(meta_col < N_META, 0.0, NEG_BIG).astype(F32)
    folded = []
    for g in range(NA_HEADS // NA_GROUP):
        sl = slice(g * GROUP_LANES, (g + 1) * GROUP_LANES)
        q4 = q_ref[0, rr * GRID_W:(rr + 1) * GRID_W, sl]
        zero = jnp.zeros_like(q4)
        qm = jnp.concatenate([jnp.where(lane_head == h, q4, zero) for h in range(NA_GROUP)], axis=0)
        kw = k_ref[0, pl.ds(start, N_WIN_KEYS), sl]
        vw = v_ref[0, pl.ds(start, N_WIN_KEYS), sl]
        hs = slice(g * NA_GROUP, (g + 1) * NA_GROUP)
        bias = jnp.concatenate(
            [bias_ref[d0 + 2 * j, hs].reshape(NA_GROUP * GRID_W, 2 * GRID_W) for j in range(NA_KH // 2)], axis=-1)
        s = _dot_nt(qm, kw) + bias
        sm = _dot_nt(qm, km_ref[:, sl]) + meta_bias
        m = jnp.maximum(jnp.max(s, axis=-1, keepdims=True), jnp.max(sm, axis=-1, keepdims=True))
        e = jnp.exp2(s - m)
        em = jnp.exp2(sm - m)
        den = jnp.sum(e, axis=-1, keepdims=True) + jnp.sum(em, axis=-1, keepdims=True)
        o = (_dot(e.astype(BF16), vw) + _dot(em.astype(BF16), vm_ref[:, sl])) * (1.0 / den)
        acc = jnp.where(lane_head == 0, o[0:GRID_W], 0.0)
        for h in range(1, NA_GROUP):
            acc = jnp.where(lane_head == h, o[h * GRID_W:(h + 1) * GRID_W], acc)
        folded.append(acc)
    return jnp.concatenate(folded, axis=-1)


def _natten_body(q_ref, k_ref, v_ref, km_ref, vm_ref, bias_ref, ng_ref, bd_ref, o_ref, *, rows):
    r0 = pl.program_id(1) * NA_ROWS_PER_STEP
    o2 = jnp.concatenate(
        [_natten_row(rr, r0 + rr, q_ref, k_ref, v_ref, km_ref, vm_ref, bias_ref, rows=rows)
         for rr in range(NA_ROWS_PER_STEP)], axis=0)
    sq_hi, sq_lo = _split_bf16(o2 * o2)
    ms = (_dot(sq_hi, bd_ref[...]) + _dot(sq_lo, bd_ref[...])) * (1.0 / NA_HEAD_DIM)
    o_ref[0] = (o2 * lax.rsqrt(ms + RMS_EPS) * ng_ref[...]).astype(o_ref.dtype)


def _natten(att, att_meta, bias_tabs, norm_g, batch, seq):
    rows = seq // GRID_W
    att3 = att.reshape(batch, seq, 3 * NA_WIDTH)
    head_of = np.arange(NA_WIDTH) // NA_HEAD_DIM
    same_head = jnp.asarray(head_of[:, None] == head_of[None, :], BF16)

    qrows = NA_ROWS_PER_STEP * GRID_W
    return pl.pallas_call(
        functools.partial(_natten_body, rows=rows),
        grid=(batch, rows // NA_ROWS_PER_STEP),
        in_specs=[
            pl.BlockSpec((1, qrows, NA_WIDTH), lambda b, r: (b, r, 0)),
            pl.BlockSpec((1, seq, NA_WIDTH), lambda b, r: (b, 0, 1)),
            pl.BlockSpec((1, seq, NA_WIDTH), lambda b, r: (b, 0, 2)),
            pl.BlockSpec((META_PAD, NA_WIDTH), lambda b, r: (0, 1)),
            pl.BlockSpec((META_PAD, NA_WIDTH), lambda b, r: (0, 2)),
            pl.BlockSpec(bias_tabs.shape, lambda b, r: (0, 0, 0, 0), pipeline_mode=pl.Buffered(1)),
            pl.BlockSpec((1, NA_WIDTH), lambda b, r: (0, 0)),
            pl.BlockSpec((NA_WIDTH, NA_WIDTH), lambda b, r: (0, 0)),
        ],
        out_specs=pl.BlockSpec((1, qrows, NA_WIDTH), lambda b, r: (b, r, 0)),
        out_shape=jax.ShapeDtypeStruct((batch, seq, NA_WIDTH), BF16),
        compiler_params=_params("arbitrary", "arbitrary"),
        name="natten",
    )(att3, att3, att3, att_meta, att_meta, bias_tabs, norm_g, same_head)


def _natten_bias_tables(rpb):
    c = np.arange(GRID_W)[:, None]
    kc = np.arange(GRID_W)[None, :]
    cs = np.clip(c - NA_KW // 2, 0, GRID_W - NA_KW)
    valid = (kc >= cs) & (kc < cs + NA_KW)
    dc = kc - c + (NA_KW - 1)
    pick = np.asarray(dc[None] == np.arange(2 * NA_KW - 1)[:, None, None], np.float32)
    t1 = jnp.einsum("hdj,jck->hdck", rpb.astype(F32) * LOG2_E, pick, precision=lax.Precision.HIGHEST)
    t1 = jnp.where(valid[None, None], t1, NEG_BIG)
    pairs = jnp.concatenate([t1[:, :-1], t1[:, 1:]], axis=-1)
    return pairs.transpose(1, 0, 2, 3)


HG_CHUNKS_PER_STEP = 8
HG_CHUNKS_PER_GROUP = 1
ROW_B, ROW_LV = 0, 1
N_EXP_BLOCKS = 1 + N_HG_LEVELS


def _hgrn_constants():
    C = HG_CHUNK
    t = np.arange(C)
    u = t[None, :]
    mats = [u <= t[:, None]]
    masks = [np.eye(C, dtype=bool)]
    for lv in range(N_HG_LEVELS):
        m = 1 << lv
        blk = t // (2 * m)
        upper = (t // m) % 2 == 1
        p = blk * 2 * m + m - 1
        q_rows = upper[:, None] & (u > p[:, None]) & (u <= t[:, None])
        k_rows = (~upper)[:, None] & (u > t[:, None]) & (u <= p[:, None])
        mats.append(q_rows | k_rows)
        masks.append((blk[:, None] == blk[None, :]) & upper[:, None] & (~upper)[None, :])
    fwd = np.concatenate(mats, axis=0).astype(np.float32)
    fwd_mask = np.stack(masks).astype(np.float32)
    blocks = fwd.reshape(N_EXP_BLOCKS, C, C)
    bwd = blocks[:, ::-1, ::-1].reshape(N_EXP_BLOCKS * C, C)
    bwd_mask = fwd_mask[:, ::-1, ::-1]
    both = np.stack([fwd, bwd])
    both = np.concatenate([both, both], axis=-1)
    both_mask = np.stack([fwd_mask, bwd_mask])
    both_mask = np.concatenate([both_mask, both_mask], axis=-1)
    return (jnp.asarray(both, BF16), jnp.asarray(both_mask, F32))


def _hgrn_exponents(mat, g):
    g_hi, g_lo = _split_bf16(g)
    return _dot(mat, jnp.concatenate([g_hi, g_lo], axis=0))


def _hgrn_init(meta_ref, mat_ref, stf_ref, stb_ref):
    C, W = HG_CHUNK, HG_WIDTH
    mb = meta_ref[...]
    v = mb[:, W:2 * W].astype(BF16)
    prefix = mat_ref[0, ROW_B * C:(ROW_B + 1) * C, :]
    for d, st_ref in ((0, stf_ref), (1, stb_ref)):
        g = mb[:, (2 + d) * W:(3 + d) * W]
        b = _hgrn_exponents(prefix, g)
        kb = ((1.0 - jnp.exp(g)) * jnp.exp(b[C - 1:C] - b)).astype(BF16)
        for h in range(HG_HEADS):
            sl = slice(h * HG_HEAD_DIM, (h + 1) * HG_HEAD_DIM)
            st_ref[h] = _dot_tn(v[:, sl], kb[:, sl])


def _hgrn_group_tasks(j0):
    js = range(j0, j0 + HG_CHUNKS_PER_GROUP)
    return ([(0, j * HG_CHUNK) for j in js] +
            [(1, (HG_CHUNKS_PER_STEP - 1 - j) * HG_CHUNK) for j in js])


def _hgrn_tasks(tasks, dir_refs, states, mat_ref, mask_ref):
    C, W = HG_CHUNK, HG_WIDTH
    heads = [slice(h * HG_HEAD_DIM, (h + 1) * HG_HEAD_DIM) for h in range(HG_HEADS)]
    qs, ks, vs, ex_bs, ex_lvs, ex_ends = {}, {}, {}, {}, {}, {}
    for d, r0 in tasks:
        q_ref, v_ref, g_ref, total_row = dir_refs[d][0], dir_refs[d][1], dir_refs[d][2], dir_refs[d][3]
        g = g_ref[0, r0:r0 + C, :]
        sums = _hgrn_exponents(mat_ref[d], g)
        ex_bs[d, r0] = jnp.exp(sums[0:C])
        ex_lvs[d, r0] = jnp.exp(sums[C:]).astype(BF16)
        ex_ends[d, r0] = jnp.exp(sums[total_row:total_row + 1] - sums[0:C])
        qs[d, r0] = q_ref[0, r0:r0 + C, :]
        ks[d, r0] = 1.0 - jnp.exp(g)
        vs[d, r0] = v_ref[0, r0:r0 + C, :].astype(BF16)

    D2 = 2 * HG_HEAD_DIM
    pairs = [slice(p * D2, (p + 1) * D2) for p in range(HG_HEADS // 2)]

    def block_diag(x):
        zero = jnp.zeros((C, HG_HEAD_DIM), x.dtype)
        return jnp.concatenate([jnp.concatenate([x[:, :HG_HEAD_DIM], zero], axis=1),
                                jnp.concatenate([zero, x[:, HG_HEAD_DIM:]], axis=1)], axis=0)

    prods = {}
    for task in tasks:
        for p, sl in enumerate(pairs):
            q2, k2 = qs[task][:, sl].astype(BF16), ks[task][:, sl].astype(BF16)
            ps = [_dot_nt(q2, block_diag(k2))]
            for lv in range(N_HG_LEVELS):
                scale = ex_lvs[task][lv * C:(lv + 1) * C, sl]
                ps.append(_dot_nt(q2 * scale, block_diag(k2 * scale)))
            prods[task, p] = ps
    attn, qbs, kbs = {}, {}, {}
    for task in tasks:
        d = task[0]
        for p, sl in enumerate(pairs):
            a = prods[task, p][0] * mask_ref[d, 0]
            for lv in range(N_HG_LEVELS):
                a = a + prods[task, p][1 + lv] * mask_ref[d, 1 + lv]
            o2 = _dot(a.astype(BF16), block_diag(vs[task][:, sl]))
            attn[task, 2 * p] = o2[:, :HG_HEAD_DIM]
            attn[task, 2 * p + 1] = o2[:, HG_HEAD_DIM:]
        for h, sl in enumerate(heads):
            qbs[task, h] = (qs[task][:, sl] * ex_bs[task][:, sl]).astype(BF16)
            kbs[task, h] = _dot_tn(vs[task][:, sl], (ks[task][:, sl] * ex_ends[task][:, sl]).astype(BF16))
    for task in tasks:
        d, r0 = task
        total_row, o_ref = dir_refs[d][3], dir_refs[d][4]
        outs = []
        for h, sl in enumerate(heads):
            st = states[d, h]
            outs.append(attn[task, h] + _dot_nt(qbs[task, h], st.astype(BF16)))
            decay = ex_bs[task][total_row:total_row + 1, sl]
            states[d, h] = decay * st + kbs[task, h]
        o_ref[0, r0:r0 + C, :] = jnp.concatenate(outs, axis=-1)


def _hgrn_body(hqf_ref, hvf_ref, hgf_ref, hqb_ref, hvb_ref, hgb_ref, meta_ref, mat_ref, mask_ref,
               of_ref, ob_ref, stf_ref, stb_ref):
    @pl.when(pl.program_id(1) == 0)
    def _init():
        _hgrn_init(meta_ref, mat_ref, stf_ref, stb_ref)

    dir_refs = ((hqf_ref, hvf_ref, hgf_ref, HG_CHUNK - 1, of_ref), (hqb_ref, hvb_ref, hgb_ref, 0, ob_ref))
    st_refs = (stf_ref, stb_ref)
    states = {(d, h): st_refs[d][h] for d in range(2) for h in range(HG_HEADS)}
    for j0 in range(0, HG_CHUNKS_PER_STEP, HG_CHUNKS_PER_GROUP):
        _hgrn_tasks(_hgrn_group_tasks(j0), dir_refs, states, mat_ref, mask_ref)
    for d in range(2):
        for h in range(HG_HEADS):
            st_refs[d][h] = states[d, h]


def _hgrn(hg, hg_meta_pad, batch, seq):
    steps = seq // (HG_CHUNK * HG_CHUNKS_PER_STEP)
    hg3 = hg.reshape(batch, seq, 5 * HG_WIDTH)
    mats, masks = _hgrn_constants()
    hblk = (1, HG_CHUNK * HG_CHUNKS_PER_STEP, HG_WIDTH)

    def fwd(col):
        return pl.BlockSpec(hblk, lambda b, s: (b, s, col))

    def bwd(col):
        return pl.BlockSpec(hblk, lambda b, s: (b, steps - 1 - s, col))

    return pl.pallas_call(
        _hgrn_body,
        grid=(batch, steps),
        in_specs=[
            fwd(0), fwd(1), fwd(2),
            bwd(0), bwd(1), bwd(3),
            pl.BlockSpec((HG_CHUNK, 5 * HG_WIDTH), lambda b, s: (0, 0)),
            pl.BlockSpec(mats.shape, lambda b, s: (0, 0, 0)),
            pl.BlockSpec(masks.shape, lambda b, s: (0, 0, 0, 0)),
        ],
        out_specs=[fwd(0), bwd(0)],
        out_shape=[jax.ShapeDtypeStruct((batch, seq, HG_WIDTH), F32)] * 2,
        scratch_shapes=[pltpu.VMEM((HG_HEADS, HG_HEAD_DIM, HG_HEAD_DIM), F32)] * 2,
        compiler_params=_params("arbitrary", "arbitrary"),
        name="hgrn",
    )(hg3, hg3, hg3, hg3, hg3, hg3, hg_meta_pad, mats, masks)


MIX_TILES_PER_STEP = 4


def _mix_route_body(yna_ref, of_ref, ob_ref, gate_ref, x_ref, wout_ref, hgn_ref, ffg_ref,
                    rwh_ref, rwl_ref, rb_ref, tri_ref, lowe_ref,
                    h1_ref, xn_ref, lpos_ref, tw_ref, cnt_ref):
    T, E = TOK_TILE, N_EXPERTS
    tiles = [slice(s * T, (s + 1) * T) for s in range(MIX_TILES_PER_STEP)]
    x_his, x_los = [], []
    for rows in tiles:
        o = of_ref[rows, :] + ob_ref[rows, :]
        parts = []
        for h in range(HG_HEADS):
            seg = o[:, h * HG_HEAD_DIM:(h + 1) * HG_HEAD_DIM]
            ms = jnp.mean(seg * seg, axis=-1, keepdims=True)
            parts.append(seg * lax.rsqrt(ms + RMS_EPS))
        yhg = jnp.concatenate(parts, axis=-1) * hgn_ref[...] * gate_ref[rows, :]
        mix = _dot(yna_ref[rows, :], wout_ref[0:NA_WIDTH, :]) + _dot(yhg.astype(BF16), wout_ref[NA_WIDTH:, :])
        h1 = x_ref[rows, :] + mix
        h1_ref[rows, :] = h1
        ms = jnp.mean(h1 * h1, axis=-1, keepdims=True)
        xn = h1 * lax.rsqrt(ms + RMS_EPS) * ffg_ref[...]
        x_hi, x_lo = _split_bf16(xn)
        xn_ref[rows, :] = x_hi
        x_his.append(x_hi)
        x_los.append(x_lo)

    curs = [(_dot_nt(rwh_ref[...], x_hi) + _dot_nt(rwh_ref[...], x_lo) + _dot_nt(rwl_ref[...], x_hi)) + rb_ref[...]
            for x_hi, x_lo in zip(x_his, x_los)]
    row = lax.broadcasted_iota(jnp.int32, (E, T), 0).astype(F32)
    sels = [[] for _ in tiles]
    vals = [[] for _ in tiles]
    for _ in range(TOP_K):
        for s in range(len(tiles)):
            m = jnp.max(curs[s], axis=0, keepdims=True)
            first = jnp.min(jnp.where(curs[s] == m, row, float(E)), axis=0, keepdims=True)
            sel = row == first
            sels[s].append(sel)
            vals[s].append(m)
            curs[s] = jnp.where(sel, -jnp.inf, curs[s])
    for s in range(len(tiles)):
        es = [jnp.exp(vk - vals[s][0]) for vk in vals[s]]
        den = es[0] + es[1] + es[2] + es[3]
        tw_ref[s] = _pack_rows([e / den for e in es], SUBLANES)

        onehot = jnp.zeros((E, T), F32)
        for sel in sels[s]:
            onehot = onehot + jnp.where(sel, 1.0, 0.0)
        lrank = _dot(onehot.astype(BF16), tri_ref[...])
        cnt = jnp.sum(onehot, axis=1, keepdims=True)
        cnt = jnp.floor((cnt + (SUBLANES - 1)) * (1.0 / SUBLANES)) * SUBLANES
        off = _dot(lowe_ref[...], jnp.broadcast_to(cnt, (E, LANES)).astype(BF16))
        base = jnp.concatenate([off] * (T // LANES), axis=1) + lrank
        lpos_ref[s] = _pack_rows(
            [jnp.sum(jnp.where(sel, base, 0.0), axis=0, keepdims=True) for sel in sels[s]],
            SUBLANES).astype(jnp.int32)
        cnt_ref[s] = cnt.astype(jnp.int32)


def _mix_route(yna, o_f, o_b, hg, x2d, wout_bf16, hgn, ffg, router_w, router_b):
    n = x2d.shape[0]
    T = TOK_TILE
    nt = n // T
    rw_t = router_w.T
    rwh = rw_t.astype(BF16)
    rwl = (rw_t - rwh.astype(F32)).astype(BF16)
    router_b = router_b.reshape(N_EXPERTS, 1)
    tri = jnp.asarray(np.triu(np.ones((T, T), np.float32), 1), BF16)
    lowe = jnp.asarray(np.tril(np.ones((N_EXPERTS, N_EXPERTS), np.float32), -1), BF16)
    row = lambda i: (i, 0)
    const = lambda i: (0, 0)
    tile3 = lambda i: (i, 0, 0)
    K = MIX_TILES_PER_STEP
    R = K * T
    return pl.pallas_call(
        _mix_route_body,
        grid=(nt // K,),
        in_specs=[
            pl.BlockSpec((R, NA_WIDTH), row),
            pl.BlockSpec((R, HG_WIDTH), row),
            pl.BlockSpec((R, HG_WIDTH), row),
            pl.BlockSpec((R, HG_WIDTH), lambda i: (i, 4)),
            pl.BlockSpec((R, D_MODEL), row),
            pl.BlockSpec((D_MODEL, D_MODEL), const),
            pl.BlockSpec((1, HG_WIDTH), const),
            pl.BlockSpec((1, D_MODEL), const),
            pl.BlockSpec((N_EXPERTS, D_MODEL), const),
            pl.BlockSpec((N_EXPERTS, D_MODEL), const),
            pl.BlockSpec((N_EXPERTS, 1), const),
            pl.BlockSpec((T, T), const),
            pl.BlockSpec((N_EXPERTS, N_EXPERTS), const),
        ],
        out_specs=[
            pl.BlockSpec((R, D_MODEL), row),
            pl.BlockSpec((R, D_MODEL), row),
            pl.BlockSpec((K, SUBLANES, T), tile3),
            pl.BlockSpec((K, SUBLANES, T), tile3),
            pl.BlockSpec((K, N_EXPERTS, 1), tile3),
        ],
        out_shape=[
            jax.ShapeDtypeStruct((n, D_MODEL), F32),
            jax.ShapeDtypeStruct((n, D_MODEL), BF16),
            jax.ShapeDtypeStruct((nt, SUBLANES, T), jnp.int32),
            jax.ShapeDtypeStruct((nt, SUBLANES, T), F32),
            jax.ShapeDtypeStruct((nt, N_EXPERTS, 1), jnp.int32),
        ],
        compiler_params=_params("arbitrary"),
        name="mix_route",
    )(yna, o_f, o_b, hg, x2d, wout_bf16, hgn, ffg, rwh, rwl, router_b, tri, lowe)


TILES_PER_STEP = 2


def _wait_rows(make_copy, rows):
    make_copy(0, 0, pl.multiple_of(rows, SUBLANES)).wait()


def _copy_runs(i, cnt_ref, off_ref, goff_ref, make_copy):
    def per_expert(e, carry):
        n = pl.multiple_of(cnt_ref[i * N_EXPERTS + e], SUBLANES)
        lo = pl.multiple_of(off_ref[i * N_EXPERTS + e], SUBLANES)
        go = pl.multiple_of(goff_ref[i * N_EXPERTS + e], SUBLANES)

        @pl.when(n > 0)
        def _start():
            make_copy(lo, go, n).start()

        return carry

    lax.fori_loop(0, N_EXPERTS, per_expert, 0)


def _slot_matrix(lpos, weights=None):
    row = lax.broadcasted_iota(jnp.int32, (STAGE_ROWS, TOK_TILE), 0)
    acc = jnp.zeros((STAGE_ROWS, TOK_TILE), F32)
    for k in range(TOP_K):
        hit = lpos[k:k + 1, :] == row
        val = 1.0 if weights is None else weights[k:k + 1, :]
        acc = jnp.where(hit, val, acc)
    return acc


def _dispatch_body(cnt_ref, off_ref, goff_ref, tot_ref, zoff_ref, zlen_ref, blkrange_ref,
                   xn_ref, lpos_ref, xbuf_ref, stage_ref, zero_ref, sems, zsem):
    i = pl.program_id(0)
    last = pl.num_programs(0) - 1

    def zero_fill(act):
        def tail(e, carry):
            n = pl.multiple_of(zlen_ref[e], SUBLANES)

            @pl.when(n > 0)
            def _():
                go = pl.multiple_of(zoff_ref[e], SUBLANES)
                act(pltpu.make_async_copy(zero_ref.at[pl.ds(0, n)], xbuf_ref.at[pl.ds(go, n)], zsem))

            return carry

        def unused(j, carry):
            go = pl.multiple_of(j * EXPERT_BLOCK, EXPERT_BLOCK)
            act(pltpu.make_async_copy(zero_ref, xbuf_ref.at[pl.ds(go, EXPERT_BLOCK)], zsem))
            return carry

        lax.fori_loop(0, N_EXPERTS, tail, 0)
        lax.fori_loop(blkrange_ref[0], blkrange_ref[1], unused, 0)

    @pl.when(i == 0)
    def _zero_start():
        zero_ref[...] = jnp.zeros_like(zero_ref)
        zero_fill(lambda c: c.start())

    for s in range(TILES_PER_STEP):
        t = i * TILES_PER_STEP + s

        def make_copy(lo, go, size, s=s):
            return pltpu.make_async_copy(stage_ref.at[s, pl.ds(lo, size)], xbuf_ref.at[pl.ds(go, size)], sems.at[s])

        @pl.when(i > 0)
        def _drain():
            _wait_rows(make_copy, tot_ref[t - TILES_PER_STEP])

        rows = slice(s * TOK_TILE, (s + 1) * TOK_TILE)
        perm = _slot_matrix(lpos_ref[s]).astype(BF16)
        stage_ref[s] = _dot(perm, xn_ref[rows, :])
        _copy_runs(t, cnt_ref, off_ref, goff_ref, make_copy)

        @pl.when(i == last)
        def _finish():
            _wait_rows(make_copy, tot_ref[t])

    @pl.when(i == last)
    def _zero_finish():
        zero_fill(lambda c: c.wait())


def _dispatch(cnt, off, goff, tot, zoff, zlen, blkrange, xn, lpos, n_rows):
    n = xn.shape[0]
    rows = TILES_PER_STEP * TOK_TILE
    return pl.pallas_call(
        _dispatch_body,
        grid_spec=pltpu.PrefetchScalarGridSpec(
            num_scalar_prefetch=7,
            grid=(n // rows,),
            in_specs=[
                pl.BlockSpec((rows, D_MODEL), lambda i, *_: (i, 0)),
                pl.BlockSpec((TILES_PER_STEP, SUBLANES, TOK_TILE), lambda i, *_: (i, 0, 0)),
            ],
            out_specs=pl.BlockSpec(memory_space=pl.ANY),
            scratch_shapes=[
                pltpu.VMEM((TILES_PER_STEP, STAGE_ROWS, D_MODEL), F32),
                pltpu.VMEM((EXPERT_BLOCK, D_MODEL), F32),
                pltpu.SemaphoreType.DMA((TILES_PER_STEP,)),
                pltpu.SemaphoreType.DMA(()),
            ],
        ),
        out_shape=jax.ShapeDtypeStruct((n_rows, D_MODEL), F32),
        compiler_params=_params("arbitrary"),
        name="moe_dispatch",
    )(cnt, off, goff, tot, zoff, zlen, blkrange, xn, lpos)


CAST_ROWS = 128


def _experts_body(blk_e_ref, first_ref, slot_ref, next_e_ref, nused_ref,
                  x_ref, bgu_ref, bd_ref, wgu_hbm, wd_hbm, y_ref,
                  wgu_f32, wd_f32, wgu_bf, wd_bf, sem_gu, sem_d):
    j = pl.program_id(0)

    def weight_copies(e, slot):
        return (pltpu.make_async_copy(wgu_hbm.at[e], wgu_f32.at[slot], sem_gu.at[slot]),
                pltpu.make_async_copy(wd_hbm.at[e], wd_f32.at[slot], sem_d.at[slot]))

    @pl.when(j < nused_ref[0])
    def _():
        e = blk_e_ref[j]
        slot = slot_ref[j]

        @pl.when(first_ref[j] == 1)
        def _new_expert():
            @pl.when(j == 0)
            def _():
                for c in weight_copies(e, slot):
                    c.start()

            for c in weight_copies(e, slot):
                c.wait()
            nxt = next_e_ref[j]

            @pl.when(nxt >= 0)
            def _():
                for c in weight_copies(nxt, 1 - slot):
                    c.start(priority=1)

            def cast(src, dst):
                def rows(r, carry):
                    sl = pl.ds(pl.multiple_of(r * CAST_ROWS, CAST_ROWS), CAST_ROWS)
                    dst[sl, :] = src[slot, sl, :].astype(BF16)
                    return carry
                lax.fori_loop(0, src.shape[1] // CAST_ROWS, rows, 0)

            cast(wgu_f32, wgu_bf)
            cast(wd_f32, wd_bf)

        x = x_ref[...].astype(BF16)
        gu = _dot(x, wgu_bf[...]) + bgu_ref[0]
        gate = jnp.minimum(gu[:, :D_FF], SWIGLU_LIMIT)
        up = jnp.clip(gu[:, D_FF:], -SWIGLU_LIMIT, SWIGLU_LIMIT)
        act = (up + 1.0) * gate * jax.nn.sigmoid(SWIGLU_ALPHA * gate)
        y_ref[...] = _dot(act.astype(BF16), wd_bf[...]) + bd_ref[0]


def _experts(blk_e, first, slot, next_e, nused, xbuf, wgu, bgu, wd, bd):
    n_rows = xbuf.shape[0]
    nblk = n_rows // EXPERT_BLOCK

    def rowblk(j, be, fi, sl, ne, nu):
        return (jnp.minimum(j, nu[0] - 1), 0)

    def expert(j, be, fi, sl, ne, nu):
        return (be[jnp.minimum(j, nu[0] - 1)], 0, 0)

    return pl.pallas_call(
        _experts_body,
        grid_spec=pltpu.PrefetchScalarGridSpec(
            num_scalar_prefetch=5,
            grid=(nblk,),
            in_specs=[
                pl.BlockSpec((EXPERT_BLOCK, D_MODEL), rowblk),
                pl.BlockSpec((1, 1, 2 * D_FF), expert),
                pl.BlockSpec((1, 1, D_MODEL), expert),
                pl.BlockSpec(memory_space=pl.ANY),
                pl.BlockSpec(memory_space=pl.ANY),
            ],
            out_specs=pl.BlockSpec((EXPERT_BLOCK, D_MODEL), rowblk),
            scratch_shapes=[
                pltpu.VMEM((2, D_MODEL, 2 * D_FF), F32),
                pltpu.VMEM((2, D_FF, D_MODEL), F32),
                pltpu.VMEM((D_MODEL, 2 * D_FF), BF16),
                pltpu.VMEM((D_FF, D_MODEL), BF16),
                pltpu.SemaphoreType.DMA((2,)),
                pltpu.SemaphoreType.DMA((2,)),
            ],
        ),
        out_shape=jax.ShapeDtypeStruct((n_rows, D_MODEL), F32),
        input_output_aliases={5: 0},
        compiler_params=_params("arbitrary"),
        name="moe_experts",
    )(blk_e, first, slot, next_e, nused, xbuf, bgu, bd, wgu, wd)


def _combine_body(cnt_ref, off_ref, goff_ref, tot_ref, ybuf_ref, lpos_ref, tw_ref, h1_ref, fg_ref, out_ref,
                  stage_ref, sems):
    i = pl.program_id(0)
    n_tiles = pl.num_programs(0) * TILES_PER_STEP

    def copier(s):
        def make_copy(lo, go, size):
            return pltpu.make_async_copy(ybuf_ref.at[pl.ds(go, size)], stage_ref.at[s, pl.ds(lo, size)], sems.at[s])
        return make_copy

    @pl.when(i == 0)
    def _init():
        stage_ref[...] = jnp.zeros_like(stage_ref)
        _copy_runs(0, cnt_ref, off_ref, goff_ref, copier(0))

    for s in range(TILES_PER_STEP):
        t = i * TILES_PER_STEP + s

        @pl.when(t + 1 < n_tiles)
        def _prefetch():
            _copy_runs(t + 1, cnt_ref, off_ref, goff_ref, copier((s + 1) % TILES_PER_STEP))

        rows = slice(s * TOK_TILE, (s + 1) * TOK_TILE)
        w = _slot_matrix(lpos_ref[s], tw_ref[s]).astype(BF16)
        _wait_rows(copier(s), tot_ref[t])
        h2 = h1_ref[rows, :] + _dot_tn(w, stage_ref[s].astype(BF16))
        ms = jnp.mean(h2 * h2, axis=-1, keepdims=True)
        out_ref[rows, :] = h2 * lax.rsqrt(ms + RMS_EPS) * fg_ref[...]


def _combine(cnt, off, goff, tot, ybuf, lpos, tw, h1, final_g):
    n = h1.shape[0]
    rows = TILES_PER_STEP * TOK_TILE
    row = lambda i, *_: (i, 0)
    return pl.pallas_call(
        _combine_body,
        grid_spec=pltpu.PrefetchScalarGridSpec(
            num_scalar_prefetch=4,
            grid=(n // rows,),
            in_specs=[
                pl.BlockSpec(memory_space=pl.ANY),
                pl.BlockSpec((TILES_PER_STEP, SUBLANES, TOK_TILE), lambda i, *_: (i, 0, 0)),
                pl.BlockSpec((TILES_PER_STEP, SUBLANES, TOK_TILE), lambda i, *_: (i, 0, 0)),
                pl.BlockSpec((rows, D_MODEL), row),
                pl.BlockSpec((1, D_MODEL), lambda i, *_: (0, 0)),
            ],
            out_specs=pl.BlockSpec((rows, D_MODEL), row),
            scratch_shapes=[pltpu.VMEM((TILES_PER_STEP, STAGE_ROWS, D_MODEL), F32),
                            pltpu.SemaphoreType.DMA((TILES_PER_STEP,))],
        ),
        out_shape=jax.ShapeDtypeStruct((n, D_MODEL), F32),
        compiler_params=_params("arbitrary"),
        name="moe_combine",
    )(cnt, off, goff, tot, ybuf, lpos, tw, h1, final_g)


def _moe_layout(cnt_tiles):
    nt = cnt_tiles.shape[0]
    total = jnp.sum(cnt_tiles, axis=0)
    nblk_e = (total + EXPERT_BLOCK - 1) // EXPERT_BLOCK
    blk_end = jnp.cumsum(nblk_e)
    pstart = (blk_end - nblk_e) * EXPERT_BLOCK
    prefix = jnp.cumsum(cnt_tiles, axis=0) - cnt_tiles
    goff = pstart[None, :] + prefix
    off = jnp.cumsum(cnt_tiles, axis=1) - cnt_tiles
    tot = jnp.sum(cnt_tiles, axis=1)
    max_rows = nt * (TOK_TILE * TOP_K + N_EXPERTS * (SUBLANES - 1)) + N_EXPERTS * (EXPERT_BLOCK - 1)
    max_blocks = (max_rows + EXPERT_BLOCK - 1) // EXPERT_BLOCK
    blk_id = jnp.arange(max_blocks, dtype=blk_end.dtype)
    blk_e = jnp.minimum(jnp.sum(blk_end[None, :] <= blk_id[:, None], axis=1), N_EXPERTS - 1).astype(jnp.int32)
    nused = blk_end[-1:].astype(jnp.int32)
    flat = lambda a: a.reshape(-1).astype(jnp.int32)
    zoff = pstart + total
    zlen = nblk_e * EXPERT_BLOCK - total
    blkrange = jnp.stack([nused[0], jnp.int32(max_blocks)])
    used = blk_id < nused[0]
    first = used & jnp.concatenate([jnp.ones((1,), bool), blk_e[1:] != blk_e[:-1]])
    slot = (jnp.cumsum(first.astype(jnp.int32)) - 1) % 2
    eid = jnp.arange(N_EXPERTS)
    later = (eid[None, :] > eid[:, None]) & (nblk_e > 0)[None, :]
    next_of = jnp.min(jnp.where(later, eid[None, :], N_EXPERTS), axis=1)
    next_of = jnp.where(next_of < N_EXPERTS, next_of, -1)
    experts_plan = (blk_e, first.astype(jnp.int32), slot.astype(jnp.int32), next_of[blk_e].astype(jnp.int32), nused)
    dispatch_plan = (flat(cnt_tiles), flat(off), flat(goff), flat(tot))
    zero_plan = (flat(zoff), flat(zlen), blkrange.astype(jnp.int32))
    return dispatch_plan, zero_plan, experts_plan, max_blocks * EXPERT_BLOCK


def kernel(x, meta_tokens, attn_norm_g, w_in, na_rpb, na_norm_g, hgrn_lb_logits, hgrn_norm_g, w_out, ffn_norm_g,
           router_w, router_b, expert_w_gu, expert_b_gu, expert_w_down, expert_b_down, final_norm_g):
    B, T, D = x.shape
    x2d = x.reshape(B * T, D)
    w_in_b = w_in[0].astype(BF16)
    ng = attn_norm_g[0].reshape(1, D)
    lbl = hgrn_lb_logits[:, :, :].reshape(4, HG_WIDTH)

    att, hg = _inproj(x2d, ng, w_in_b, lbl, 512)
    att_m, hg_m = _inproj(meta_tokens.astype(F32), ng, w_in_b, lbl, N_META)
    att_m = jnp.pad(att_m, ((0, META_PAD - N_META), (0, 0)))
    hg_m = jnp.pad(hg_m, ((0, HG_CHUNK - N_META), (0, 0)))

    yna = _natten(att, att_m, _natten_bias_tables(na_rpb[0]), na_norm_g[0].reshape(1, NA_WIDTH), B, T)
    o_f, o_b = _hgrn(hg, hg_m, B, T)

    h1, xn, lpos, tw, cnt_tiles = _mix_route(
        yna.reshape(B * T, NA_WIDTH), o_f.reshape(B * T, HG_WIDTH), o_b.reshape(B * T, HG_WIDTH), hg, x2d,
        w_out[0].astype(BF16), hgrn_norm_g[0].reshape(1, HG_WIDTH), ffn_norm_g[0].reshape(1, D),
        router_w[0], router_b[0].reshape(1, N_EXPERTS))

    dispatch_plan, zero_plan, experts_plan, n_rows = _moe_layout(cnt_tiles[:, :, 0])
    xbuf = _dispatch(*dispatch_plan, *zero_plan, xn, lpos, n_rows)
    ybuf = _experts(*experts_plan, xbuf, expert_w_gu[0], expert_b_gu[0][:, None, :],
                    expert_w_down[0], expert_b_down[0][:, None, :])
    out = _combine(*dispatch_plan, ybuf, lpos, tw, h1, final_norm_g.reshape(1, D))
    return out.reshape(B, T, D)
```

```python
import functools

import numpy as np
import jax
import jax.numpy as jnp
from jax import lax
from jax.experimental import pallas as pl
from jax.experimental.pallas import tpu as pltpu

F32 = jnp.float32
BF16 = jnp.bfloat16

D_MODEL = 1024
N_META = 16
GRID_W = 64
NA_WIDTH = 512
NA_HEAD_DIM = 64
NA_HEADS = 8
NA_KH = 8
NA_KW = 16
HG_WIDTH = 512
HG_HEAD_DIM = 128
HG_HEADS = 4
HG_CHUNK = 64
IN_COLS = 3 * NA_WIDTH + 5 * HG_WIDTH
N_EXPERTS = 32
TOP_K = 4
D_FF = 1024
SWIGLU_LIMIT = 7.0
SWIGLU_ALPHA = 1.702
RMS_EPS = 1e-6
NEG_BIG = -1e30
LOG2_E = 1.4426950408889634

LANES = 128
VMEM_LIMIT_BYTES = 56 * 1024 * 1024

TOK_TILE = 256
EXPERT_BLOCK = 512
SUBLANES = 8
RUN_ALIGN = 16
STAGE_ROWS = TOK_TILE * TOP_K + N_EXPERTS * RUN_ALIGN
N_HG_LEVELS = 6


def _dot(a, b):
    return jnp.dot(a, b, preferred_element_type=F32)


def _dot_nt(a, b):
    return lax.dot_general(a, b, (((1,), (1,)), ((), ())), preferred_element_type=F32)


def _dot_tn(a, b):
    return lax.dot_general(a, b, (((0,), (0,)), ((), ())), preferred_element_type=F32)


def _split_bf16(x):
    hi = x.astype(BF16)
    lo = (x - hi.astype(F32)).astype(BF16)
    return hi, lo


def _pack_rows(rows, n_rows):
    idx = lax.broadcasted_iota(jnp.int32, (n_rows, rows[0].shape[1]), 0)
    out = jnp.zeros(idx.shape, rows[0].dtype)
    for k, r in enumerate(rows):
        out = jnp.where(idx == k, r, out)
    return out


def _params(*sem):
    return pltpu.CompilerParams(dimension_semantics=sem, vmem_limit_bytes=VMEM_LIMIT_BYTES)


def _inproj_body(x_ref, ng_ref, w_ref, lbl_ref, att_ref, hg_ref):
    x = x_ref[...]
    ms = jnp.mean(x * x, axis=-1, keepdims=True)
    n = (x * lax.rsqrt(ms + RMS_EPS) * ng_ref[...]).astype(BF16)

    def proj(lo, hi):
        return _dot(n, w_ref[:, lo:hi])

    pq = proj(0, NA_WIDTH)
    att_ref[:, 0:NA_WIDTH] = (pq * (NA_HEAD_DIM ** -0.5 * LOG2_E)).astype(BF16)
    att_ref[:, NA_WIDTH:3 * NA_WIDTH] = proj(NA_WIDTH, 3 * NA_WIDTH).astype(BF16)

    base = 3 * NA_WIDTH
    W = HG_WIDTH
    qh = proj(base, base + W)
    hg_ref[:, 0:W] = qh * jax.nn.sigmoid(qh)
    hg_ref[:, W:2 * W] = proj(base + W, base + 2 * W)
    lbl = lbl_ref[...]
    for d in range(2):
        a0 = lbl[2 * d:2 * d + 1, :]
        a1 = lbl[2 * d + 1:2 * d + 2, :]
        m = jnp.maximum(a0, a1)
        e0 = jnp.exp(a0 - m)
        e1 = jnp.exp(a1 - m)
        lb = e0 / (e0 + e1)
        raw = proj(base + (2 + d) * W, base + (3 + d) * W)
        f = lb + (1.0 - lb) * jax.nn.sigmoid(raw)
        hg_ref[:, (2 + d) * W:(3 + d) * W] = jnp.log(f)
    gt = proj(base + 4 * W, base + 5 * W)
    hg_ref[:, 4 * W:5 * W] = gt * jax.nn.sigmoid(gt)


def _inproj(x2d, norm_g, w_bf16, lb_logits4, tm):
    n = x2d.shape[0]
    return pl.pallas_call(
        _inproj_body,
        grid=(n // tm,),
        in_specs=[
            pl.BlockSpec((tm, D_MODEL), lambda i: (i, 0)),
            pl.BlockSpec((1, D_MODEL), lambda i: (0, 0)),
            pl.BlockSpec((D_MODEL, IN_COLS), lambda i: (0, 0)),
            pl.BlockSpec((4, HG_WIDTH), lambda i: (0, 0)),
        ],
        out_specs=[
            pl.BlockSpec((tm, 3 * NA_WIDTH), lambda i: (i, 0)),
            pl.BlockSpec((tm, 5 * HG_WIDTH), lambda i: (i, 0)),
        ],
        out_shape=[
            jax.ShapeDtypeStruct((n, 3 * NA_WIDTH), BF16),
            jax.ShapeDtypeStruct((n, 5 * HG_WIDTH), F32),
        ],
        compiler_params=_params("arbitrary"),
        name="inproj",
    )(x2d, norm_g, w_bf16, lb_logits4)


N_WIN_KEYS = NA_KH * GRID_W
META_PAD = LANES


NA_GROUP = 4
GROUP_LANES = NA_GROUP * NA_HEAD_DIM


NA_ROWS_PER_STEP = 8


def _natten_row(rr, r, q_ref, k_ref, v_ref, km_ref, vm_ref, bias_ref, *, rows):
    rs = jnp.clip(r - NA_KH // 2, 0, rows - NA_KH)
    d0 = rs - r + (NA_KH - 1)
    start = pl.multiple_of(rs * GRID_W, GRID_W)
    lane_head = lax.broadcasted_iota(jnp.int32, (GRID_W, GROUP_LANES), 1) // NA_HEAD_DIM
    meta_col = lax.broadcasted_iota(jnp.int32, (1, META_PAD), 1)
    meta_bias = jnp.where(meta_col < N_META, 0.0, NEG_BIG).astype(F32)
    folded = []
    for g in range(NA_HEADS // NA_GROUP):
        sl = slice(g * GROUP_LANES, (g + 1) * GROUP_LANES)
        q4 = q_ref[0, rr * GRID_W:(rr + 1) * GRID_W, sl]
        zero = jnp.zeros_like(q4)
        qm = jnp.concatenate([jnp.where(lane_head == h, q4, zero) for h in range(NA_GROUP)], axis=0)
        kw = k_ref[0, pl.ds(start, N_WIN_KEYS), sl]
        vw = v_ref[0, pl.ds(start, N_WIN_KEYS), sl]
        hs = slice(g * NA_GROUP, (g + 1) * NA_GROUP)
        bias = jnp.concatenate(
            [bias_ref[d0 + 2 * j, hs].reshape(NA_GROUP * GRID_W, 2 * GRID_W) for j in range(NA_KH // 2)], axis=-1)
        s = _dot_nt(qm, kw) + bias
        sm = _dot_nt(qm, km_ref[:, sl]) + meta_bias
        m = jnp.maximum(jnp.max(s, axis=-1, keepdims=True), jnp.max(sm, axis=-1, keepdims=True))
        e = jnp.exp2(s - m)
        em = jnp.exp2(sm - m)
        den = jnp.sum(e, axis=-1, keepdims=True) + jnp.sum(em, axis=-1, keepdims=True)
        o = (_dot(e.astype(BF16), vw) + _dot(em.astype(BF16), vm_ref[:, sl])) * (1.0 / den)
        acc = jnp.where(lane_head == 0, o[0:GRID_W], 0.0)
        for h in range(1, NA_GROUP):
            acc = jnp.where(lane_head == h, o[h * GRID_W:(h + 1) * GRID_W], acc)
        folded.append(acc)
    return jnp.concatenate(folded, axis=-1)


def _natten_body(q_ref, k_ref, v_ref, km_ref, vm_ref, bias_ref, ng_ref, bd_ref, o_ref, *, rows):
    r0 = pl.program_id(1) * NA_ROWS_PER_STEP
    o2 = jnp.concatenate(
        [_natten_row(rr, r0 + rr, q_ref, k_ref, v_ref, km_ref, vm_ref, bias_ref, rows=rows)
         for rr in range(NA_ROWS_PER_STEP)], axis=0)
    sq_hi, sq_lo = _split_bf16(o2 * o2)
    ms = (_dot(sq_hi, bd_ref[...]) + _dot(sq_lo, bd_ref[...])) * (1.0 / NA_HEAD_DIM)
    o_ref[0] = (o2 * lax.rsqrt(ms + RMS_EPS) * ng_ref[...]).astype(o_ref.dtype)


def _natten(att, att_meta, bias_tabs, norm_g, batch, seq):
    rows = seq // GRID_W
    att3 = att.reshape(batch, seq, 3 * NA_WIDTH)
    head_of = np.arange(NA_WIDTH) // NA_HEAD_DIM
    same_head = jnp.asarray(head_of[:, None] == head_of[None, :], BF16)

    qrows = NA_ROWS_PER_STEP * GRID_W
    return pl.pallas_call(
        functools.partial(_natten_body, rows=rows),
        grid=(batch, rows // NA_ROWS_PER_STEP),
        in_specs=[
            pl.BlockSpec((1, qrows, NA_WIDTH), lambda b, r: (b, r, 0)),
            pl.BlockSpec((1, seq, NA_WIDTH), lambda b, r: (b, 0, 1)),
            pl.BlockSpec((1, seq, NA_WIDTH), lambda b, r: (b, 0, 2)),
            pl.BlockSpec((META_PAD, NA_WIDTH), lambda b, r: (0, 1)),
            pl.BlockSpec((META_PAD, NA_WIDTH), lambda b, r: (0, 2)),
            pl.BlockSpec(bias_tabs.shape, lambda b, r: (0, 0, 0, 0), pipeline_mode=pl.Buffered(1)),
            pl.BlockSpec((1, NA_WIDTH), lambda b, r: (0, 0)),
            pl.BlockSpec((NA_WIDTH, NA_WIDTH), lambda b, r: (0, 0)),
        ],
        out_specs=pl.BlockSpec((1, qrows, NA_WIDTH), lambda b, r: (b, r, 0)),
        out_shape=jax.ShapeDtypeStruct((batch, seq, NA_WIDTH), BF16),
        compiler_params=_params("arbitrary", "arbitrary"),
        name="natten",
    )(att3, att3, att3, att_meta, att_meta, bias_tabs, norm_g, same_head)


def _natten_bias_tables(rpb):
    c = np.arange(GRID_W)[:, None]
    kc = np.arange(GRID_W)[None, :]
    cs = np.clip(c - NA_KW // 2, 0, GRID_W - NA_KW)
    valid = (kc >= cs) & (kc < cs + NA_KW)
    dc = kc - c + (NA_KW - 1)
    pick = np.asarray(dc[None] == np.arange(2 * NA_KW - 1)[:, None, None], np.float32)
    t1 = jnp.einsum("hdj,jck->hdck", rpb.astype(F32) * LOG2_E, pick, precision=lax.Precision.HIGHEST)
    t1 = jnp.where(valid[None, None], t1, NEG_BIG)
    pairs = jnp.concatenate([t1[:, :-1], t1[:, 1:]], axis=-1)
    return pairs.transpose(1, 0, 2, 3)


HG_CHUNKS_PER_STEP = 8
HG_CHUNKS_PER_GROUP = 1
ROW_B, ROW_LV = 0, 1
N_EXP_BLOCKS = 1 + N_HG_LEVELS


def _hgrn_constants():
    C = HG_CHUNK
    t = np.arange(C)
    u = t[None, :]
    mats = [u <= t[:, None]]
    masks = [np.eye(C, dtype=bool)]
    for lv in range(N_HG_LEVELS):
        m = 1 << lv
        blk = t // (2 * m)
        upper = (t // m) % 2 == 1
        p = blk * 2 * m + m - 1
        q_rows = upper[:, None] & (u > p[:, None]) & (u <= t[:, None])
        k_rows = (~upper)[:, None] & (u > t[:, None]) & (u <= p[:, None])
        mats.append(q_rows | k_rows)
        masks.append((blk[:, None] == blk[None, :]) & upper[:, None] & (~upper)[None, :])
    fwd = np.concatenate(mats, axis=0).astype(np.float32)
    fwd_mask = np.stack(masks).astype(np.float32)
    blocks = fwd.reshape(N_EXP_BLOCKS, C, C)
    bwd = blocks[:, ::-1, ::-1].reshape(N_EXP_BLOCKS * C, C)
    bwd_mask = fwd_mask[:, ::-1, ::-1]
    both = np.stack([fwd, bwd])
    both = np.concatenate([both, both], axis=-1)
    both_mask = np.stack([fwd_mask, bwd_mask])
    both_mask = np.concatenate([both_mask, both_mask], axis=-1)
    return (jnp.asarray(both, BF16), jnp.asarray(both_mask, F32))


def _hgrn_exponents(mat, g):
    g_hi, g_lo = _split_bf16(g)
    return _dot(mat, jnp.concatenate([g_hi, g_lo], axis=0))


def _hgrn_init(meta_ref, mat_ref, stf_ref, stb_ref):
    C, W = HG_CHUNK, HG_WIDTH
    mb = meta_ref[...]
    v = mb[:, W:2 * W].astype(BF16)
    prefix = mat_ref[0, ROW_B * C:(ROW_B + 1) * C, :]
    for d, st_ref in ((0, stf_ref), (1, stb_ref)):
        g = mb[:, (2 + d) * W:(3 + d) * W]
        b = _hgrn_exponents(prefix, g)
        kb = ((1.0 - jnp.exp(g)) * jnp.exp(b[C - 1:C] - b)).astype(BF16)
        for h in range(HG_HEADS):
            sl = slice(h * HG_HEAD_DIM, (h + 1) * HG_HEAD_DIM)
            st_ref[h] = _dot_tn(v[:, sl], kb[:, sl])


def _hgrn_group_tasks(j0):
    js = range(j0, j0 + HG_CHUNKS_PER_GROUP)
    return ([(0, j * HG_CHUNK) for j in js] +
            [(1, (HG_CHUNKS_PER_STEP - 1 - j) * HG_CHUNK) for j in js])


def _hgrn_tasks(tasks, dir_refs, states, mat_ref, mask_ref):
    C, W = HG_CHUNK, HG_WIDTH
    heads = [slice(h * HG_HEAD_DIM, (h + 1) * HG_HEAD_DIM) for h in range(HG_HEADS)]
    qs, ks, vs, ex_bs, ex_lvs, ex_ends = {}, {}, {}, {}, {}, {}
    for d, r0 in tasks:
        q_ref, v_ref, g_ref, total_row = dir_refs[d][0], dir_refs[d][1], dir_refs[d][2], dir_refs[d][3]
        g = g_ref[0, r0:r0 + C, :]
        sums = _hgrn_exponents(mat_ref[d], g)
        ex_bs[d, r0] = jnp.exp(sums[0:C])
        ex_lvs[d, r0] = jnp.exp(sums[C:]).astype(BF16)
        ex_ends[d, r0] = jnp.exp(sums[total_row:total_row + 1] - sums[0:C])
        qs[d, r0] = q_ref[0, r0:r0 + C, :]
        ks[d, r0] = 1.0 - jnp.exp(g)
        vs[d, r0] = v_ref[0, r0:r0 + C, :].astype(BF16)

    D2 = 2 * HG_HEAD_DIM
    pairs = [slice(p * D2, (p + 1) * D2) for p in range(HG_HEADS // 2)]

    def block_diag(x):
        zero = jnp.zeros((C, HG_HEAD_DIM), x.dtype)
        return jnp.concatenate([jnp.concatenate([x[:, :HG_HEAD_DIM], zero], axis=1),
                                jnp.concatenate([zero, x[:, HG_HEAD_DIM:]], axis=1)], axis=0)

    prods = {}
    for task in tasks:
        for p, sl in enumerate(pairs):
            q2, k2 = qs[task][:, sl].astype(BF16), ks[task][:, sl].astype(BF16)
            ps = [_dot_nt(q2, block_diag(k2))]
            for lv in range(N_HG_LEVELS):
                scale = ex_lvs[task][lv * C:(lv + 1) * C, sl]
                ps.append(_dot_nt(q2 * scale, block_diag(k2 * scale)))
            prods[task, p] = ps
    attn, qbs, kbs = {}, {}, {}
    for task in tasks:
        d = task[0]
        for p, sl in enumerate(pairs):
            a = prods[task, p][0] * mask_ref[d, 0]
            for lv in range(N_HG_LEVELS):
                a = a + prods[task, p][1 + lv] * mask_ref[d, 1 + lv]
            o2 = _dot(a.astype(BF16), block_diag(vs[task][:, sl]))
            attn[task, 2 * p] = o2[:, :HG_HEAD_DIM]
            attn[task, 2 * p + 1] = o2[:, HG_HEAD_DIM:]
        for h, sl in enumerate(heads):
            qbs[task, h] = (qs[task][:, sl] * ex_bs[task][:, sl]).astype(BF16)
            kbs[task, h] = _dot_tn(vs[task][:, sl], (ks[task][:, sl] * ex_ends[task][:, sl]).astype(BF16))
    for task in tasks:
        d, r0 = task
        total_row, o_ref = dir_refs[d][3], dir_refs[d][4]
        outs = []
        for h, sl in enumerate(heads):
            st = states[d, h]
            outs.append(attn[task, h] + _dot_nt(qbs[task, h], st.astype(BF16)))
            decay = ex_bs[task][total_row:total_row + 1, sl]
            states[d, h] = decay * st + kbs[task, h]
        o_ref[0, r0:r0 + C, :] = jnp.concatenate(outs, axis=-1)


def _hgrn_body(hqf_ref, hvf_ref, hgf_ref, hqb_ref, hvb_ref, hgb_ref, meta_ref, mat_ref, mask_ref,
               of_ref, ob_ref, stf_ref, stb_ref):
    @pl.when(pl.program_id(1) == 0)
    def _init():
        _hgrn_init(meta_ref, mat_ref, stf_ref, stb_ref)

    dir_refs = ((hqf_ref, hvf_ref, hgf_ref, HG_CHUNK - 1, of_ref), (hqb_ref, hvb_ref, hgb_ref, 0, ob_ref))
    st_refs = (stf_ref, stb_ref)
    states = {(d, h): st_refs[d][h] for d in range(2) for h in range(HG_HEADS)}
    for j0 in range(0, HG_CHUNKS_PER_STEP, HG_CHUNKS_PER_GROUP):
        _hgrn_tasks(_hgrn_group_tasks(j0), dir_refs, states, mat_ref, mask_ref)
    for d in range(2):
        for h in range(HG_HEADS):
            st_refs[d][h] = states[d, h]


def _hgrn(hg, hg_meta_pad, batch, seq):
    steps = seq // (HG_CHUNK * HG_CHUNKS_PER_STEP)
    hg3 = hg.reshape(batch, seq, 5 * HG_WIDTH)
    mats, masks = _hgrn_constants()
    hblk = (1, HG_CHUNK * HG_CHUNKS_PER_STEP, HG_WIDTH)

    def fwd(col):
        return pl.BlockSpec(hblk, lambda b, s: (b, s, col))

    def bwd(col):
        return pl.BlockSpec(hblk, lambda b, s: (b, steps - 1 - s, col))

    return pl.pallas_call(
        _hgrn_body,
        grid=(batch, steps),
        in_specs=[
            fwd(0), fwd(1), fwd(2),
            bwd(0), bwd(1), bwd(3),
            pl.BlockSpec((HG_CHUNK, 5 * HG_WIDTH), lambda b, s: (0, 0)),
            pl.BlockSpec(mats.shape, lambda b, s: (0, 0, 0)),
            pl.BlockSpec(masks.shape, lambda b, s: (0, 0, 0, 0)),
        ],
        out_specs=[fwd(0), bwd(0)],
        out_shape=[jax.ShapeDtypeStruct((batch, seq, HG_WIDTH), F32)] * 2,
        scratch_shapes=[pltpu.VMEM((HG_HEADS, HG_HEAD_DIM, HG_HEAD_DIM), F32)] * 2,
        compiler_params=_params("arbitrary", "arbitrary"),
        name="hgrn",
    )(hg3, hg3, hg3, hg3, hg3, hg3, hg_meta_pad, mats, masks)


MIX_TILES_PER_STEP = 4


def _mix_route_body(yna_ref, of_ref, ob_ref, gate_ref, x_ref, wout_ref, hgn_ref, ffg_ref,
                    rwh_ref, rwl_ref, rb_ref, tri_ref, lowe_ref,
                    h1_ref, xn_ref, lpos_ref, tw_ref, cnt_ref):
    T, E = TOK_TILE, N_EXPERTS
    tiles = [slice(s * T, (s + 1) * T) for s in range(MIX_TILES_PER_STEP)]
    x_his, x_los = [], []
    for rows in tiles:
        o = of_ref[rows, :] + ob_ref[rows, :]
        parts = []
        for h in range(HG_HEADS):
            seg = o[:, h * HG_HEAD_DIM:(h + 1) * HG_HEAD_DIM]
            ms = jnp.mean(seg * seg, axis=-1, keepdims=True)
            parts.append(seg * lax.rsqrt(ms + RMS_EPS))
        yhg = jnp.concatenate(parts, axis=-1) * hgn_ref[...] * gate_ref[rows, :]
        mix = _dot(yna_ref[rows, :], wout_ref[0:NA_WIDTH, :]) + _dot(yhg.astype(BF16), wout_ref[NA_WIDTH:, :])
        h1 = x_ref[rows, :] + mix
        h1_ref[rows, :] = h1
        ms = jnp.mean(h1 * h1, axis=-1, keepdims=True)
        xn = h1 * lax.rsqrt(ms + RMS_EPS) * ffg_ref[...]
        x_hi, x_lo = _split_bf16(xn)
        xn_ref[rows, :] = x_hi
        x_his.append(x_hi)
        x_los.append(x_lo)

    curs = [(_dot_nt(rwh_ref[...], x_hi) + _dot_nt(rwh_ref[...], x_lo) + _dot_nt(rwl_ref[...], x_hi)) + rb_ref[...]
            for x_hi, x_lo in zip(x_his, x_los)]
    row = lax.broadcasted_iota(jnp.int32, (E, T), 0).astype(F32)
    sels = [[] for _ in tiles]
    vals = [[] for _ in tiles]
    for _ in range(TOP_K):
        for s in range(len(tiles)):
            m = jnp.max(curs[s], axis=0, keepdims=True)
            first = jnp.min(jnp.where(curs[s] == m, row, float(E)), axis=0, keepdims=True)
            sel = row == first
            sels[s].append(sel)
            vals[s].append(m)
            curs[s] = jnp.where(sel, -jnp.inf, curs[s])
    for s in range(len(tiles)):
        es = [jnp.exp(vk - vals[s][0]) for vk in vals[s]]
        den = es[0] + es[1] + es[2] + es[3]
        tw_ref[s] = _pack_rows([e / den for e in es], SUBLANES)

        onehot = jnp.zeros((E, T), F32)
        for sel in sels[s]:
            onehot = onehot + jnp.where(sel, 1.0, 0.0)
        lrank = _dot(onehot.astype(BF16), tri_ref[...])
        cnt = jnp.sum(onehot, axis=1, keepdims=True)
        cnt = jnp.floor((cnt + (RUN_ALIGN - 1)) * (1.0 / RUN_ALIGN)) * RUN_ALIGN
        off = _dot(lowe_ref[...], jnp.broadcast_to(cnt, (E, LANES)).astype(BF16))
        base = jnp.concatenate([off] * (T // LANES), axis=1) + lrank
        lpos_ref[s] = _pack_rows(
            [jnp.sum(jnp.where(sel, base, 0.0), axis=0, keepdims=True) for sel in sels[s]],
            SUBLANES).astype(jnp.int32)
        cnt_ref[s] = cnt.astype(jnp.int32)


def _mix_route(yna, o_f, o_b, hg, x2d, wout_bf16, hgn, ffg, router_w, router_b):
    n = x2d.shape[0]
    T = TOK_TILE
    nt = n // T
    rw_t = router_w.T
    rwh = rw_t.astype(BF16)
    rwl = (rw_t - rwh.astype(F32)).astype(BF16)
    router_b = router_b.reshape(N_EXPERTS, 1)
    tri = jnp.asarray(np.triu(np.ones((T, T), np.float32), 1), BF16)
    lowe = jnp.asarray(np.tril(np.ones((N_EXPERTS, N_EXPERTS), np.float32), -1), BF16)
    row = lambda i: (i, 0)
    const = lambda i: (0, 0)
    tile3 = lambda i: (i, 0, 0)
    K = MIX_TILES_PER_STEP
    R = K * T
    return pl.pallas_call(
        _mix_route_body,
        grid=(nt // K,),
        in_specs=[
            pl.BlockSpec((R, NA_WIDTH), row),
            pl.BlockSpec((R, HG_WIDTH), row),
            pl.BlockSpec((R, HG_WIDTH), row),
            pl.BlockSpec((R, HG_WIDTH), lambda i: (i, 4)),
            pl.BlockSpec((R, D_MODEL), row),
            pl.BlockSpec((D_MODEL, D_MODEL), const),
            pl.BlockSpec((1, HG_WIDTH), const),
            pl.BlockSpec((1, D_MODEL), const),
            pl.BlockSpec((N_EXPERTS, D_MODEL), const),
            pl.BlockSpec((N_EXPERTS, D_MODEL), const),
            pl.BlockSpec((N_EXPERTS, 1), const),
            pl.BlockSpec((T, T), const),
            pl.BlockSpec((N_EXPERTS, N_EXPERTS), const),
        ],
        out_specs=[
            pl.BlockSpec((R, D_MODEL), row),
            pl.BlockSpec((R, D_MODEL), row),
            pl.BlockSpec((K, SUBLANES, T), tile3),
            pl.BlockSpec((K, SUBLANES, T), tile3),
            pl.BlockSpec((K, N_EXPERTS, 1), tile3),
        ],
        out_shape=[
            jax.ShapeDtypeStruct((n, D_MODEL), F32),
            jax.ShapeDtypeStruct((n, D_MODEL), BF16),
            jax.ShapeDtypeStruct((nt, SUBLANES, T), jnp.int32),
            jax.ShapeDtypeStruct((nt, SUBLANES, T), F32),
            jax.ShapeDtypeStruct((nt, N_EXPERTS, 1), jnp.int32),
        ],
        compiler_params=_params("arbitrary"),
        name="mix_route",
    )(yna, o_f, o_b, hg, x2d, wout_bf16, hgn, ffg, rwh, rwl, router_b, tri, lowe)


TILES_PER_STEP = 2


def _wait_rows(make_copy, rows):
    make_copy(0, 0, pl.multiple_of(rows, RUN_ALIGN)).wait()


def _copy_runs(i, cnt_ref, off_ref, goff_ref, make_copy):
    def per_expert(e, carry):
        n = pl.multiple_of(cnt_ref[i * N_EXPERTS + e], RUN_ALIGN)
        lo = pl.multiple_of(off_ref[i * N_EXPERTS + e], RUN_ALIGN)
        go = pl.multiple_of(goff_ref[i * N_EXPERTS + e], RUN_ALIGN)

        @pl.when(n > 0)
        def _start():
            make_copy(lo, go, n).start()

        return carry

    lax.fori_loop(0, N_EXPERTS, per_expert, 0)


def _slot_matrix(lpos, weights=None):
    row = lax.broadcasted_iota(jnp.int32, (STAGE_ROWS, TOK_TILE), 0)
    acc = jnp.zeros((STAGE_ROWS, TOK_TILE), F32)
    for k in range(TOP_K):
        hit = lpos[k:k + 1, :] == row
        val = 1.0 if weights is None else weights[k:k + 1, :]
        acc = jnp.where(hit, val, acc)
    return acc


def _dispatch_body(cnt_ref, off_ref, goff_ref, tot_ref, zoff_ref, zlen_ref, blkrange_ref,
                   xn_ref, lpos_ref, xbuf_ref, stage_ref, zero_ref, sems, zsem):
    i = pl.program_id(0)
    last = pl.num_programs(0) - 1

    def zero_fill(act):
        def tail(e, carry):
            n = pl.multiple_of(zlen_ref[e], RUN_ALIGN)

            @pl.when(n > 0)
            def _():
                go = pl.multiple_of(zoff_ref[e], RUN_ALIGN)
                act(pltpu.make_async_copy(zero_ref.at[pl.ds(0, n)], xbuf_ref.at[pl.ds(go, n)], zsem))

            return carry

        def unused(j, carry):
            go = pl.multiple_of(j * EXPERT_BLOCK, EXPERT_BLOCK)
            act(pltpu.make_async_copy(zero_ref, xbuf_ref.at[pl.ds(go, EXPERT_BLOCK)], zsem))
            return carry

        lax.fori_loop(0, N_EXPERTS, tail, 0)
        lax.fori_loop(blkrange_ref[0], blkrange_ref[1], unused, 0)

    @pl.when(i == 0)
    def _zero_start():
        zero_ref[...] = jnp.zeros_like(zero_ref)
        zero_fill(lambda c: c.start())

    for s in range(TILES_PER_STEP):
        t = i * TILES_PER_STEP + s

        def make_copy(lo, go, size, s=s):
            return pltpu.make_async_copy(stage_ref.at[s, pl.ds(lo, size)], xbuf_ref.at[pl.ds(go, size)], sems.at[s])

        @pl.when(i > 0)
        def _drain():
            _wait_rows(make_copy, tot_ref[t - TILES_PER_STEP])

        rows = slice(s * TOK_TILE, (s + 1) * TOK_TILE)
        perm = _slot_matrix(lpos_ref[s]).astype(BF16)
        stage_ref[s] = _dot(perm, xn_ref[rows, :]).astype(BF16)
        _copy_runs(t, cnt_ref, off_ref, goff_ref, make_copy)

        @pl.when(i == last)
        def _finish():
            _wait_rows(make_copy, tot_ref[t])

    @pl.when(i == last)
    def _zero_finish():
        zero_fill(lambda c: c.wait())


def _dispatch(cnt, off, goff, tot, zoff, zlen, blkrange, xn, lpos, n_rows):
    n = xn.shape[0]
    rows = TILES_PER_STEP * TOK_TILE
    return pl.pallas_call(
        _dispatch_body,
        grid_spec=pltpu.PrefetchScalarGridSpec(
            num_scalar_prefetch=7,
            grid=(n // rows,),
            in_specs=[
                pl.BlockSpec((rows, D_MODEL), lambda i, *_: (i, 0)),
                pl.BlockSpec((TILES_PER_STEP, SUBLANES, TOK_TILE), lambda i, *_: (i, 0, 0)),
            ],
            out_specs=pl.BlockSpec(memory_space=pl.ANY),
            scratch_shapes=[
                pltpu.VMEM((TILES_PER_STEP, STAGE_ROWS, D_MODEL), BF16),
                pltpu.VMEM((EXPERT_BLOCK, D_MODEL), BF16),
                pltpu.SemaphoreType.DMA((TILES_PER_STEP,)),
                pltpu.SemaphoreType.DMA(()),
            ],
        ),
        out_shape=jax.ShapeDtypeStruct((n_rows, D_MODEL), BF16),
        compiler_params=_params("arbitrary"),
        name="moe_dispatch",
    )(cnt, off, goff, tot, zoff, zlen, blkrange, xn, lpos)


CAST_ROWS = 128


def _experts_body(blk_e_ref, first_ref, slot_ref, next_e_ref, nused_ref,
                  x_ref, bgu_ref, bd_ref, wgu_hbm, wd_hbm, y_ref,
                  wgu_f32, wd_f32, wgu_bf, wd_bf, sem_gu, sem_d):
    j = pl.program_id(0)

    def weight_copies(e, slot):
        return (pltpu.make_async_copy(wgu_hbm.at[e], wgu_f32.at[slot], sem_gu.at[slot]),
                pltpu.make_async_copy(wd_hbm.at[e], wd_f32.at[slot], sem_d.at[slot]))

    @pl.when(j < nused_ref[0])
    def _():
        e = blk_e_ref[j]
        slot = slot_ref[j]

        @pl.when(first_ref[j] == 1)
        def _new_expert():
            @pl.when(j == 0)
            def _():
                for c in weight_copies(e, slot):
                    c.start()

            for c in weight_copies(e, slot):
                c.wait()
            nxt = next_e_ref[j]

            @pl.when(nxt >= 0)
            def _():
                for c in weight_copies(nxt, 1 - slot):
                    c.start(priority=1)

            def cast(src, dst):
                def rows(r, carry):
                    sl = pl.ds(pl.multiple_of(r * CAST_ROWS, CAST_ROWS), CAST_ROWS)
                    dst[sl, :] = src[slot, sl, :].astype(BF16)
                    return carry
                lax.fori_loop(0, src.shape[1] // CAST_ROWS, rows, 0)

            cast(wgu_f32, wgu_bf)
            cast(wd_f32, wd_bf)

        gu = _dot(x_ref[...], wgu_bf[...]) + bgu_ref[0]
        gate = jnp.minimum(gu[:, :D_FF], SWIGLU_LIMIT)
        up = jnp.clip(gu[:, D_FF:], -SWIGLU_LIMIT, SWIGLU_LIMIT)
        act = (up + 1.0) * gate * jax.nn.sigmoid(SWIGLU_ALPHA * gate)
        y_ref[...] = (_dot(act.astype(BF16), wd_bf[...]) + bd_ref[0]).astype(BF16)


def _experts(blk_e, first, slot, next_e, nused, xbuf, wgu, bgu, wd, bd):
    n_rows = xbuf.shape[0]
    nblk = n_rows // EXPERT_BLOCK

    def rowblk(j, be, fi, sl, ne, nu):
        return (jnp.minimum(j, nu[0] - 1), 0)

    def expert(j, be, fi, sl, ne, nu):
        return (be[jnp.minimum(j, nu[0] - 1)], 0, 0)

    return pl.pallas_call(
        _experts_body,
        grid_spec=pltpu.PrefetchScalarGridSpec(
            num_scalar_prefetch=5,
            grid=(nblk,),
            in_specs=[
                pl.BlockSpec((EXPERT_BLOCK, D_MODEL), rowblk),
                pl.BlockSpec((1, 1, 2 * D_FF), expert),
                pl.BlockSpec((1, 1, D_MODEL), expert),
                pl.BlockSpec(memory_space=pl.ANY),
                pl.BlockSpec(memory_space=pl.ANY),
            ],
            out_specs=pl.BlockSpec((EXPERT_BLOCK, D_MODEL), rowblk),
            scratch_shapes=[
                pltpu.VMEM((2, D_MODEL, 2 * D_FF), F32),
                pltpu.VMEM((2, D_FF, D_MODEL), F32),
                pltpu.VMEM((D_MODEL, 2 * D_FF), BF16),
                pltpu.VMEM((D_FF, D_MODEL), BF16),
                pltpu.SemaphoreType.DMA((2,)),
                pltpu.SemaphoreType.DMA((2,)),
            ],
        ),
        out_shape=jax.ShapeDtypeStruct((n_rows, D_MODEL), BF16),
        input_output_aliases={5: 0},
        compiler_params=_params("arbitrary"),
        name="moe_experts",
    )(blk_e, first, slot, next_e, nused, xbuf, bgu, bd, wgu, wd)


def _combine_body(cnt_ref, off_ref, goff_ref, tot_ref, ybuf_ref, lpos_ref, tw_ref, h1_ref, fg_ref, out_ref,
                  stage_ref, sems):
    i = pl.program_id(0)
    n_tiles = pl.num_programs(0) * TILES_PER_STEP

    def copier(s):
        def make_copy(lo, go, size):
            return pltpu.make_async_copy(ybuf_ref.at[pl.ds(go, size)], stage_ref.at[s, pl.ds(lo, size)], sems.at[s])
        return make_copy

    @pl.when(i == 0)
    def _init():
        stage_ref[...] = jnp.zeros_like(stage_ref)
        _copy_runs(0, cnt_ref, off_ref, goff_ref, copier(0))

    for s in range(TILES_PER_STEP):
        t = i * TILES_PER_STEP + s

        @pl.when(t + 1 < n_tiles)
        def _prefetch():
            _copy_runs(t + 1, cnt_ref, off_ref, goff_ref, copier((s + 1) % TILES_PER_STEP))

        rows = slice(s * TOK_TILE, (s + 1) * TOK_TILE)
        w = _slot_matrix(lpos_ref[s], tw_ref[s]).astype(BF16)
        _wait_rows(copier(s), tot_ref[t])
        h2 = h1_ref[rows, :] + _dot_tn(w, stage_ref[s])
        ms = jnp.mean(h2 * h2, axis=-1, keepdims=True)
        out_ref[rows, :] = h2 * lax.rsqrt(ms + RMS_EPS) * fg_ref[...]


def _combine(cnt, off, goff, tot, ybuf, lpos, tw, h1, final_g):
    n = h1.shape[0]
    rows = TILES_PER_STEP * TOK_TILE
    row = lambda i, *_: (i, 0)
    return pl.pallas_call(
        _combine_body,
        grid_spec=pltpu.PrefetchScalarGridSpec(
            num_scalar_prefetch=4,
            grid=(n // rows,),
            in_specs=[
                pl.BlockSpec(memory_space=pl.ANY),
                pl.BlockSpec((TILES_PER_STEP, SUBLANES, TOK_TILE), lambda i, *_: (i, 0, 0)),
                pl.BlockSpec((TILES_PER_STEP, SUBLANES, TOK_TILE), lambda i, *_: (i, 0, 0)),
                pl.BlockSpec((rows, D_MODEL), row),
                pl.BlockSpec((1, D_MODEL), lambda i, *_: (0, 0)),
            ],
            out_specs=pl.BlockSpec((rows, D_MODEL), row),
            scratch_shapes=[pltpu.VMEM((TILES_PER_STEP, STAGE_ROWS, D_MODEL), BF16),
                            pltpu.SemaphoreType.DMA((TILES_PER_STEP,))],
        ),
        out_shape=jax.ShapeDtypeStruct((n, D_MODEL), F32),
        compiler_params=_params("arbitrary"),
        name="moe_combine",
    )(cnt, off, goff, tot, ybuf, lpos, tw, h1, final_g)


def _moe_layout(cnt_tiles):
    nt = cnt_tiles.shape[0]
    total = jnp.sum(cnt_tiles, axis=0)
    nblk_e = (total + EXPERT_BLOCK - 1) // EXPERT_BLOCK
    blk_end = jnp.cumsum(nblk_e)
    pstart = (blk_end - nblk_e) * EXPERT_BLOCK
    prefix = jnp.cumsum(cnt_tiles, axis=0) - cnt_tiles
    goff = pstart[None, :] + prefix
    off = jnp.cumsum(cnt_tiles, axis=1) - cnt_tiles
    tot = jnp.sum(cnt_tiles, axis=1)
    max_rows = nt * (TOK_TILE * TOP_K + N_EXPERTS * (RUN_ALIGN - 1)) + N_EXPERTS * (EXPERT_BLOCK - 1)
    max_blocks = (max_rows + EXPERT_BLOCK - 1) // EXPERT_BLOCK
    blk_id = jnp.arange(max_blocks, dtype=blk_end.dtype)
    blk_e = jnp.minimum(jnp.sum(blk_end[None, :] <= blk_id[:, None], axis=1), N_EXPERTS - 1).astype(jnp.int32)
    nused = blk_end[-1:].astype(jnp.int32)
    flat = lambda a: a.reshape(-1).astype(jnp.int32)
    zoff = pstart + total
    zlen = nblk_e * EXPERT_BLOCK - total
    blkrange = jnp.stack([nused[0], jnp.int32(max_blocks)])
    used = blk_id < nused[0]
    first = used & jnp.concatenate([jnp.ones((1,), bool), blk_e[1:] != blk_e[:-1]])
    slot = (jnp.cumsum(first.astype(jnp.int32)) - 1) % 2
    eid = jnp.arange(N_EXPERTS)
    later = (eid[None, :] > eid[:, None]) & (nblk_e > 0)[None, :]
    next_of = jnp.min(jnp.where(later, eid[None, :], N_EXPERTS), axis=1)
    next_of = jnp.where(next_of < N_EXPERTS, next_of, -1)
    experts_plan = (blk_e, first.astype(jnp.int32), slot.astype(jnp.int32), next_of[blk_e].astype(jnp.int32), nused)
    dispatch_plan = (flat(cnt_tiles), flat(off), flat(goff), flat(tot))
    zero_plan = (flat(zoff), flat(zlen), blkrange.astype(jnp.int32))
    return dispatch_plan, zero_plan, experts_plan, max_blocks * EXPERT_BLOCK


def kernel(x, meta_tokens, attn_norm_g, w_in, na_rpb, na_norm_g, hgrn_lb_logits, hgrn_norm_g, w_out, ffn_norm_g,
           router_w, router_b, expert_w_gu, expert_b_gu, expert_w_down, expert_b_down, final_norm_g):
    B, T, D = x.shape
    x2d = x.reshape(B * T, D)
    w_in_b = w_in[0].astype(BF16)
    ng = attn_norm_g[0].reshape(1, D)
    lbl = hgrn_lb_logits[:, :, :].reshape(4, HG_WIDTH)

    att, hg = _inproj(x2d, ng, w_in_b, lbl, 512)
    att_m, hg_m = _inproj(meta_tokens.astype(F32), ng, w_in_b, lbl, N_META)
    att_m = jnp.pad(att_m, ((0, META_PAD - N_META), (0, 0)))
    hg_m = jnp.pad(hg_m, ((0, HG_CHUNK - N_META), (0, 0)))

    yna = _natten(att, att_m, _natten_bias_tables(na_rpb[0]), na_norm_g[0].reshape(1, NA_WIDTH), B, T)
    o_f, o_b = _hgrn(hg, hg_m, B, T)

    h1, xn, lpos, tw, cnt_tiles = _mix_route(
        yna.reshape(B * T, NA_WIDTH), o_f.reshape(B * T, HG_WIDTH), o_b.reshape(B * T, HG_WIDTH), hg, x2d,
        w_out[0].astype(BF16), hgrn_norm_g[0].reshape(1, HG_WIDTH), ffn_norm_g[0].reshape(1, D),
        router_w[0], router_b[0].reshape(1, N_EXPERTS))

    dispatch_plan, zero_plan, experts_plan, n_rows = _moe_layout(cnt_tiles[:, :, 0])
    xbuf = _dispatch(*dispatch_plan, *zero_plan, xn, lpos, n_rows)
    ybuf = _experts(*experts_plan, xbuf, expert_w_gu[0], expert_b_gu[0][:, None, :],
                    expert_w_down[0], expert_b_down[0][:, None, :])
    out = _combine(*dispatch_plan, ybuf, lpos, tw, h1, final_norm_g.reshape(1, D))
    return out.reshape(B, T, D)
```

```python
import functools

import numpy as np
import jax
import jax.numpy as jnp
from jax import lax
from jax.experimental import pallas as pl
from jax.experimental.pallas import tpu as pltpu

F32 = jnp.float32
BF16 = jnp.bfloat16

D_MODEL = 1024
N_META = 16
GRID_W = 64
NA_WIDTH = 512
NA_HEAD_DIM = 64
NA_HEADS = 8
NA_KH = 8
NA_KW = 16
HG_WIDTH = 512
HG_HEAD_DIM = 128
HG_HEADS = 4
HG_CHUNK = 64
IN_COLS = 3 * NA_WIDTH + 5 * HG_WIDTH
N_EXPERTS = 32
TOP_K = 4
D_FF = 1024
SWIGLU_LIMIT = 7.0
SWIGLU_ALPHA = 1.702
RMS_EPS = 1e-6
NEG_BIG = -1e30
LOG2_E = 1.4426950408889634

LANES = 128
VMEM_LIMIT_BYTES = 56 * 1024 * 1024

TOK_TILE = 256
EXPERT_BLOCK = 512
SUBLANES = 8
RUN_ALIGN = SUBLANES
STAGE_ROWS = TOK_TILE * TOP_K + N_EXPERTS * RUN_ALIGN
N_HG_LEVELS = 6


def _dot(a, b):
    return jnp.dot(a, b, preferred_element_type=F32)


def _dot_nt(a, b):
    return lax.dot_general(a, b, (((1,), (1,)), ((), ())), preferred_element_type=F32)


def _dot_tn(a, b):
    return lax.dot_general(a, b, (((0,), (0,)), ((), ())), preferred_element_type=F32)


def _split_bf16(x):
    hi = x.astype(BF16)
    lo = (x - hi.astype(F32)).astype(BF16)
    return hi, lo


def _pack_rows(rows, n_rows):
    idx = lax.broadcasted_iota(jnp.int32, (n_rows, rows[0].shape[1]), 0)
    out = jnp.zeros(idx.shape, rows[0].dtype)
    for k, r in enumerate(rows):
        out = jnp.where(idx == k, r, out)
    return out


def _params(*sem):
    return pltpu.CompilerParams(dimension_semantics=sem, vmem_limit_bytes=VMEM_LIMIT_BYTES)


def _inproj_body(x_ref, ng_ref, w_ref, lbl_ref, att_ref, hg_ref):
    x = x_ref[...]
    ms = jnp.mean(x * x, axis=-1, keepdims=True)
    n = (x * lax.rsqrt(ms + RMS_EPS) * ng_ref[...]).astype(BF16)

    def proj(lo, hi):
        return _dot(n, w_ref[:, lo:hi])

    pq = proj(0, NA_WIDTH)
    att_ref[:, 0:NA_WIDTH] = (pq * (NA_HEAD_DIM ** -0.5 * LOG2_E)).astype(BF16)
    att_ref[:, NA_WIDTH:3 * NA_WIDTH] = proj(NA_WIDTH, 3 * NA_WIDTH).astype(BF16)

    base = 3 * NA_WIDTH
    W = HG_WIDTH
    qh = proj(base, base + W)
    hg_ref[:, 0:W] = qh * jax.nn.sigmoid(qh)
    hg_ref[:, W:2 * W] = proj(base + W, base + 2 * W)
    lbl = lbl_ref[...]
    for d in range(2):
        a0 = lbl[2 * d:2 * d + 1, :]
        a1 = lbl[2 * d + 1:2 * d + 2, :]
        m = jnp.maximum(a0, a1)
        e0 = jnp.exp(a0 - m)
        e1 = jnp.exp(a1 - m)
        lb = e0 / (e0 + e1)
        raw = proj(base + (2 + d) * W, base + (3 + d) * W)
        f = lb + (1.0 - lb) * jax.nn.sigmoid(raw)
        hg_ref[:, (2 + d) * W:(3 + d) * W] = jnp.log(f)
    gt = proj(base + 4 * W, base + 5 * W)
    hg_ref[:, 4 * W:5 * W] = gt * jax.nn.sigmoid(gt)


def _inproj(x2d, norm_g, w_bf16, lb_logits4, tm):
    n = x2d.shape[0]
    return pl.pallas_call(
        _inproj_body,
        grid=(n // tm,),
        in_specs=[
            pl.BlockSpec((tm, D_MODEL), lambda i: (i, 0)),
            pl.BlockSpec((1, D_MODEL), lambda i: (0, 0)),
            pl.BlockSpec((D_MODEL, IN_COLS), lambda i: (0, 0)),
            pl.BlockSpec((4, HG_WIDTH), lambda i: (0, 0)),
        ],
        out_specs=[
            pl.BlockSpec((tm, 3 * NA_WIDTH), lambda i: (i, 0)),
            pl.BlockSpec((tm, 5 * HG_WIDTH), lambda i: (i, 0)),
        ],
        out_shape=[
            jax.ShapeDtypeStruct((n, 3 * NA_WIDTH), BF16),
            jax.ShapeDtypeStruct((n, 5 * HG_WIDTH), F32),
        ],
        compiler_params=_params("arbitrary"),
        name="inproj",
    )(x2d, norm_g, w_bf16, lb_logits4)


N_WIN_KEYS = NA_KH * GRID_W
META_PAD = LANES


NA_GROUP = 4
GROUP_LANES = NA_GROUP * NA_HEAD_DIM


NA_ROWS_PER_STEP = 8


def _natten_row(rr, r, q_ref, k_ref, v_ref, km_ref, vm_ref, bias_ref, *, rows):
    rs = jnp.clip(r - NA_KH // 2, 0, rows - NA_KH)
    d0 = rs - r + (NA_KH - 1)
    start = pl.multiple_of(rs * GRID_W, GRID_W)
    lane_head = lax.broadcasted_iota(jnp.int32, (GRID_W, GROUP_LANES), 1) // NA_HEAD_DIM
    meta_col = lax.broadcasted_iota(jnp.int32, (1, META_PAD), 1)
    meta_bias = jnp.where(meta_col < N_META, 0.0, NEG_BIG).astype(F32)
    folded = []
    for g in range(NA_HEADS // NA_GROUP):
        sl = slice(g * GROUP_LANES, (g + 1) * GROUP_LANES)
        q4 = q_ref[0, rr * GRID_W:(rr + 1) * GRID_W, sl]
        zero = jnp.zeros_like(q4)
        qm = jnp.concatenate([jnp.where(lane_head == h, q4, zero) for h in range(NA_GROUP)], axis=0)
        kw = k_ref[0, pl.ds(start, N_WIN_KEYS), sl]
        vw = v_ref[0, pl.ds(start, N_WIN_KEYS), sl]
        hs = slice(g * NA_GROUP, (g + 1) * NA_GROUP)
        bias = jnp.concatenate(
            [bias_ref[d0 + 2 * j, hs].reshape(NA_GROUP * GRID_W, 2 * GRID_W) for j in range(NA_KH // 2)], axis=-1)
        s = _dot_nt(qm, kw) + bias
        sm = _dot_nt(qm, km_ref[:, sl]) + meta_bias
        m = jnp.maximum(jnp.max(s, axis=-1, keepdims=True), jnp.max(sm, axis=-1, keepdims=True))
        e = jnp.exp2(s - m)
        em = jnp.exp2(sm - m)
        den = jnp.sum(e, axis=-1, keepdims=True) + jnp.sum(em, axis=-1, keepdims=True)
        o = (_dot(e.astype(BF16), vw) + _dot(em.astype(BF16), vm_ref[:, sl])) * (1.0 / den)
        acc = jnp.where(lane_head == 0, o[0:GRID_W], 0.0)
        for h in range(1, NA_GROUP):
            acc = jnp.where(lane_head == h, o[h * GRID_W:(h + 1) * GRID_W], acc)
        folded.append(acc)
    return jnp.concatenate(folded, axis=-1)


def _natten_body(q_ref, k_ref, v_ref, km_ref, vm_ref, bias_ref, ng_ref, bd_ref, o_ref, *, rows):
    r0 = pl.program_id(1) * NA_ROWS_PER_STEP
    o2 = jnp.concatenate(
        [_natten_row(rr, r0 + rr, q_ref, k_ref, v_ref, km_ref, vm_ref, bias_ref, rows=rows)
         for rr in range(NA_ROWS_PER_STEP)], axis=0)
    sq_hi, sq_lo = _split_bf16(o2 * o2)
    ms = (_dot(sq_hi, bd_ref[...]) + _dot(sq_lo, bd_ref[...])) * (1.0 / NA_HEAD_DIM)
    o_ref[0] = (o2 * lax.rsqrt(ms + RMS_EPS) * ng_ref[...]).astype(o_ref.dtype)


def _natten(att, att_meta, bias_tabs, norm_g, batch, seq):
    rows = seq // GRID_W
    att3 = att.reshape(batch, seq, 3 * NA_WIDTH)
    head_of = np.arange(NA_WIDTH) // NA_HEAD_DIM
    same_head = jnp.asarray(head_of[:, None] == head_of[None, :], BF16)

    qrows = NA_ROWS_PER_STEP * GRID_W
    return pl.pallas_call(
        functools.partial(_natten_body, rows=rows),
        grid=(batch, rows // NA_ROWS_PER_STEP),
        in_specs=[
            pl.BlockSpec((1, qrows, NA_WIDTH), lambda b, r: (b, r, 0)),
            pl.BlockSpec((1, seq, NA_WIDTH), lambda b, r: (b, 0, 1)),
            pl.BlockSpec((1, seq, NA_WIDTH), lambda b, r: (b, 0, 2)),
            pl.BlockSpec((META_PAD, NA_WIDTH), lambda b, r: (0, 1)),
            pl.BlockSpec((META_PAD, NA_WIDTH), lambda b, r: (0, 2)),
            pl.BlockSpec(bias_tabs.shape, lambda b, r: (0, 0, 0, 0), pipeline_mode=pl.Buffered(1)),
            pl.BlockSpec((1, NA_WIDTH), lambda b, r: (0, 0)),
            pl.BlockSpec((NA_WIDTH, NA_WIDTH), lambda b, r: (0, 0)),
        ],
        out_specs=pl.BlockSpec((1, qrows, NA_WIDTH), lambda b, r: (b, r, 0)),
        out_shape=jax.ShapeDtypeStruct((batch, seq, NA_WIDTH), BF16),
        compiler_params=_params("arbitrary", "arbitrary"),
        name="natten",
    )(att3, att3, att3, att_meta, att_meta, bias_tabs, norm_g, same_head)


def _natten_bias_tables(rpb):
    c = np.arange(GRID_W)[:, None]
    kc = np.arange(GRID_W)[None, :]
    cs = np.clip(c - NA_KW // 2, 0, GRID_W - NA_KW)
    valid = (kc >= cs) & (kc < cs + NA_KW)
    dc = kc - c + (NA_KW - 1)
    pick = np.asarray(dc[None] == np.arange(2 * NA_KW - 1)[:, None, None], np.float32)
    t1 = jnp.einsum("hdj,jck->hdck", rpb.astype(F32) * LOG2_E, pick, precision=lax.Precision.HIGHEST)
    t1 = jnp.where(valid[None, None], t1, NEG_BIG)
    pairs = jnp.concatenate([t1[:, :-1], t1[:, 1:]], axis=-1)
    return pairs.transpose(1, 0, 2, 3)


HG_CHUNKS_PER_STEP = 8
HG_CHUNKS_PER_GROUP = 1
ROW_B, ROW_LV = 0, 1
N_EXP_BLOCKS = 1 + N_HG_LEVELS


def _hgrn_constants():
    C = HG_CHUNK
    t = np.arange(C)
    u = t[None, :]
    mats = [u <= t[:, None]]
    masks = [np.eye(C, dtype=bool)]
    for lv in range(N_HG_LEVELS):
        m = 1 << lv
        blk = t // (2 * m)
        upper = (t // m) % 2 == 1
        p = blk * 2 * m + m - 1
        q_rows = upper[:, None] & (u > p[:, None]) & (u <= t[:, None])
        k_rows = (~upper)[:, None] & (u > t[:, None]) & (u <= p[:, None])
        mats.append(q_rows | k_rows)
        masks.append((blk[:, None] == blk[None, :]) & upper[:, None] & (~upper)[None, :])
    fwd = np.concatenate(mats, axis=0).astype(np.float32)
    fwd_mask = np.stack(masks).astype(np.float32)
    blocks = fwd.reshape(N_EXP_BLOCKS, C, C)
    bwd = blocks[:, ::-1, ::-1].reshape(N_EXP_BLOCKS * C, C)
    bwd_mask = fwd_mask[:, ::-1, ::-1]
    both = np.stack([fwd, bwd])
    both = np.concatenate([both, both], axis=-1)
    both_mask = np.stack([fwd_mask, bwd_mask])
    both_mask = np.concatenate([both_mask, both_mask], axis=-1)
    return (jnp.asarray(both, BF16), jnp.asarray(both_mask, F32))


def _hgrn_exponents(mat, g):
    g_hi, g_lo = _split_bf16(g)
    return _dot(mat, jnp.concatenate([g_hi, g_lo], axis=0))


def _hgrn_init(meta_ref, mat_ref, stf_ref, stb_ref):
    C, W = HG_CHUNK, HG_WIDTH
    mb = meta_ref[...]
    v = mb[:, W:2 * W].astype(BF16)
    prefix = mat_ref[0, ROW_B * C:(ROW_B + 1) * C, :]
    for d, st_ref in ((0, stf_ref), (1, stb_ref)):
        g = mb[:, (2 + d) * W:(3 + d) * W]
        b = _hgrn_exponents(prefix, g)
        kb = ((1.0 - jnp.exp(g)) * jnp.exp(b[C - 1:C] - b)).astype(BF16)
        for h in range(HG_HEADS):
            sl = slice(h * HG_HEAD_DIM, (h + 1) * HG_HEAD_DIM)
            st_ref[h] = _dot_tn(v[:, sl], kb[:, sl])


def _hgrn_group_tasks(j0):
    js = range(j0, j0 + HG_CHUNKS_PER_GROUP)
    return ([(0, j * HG_CHUNK) for j in js] +
            [(1, (HG_CHUNKS_PER_STEP - 1 - j) * HG_CHUNK) for j in js])


def _hgrn_tasks(tasks, dir_refs, states, mat_ref, mask_ref):
    C, W = HG_CHUNK, HG_WIDTH
    heads = [slice(h * HG_HEAD_DIM, (h + 1) * HG_HEAD_DIM) for h in range(HG_HEADS)]
    qs, ks, vs, ex_bs, ex_lvs, ex_ends = {}, {}, {}, {}, {}, {}
    for d, r0 in tasks:
        q_ref, v_ref, g_ref, total_row = dir_refs[d][0], dir_refs[d][1], dir_refs[d][2], dir_refs[d][3]
        g = g_ref[0, r0:r0 + C, :]
        sums = _hgrn_exponents(mat_ref[d], g)
        ex_bs[d, r0] = jnp.exp(sums[0:C])
        ex_lvs[d, r0] = jnp.exp(sums[C:]).astype(BF16)
        ex_ends[d, r0] = jnp.exp(sums[total_row:total_row + 1] - sums[0:C])
        qs[d, r0] = q_ref[0, r0:r0 + C, :]
        ks[d, r0] = 1.0 - jnp.exp(g)
        vs[d, r0] = v_ref[0, r0:r0 + C, :].astype(BF16)

    D2 = 2 * HG_HEAD_DIM
    pairs = [slice(p * D2, (p + 1) * D2) for p in range(HG_HEADS // 2)]

    def block_diag(x):
        zero = jnp.zeros((C, HG_HEAD_DIM), x.dtype)
        return jnp.concatenate([jnp.concatenate([x[:, :HG_HEAD_DIM], zero], axis=1),
                                jnp.concatenate([zero, x[:, HG_HEAD_DIM:]], axis=1)], axis=0)

    prods = {}
    for task in tasks:
        for p, sl in enumerate(pairs):
            q2, k2 = qs[task][:, sl].astype(BF16), ks[task][:, sl].astype(BF16)
            ps = [_dot_nt(q2, block_diag(k2))]
            for lv in range(N_HG_LEVELS):
                scale = ex_lvs[task][lv * C:(lv + 1) * C, sl]
                ps.append(_dot_nt(q2 * scale, block_diag(k2 * scale)))
            prods[task, p] = ps
    attn, qbs, kbs = {}, {}, {}
    for task in tasks:
        d = task[0]
        for p, sl in enumerate(pairs):
            a = prods[task, p][0] * mask_ref[d, 0]
            for lv in range(N_HG_LEVELS):
                a = a + prods[task, p][1 + lv] * mask_ref[d, 1 + lv]
            o2 = _dot(a.astype(BF16), block_diag(vs[task][:, sl]))
            attn[task, 2 * p] = o2[:, :HG_HEAD_DIM]
            attn[task, 2 * p + 1] = o2[:, HG_HEAD_DIM:]
        for h, sl in enumerate(heads):
            qbs[task, h] = (qs[task][:, sl] * ex_bs[task][:, sl]).astype(BF16)
            kbs[task, h] = _dot_tn(vs[task][:, sl], (ks[task][:, sl] * ex_ends[task][:, sl]).astype(BF16))
    for task in tasks:
        d, r0 = task
        total_row, o_ref = dir_refs[d][3], dir_refs[d][4]
        outs = []
        for h, sl in enumerate(heads):
            st = states[d, h]
            outs.append(attn[task, h] + _dot_nt(qbs[task, h], st.astype(BF16)))
            decay = ex_bs[task][total_row:total_row + 1, sl]
            states[d, h] = decay * st + kbs[task, h]
        o_ref[0, r0:r0 + C, :] = jnp.concatenate(outs, axis=-1)


def _hgrn_body(hqf_ref, hvf_ref, hgf_ref, hqb_ref, hvb_ref, hgb_ref, meta_ref, mat_ref, mask_ref,
               of_ref, ob_ref, stf_ref, stb_ref):
    @pl.when(pl.program_id(1) == 0)
    def _init():
        _hgrn_init(meta_ref, mat_ref, stf_ref, stb_ref)

    dir_refs = ((hqf_ref, hvf_ref, hgf_ref, HG_CHUNK - 1, of_ref), (hqb_ref, hvb_ref, hgb_ref, 0, ob_ref))
    st_refs = (stf_ref, stb_ref)
    states = {(d, h): st_refs[d][h] for d in range(2) for h in range(HG_HEADS)}
    for j0 in range(0, HG_CHUNKS_PER_STEP, HG_CHUNKS_PER_GROUP):
        _hgrn_tasks(_hgrn_group_tasks(j0), dir_refs, states, mat_ref, mask_ref)
    for d in range(2):
        for h in range(HG_HEADS):
            st_refs[d][h] = states[d, h]


def _hgrn(hg, hg_meta_pad, batch, seq):
    steps = seq // (HG_CHUNK * HG_CHUNKS_PER_STEP)
    hg3 = hg.reshape(batch, seq, 5 * HG_WIDTH)
    mats, masks = _hgrn_constants()
    hblk = (1, HG_CHUNK * HG_CHUNKS_PER_STEP, HG_WIDTH)

    def fwd(col):
        return pl.BlockSpec(hblk, lambda b, s: (b, s, col))

    def bwd(col):
        return pl.BlockSpec(hblk, lambda b, s: (b, steps - 1 - s, col))

    return pl.pallas_call(
        _hgrn_body,
        grid=(batch, steps),
        in_specs=[
            fwd(0), fwd(1), fwd(2),
            bwd(0), bwd(1), bwd(3),
            pl.BlockSpec((HG_CHUNK, 5 * HG_WIDTH), lambda b, s: (0, 0)),
            pl.BlockSpec(mats.shape, lambda b, s: (0, 0, 0)),
            pl.BlockSpec(masks.shape, lambda b, s: (0, 0, 0, 0)),
        ],
        out_specs=[fwd(0), bwd(0)],
        out_shape=[jax.ShapeDtypeStruct((batch, seq, HG_WIDTH), F32)] * 2,
        scratch_shapes=[pltpu.VMEM((HG_HEADS, HG_HEAD_DIM, HG_HEAD_DIM), F32)] * 2,
        compiler_params=_params("arbitrary", "arbitrary"),
        name="hgrn",
    )(hg3, hg3, hg3, hg3, hg3, hg3, hg_meta_pad, mats, masks)


MIX_TILES_PER_STEP = 4


def _mix_route_body(yna_ref, of_ref, ob_ref, gate_ref, x_ref, wout_ref, hgn_ref, ffg_ref,
                    rwh_ref, rwl_ref, rb_ref, tri_ref, lowe_ref,
                    h1_ref, xn_ref, lpos_ref, tw_ref, cnt_ref):
    T, E = TOK_TILE, N_EXPERTS
    tiles = [slice(s * T, (s + 1) * T) for s in range(MIX_TILES_PER_STEP)]
    x_his, x_los = [], []
    for rows in tiles:
        o = of_ref[rows, :] + ob_ref[rows, :]
        parts = []
        for h in range(HG_HEADS):
            seg = o[:, h * HG_HEAD_DIM:(h + 1) * HG_HEAD_DIM]
            ms = jnp.mean(seg * seg, axis=-1, keepdims=True)
            parts.append(seg * lax.rsqrt(ms + RMS_EPS))
        yhg = jnp.concatenate(parts, axis=-1) * hgn_ref[...] * gate_ref[rows, :]
        mix = _dot(yna_ref[rows, :], wout_ref[0:NA_WIDTH, :]) + _dot(yhg.astype(BF16), wout_ref[NA_WIDTH:, :])
        h1 = x_ref[rows, :] + mix
        h1_ref[rows, :] = h1
        ms = jnp.mean(h1 * h1, axis=-1, keepdims=True)
        xn = h1 * lax.rsqrt(ms + RMS_EPS) * ffg_ref[...]
        x_hi, x_lo = _split_bf16(xn)
        xn_ref[rows, :] = x_hi
        x_his.append(x_hi)
        x_los.append(x_lo)

    curs = [(_dot_nt(rwh_ref[...], x_hi) + _dot_nt(rwh_ref[...], x_lo) + _dot_nt(rwl_ref[...], x_hi)) + rb_ref[...]
            for x_hi, x_lo in zip(x_his, x_los)]
    row = lax.broadcasted_iota(jnp.int32, (E, T), 0).astype(F32)
    sels = [[] for _ in tiles]
    vals = [[] for _ in tiles]
    for _ in range(TOP_K):
        for s in range(len(tiles)):
            m = jnp.max(curs[s], axis=0, keepdims=True)
            first = jnp.min(jnp.where(curs[s] == m, row, float(E)), axis=0, keepdims=True)
            sel = row == first
            sels[s].append(sel)
            vals[s].append(m)
            curs[s] = jnp.where(sel, -jnp.inf, curs[s])
    for s in range(len(tiles)):
        es = [jnp.exp(vk - vals[s][0]) for vk in vals[s]]
        den = es[0] + es[1] + es[2] + es[3]
        tw_ref[s] = _pack_rows([e / den for e in es], SUBLANES)

        onehot = jnp.zeros((E, T), F32)
        for sel in sels[s]:
            onehot = onehot + jnp.where(sel, 1.0, 0.0)
        lrank = _dot(onehot.astype(BF16), tri_ref[...])
        cnt = jnp.sum(onehot, axis=1, keepdims=True)
        cnt = jnp.floor((cnt + (RUN_ALIGN - 1)) * (1.0 / RUN_ALIGN)) * RUN_ALIGN
        off = _dot(lowe_ref[...], jnp.broadcast_to(cnt, (E, LANES)).astype(BF16))
        base = jnp.concatenate([off] * (T // LANES), axis=1) + lrank
        lpos_ref[s] = _pack_rows(
            [jnp.sum(jnp.where(sel, base, 0.0), axis=0, keepdims=True) for sel in sels[s]],
            SUBLANES).astype(jnp.int32)
        cnt_ref[s] = cnt.astype(jnp.int32)


def _mix_route(yna, o_f, o_b, hg, x2d, wout_bf16, hgn, ffg, router_w, router_b):
    n = x2d.shape[0]
    T = TOK_TILE
    nt = n // T
    rw_t = router_w.T
    rwh = rw_t.astype(BF16)
    rwl = (rw_t - rwh.astype(F32)).astype(BF16)
    router_b = router_b.reshape(N_EXPERTS, 1)
    tri = jnp.asarray(np.triu(np.ones((T, T), np.float32), 1), BF16)
    lowe = jnp.asarray(np.tril(np.ones((N_EXPERTS, N_EXPERTS), np.float32), -1), BF16)
    row = lambda i: (i, 0)
    const = lambda i: (0, 0)
    tile3 = lambda i: (i, 0, 0)
    K = MIX_TILES_PER_STEP
    R = K * T
    return pl.pallas_call(
        _mix_route_body,
        grid=(nt // K,),
        in_specs=[
            pl.BlockSpec((R, NA_WIDTH), row),
            pl.BlockSpec((R, HG_WIDTH), row),
            pl.BlockSpec((R, HG_WIDTH), row),
            pl.BlockSpec((R, HG_WIDTH), lambda i: (i, 4)),
            pl.BlockSpec((R, D_MODEL), row),
            pl.BlockSpec((D_MODEL, D_MODEL), const),
            pl.BlockSpec((1, HG_WIDTH), const),
            pl.BlockSpec((1, D_MODEL), const),
            pl.BlockSpec((N_EXPERTS, D_MODEL), const),
            pl.BlockSpec((N_EXPERTS, D_MODEL), const),
            pl.BlockSpec((N_EXPERTS, 1), const),
            pl.BlockSpec((T, T), const),
            pl.BlockSpec((N_EXPERTS, N_EXPERTS), const),
        ],
        out_specs=[
            pl.BlockSpec((R, D_MODEL), row),
            pl.BlockSpec((R, D_MODEL), row),
            pl.BlockSpec((K, SUBLANES, T), tile3),
            pl.BlockSpec((K, SUBLANES, T), tile3),
            pl.BlockSpec((K, N_EXPERTS, 1), tile3),
        ],
        out_shape=[
            jax.ShapeDtypeStruct((n, D_MODEL), F32),
            jax.ShapeDtypeStruct((n, D_MODEL), BF16),
            jax.ShapeDtypeStruct((nt, SUBLANES, T), jnp.int32),
            jax.ShapeDtypeStruct((nt, SUBLANES, T), F32),
            jax.ShapeDtypeStruct((nt, N_EXPERTS, 1), jnp.int32),
        ],
        compiler_params=_params("arbitrary"),
        name="mix_route",
    )(yna, o_f, o_b, hg, x2d, wout_bf16, hgn, ffg, rwh, rwl, router_b, tri, lowe)


TILES_PER_STEP = 2


def _wait_rows(make_copy, rows):
    make_copy(0, 0, pl.multiple_of(rows, RUN_ALIGN)).wait()


def _copy_runs(i, cnt_ref, off_ref, goff_ref, make_copy):
    def per_expert(e, carry):
        n = pl.multiple_of(cnt_ref[i * N_EXPERTS + e], RUN_ALIGN)
        lo = pl.multiple_of(off_ref[i * N_EXPERTS + e], RUN_ALIGN)
        go = pl.multiple_of(goff_ref[i * N_EXPERTS + e], RUN_ALIGN)

        @pl.when(n > 0)
        def _start():
            make_copy(lo, go, n).start()

        return carry

    lax.fori_loop(0, N_EXPERTS, per_expert, 0)


def _slot_matrix(lpos, weights=None):
    row = lax.broadcasted_iota(jnp.int32, (STAGE_ROWS, TOK_TILE), 0)
    acc = jnp.zeros((STAGE_ROWS, TOK_TILE), F32)
    for k in range(TOP_K):
        hit = lpos[k:k + 1, :] == row
        val = 1.0 if weights is None else weights[k:k + 1, :]
        acc = jnp.where(hit, val, acc)
    return acc


def _dispatch_body(cnt_ref, off_ref, goff_ref, tot_ref, zoff_ref, zlen_ref, blkrange_ref,
                   xn_ref, lpos_ref, xbuf_ref, stage_ref, zero_ref, sems, zsem):
    i = pl.program_id(0)
    last = pl.num_programs(0) - 1

    def zero_fill(act):
        def tail(e, carry):
            n = pl.multiple_of(zlen_ref[e], RUN_ALIGN)

            @pl.when(n > 0)
            def _():
                go = pl.multiple_of(zoff_ref[e], RUN_ALIGN)
                act(pltpu.make_async_copy(zero_ref.at[pl.ds(0, n)], xbuf_ref.at[pl.ds(go, n)], zsem))

            return carry

        def unused(j, carry):
            go = pl.multiple_of(j * EXPERT_BLOCK, EXPERT_BLOCK)
            act(pltpu.make_async_copy(zero_ref, xbuf_ref.at[pl.ds(go, EXPERT_BLOCK)], zsem))
            return carry

        lax.fori_loop(0, N_EXPERTS, tail, 0)
        lax.fori_loop(blkrange_ref[0], blkrange_ref[1], unused, 0)

    @pl.when(i == 0)
    def _zero_start():
        zero_ref[...] = jnp.zeros_like(zero_ref)
        zero_fill(lambda c: c.start())

    for s in range(TILES_PER_STEP):
        t = i * TILES_PER_STEP + s

        def make_copy(lo, go, size, s=s):
            return pltpu.make_async_copy(stage_ref.at[s, pl.ds(lo, size)], xbuf_ref.at[pl.ds(go, size)], sems.at[s])

        @pl.when(i > 0)
        def _drain():
            _wait_rows(make_copy, tot_ref[t - TILES_PER_STEP])

        rows = slice(s * TOK_TILE, (s + 1) * TOK_TILE)
        perm = _slot_matrix(lpos_ref[s]).astype(BF16)
        stage_ref[s] = _dot(perm, xn_ref[rows, :])
        _copy_runs(t, cnt_ref, off_ref, goff_ref, make_copy)

        @pl.when(i == last)
        def _finish():
            _wait_rows(make_copy, tot_ref[t])

    @pl.when(i == last)
    def _zero_finish():
        zero_fill(lambda c: c.wait())


def _dispatch(cnt, off, goff, tot, zoff, zlen, blkrange, xn, lpos, n_rows):
    n = xn.shape[0]
    rows = TILES_PER_STEP * TOK_TILE
    return pl.pallas_call(
        _dispatch_body,
        grid_spec=pltpu.PrefetchScalarGridSpec(
            num_scalar_prefetch=7,
            grid=(n // rows,),
            in_specs=[
                pl.BlockSpec((rows, D_MODEL), lambda i, *_: (i, 0)),
                pl.BlockSpec((TILES_PER_STEP, SUBLANES, TOK_TILE), lambda i, *_: (i, 0, 0)),
            ],
            out_specs=pl.BlockSpec(memory_space=pl.ANY),
            scratch_shapes=[
                pltpu.VMEM((TILES_PER_STEP, STAGE_ROWS, D_MODEL), F32),
                pltpu.VMEM((EXPERT_BLOCK, D_MODEL), F32),
                pltpu.SemaphoreType.DMA((TILES_PER_STEP,)),
                pltpu.SemaphoreType.DMA(()),
            ],
        ),
        out_shape=jax.ShapeDtypeStruct((n_rows, D_MODEL), F32),
        compiler_params=_params("arbitrary"),
        name="moe_dispatch",
    )(cnt, off, goff, tot, zoff, zlen, blkrange, xn, lpos)


CAST_ROWS = 128
EXPERT_ROW_STEP = 128


def _experts_body(blk_e_ref, first_ref, slot_ref, next_e_ref, parts_ref, nused_ref,
                  x_ref, bgu_ref, bd_ref, wgu_hbm, wd_hbm, y_ref,
                  wgu_f32, wd_f32, wgu_bf, wd_bf, sem_gu, sem_d):
    j = pl.program_id(0)

    def weight_copies(e, slot):
        return (pltpu.make_async_copy(wgu_hbm.at[e], wgu_f32.at[slot], sem_gu.at[slot]),
                pltpu.make_async_copy(wd_hbm.at[e], wd_f32.at[slot], sem_d.at[slot]))

    @pl.when(j < nused_ref[0])
    def _():
        e = blk_e_ref[j]
        slot = slot_ref[j]

        @pl.when(first_ref[j] == 1)
        def _new_expert():
            @pl.when(j == 0)
            def _():
                for c in weight_copies(e, slot):
                    c.start()

            for c in weight_copies(e, slot):
                c.wait()
            nxt = next_e_ref[j]

            @pl.when(nxt >= 0)
            def _():
                for c in weight_copies(nxt, 1 - slot):
                    c.start(priority=1)

            def cast(src, dst):
                def rows(r, carry):
                    sl = pl.ds(pl.multiple_of(r * CAST_ROWS, CAST_ROWS), CAST_ROWS)
                    dst[sl, :] = src[slot, sl, :].astype(BF16)
                    return carry
                lax.fori_loop(0, src.shape[1] // CAST_ROWS, rows, 0)

            cast(wgu_f32, wgu_bf)
            cast(wd_f32, wd_bf)

        def ffn(rows):
            x = x_ref[0:rows, :].astype(BF16)
            gu = _dot(x, wgu_bf[...]) + bgu_ref[0]
            gate = jnp.minimum(gu[:, :D_FF], SWIGLU_LIMIT)
            up = jnp.clip(gu[:, D_FF:], -SWIGLU_LIMIT, SWIGLU_LIMIT)
            act = (up + 1.0) * gate * jax.nn.sigmoid(SWIGLU_ALPHA * gate)
            y_ref[0:rows, :] = _dot(act.astype(BF16), wd_bf[...]) + bd_ref[0]
            if rows < EXPERT_BLOCK:
                y_ref[rows:, :] = jnp.zeros((EXPERT_BLOCK - rows, D_MODEL), F32)

        for parts in range(1, EXPERT_BLOCK // EXPERT_ROW_STEP + 1):
            pl.when(parts_ref[j] == parts)(functools.partial(ffn, parts * EXPERT_ROW_STEP))


def _experts(blk_e, first, slot, next_e, parts, nused, xbuf, wgu, bgu, wd, bd):
    n_rows = xbuf.shape[0]
    nblk = n_rows // EXPERT_BLOCK

    def rowblk(j, be, fi, sl, ne, pa, nu):
        return (jnp.minimum(j, nu[0] - 1), 0)

    def expert(j, be, fi, sl, ne, pa, nu):
        return (be[jnp.minimum(j, nu[0] - 1)], 0, 0)

    return pl.pallas_call(
        _experts_body,
        grid_spec=pltpu.PrefetchScalarGridSpec(
            num_scalar_prefetch=6,
            grid=(nblk,),
            in_specs=[
                pl.BlockSpec((EXPERT_BLOCK, D_MODEL), rowblk),
                pl.BlockSpec((1, 1, 2 * D_FF), expert),
                pl.BlockSpec((1, 1, D_MODEL), expert),
                pl.BlockSpec(memory_space=pl.ANY),
                pl.BlockSpec(memory_space=pl.ANY),
            ],
            out_specs=pl.BlockSpec((EXPERT_BLOCK, D_MODEL), rowblk),
            scratch_shapes=[
                pltpu.VMEM((2, D_MODEL, 2 * D_FF), F32),
                pltpu.VMEM((2, D_FF, D_MODEL), F32),
                pltpu.VMEM((D_MODEL, 2 * D_FF), BF16),
                pltpu.VMEM((D_FF, D_MODEL), BF16),
                pltpu.SemaphoreType.DMA((2,)),
                pltpu.SemaphoreType.DMA((2,)),
            ],
        ),
        out_shape=jax.ShapeDtypeStruct((n_rows, D_MODEL), F32),
        input_output_aliases={6: 0},
        compiler_params=_params("arbitrary"),
        name="moe_experts",
    )(blk_e, first, slot, next_e, parts, nused, xbuf, bgu, bd, wgu, wd)


def _combine_body(cnt_ref, off_ref, goff_ref, tot_ref, ybuf_ref, lpos_ref, tw_ref, h1_ref, fg_ref, out_ref,
                  stage_ref, sems):
    i = pl.program_id(0)
    n_tiles = pl.num_programs(0) * TILES_PER_STEP

    def copier(s):
        def make_copy(lo, go, size):
            return pltpu.make_async_copy(ybuf_ref.at[pl.ds(go, size)], stage_ref.at[s, pl.ds(lo, size)], sems.at[s])
        return make_copy

    @pl.when(i == 0)
    def _init():
        stage_ref[...] = jnp.zeros_like(stage_ref)
        _copy_runs(0, cnt_ref, off_ref, goff_ref, copier(0))

    for s in range(TILES_PER_STEP):
        t = i * TILES_PER_STEP + s

        @pl.when(t + 1 < n_tiles)
        def _prefetch():
            _copy_runs(t + 1, cnt_ref, off_ref, goff_ref, copier((s + 1) % TILES_PER_STEP))

        rows = slice(s * TOK_TILE, (s + 1) * TOK_TILE)
        w = _slot_matrix(lpos_ref[s], tw_ref[s]).astype(BF16)
        _wait_rows(copier(s), tot_ref[t])
        h2 = h1_ref[rows, :] + _dot_tn(w, stage_ref[s].astype(BF16))
        ms = jnp.mean(h2 * h2, axis=-1, keepdims=True)
        out_ref[rows, :] = h2 * lax.rsqrt(ms + RMS_EPS) * fg_ref[...]


def _combine(cnt, off, goff, tot, ybuf, lpos, tw, h1, final_g):
    n = h1.shape[0]
    rows = TILES_PER_STEP * TOK_TILE
    row = lambda i, *_: (i, 0)
    return pl.pallas_call(
        _combine_body,
        grid_spec=pltpu.PrefetchScalarGridSpec(
            num_scalar_prefetch=4,
            grid=(n // rows,),
            in_specs=[
                pl.BlockSpec(memory_space=pl.ANY),
                pl.BlockSpec((TILES_PER_STEP, SUBLANES, TOK_TILE), lambda i, *_: (i, 0, 0)),
                pl.BlockSpec((TILES_PER_STEP, SUBLANES, TOK_TILE), lambda i, *_: (i, 0, 0)),
                pl.BlockSpec((rows, D_MODEL), row),
                pl.BlockSpec((1, D_MODEL), lambda i, *_: (0, 0)),
            ],
            out_specs=pl.BlockSpec((rows, D_MODEL), row),
            scratch_shapes=[pltpu.VMEM((TILES_PER_STEP, STAGE_ROWS, D_MODEL), F32),
                            pltpu.SemaphoreType.DMA((TILES_PER_STEP,))],
        ),
        out_shape=jax.ShapeDtypeStruct((n, D_MODEL), F32),
        compiler_params=_params("arbitrary"),
        name="moe_combine",
    )(cnt, off, goff, tot, ybuf, lpos, tw, h1, final_g)


def _moe_layout(cnt_tiles):
    nt = cnt_tiles.shape[0]
    total = jnp.sum(cnt_tiles, axis=0)
    nblk_e = (total + EXPERT_BLOCK - 1) // EXPERT_BLOCK
    blk_end = jnp.cumsum(nblk_e)
    pstart = (blk_end - nblk_e) * EXPERT_BLOCK
    prefix = jnp.cumsum(cnt_tiles, axis=0) - cnt_tiles
    goff = pstart[None, :] + prefix
    off = jnp.cumsum(cnt_tiles, axis=1) - cnt_tiles
    tot = jnp.sum(cnt_tiles, axis=1)
    max_rows = nt * (TOK_TILE * TOP_K + N_EXPERTS * (RUN_ALIGN - 1)) + N_EXPERTS * (EXPERT_BLOCK - 1)
    max_blocks = (max_rows + EXPERT_BLOCK - 1) // EXPERT_BLOCK
    blk_id = jnp.arange(max_blocks, dtype=blk_end.dtype)
    blk_e = jnp.minimum(jnp.sum(blk_end[None, :] <= blk_id[:, None], axis=1), N_EXPERTS - 1).astype(jnp.int32)
    nused = blk_end[-1:].astype(jnp.int32)
    flat = lambda a: a.reshape(-1).astype(jnp.int32)
    zoff = pstart + total
    zlen = nblk_e * EXPERT_BLOCK - total
    blkrange = jnp.stack([nused[0], jnp.int32(max_blocks)])
    used = blk_id < nused[0]
    first = used & jnp.concatenate([jnp.ones((1,), bool), blk_e[1:] != blk_e[:-1]])
    slot = (jnp.cumsum(first.astype(jnp.int32)) - 1) % 2
    eid = jnp.arange(N_EXPERTS)
    later = (eid[None, :] > eid[:, None]) & (nblk_e > 0)[None, :]
    next_of = jnp.min(jnp.where(later, eid[None, :], N_EXPERTS), axis=1)
    next_of = jnp.where(next_of < N_EXPERTS, next_of, -1)
    first_blk = (blk_end - nblk_e)[blk_e]
    rows_left = jnp.clip(total[blk_e] - (blk_id - first_blk) * EXPERT_BLOCK, 0, EXPERT_BLOCK)
    parts = (rows_left + EXPERT_ROW_STEP - 1) // EXPERT_ROW_STEP
    experts_plan = (blk_e, first.astype(jnp.int32), slot.astype(jnp.int32), next_of[blk_e].astype(jnp.int32),
                    parts.astype(jnp.int32), nused)
    dispatch_plan = (flat(cnt_tiles), flat(off), flat(goff), flat(tot))
    zero_plan = (flat(zoff), flat(zlen), blkrange.astype(jnp.int32))
    return dispatch_plan, zero_plan, experts_plan, max_blocks * EXPERT_BLOCK


def kernel(x, meta_tokens, attn_norm_g, w_in, na_rpb, na_norm_g, hgrn_lb_logits, hgrn_norm_g, w_out, ffn_norm_g,
           router_w, router_b, expert_w_gu, expert_b_gu, expert_w_down, expert_b_down, final_norm_g):
    B, T, D = x.shape
    x2d = x.reshape(B * T, D)
    w_in_b = w_in[0].astype(BF16)
    ng = attn_norm_g[0].reshape(1, D)
    lbl = hgrn_lb_logits[:, :, :].reshape(4, HG_WIDTH)

    att, hg = _inproj(x2d, ng, w_in_b, lbl, 512)
    att_m, hg_m = _inproj(meta_tokens.astype(F32), ng, w_in_b, lbl, N_META)
    att_m = jnp.pad(att_m, ((0, META_PAD - N_META), (0, 0)))
    hg_m = jnp.pad(hg_m, ((0, HG_CHUNK - N_META), (0, 0)))

    yna = _natten(att, att_m, _natten_bias_tables(na_rpb[0]), na_norm_g[0].reshape(1, NA_WIDTH), B, T)
    o_f, o_b = _hgrn(hg, hg_m, B, T)

    h1, xn, lpos, tw, cnt_tiles = _mix_route(
        yna.reshape(B * T, NA_WIDTH), o_f.reshape(B * T, HG_WIDTH), o_b.reshape(B * T, HG_WIDTH), hg, x2d,
        w_out[0].astype(BF16), hgrn_norm_g[0].reshape(1, HG_WIDTH), ffn_norm_g[0].reshape(1, D),
        router_w[0], router_b[0].reshape(1, N_EXPERTS))

    dispatch_plan, zero_plan, experts_plan, n_rows = _moe_layout(cnt_tiles[:, :, 0])
    xbuf = _dispatch(*dispatch_plan, *zero_plan, xn, lpos, n_rows)
    ybuf = _experts(*experts_plan, xbuf, expert_w_gu[0], expert_b_gu[0][:, None, :],
                    expert_w_down[0], expert_b_down[0][:, None, :])
    out = _combine(*dispatch_plan, ybuf, lpos, tw, h1, final_norm_g.reshape(1, D))
    return out.reshape(B, T, D)
```

```python
import functools

import numpy as np
import jax
import jax.numpy as jnp
from jax import lax
from jax.experimental import pallas as pl
from jax.experimental.pallas import tpu as pltpu

F32 = jnp.float32
BF16 = jnp.bfloat16

D_MODEL = 1024
N_META = 16
GRID_W = 64
NA_WIDTH = 512
NA_HEAD_DIM = 64
NA_HEADS = 8
NA_KH = 8
NA_KW = 16
HG_WIDTH = 512
HG_HEAD_DIM = 128
HG_HEADS = 4
HG_CHUNK = 64
IN_COLS = 3 * NA_WIDTH + 5 * HG_WIDTH
N_EXPERTS = 32
TOP_K = 4
D_FF = 1024
SWIGLU_LIMIT = 7.0
SWIGLU_ALPHA = 1.702
RMS_EPS = 1e-6
NEG_BIG = -1e30
LOG2_E = 1.4426950408889634

LANES = 128
VMEM_LIMIT_BYTES = 56 * 1024 * 1024

TOK_TILE = 256
EXPERT_BLOCK = 512
SUBLANES = 8
RUN_ALIGN = SUBLANES
STAGE_ROWS = TOK_TILE * TOP_K + N_EXPERTS * RUN_ALIGN
N_HG_LEVELS = 6


def _dot(a, b):
    return jnp.dot(a, b, preferred_element_type=F32)


def _dot_nt(a, b):
    return lax.dot_general(a, b, (((1,), (1,)), ((), ())), preferred_element_type=F32)


def _dot_tn(a, b):
    return lax.dot_general(a, b, (((0,), (0,)), ((), ())), preferred_element_type=F32)


def _split_bf16(x):
    hi = x.astype(BF16)
    lo = (x - hi.astype(F32)).astype(BF16)
    return hi, lo


def _pack_rows(rows, n_rows):
    idx = lax.broadcasted_iota(jnp.int32, (n_rows, rows[0].shape[1]), 0)
    out = jnp.zeros(idx.shape, rows[0].dtype)
    for k, r in enumerate(rows):
        out = jnp.where(idx == k, r, out)
    return out


def _params(*sem):
    return pltpu.CompilerParams(dimension_semantics=sem, vmem_limit_bytes=VMEM_LIMIT_BYTES)


def _inproj_body(x_ref, ng_ref, w_ref, lbl_ref, att_ref, hg_ref):
    x = x_ref[...]
    ms = jnp.mean(x * x, axis=-1, keepdims=True)
    n = (x * lax.rsqrt(ms + RMS_EPS) * ng_ref[...]).astype(BF16)

    def proj(lo, hi):
        return _dot(n, w_ref[:, lo:hi])

    pq = proj(0, NA_WIDTH)
    att_ref[:, 0:NA_WIDTH] = (pq * (NA_HEAD_DIM ** -0.5 * LOG2_E)).astype(BF16)
    att_ref[:, NA_WIDTH:3 * NA_WIDTH] = proj(NA_WIDTH, 3 * NA_WIDTH).astype(BF16)

    base = 3 * NA_WIDTH
    W = HG_WIDTH
    qh = proj(base, base + W)
    hg_ref[:, 0:W] = qh * jax.nn.sigmoid(qh)
    hg_ref[:, W:2 * W] = proj(base + W, base + 2 * W)
    lbl = lbl_ref[...]
    for d in range(2):
        a0 = lbl[2 * d:2 * d + 1, :]
        a1 = lbl[2 * d + 1:2 * d + 2, :]
        m = jnp.maximum(a0, a1)
        e0 = jnp.exp(a0 - m)
        e1 = jnp.exp(a1 - m)
        lb = e0 / (e0 + e1)
        raw = proj(base + (2 + d) * W, base + (3 + d) * W)
        f = lb + (1.0 - lb) * jax.nn.sigmoid(raw)
        hg_ref[:, (2 + d) * W:(3 + d) * W] = jnp.log(f)
    gt = proj(base + 4 * W, base + 5 * W)
    hg_ref[:, 4 * W:5 * W] = gt * jax.nn.sigmoid(gt)


def _inproj(x2d, norm_g, w_bf16, lb_logits4, tm):
    n = x2d.shape[0]
    return pl.pallas_call(
        _inproj_body,
        grid=(n // tm,),
        in_specs=[
            pl.BlockSpec((tm, D_MODEL), lambda i: (i, 0)),
            pl.BlockSpec((1, D_MODEL), lambda i: (0, 0)),
            pl.BlockSpec((D_MODEL, IN_COLS), lambda i: (0, 0)),
            pl.BlockSpec((4, HG_WIDTH), lambda i: (0, 0)),
        ],
        out_specs=[
            pl.BlockSpec((tm, 3 * NA_WIDTH), lambda i: (i, 0)),
            pl.BlockSpec((tm, 5 * HG_WIDTH), lambda i: (i, 0)),
        ],
        out_shape=[
            jax.ShapeDtypeStruct((n, 3 * NA_WIDTH), BF16),
            jax.ShapeDtypeStruct((n, 5 * HG_WIDTH), F32),
        ],
        compiler_params=_params("arbitrary"),
        name="inproj",
    )(x2d, norm_g, w_bf16, lb_logits4)


N_WIN_KEYS = NA_KH * GRID_W
META_PAD = LANES


NA_GROUP = 4
GROUP_LANES = NA_GROUP * NA_HEAD_DIM


NA_ROWS_PER_STEP = 8


def _natten_row(rr, r, q_ref, k_ref, v_ref, km_ref, vm_ref, bias_ref, *, rows):
    rs = jnp.clip(r - NA_KH // 2, 0, rows - NA_KH)
    d0 = rs - r + (NA_KH - 1)
    start = pl.multiple_of(rs * GRID_W, GRID_W)
    lane_head = lax.broadcasted_iota(jnp.int32, (GRID_W, GROUP_LANES), 1) // NA_HEAD_DIM
    meta_col = lax.broadcasted_iota(jnp.int32, (1, META_PAD), 1)
    meta_bias = jnp.where(meta_col < N_META, 0.0, NEG_BIG).astype(F32)
    folded = []
    for g in range(NA_HEADS // NA_GROUP):
        sl = slice(g * GROUP_LANES, (g + 1) * GROUP_LANES)
        q4 = q_ref[0, rr * GRID_W:(rr + 1) * GRID_W, sl]
        zero = jnp.zeros_like(q4)
        qm = jnp.concatenate([jnp.where(lane_head == h, q4, zero) for h in range(NA_GROUP)], axis=0)
        kw = k_ref[0, pl.ds(start, N_WIN_KEYS), sl]
        vw = v_ref[0, pl.ds(start, N_WIN_KEYS), sl]
        hs = slice(g * NA_GROUP, (g + 1) * NA_GROUP)
        bias = jnp.concatenate(
            [bias_ref[d0 + 2 * j, hs].reshape(NA_GROUP * GRID_W, 2 * GRID_W) for j in range(NA_KH // 2)], axis=-1)
        s = _dot_nt(qm, kw) + bias
        sm = _dot_nt(qm, km_ref[:, sl]) + meta_bias
        m = jnp.maximum(jnp.max(s, axis=-1, keepdims=True), jnp.max(sm, axis=-1, keepdims=True))
        e = jnp.exp2(s - m)
        em = jnp.exp2(sm - m)
        den = jnp.sum(e, axis=-1, keepdims=True) + jnp.sum(em, axis=-1, keepdims=True)
        o = (_dot(e.astype(BF16), vw) + _dot(em.astype(BF16), vm_ref[:, sl])) * (1.0 / den)
        acc = jnp.where(lane_head == 0, o[0:GRID_W], 0.0)
        for h in range(1, NA_GROUP):
            acc = jnp.where(lane_head == h, o[h * GRID_W:(h + 1) * GRID_W], acc)
        folded.append(acc)
    return jnp.concatenate(folded, axis=-1)


def _natten_body(q_ref, k_ref, v_ref, km_ref, vm_ref, bias_ref, ng_ref, bd_ref, o_ref, *, rows):
    r0 = pl.program_id(1) * NA_ROWS_PER_STEP
    o2 = jnp.concatenate(
        [_natten_row(rr, r0 + rr, q_ref, k_ref, v_ref, km_ref, vm_ref, bias_ref, rows=rows)
         for rr in range(NA_ROWS_PER_STEP)], axis=0)
    sq_hi, sq_lo = _split_bf16(o2 * o2)
    ms = (_dot(sq_hi, bd_ref[...]) + _dot(sq_lo, bd_ref[...])) * (1.0 / NA_HEAD_DIM)
    o_ref[0] = (o2 * lax.rsqrt(ms + RMS_EPS) * ng_ref[...]).astype(o_ref.dtype)


def _natten(att, att_meta, bias_tabs, norm_g, batch, seq):
    rows = seq // GRID_W
    att3 = att.reshape(batch, seq, 3 * NA_WIDTH)
    head_of = np.arange(NA_WIDTH) // NA_HEAD_DIM
    same_head = jnp.asarray(head_of[:, None] == head_of[None, :], BF16)

    qrows = NA_ROWS_PER_STEP * GRID_W
    return pl.pallas_call(
        functools.partial(_natten_body, rows=rows),
        grid=(batch, rows // NA_ROWS_PER_STEP),
        in_specs=[
            pl.BlockSpec((1, qrows, NA_WIDTH), lambda b, r: (b, r, 0)),
            pl.BlockSpec((1, seq, NA_WIDTH), lambda b, r: (b, 0, 1)),
            pl.BlockSpec((1, seq, NA_WIDTH), lambda b, r: (b, 0, 2)),
            pl.BlockSpec((META_PAD, NA_WIDTH), lambda b, r: (0, 1)),
            pl.BlockSpec((META_PAD, NA_WIDTH), lambda b, r: (0, 2)),
            pl.BlockSpec(bias_tabs.shape, lambda b, r: (0, 0, 0, 0), pipeline_mode=pl.Buffered(1)),
            pl.BlockSpec((1, NA_WIDTH), lambda b, r: (0, 0)),
            pl.BlockSpec((NA_WIDTH, NA_WIDTH), lambda b, r: (0, 0)),
        ],
        out_specs=pl.BlockSpec((1, qrows, NA_WIDTH), lambda b, r: (b, r, 0)),
        out_shape=jax.ShapeDtypeStruct((batch, seq, NA_WIDTH), BF16),
        compiler_params=_params("arbitrary", "arbitrary"),
        name="natten",
    )(att3, att3, att3, att_meta, att_meta, bias_tabs, norm_g, same_head)


def _natten_bias_tables(rpb):
    c = np.arange(GRID_W)[:, None]
    kc = np.arange(GRID_W)[None, :]
    cs = np.clip(c - NA_KW // 2, 0, GRID_W - NA_KW)
    valid = (kc >= cs) & (kc < cs + NA_KW)
    dc = kc - c + (NA_KW - 1)
    pick = np.asarray(dc[None] == np.arange(2 * NA_KW - 1)[:, None, None], np.float32)
    t1 = jnp.einsum("hdj,jck->hdck", rpb.astype(F32) * LOG2_E, pick, precision=lax.Precision.HIGHEST)
    t1 = jnp.where(valid[None, None], t1, NEG_BIG)
    pairs = jnp.concatenate([t1[:, :-1], t1[:, 1:]], axis=-1)
    return pairs.transpose(1, 0, 2, 3)


HG_CHUNKS_PER_STEP = 8
HG_CHUNKS_PER_GROUP = 1
ROW_B, ROW_LV = 0, 1
N_EXP_BLOCKS = 1 + N_HG_LEVELS


def _hgrn_constants():
    C = HG_CHUNK
    t = np.arange(C)
    u = t[None, :]
    mats = [u <= t[:, None]]
    masks = [np.eye(C, dtype=bool)]
    for lv in range(N_HG_LEVELS):
        m = 1 << lv
        blk = t // (2 * m)
        upper = (t // m) % 2 == 1
        p = blk * 2 * m + m - 1
        q_rows = upper[:, None] & (u > p[:, None]) & (u <= t[:, None])
        k_rows = (~upper)[:, None] & (u > t[:, None]) & (u <= p[:, None])
        mats.append(q_rows | k_rows)
        masks.append((blk[:, None] == blk[None, :]) & upper[:, None] & (~upper)[None, :])
    fwd = np.concatenate(mats, axis=0).astype(np.float32)
    fwd_mask = np.stack(masks).astype(np.float32)
    blocks = fwd.reshape(N_EXP_BLOCKS, C, C)
    bwd = blocks[:, ::-1, ::-1].reshape(N_EXP_BLOCKS * C, C)
    bwd_mask = fwd_mask[:, ::-1, ::-1]
    both = np.stack([fwd, bwd])
    both = np.concatenate([both, both], axis=-1)
    both_mask = np.stack([fwd_mask, bwd_mask])
    both_mask = np.concatenate([both_mask, both_mask], axis=-1)
    return (jnp.asarray(both, BF16), jnp.asarray(both_mask, F32))


def _hgrn_exponents(mat, g):
    g_hi, g_lo = _split_bf16(g)
    return _dot(mat, jnp.concatenate([g_hi, g_lo], axis=0))


def _hgrn_init(meta_ref, mat_ref, stf_ref, stb_ref):
    C, W = HG_CHUNK, HG_WIDTH
    mb = meta_ref[...]
    v = mb[:, W:2 * W].astype(BF16)
    prefix = mat_ref[0, ROW_B * C:(ROW_B + 1) * C, :]
    for d, st_ref in ((0, stf_ref), (1, stb_ref)):
        g = mb[:, (2 + d) * W:(3 + d) * W]
        b = _hgrn_exponents(prefix, g)
        kb = ((1.0 - jnp.exp(g)) * jnp.exp(b[C - 1:C] - b)).astype(BF16)
        for h in range(HG_HEADS):
            sl = slice(h * HG_HEAD_DIM, (h + 1) * HG_HEAD_DIM)
            st_ref[h] = _dot_tn(v[:, sl], kb[:, sl])


def _hgrn_group_tasks(j0):
    js = range(j0, j0 + HG_CHUNKS_PER_GROUP)
    return ([(0, j * HG_CHUNK) for j in js] +
            [(1, (HG_CHUNKS_PER_STEP - 1 - j) * HG_CHUNK) for j in js])


def _hgrn_tasks(tasks, dir_refs, states, mat_ref, mask_ref):
    C, W = HG_CHUNK, HG_WIDTH
    heads = [slice(h * HG_HEAD_DIM, (h + 1) * HG_HEAD_DIM) for h in range(HG_HEADS)]
    qs, ks, vs, ex_bs, ex_lvs, ex_ends = {}, {}, {}, {}, {}, {}
    for d, r0 in tasks:
        q_ref, v_ref, g_ref, total_row = dir_refs[d][0], dir_refs[d][1], dir_refs[d][2], dir_refs[d][3]
        g = g_ref[0, r0:r0 + C, :]
        sums = _hgrn_exponents(mat_ref[d], g)
        ex_bs[d, r0] = jnp.exp(sums[0:C])
        ex_lvs[d, r0] = jnp.exp(sums[C:]).astype(BF16)
        ex_ends[d, r0] = jnp.exp(sums[total_row:total_row + 1] - sums[0:C])
        qs[d, r0] = q_ref[0, r0:r0 + C, :]
        ks[d, r0] = 1.0 - jnp.exp(g)
        vs[d, r0] = v_ref[0, r0:r0 + C, :].astype(BF16)

    D2 = 2 * HG_HEAD_DIM
    pairs = [slice(p * D2, (p + 1) * D2) for p in range(HG_HEADS // 2)]

    def block_diag(x):
        zero = jnp.zeros((C, HG_HEAD_DIM), x.dtype)
        return jnp.concatenate([jnp.concatenate([x[:, :HG_HEAD_DIM], zero], axis=1),
                                jnp.concatenate([zero, x[:, HG_HEAD_DIM:]], axis=1)], axis=0)

    prods = {}
    for task in tasks:
        for p, sl in enumerate(pairs):
            q2, k2 = qs[task][:, sl].astype(BF16), ks[task][:, sl].astype(BF16)
            ps = [_dot_nt(q2, block_diag(k2))]
            for lv in range(N_HG_LEVELS):
                scale = ex_lvs[task][lv * C:(lv + 1) * C, sl]
                ps.append(_dot_nt(q2 * scale, block_diag(k2 * scale)))
            prods[task, p] = ps
    attn, qbs, kbs = {}, {}, {}
    for task in tasks:
        d = task[0]
        for p, sl in enumerate(pairs):
            a = prods[task, p][0] * mask_ref[d, 0]
            for lv in range(N_HG_LEVELS):
                a = a + prods[task, p][1 + lv] * mask_ref[d, 1 + lv]
            o2 = _dot(a.astype(BF16), block_diag(vs[task][:, sl]))
            attn[task, 2 * p] = o2[:, :HG_HEAD_DIM]
            attn[task, 2 * p + 1] = o2[:, HG_HEAD_DIM:]
        for h, sl in enumerate(heads):
            qbs[task, h] = (qs[task][:, sl] * ex_bs[task][:, sl]).astype(BF16)
            kbs[task, h] = _dot_tn(vs[task][:, sl], (ks[task][:, sl] * ex_ends[task][:, sl]).astype(BF16))
    for task in tasks:
        d, r0 = task
        total_row, o_ref = dir_refs[d][3], dir_refs[d][4]
        outs = []
        for h, sl in enumerate(heads):
            st = states[d, h]
            outs.append(attn[task, h] + _dot_nt(qbs[task, h], st.astype(BF16)))
            decay = ex_bs[task][total_row:total_row + 1, sl]
            states[d, h] = decay * st + kbs[task, h]
        o_ref[0, r0:r0 + C, :] = jnp.concatenate(outs, axis=-1)


def _hgrn_body(hqf_ref, hvf_ref, hgf_ref, hqb_ref, hvb_ref, hgb_ref, meta_ref, mat_ref, mask_ref,
               of_ref, ob_ref, stf_ref, stb_ref):
    @pl.when(pl.program_id(1) == 0)
    def _init():
        _hgrn_init(meta_ref, mat_ref, stf_ref, stb_ref)

    dir_refs = ((hqf_ref, hvf_ref, hgf_ref, HG_CHUNK - 1, of_ref), (hqb_ref, hvb_ref, hgb_ref, 0, ob_ref))
    st_refs = (stf_ref, stb_ref)
    states = {(d, h): st_refs[d][h] for d in range(2) for h in range(HG_HEADS)}
    for j0 in range(0, HG_CHUNKS_PER_STEP, HG_CHUNKS_PER_GROUP):
        _hgrn_tasks(_hgrn_group_tasks(j0), dir_refs, states, mat_ref, mask_ref)
    for d in range(2):
        for h in range(HG_HEADS):
            st_refs[d][h] = states[d, h]


def _hgrn(hg, hg_meta_pad, batch, seq):
    steps = seq // (HG_CHUNK * HG_CHUNKS_PER_STEP)
    hg3 = hg.reshape(batch, seq, 5 * HG_WIDTH)
    mats, masks = _hgrn_constants()
    hblk = (1, HG_CHUNK * HG_CHUNKS_PER_STEP, HG_WIDTH)

    def fwd(col):
        return pl.BlockSpec(hblk, lambda b, s: (b, s, col))

    def bwd(col):
        return pl.BlockSpec(hblk, lambda b, s: (b, steps - 1 - s, col))

    return pl.pallas_call(
        _hgrn_body,
        grid=(batch, steps),
        in_specs=[
            fwd(0), fwd(1), fwd(2),
            bwd(0), bwd(1), bwd(3),
            pl.BlockSpec((HG_CHUNK, 5 * HG_WIDTH), lambda b, s: (0, 0)),
            pl.BlockSpec(mats.shape, lambda b, s: (0, 0, 0)),
            pl.BlockSpec(masks.shape, lambda b, s: (0, 0, 0, 0)),
        ],
        out_specs=[fwd(0), bwd(0)],
        out_shape=[jax.ShapeDtypeStruct((batch, seq, HG_WIDTH), F32)] * 2,
        scratch_shapes=[pltpu.VMEM((HG_HEADS, HG_HEAD_DIM, HG_HEAD_DIM), F32)] * 2,
        compiler_params=_params("arbitrary", "arbitrary"),
        name="hgrn",
    )(hg3, hg3, hg3, hg3, hg3, hg3, hg_meta_pad, mats, masks)


MIX_TILES_PER_STEP = 4


def _mix_route_body(yna_ref, of_ref, ob_ref, gate_ref, x_ref, wout_ref, hgn_ref, ffg_ref,
                    rwh_ref, rwl_ref, rb_ref, tri_ref, lowe_ref,
                    h1_ref, xn_ref, lpos_ref, tw_ref, cnt_ref):
    T, E = TOK_TILE, N_EXPERTS
    tiles = [slice(s * T, (s + 1) * T) for s in range(MIX_TILES_PER_STEP)]
    x_his, x_los = [], []
    for rows in tiles:
        o = of_ref[rows, :] + ob_ref[rows, :]
        parts = []
        for h in range(HG_HEADS):
            seg = o[:, h * HG_HEAD_DIM:(h + 1) * HG_HEAD_DIM]
            ms = jnp.mean(seg * seg, axis=-1, keepdims=True)
            parts.append(seg * lax.rsqrt(ms + RMS_EPS))
        yhg = jnp.concatenate(parts, axis=-1) * hgn_ref[...] * gate_ref[rows, :]
        mix = _dot(yna_ref[rows, :], wout_ref[0:NA_WIDTH, :]) + _dot(yhg.astype(BF16), wout_ref[NA_WIDTH:, :])
        h1 = x_ref[rows, :] + mix
        h1_ref[rows, :] = h1
        ms = jnp.mean(h1 * h1, axis=-1, keepdims=True)
        xn = h1 * lax.rsqrt(ms + RMS_EPS) * ffg_ref[...]
        x_hi, x_lo = _split_bf16(xn)
        xn_ref[rows, :] = x_hi
        x_his.append(x_hi)
        x_los.append(x_lo)

    curs = [(_dot_nt(rwh_ref[...], x_hi) + _dot_nt(rwh_ref[...], x_lo) + _dot_nt(rwl_ref[...], x_hi)) + rb_ref[...]
            for x_hi, x_lo in zip(x_his, x_los)]
    row = lax.broadcasted_iota(jnp.int32, (E, T), 0).astype(F32)
    sels = [[] for _ in tiles]
    vals = [[] for _ in tiles]
    for _ in range(TOP_K):
        for s in range(len(tiles)):
            m = jnp.max(curs[s], axis=0, keepdims=True)
            first = jnp.min(jnp.where(curs[s] == m, row, float(E)), axis=0, keepdims=True)
            sel = row == first
            sels[s].append(sel)
            vals[s].append(m)
            curs[s] = jnp.where(sel, -jnp.inf, curs[s])
    for s in range(len(tiles)):
        es = [jnp.exp(vk - vals[s][0]) for vk in vals[s]]
        den = es[0] + es[1] + es[2] + es[3]
        tw_ref[s] = _pack_rows([e / den for e in es], SUBLANES)

        onehot = jnp.zeros((E, T), F32)
        for sel in sels[s]:
            onehot = onehot + jnp.where(sel, 1.0, 0.0)
        lrank = _dot(onehot.astype(BF16), tri_ref[...])
        cnt = jnp.sum(onehot, axis=1, keepdims=True)
        cnt = jnp.floor((cnt + (RUN_ALIGN - 1)) * (1.0 / RUN_ALIGN)) * RUN_ALIGN
        off = _dot(lowe_ref[...], jnp.broadcast_to(cnt, (E, LANES)).astype(BF16))
        base = jnp.concatenate([off] * (T // LANES), axis=1) + lrank
        lpos_ref[s] = _pack_rows(
            [jnp.sum(jnp.where(sel, base, 0.0), axis=0, keepdims=True) for sel in sels[s]],
            SUBLANES).astype(jnp.int32)
        cnt_ref[s] = cnt.astype(jnp.int32)


def _mix_route(yna, o_f, o_b, hg, x2d, wout_bf16, hgn, ffg, router_w, router_b):
    n = x2d.shape[0]
    T = TOK_TILE
    nt = n // T
    rw_t = router_w.T
    rwh = rw_t.astype(BF16)
    rwl = (rw_t - rwh.astype(F32)).astype(BF16)
    router_b = router_b.reshape(N_EXPERTS, 1)
    tri = jnp.asarray(np.triu(np.ones((T, T), np.float32), 1), BF16)
    lowe = jnp.asarray(np.tril(np.ones((N_EXPERTS, N_EXPERTS), np.float32), -1), BF16)
    row = lambda i: (i, 0)
    const = lambda i: (0, 0)
    tile3 = lambda i: (i, 0, 0)
    K = MIX_TILES_PER_STEP
    R = K * T
    return pl.pallas_call(
        _mix_route_body,
        grid=(nt // K,),
        in_specs=[
            pl.BlockSpec((R, NA_WIDTH), row),
            pl.BlockSpec((R, HG_WIDTH), row),
            pl.BlockSpec((R, HG_WIDTH), row),
            pl.BlockSpec((R, HG_WIDTH), lambda i: (i, 4)),
            pl.BlockSpec((R, D_MODEL), row),
            pl.BlockSpec((D_MODEL, D_MODEL), const),
            pl.BlockSpec((1, HG_WIDTH), const),
            pl.BlockSpec((1, D_MODEL), const),
            pl.BlockSpec((N_EXPERTS, D_MODEL), const),
            pl.BlockSpec((N_EXPERTS, D_MODEL), const),
            pl.BlockSpec((N_EXPERTS, 1), const),
            pl.BlockSpec((T, T), const),
            pl.BlockSpec((N_EXPERTS, N_EXPERTS), const),
        ],
        out_specs=[
            pl.BlockSpec((R, D_MODEL), row),
            pl.BlockSpec((R, D_MODEL), row),
            pl.BlockSpec((K, SUBLANES, T), tile3),
            pl.BlockSpec((K, SUBLANES, T), tile3),
            pl.BlockSpec((K, N_EXPERTS, 1), tile3),
        ],
        out_shape=[
            jax.ShapeDtypeStruct((n, D_MODEL), F32),
            jax.ShapeDtypeStruct((n, D_MODEL), BF16),
            jax.ShapeDtypeStruct((nt, SUBLANES, T), jnp.int32),
            jax.ShapeDtypeStruct((nt, SUBLANES, T), F32),
            jax.ShapeDtypeStruct((nt, N_EXPERTS, 1), jnp.int32),
        ],
        compiler_params=_params("arbitrary"),
        name="mix_route",
    )(yna, o_f, o_b, hg, x2d, wout_bf16, hgn, ffg, rwh, rwl, router_b, tri, lowe)


TILES_PER_STEP = 4


def _wait_rows(make_copy, rows):
    make_copy(0, 0, pl.multiple_of(rows, RUN_ALIGN)).wait()


def _copy_runs(i, cnt_ref, off_ref, goff_ref, make_copy):
    def per_expert(e, carry):
        n = pl.multiple_of(cnt_ref[i * N_EXPERTS + e], RUN_ALIGN)
        lo = pl.multiple_of(off_ref[i * N_EXPERTS + e], RUN_ALIGN)
        go = pl.multiple_of(goff_ref[i * N_EXPERTS + e], RUN_ALIGN)

        @pl.when(n > 0)
        def _start():
            make_copy(lo, go, n).start()

        return carry

    lax.fori_loop(0, N_EXPERTS, per_expert, 0)


def _slot_matrix(lpos, weights=None):
    row = lax.broadcasted_iota(jnp.int32, (STAGE_ROWS, TOK_TILE), 0)
    acc = jnp.zeros((STAGE_ROWS, TOK_TILE), F32)
    for k in range(TOP_K):
        hit = lpos[k:k + 1, :] == row
        val = 1.0 if weights is None else weights[k:k + 1, :]
        acc = jnp.where(hit, val, acc)
    return acc


def _dispatch_body(cnt_ref, off_ref, goff_ref, tot_ref, zoff_ref, zlen_ref, blkrange_ref,
                   xn_ref, lpos_ref, xbuf_ref, stage_ref, zero_ref, sems, zsem):
    i = pl.program_id(0)
    last = pl.num_programs(0) - 1

    def zero_fill(act):
        def tail(e, carry):
            n = pl.multiple_of(zlen_ref[e], RUN_ALIGN)

            @pl.when(n > 0)
            def _():
                go = pl.multiple_of(zoff_ref[e], RUN_ALIGN)
                act(pltpu.make_async_copy(zero_ref.at[pl.ds(0, n)], xbuf_ref.at[pl.ds(go, n)], zsem))

            return carry

        def unused(j, carry):
            go = pl.multiple_of(j * EXPERT_BLOCK, EXPERT_BLOCK)
            act(pltpu.make_async_copy(zero_ref, xbuf_ref.at[pl.ds(go, EXPERT_BLOCK)], zsem))
            return carry

        lax.fori_loop(0, N_EXPERTS, tail, 0)
        lax.fori_loop(blkrange_ref[0], blkrange_ref[1], unused, 0)

    @pl.when(i == 0)
    def _zero_start():
        zero_ref[...] = jnp.zeros_like(zero_ref)
        zero_fill(lambda c: c.start())

    for s in range(TILES_PER_STEP):
        t = i * TILES_PER_STEP + s

        def make_copy(lo, go, size, s=s):
            return pltpu.make_async_copy(stage_ref.at[s, pl.ds(lo, size)], xbuf_ref.at[pl.ds(go, size)], sems.at[s])

        @pl.when(i > 0)
        def _drain():
            _wait_rows(make_copy, tot_ref[t - TILES_PER_STEP])

        rows = slice(s * TOK_TILE, (s + 1) * TOK_TILE)
        perm = _slot_matrix(lpos_ref[s]).astype(BF16)
        stage_ref[s] = _dot(perm, xn_ref[rows, :])
        _copy_runs(t, cnt_ref, off_ref, goff_ref, make_copy)

        @pl.when(i == last)
        def _finish():
            _wait_rows(make_copy, tot_ref[t])

    @pl.when(i == last)
    def _zero_finish():
        zero_fill(lambda c: c.wait())


def _dispatch(cnt, off, goff, tot, zoff, zlen, blkrange, xn, lpos, n_rows):
    n = xn.shape[0]
    rows = TILES_PER_STEP * TOK_TILE
    return pl.pallas_call(
        _dispatch_body,
        grid_spec=pltpu.PrefetchScalarGridSpec(
            num_scalar_prefetch=7,
            grid=(n // rows,),
            in_specs=[
                pl.BlockSpec((rows, D_MODEL), lambda i, *_: (i, 0)),
                pl.BlockSpec((TILES_PER_STEP, SUBLANES, TOK_TILE), lambda i, *_: (i, 0, 0)),
            ],
            out_specs=pl.BlockSpec(memory_space=pl.ANY),
            scratch_shapes=[
                pltpu.VMEM((TILES_PER_STEP, STAGE_ROWS, D_MODEL), F32),
                pltpu.VMEM((EXPERT_BLOCK, D_MODEL), F32),
                pltpu.SemaphoreType.DMA((TILES_PER_STEP,)),
                pltpu.SemaphoreType.DMA(()),
            ],
        ),
        out_shape=jax.ShapeDtypeStruct((n_rows, D_MODEL), F32),
        compiler_params=_params("arbitrary"),
        name="moe_dispatch",
    )(cnt, off, goff, tot, zoff, zlen, blkrange, xn, lpos)


CAST_ROWS = 128
EXPERT_ROW_STEP = 128


def _experts_body(blk_e_ref, first_ref, slot_ref, next_e_ref, parts_ref, nused_ref,
                  x_ref, bgu_ref, bd_ref, wgu_hbm, wd_hbm, y_ref,
                  wgu_f32, wd_f32, wgu_bf, wd_bf, sem_gu, sem_d):
    j = pl.program_id(0)

    def weight_copies(e, slot):
        return (pltpu.make_async_copy(wgu_hbm.at[e], wgu_f32.at[slot], sem_gu.at[slot]),
                pltpu.make_async_copy(wd_hbm.at[e], wd_f32.at[slot], sem_d.at[slot]))

    @pl.when(j < nused_ref[0])
    def _():
        e = blk_e_ref[j]
        slot = slot_ref[j]

        @pl.when(first_ref[j] == 1)
        def _new_expert():
            @pl.when(j == 0)
            def _():
                for c in weight_copies(e, slot):
                    c.start()

            for c in weight_copies(e, slot):
                c.wait()
            nxt = next_e_ref[j]

            @pl.when(nxt >= 0)
            def _():
                for c in weight_copies(nxt, 1 - slot):
                    c.start(priority=1)

            def cast(src, dst):
                def rows(r, carry):
                    sl = pl.ds(pl.multiple_of(r * CAST_ROWS, CAST_ROWS), CAST_ROWS)
                    dst[sl, :] = src[slot, sl, :].astype(BF16)
                    return carry
                lax.fori_loop(0, src.shape[1] // CAST_ROWS, rows, 0)

            cast(wgu_f32, wgu_bf)
            cast(wd_f32, wd_bf)

        def ffn(rows):
            x = x_ref[0:rows, :].astype(BF16)
            gu = _dot(x, wgu_bf[...]) + bgu_ref[0]
            gate = jnp.minimum(gu[:, :D_FF], SWIGLU_LIMIT)
            up = jnp.clip(gu[:, D_FF:], -SWIGLU_LIMIT, SWIGLU_LIMIT)
            act = (up + 1.0) * gate * jax.nn.sigmoid(SWIGLU_ALPHA * gate)
            y_ref[0:rows, :] = _dot(act.astype(BF16), wd_bf[...]) + bd_ref[0]
            if rows < EXPERT_BLOCK:
                y_ref[rows:, :] = jnp.zeros((EXPERT_BLOCK - rows, D_MODEL), F32)

        for parts in range(1, EXPERT_BLOCK // EXPERT_ROW_STEP + 1):
            pl.when(parts_ref[j] == parts)(functools.partial(ffn, parts * EXPERT_ROW_STEP))


def _experts(blk_e, first, slot, next_e, parts, nused, xbuf, wgu, bgu, wd, bd):
    n_rows = xbuf.shape[0]
    nblk = n_rows // EXPERT_BLOCK

    def rowblk(j, be, fi, sl, ne, pa, nu):
        return (jnp.minimum(j, nu[0] - 1), 0)

    def expert(j, be, fi, sl, ne, pa, nu):
        return (be[jnp.minimum(j, nu[0] - 1)], 0, 0)

    return pl.pallas_call(
        _experts_body,
        grid_spec=pltpu.PrefetchScalarGridSpec(
            num_scalar_prefetch=6,
            grid=(nblk,),
            in_specs=[
                pl.BlockSpec((EXPERT_BLOCK, D_MODEL), rowblk),
                pl.BlockSpec((1, 1, 2 * D_FF), expert),
                pl.BlockSpec((1, 1, D_MODEL), expert),
                pl.BlockSpec(memory_space=pl.ANY),
                pl.BlockSpec(memory_space=pl.ANY),
            ],
            out_specs=pl.BlockSpec((EXPERT_BLOCK, D_MODEL), rowblk),
            scratch_shapes=[
                pltpu.VMEM((2, D_MODEL, 2 * D_FF), F32),
                pltpu.VMEM((2, D_FF, D_MODEL), F32),
                pltpu.VMEM((D_MODEL, 2 * D_FF), BF16),
                pltpu.VMEM((D_FF, D_MODEL), BF16),
                pltpu.SemaphoreType.DMA((2,)),
                pltpu.SemaphoreType.DMA((2,)),
            ],
        ),
        out_shape=jax.ShapeDtypeStruct((n_rows, D_MODEL), F32),
        input_output_aliases={6: 0},
        compiler_params=_params("arbitrary"),
        name="moe_experts",
    )(blk_e, first, slot, next_e, parts, nused, xbuf, bgu, bd, wgu, wd)


def _combine_body(cnt_ref, off_ref, goff_ref, tot_ref, ybuf_ref, lpos_ref, tw_ref, h1_ref, fg_ref, out_ref,
                  stage_ref, sems):
    i = pl.program_id(0)
    n_tiles = pl.num_programs(0) * TILES_PER_STEP

    def copier(s):
        def make_copy(lo, go, size):
            return pltpu.make_async_copy(ybuf_ref.at[pl.ds(go, size)], stage_ref.at[s, pl.ds(lo, size)], sems.at[s])
        return make_copy

    ahead = TILES_PER_STEP - 1

    @pl.when(i == 0)
    def _init():
        stage_ref[...] = jnp.zeros_like(stage_ref)
        for s in range(ahead):
            _copy_runs(s, cnt_ref, off_ref, goff_ref, copier(s))

    for s in range(TILES_PER_STEP):
        t = i * TILES_PER_STEP + s

        @pl.when(t + ahead < n_tiles)
        def _prefetch():
            _copy_runs(t + ahead, cnt_ref, off_ref, goff_ref, copier((s + ahead) % TILES_PER_STEP))

        rows = slice(s * TOK_TILE, (s + 1) * TOK_TILE)
        w = _slot_matrix(lpos_ref[s], tw_ref[s]).astype(BF16)
        _wait_rows(copier(s), tot_ref[t])
        h2 = h1_ref[rows, :] + _dot_tn(w, stage_ref[s].astype(BF16))
        ms = jnp.mean(h2 * h2, axis=-1, keepdims=True)
        out_ref[rows, :] = h2 * lax.rsqrt(ms + RMS_EPS) * fg_ref[...]


def _combine(cnt, off, goff, tot, ybuf, lpos, tw, h1, final_g):
    n = h1.shape[0]
    rows = TILES_PER_STEP * TOK_TILE
    row = lambda i, *_: (i, 0)
    return pl.pallas_call(
        _combine_body,
        grid_spec=pltpu.PrefetchScalarGridSpec(
            num_scalar_prefetch=4,
            grid=(n // rows,),
            in_specs=[
                pl.BlockSpec(memory_space=pl.ANY),
                pl.BlockSpec((TILES_PER_STEP, SUBLANES, TOK_TILE), lambda i, *_: (i, 0, 0)),
                pl.BlockSpec((TILES_PER_STEP, SUBLANES, TOK_TILE), lambda i, *_: (i, 0, 0)),
                pl.BlockSpec((rows, D_MODEL), row),
                pl.BlockSpec((1, D_MODEL), lambda i, *_: (0, 0)),
            ],
            out_specs=pl.BlockSpec((rows, D_MODEL), row),
            scratch_shapes=[pltpu.VMEM((TILES_PER_STEP, STAGE_ROWS, D_MODEL), F32),
                            pltpu.SemaphoreType.DMA((TILES_PER_STEP,))],
        ),
        out_shape=jax.ShapeDtypeStruct((n, D_MODEL), F32),
        compiler_params=_params("arbitrary"),
        name="moe_combine",
    )(cnt, off, goff, tot, ybuf, lpos, tw, h1, final_g)


def _moe_layout(cnt_tiles):
    nt = cnt_tiles.shape[0]
    total = jnp.sum(cnt_tiles, axis=0)
    nblk_e = (total + EXPERT_BLOCK - 1) // EXPERT_BLOCK
    blk_end = jnp.cumsum(nblk_e)
    pstart = (blk_end - nblk_e) * EXPERT_BLOCK
    prefix = jnp.cumsum(cnt_tiles, axis=0) - cnt_tiles
    goff = pstart[None, :] + prefix
    off = jnp.cumsum(cnt_tiles, axis=1) - cnt_tiles
    tot = jnp.sum(cnt_tiles, axis=1)
    max_rows = nt * (TOK_TILE * TOP_K + N_EXPERTS * (RUN_ALIGN - 1)) + N_EXPERTS * (EXPERT_BLOCK - 1)
    max_blocks = (max_rows + EXPERT_BLOCK - 1) // EXPERT_BLOCK
    blk_id = jnp.arange(max_blocks, dtype=blk_end.dtype)
    blk_e = jnp.minimum(jnp.sum(blk_end[None, :] <= blk_id[:, None], axis=1), N_EXPERTS - 1).astype(jnp.int32)
    nused = blk_end[-1:].astype(jnp.int32)
    flat = lambda a: a.reshape(-1).astype(jnp.int32)
    zoff = pstart + total
    zlen = nblk_e * EXPERT_BLOCK - total
    blkrange = jnp.stack([nused[0], jnp.int32(max_blocks)])
    used = blk_id < nused[0]
    first = used & jnp.concatenate([jnp.ones((1,), bool), blk_e[1:] != blk_e[:-1]])
    slot = (jnp.cumsum(first.astype(jnp.int32)) - 1) % 2
    eid = jnp.arange(N_EXPERTS)
    later = (eid[None, :] > eid[:, None]) & (nblk_e > 0)[None, :]
    next_of = jnp.min(jnp.where(later, eid[None, :], N_EXPERTS), axis=1)
    next_of = jnp.where(next_of < N_EXPERTS, next_of, -1)
    is_e = blk_e[:, None] == eid[None, :]
    pick = lambda per_expert: jnp.sum(jnp.where(is_e, per_expert[None, :], 0), axis=1)
    rows_left = jnp.clip(pick(total) - (blk_id - pick(blk_end - nblk_e)) * EXPERT_BLOCK, 0, EXPERT_BLOCK)
    parts = (rows_left + EXPERT_ROW_STEP - 1) // EXPERT_ROW_STEP
    experts_plan = (blk_e, first.astype(jnp.int32), slot.astype(jnp.int32), pick(next_of).astype(jnp.int32),
                    parts.astype(jnp.int32), nused)
    dispatch_plan = (flat(cnt_tiles), flat(off), flat(goff), flat(tot))
    zero_plan = (flat(zoff), flat(zlen), blkrange.astype(jnp.int32))
    return dispatch_plan, zero_plan, experts_plan, max_blocks * EXPERT_BLOCK


def kernel(x, meta_tokens, attn_norm_g, w_in, na_rpb, na_norm_g, hgrn_lb_logits, hgrn_norm_g, w_out, ffn_norm_g,
           router_w, router_b, expert_w_gu, expert_b_gu, expert_w_down, expert_b_down, final_norm_g):
    B, T, D = x.shape
    x2d = x.reshape(B * T, D)
    w_in_b = w_in[0].astype(BF16)
    ng = attn_norm_g[0].reshape(1, D)
    lbl = hgrn_lb_logits[:, :, :].reshape(4, HG_WIDTH)

    att, hg = _inproj(x2d, ng, w_in_b, lbl, 512)
    att_m, hg_m = _inproj(meta_tokens.astype(F32), ng, w_in_b, lbl, N_META)
    att_m = jnp.pad(att_m, ((0, META_PAD - N_META), (0, 0)))
    hg_m = jnp.pad(hg_m, ((0, HG_CHUNK - N_META), (0, 0)))

    yna = _natten(att, att_m, _natten_bias_tables(na_rpb[0]), na_norm_g[0].reshape(1, NA_WIDTH), B, T)
    o_f, o_b = _hgrn(hg, hg_m, B, T)

    h1, xn, lpos, tw, cnt_tiles = _mix_route(
        yna.reshape(B * T, NA_WIDTH), o_f.reshape(B * T, HG_WIDTH), o_b.reshape(B * T, HG_WIDTH), hg, x2d,
        w_out[0].astype(BF16), hgrn_norm_g[0].reshape(1, HG_WIDTH), ffn_norm_g[0].reshape(1, D),
        router_w[0], router_b[0].reshape(1, N_EXPERTS))

    dispatch_plan, zero_plan, experts_plan, n_rows = _moe_layout(cnt_tiles[:, :, 0])
    xbuf = _dispatch(*dispatch_plan, *zero_plan, xn, lpos, n_rows)
    ybuf = _experts(*experts_plan, xbuf, expert_w_gu[0], expert_b_gu[0][:, None, :],
                    expert_w_down[0], expert_b_down[0][:, None, :])
    out = _combine(*dispatch_plan, ybuf, lpos, tw, h1, final_norm_g.reshape(1, D))
    return out.reshape(B, T, D)
```

```python
import functools

import numpy as np
import jax
import jax.numpy as jnp
from jax import lax
from jax.experimental import pallas as pl
from jax.experimental.pallas import tpu as pltpu

F32 = jnp.float32
BF16 = jnp.bfloat16

D_MODEL = 1024
N_META = 16
GRID_W = 64
NA_WIDTH = 512
NA_HEAD_DIM = 64
NA_HEADS = 8
NA_KH = 8
NA_KW = 16
HG_WIDTH = 512
HG_HEAD_DIM = 128
HG_HEADS = 4
HG_CHUNK = 64
IN_COLS = 3 * NA_WIDTH + 5 * HG_WIDTH
N_EXPERTS = 32
TOP_K = 4
D_FF = 1024
SWIGLU_LIMIT = 7.0
SWIGLU_ALPHA = 1.702
RMS_EPS = 1e-6
NEG_BIG = -1e30
LOG2_E = 1.4426950408889634

LANES = 128
VMEM_LIMIT_BYTES = 56 * 1024 * 1024

TOK_TILE = 256
EXPERT_BLOCK = 512
SUBLANES = 8
RUN_ALIGN = SUBLANES
STAGE_ROWS = TOK_TILE * TOP_K + N_EXPERTS * RUN_ALIGN
N_HG_LEVELS = 6


def _dot(a, b):
    return jnp.dot(a, b, preferred_element_type=F32)


def _dot_nt(a, b):
    return lax.dot_general(a, b, (((1,), (1,)), ((), ())), preferred_element_type=F32)


def _dot_tn(a, b):
    return lax.dot_general(a, b, (((0,), (0,)), ((), ())), preferred_element_type=F32)


def _split_bf16(x):
    hi = x.astype(BF16)
    lo = (x - hi.astype(F32)).astype(BF16)
    return hi, lo


def _pack_rows(rows, n_rows):
    idx = lax.broadcasted_iota(jnp.int32, (n_rows, rows[0].shape[1]), 0)
    out = jnp.zeros(idx.shape, rows[0].dtype)
    for k, r in enumerate(rows):
        out = jnp.where(idx == k, r, out)
    return out


def _params(*sem):
    return pltpu.CompilerParams(dimension_semantics=sem, vmem_limit_bytes=VMEM_LIMIT_BYTES)


def _inproj_body(x_ref, ng_ref, w_ref, lbl_ref, att_ref, hga_ref, hgg_ref):
    x = x_ref[...]
    ms = jnp.mean(x * x, axis=-1, keepdims=True)
    n = (x * lax.rsqrt(ms + RMS_EPS) * ng_ref[...]).astype(BF16)

    def proj(lo, hi):
        return _dot(n, w_ref[:, lo:hi])

    pq = proj(0, NA_WIDTH)
    att_ref[:, 0:NA_WIDTH] = (pq * (NA_HEAD_DIM ** -0.5 * LOG2_E)).astype(BF16)
    att_ref[:, NA_WIDTH:3 * NA_WIDTH] = proj(NA_WIDTH, 3 * NA_WIDTH).astype(BF16)

    base = 3 * NA_WIDTH
    W = HG_WIDTH
    qh = proj(base, base + W)
    hga_ref[:, 0:W] = (qh * jax.nn.sigmoid(qh)).astype(BF16)
    hga_ref[:, W:2 * W] = proj(base + W, base + 2 * W).astype(BF16)
    lbl = lbl_ref[...]
    for d in range(2):
        a0 = lbl[2 * d:2 * d + 1, :]
        a1 = lbl[2 * d + 1:2 * d + 2, :]
        m = jnp.maximum(a0, a1)
        e0 = jnp.exp(a0 - m)
        e1 = jnp.exp(a1 - m)
        lb = e0 / (e0 + e1)
        raw = proj(base + (2 + d) * W, base + (3 + d) * W)
        f = lb + (1.0 - lb) * jax.nn.sigmoid(raw)
        hgg_ref[:, d * W:(d + 1) * W] = jnp.log(f)
    gt = proj(base + 4 * W, base + 5 * W)
    hga_ref[:, 2 * W:3 * W] = (gt * jax.nn.sigmoid(gt)).astype(BF16)


def _inproj(x2d, norm_g, w_bf16, lb_logits4, tm):
    n = x2d.shape[0]
    return pl.pallas_call(
        _inproj_body,
        grid=(n // tm,),
        in_specs=[
            pl.BlockSpec((tm, D_MODEL), lambda i: (i, 0)),
            pl.BlockSpec((1, D_MODEL), lambda i: (0, 0)),
            pl.BlockSpec((D_MODEL, IN_COLS), lambda i: (0, 0)),
            pl.BlockSpec((4, HG_WIDTH), lambda i: (0, 0)),
        ],
        out_specs=[
            pl.BlockSpec((tm, 3 * NA_WIDTH), lambda i: (i, 0)),
            pl.BlockSpec((tm, 3 * HG_WIDTH), lambda i: (i, 0)),
            pl.BlockSpec((tm, 2 * HG_WIDTH), lambda i: (i, 0)),
        ],
        out_shape=[
            jax.ShapeDtypeStruct((n, 3 * NA_WIDTH), BF16),
            jax.ShapeDtypeStruct((n, 3 * HG_WIDTH), BF16),
            jax.ShapeDtypeStruct((n, 2 * HG_WIDTH), F32),
        ],
        compiler_params=_params("arbitrary"),
        name="inproj",
    )(x2d, norm_g, w_bf16, lb_logits4)


N_WIN_KEYS = NA_KH * GRID_W
META_PAD = LANES


NA_GROUP = 4
GROUP_LANES = NA_GROUP * NA_HEAD_DIM


NA_ROWS_PER_STEP = 8


def _natten_row(rr, r, q_ref, k_ref, v_ref, km_ref, vm_ref, bias_ref, *, rows):
    rs = jnp.clip(r - NA_KH // 2, 0, rows - NA_KH)
    d0 = rs - r + (NA_KH - 1)
    start = pl.multiple_of(rs * GRID_W, GRID_W)
    lane_head = lax.broadcasted_iota(jnp.int32, (GRID_W, GROUP_LANES), 1) // NA_HEAD_DIM
    meta_col = lax.broadcasted_iota(jnp.int32, (1, META_PAD), 1)
    meta_bias = jnp.where(meta_col < N_META, 0.0, NEG_BIG).astype(F32)
    folded = []
    for g in range(NA_HEADS // NA_GROUP):
        sl = slice(g * GROUP_LANES, (g + 1) * GROUP_LANES)
        q4 = q_ref[0, rr * GRID_W:(rr + 1) * GRID_W, sl]
        zero = jnp.zeros_like(q4)
        qm = jnp.concatenate([jnp.where(lane_head == h, q4, zero) for h in range(NA_GROUP)], axis=0)
        kw = k_ref[0, pl.ds(start, N_WIN_KEYS), sl]
        vw = v_ref[0, pl.ds(start, N_WIN_KEYS), sl]
        hs = slice(g * NA_GROUP, (g + 1) * NA_GROUP)
        bias = jnp.concatenate(
            [bias_ref[d0 + 2 * j, hs].reshape(NA_GROUP * GRID_W, 2 * GRID_W) for j in range(NA_KH // 2)], axis=-1)
        s = _dot_nt(qm, kw) + bias
        sm = _dot_nt(qm, km_ref[:, sl]) + meta_bias
        m = jnp.maximum(jnp.max(s, axis=-1, keepdims=True), jnp.max(sm, axis=-1, keepdims=True))
        e = jnp.exp2(s - m)
        em = jnp.exp2(sm - m)
        den = jnp.sum(e, axis=-1, keepdims=True) + jnp.sum(em, axis=-1, keepdims=True)
        o = (_dot(e.astype(BF16), vw) + _dot(em.astype(BF16), vm_ref[:, sl])) * (1.0 / den)
        acc = jnp.where(lane_head == 0, o[0:GRID_W], 0.0)
        for h in range(1, NA_GROUP):
            acc = jnp.where(lane_head == h, o[h * GRID_W:(h + 1) * GRID_W], acc)
        folded.append(acc)
    return jnp.concatenate(folded, axis=-1)


def _natten_body(q_ref, k_ref, v_ref, km_ref, vm_ref, bias_ref, ng_ref, bd_ref, o_ref, *, rows):
    r0 = pl.program_id(1) * NA_ROWS_PER_STEP
    o2 = jnp.concatenate(
        [_natten_row(rr, r0 + rr, q_ref, k_ref, v_ref, km_ref, vm_ref, bias_ref, rows=rows)
         for rr in range(NA_ROWS_PER_STEP)], axis=0)
    sq_hi, sq_lo = _split_bf16(o2 * o2)
    ms = (_dot(sq_hi, bd_ref[...]) + _dot(sq_lo, bd_ref[...])) * (1.0 / NA_HEAD_DIM)
    o_ref[0] = (o2 * lax.rsqrt(ms + RMS_EPS) * ng_ref[...]).astype(o_ref.dtype)


def _natten(att, att_meta, bias_tabs, norm_g, batch, seq):
    rows = seq // GRID_W
    att3 = att.reshape(batch, seq, 3 * NA_WIDTH)
    head_of = np.arange(NA_WIDTH) // NA_HEAD_DIM
    same_head = jnp.asarray(head_of[:, None] == head_of[None, :], BF16)

    qrows = NA_ROWS_PER_STEP * GRID_W
    return pl.pallas_call(
        functools.partial(_natten_body, rows=rows),
        grid=(batch, rows // NA_ROWS_PER_STEP),
        in_specs=[
            pl.BlockSpec((1, qrows, NA_WIDTH), lambda b, r: (b, r, 0)),
            pl.BlockSpec((1, seq, NA_WIDTH), lambda b, r: (b, 0, 1)),
            pl.BlockSpec((1, seq, NA_WIDTH), lambda b, r: (b, 0, 2)),
            pl.BlockSpec((META_PAD, NA_WIDTH), lambda b, r: (0, 1)),
            pl.BlockSpec((META_PAD, NA_WIDTH), lambda b, r: (0, 2)),
            pl.BlockSpec(bias_tabs.shape, lambda b, r: (0, 0, 0, 0), pipeline_mode=pl.Buffered(1)),
            pl.BlockSpec((1, NA_WIDTH), lambda b, r: (0, 0)),
            pl.BlockSpec((NA_WIDTH, NA_WIDTH), lambda b, r: (0, 0)),
        ],
        out_specs=pl.BlockSpec((1, qrows, NA_WIDTH), lambda b, r: (b, r, 0)),
        out_shape=jax.ShapeDtypeStruct((batch, seq, NA_WIDTH), BF16),
        compiler_params=_params("arbitrary", "arbitrary"),
        name="natten",
    )(att3, att3, att3, att_meta, att_meta, bias_tabs, norm_g, same_head)


def _natten_bias_tables(rpb):
    c = np.arange(GRID_W)[:, None]
    kc = np.arange(GRID_W)[None, :]
    cs = np.clip(c - NA_KW // 2, 0, GRID_W - NA_KW)
    valid = (kc >= cs) & (kc < cs + NA_KW)
    dc = kc - c + (NA_KW - 1)
    pick = np.asarray(dc[None] == np.arange(2 * NA_KW - 1)[:, None, None], np.float32)
    t1 = jnp.einsum("hdj,jck->hdck", rpb.astype(F32) * LOG2_E, pick, precision=lax.Precision.HIGHEST)
    t1 = jnp.where(valid[None, None], t1, NEG_BIG)
    pairs = jnp.concatenate([t1[:, :-1], t1[:, 1:]], axis=-1)
    return pairs.transpose(1, 0, 2, 3)


HG_CHUNKS_PER_STEP = 8
HG_CHUNKS_PER_GROUP = 1
ROW_B, ROW_LV = 0, 1
N_EXP_BLOCKS = 1 + N_HG_LEVELS


def _hgrn_constants():
    C = HG_CHUNK
    t = np.arange(C)
    u = t[None, :]
    mats = [u <= t[:, None]]
    masks = [np.eye(C, dtype=bool)]
    for lv in range(N_HG_LEVELS):
        m = 1 << lv
        blk = t // (2 * m)
        upper = (t // m) % 2 == 1
        p = blk * 2 * m + m - 1
        q_rows = upper[:, None] & (u > p[:, None]) & (u <= t[:, None])
        k_rows = (~upper)[:, None] & (u > t[:, None]) & (u <= p[:, None])
        mats.append(q_rows | k_rows)
        masks.append((blk[:, None] == blk[None, :]) & upper[:, None] & (~upper)[None, :])
    fwd = np.concatenate(mats, axis=0).astype(np.float32)
    fwd_mask = np.stack(masks).astype(np.float32)
    blocks = fwd.reshape(N_EXP_BLOCKS, C, C)
    bwd = blocks[:, ::-1, ::-1].reshape(N_EXP_BLOCKS * C, C)
    bwd_mask = fwd_mask[:, ::-1, ::-1]
    both = np.stack([fwd, bwd])
    both = np.concatenate([both, both], axis=-1)
    both_mask = np.stack([fwd_mask, bwd_mask])
    both_mask = np.concatenate([both_mask, both_mask], axis=-1)
    return (jnp.asarray(both, BF16), jnp.asarray(both_mask, F32))


def _hgrn_exponents(mat, g):
    g_hi, g_lo = _split_bf16(g)
    return _dot(mat, jnp.concatenate([g_hi, g_lo], axis=0))


def _hgrn_init(meta_a_ref, meta_g_ref, mat_ref, stf_ref, stb_ref):
    C, W = HG_CHUNK, HG_WIDTH
    v = meta_a_ref[:, W:2 * W]
    prefix = mat_ref[0, ROW_B * C:(ROW_B + 1) * C, :]
    for d, st_ref in ((0, stf_ref), (1, stb_ref)):
        g = meta_g_ref[:, d * W:(d + 1) * W]
        b = _hgrn_exponents(prefix, g)
        kb = ((1.0 - jnp.exp(g)) * jnp.exp(b[C - 1:C] - b)).astype(BF16)
        for h in range(HG_HEADS):
            sl = slice(h * HG_HEAD_DIM, (h + 1) * HG_HEAD_DIM)
            st_ref[h] = _dot_tn(v[:, sl], kb[:, sl])


def _hgrn_group_tasks(j0):
    js = range(j0, j0 + HG_CHUNKS_PER_GROUP)
    return ([(0, j * HG_CHUNK) for j in js] +
            [(1, (HG_CHUNKS_PER_STEP - 1 - j) * HG_CHUNK) for j in js])


def _hgrn_tasks(tasks, dir_refs, states, mat_ref, mask_ref):
    C, W = HG_CHUNK, HG_WIDTH
    heads = [slice(h * HG_HEAD_DIM, (h + 1) * HG_HEAD_DIM) for h in range(HG_HEADS)]
    qs, ks, vs, ex_bs, ex_lvs, ex_ends = {}, {}, {}, {}, {}, {}
    for d, r0 in tasks:
        q_ref, v_ref, g_ref, total_row = dir_refs[d][0], dir_refs[d][1], dir_refs[d][2], dir_refs[d][3]
        g = g_ref[0, r0:r0 + C, :]
        sums = _hgrn_exponents(mat_ref[d], g)
        ex_bs[d, r0] = jnp.exp(sums[0:C])
        ex_lvs[d, r0] = jnp.exp(sums[C:]).astype(BF16)
        ex_ends[d, r0] = jnp.exp(sums[total_row:total_row + 1] - sums[0:C])
        qs[d, r0] = q_ref[0, r0:r0 + C, :]
        ks[d, r0] = 1.0 - jnp.exp(g)
        vs[d, r0] = v_ref[0, r0:r0 + C, :]

    D2 = 2 * HG_HEAD_DIM
    pairs = [slice(p * D2, (p + 1) * D2) for p in range(HG_HEADS // 2)]

    def block_diag(x):
        zero = jnp.zeros((C, HG_HEAD_DIM), x.dtype)
        return jnp.concatenate([jnp.concatenate([x[:, :HG_HEAD_DIM], zero], axis=1),
                                jnp.concatenate([zero, x[:, HG_HEAD_DIM:]], axis=1)], axis=0)

    prods = {}
    for task in tasks:
        for p, sl in enumerate(pairs):
            q2, k2 = qs[task][:, sl], ks[task][:, sl].astype(BF16)
            ps = [_dot_nt(q2, block_diag(k2))]
            for lv in range(N_HG_LEVELS):
                scale = ex_lvs[task][lv * C:(lv + 1) * C, sl]
                ps.append(_dot_nt(q2 * scale, block_diag(k2 * scale)))
            prods[task, p] = ps
    attn, qbs, kbs = {}, {}, {}
    for task in tasks:
        d = task[0]
        for p, sl in enumerate(pairs):
            a = prods[task, p][0] * mask_ref[d, 0]
            for lv in range(N_HG_LEVELS):
                a = a + prods[task, p][1 + lv] * mask_ref[d, 1 + lv]
            o2 = _dot(a.astype(BF16), block_diag(vs[task][:, sl]))
            attn[task, 2 * p] = o2[:, :HG_HEAD_DIM]
            attn[task, 2 * p + 1] = o2[:, HG_HEAD_DIM:]
        for h, sl in enumerate(heads):
            qbs[task, h] = (qs[task][:, sl].astype(F32) * ex_bs[task][:, sl]).astype(BF16)
            kbs[task, h] = _dot_tn(vs[task][:, sl], (ks[task][:, sl] * ex_ends[task][:, sl]).astype(BF16))
    for task in tasks:
        d, r0 = task
        total_row, o_ref = dir_refs[d][3], dir_refs[d][4]
        outs = []
        for h, sl in enumerate(heads):
            st = states[d, h]
            outs.append(attn[task, h] + _dot_nt(qbs[task, h], st.astype(BF16)))
            decay = ex_bs[task][total_row:total_row + 1, sl]
            states[d, h] = decay * st + kbs[task, h]
        o_ref[0, r0:r0 + C, :] = jnp.concatenate(outs, axis=-1).astype(o_ref.dtype)


def _hgrn_body(hqf_ref, hvf_ref, hgf_ref, hqb_ref, hvb_ref, hgb_ref, meta_a_ref, meta_g_ref, mat_ref, mask_ref,
               of_ref, ob_ref, stf_ref, stb_ref):
    @pl.when(pl.program_id(1) == 0)
    def _init():
        _hgrn_init(meta_a_ref, meta_g_ref, mat_ref, stf_ref, stb_ref)

    dir_refs = ((hqf_ref, hvf_ref, hgf_ref, HG_CHUNK - 1, of_ref), (hqb_ref, hvb_ref, hgb_ref, 0, ob_ref))
    st_refs = (stf_ref, stb_ref)
    states = {(d, h): st_refs[d][h] for d in range(2) for h in range(HG_HEADS)}
    for j0 in range(0, HG_CHUNKS_PER_STEP, HG_CHUNKS_PER_GROUP):
        _hgrn_tasks(_hgrn_group_tasks(j0), dir_refs, states, mat_ref, mask_ref)
    for d in range(2):
        for h in range(HG_HEADS):
            st_refs[d][h] = states[d, h]


def _hgrn(hga, hgg, hga_meta_pad, hgg_meta_pad, batch, seq):
    steps = seq // (HG_CHUNK * HG_CHUNKS_PER_STEP)
    a3 = hga.reshape(batch, seq, 3 * HG_WIDTH)
    g3 = hgg.reshape(batch, seq, 2 * HG_WIDTH)
    mats, masks = _hgrn_constants()
    hblk = (1, HG_CHUNK * HG_CHUNKS_PER_STEP, HG_WIDTH)

    def fwd(col):
        return pl.BlockSpec(hblk, lambda b, s: (b, s, col))

    def bwd(col):
        return pl.BlockSpec(hblk, lambda b, s: (b, steps - 1 - s, col))

    return pl.pallas_call(
        _hgrn_body,
        grid=(batch, steps),
        in_specs=[
            fwd(0), fwd(1), fwd(0),
            bwd(0), bwd(1), bwd(1),
            pl.BlockSpec((HG_CHUNK, 3 * HG_WIDTH), lambda b, s: (0, 0)),
            pl.BlockSpec((HG_CHUNK, 2 * HG_WIDTH), lambda b, s: (0, 0)),
            pl.BlockSpec(mats.shape, lambda b, s: (0, 0, 0)),
            pl.BlockSpec(masks.shape, lambda b, s: (0, 0, 0, 0)),
        ],
        out_specs=[fwd(0), bwd(0)],
        out_shape=[jax.ShapeDtypeStruct((batch, seq, HG_WIDTH), BF16)] * 2,
        scratch_shapes=[pltpu.VMEM((HG_HEADS, HG_HEAD_DIM, HG_HEAD_DIM), F32)] * 2,
        compiler_params=_params("arbitrary", "arbitrary"),
        name="hgrn",
    )(a3, a3, g3, a3, a3, g3, hga_meta_pad, hgg_meta_pad, mats, masks)


MIX_TILES_PER_STEP = 4


def _mix_route_body(yna_ref, of_ref, ob_ref, gate_ref, x_ref, wout_ref, hgn_ref, ffg_ref,
                    rwh_ref, rwl_ref, rb_ref, tri_ref, lowe_ref,
                    h1_ref, xn_ref, lpos_ref, tw_ref, cnt_ref):
    T, E = TOK_TILE, N_EXPERTS
    tiles = [slice(s * T, (s + 1) * T) for s in range(MIX_TILES_PER_STEP)]
    x_his, x_los = [], []
    for rows in tiles:
        o = of_ref[rows, :].astype(F32) + ob_ref[rows, :].astype(F32)
        parts = []
        for h in range(HG_HEADS):
            seg = o[:, h * HG_HEAD_DIM:(h + 1) * HG_HEAD_DIM]
            ms = jnp.mean(seg * seg, axis=-1, keepdims=True)
            parts.append(seg * lax.rsqrt(ms + RMS_EPS))
        yhg = jnp.concatenate(parts, axis=-1) * hgn_ref[...] * gate_ref[rows, :]
        mix = _dot(yna_ref[rows, :], wout_ref[0:NA_WIDTH, :]) + _dot(yhg.astype(BF16), wout_ref[NA_WIDTH:, :])
        h1 = x_ref[rows, :] + mix
        h1_ref[rows, :] = h1
        ms = jnp.mean(h1 * h1, axis=-1, keepdims=True)
        xn = h1 * lax.rsqrt(ms + RMS_EPS) * ffg_ref[...]
        x_hi, x_lo = _split_bf16(xn)
        xn_ref[rows, :] = x_hi
        x_his.append(x_hi)
        x_los.append(x_lo)

    curs = [(_dot_nt(rwh_ref[...], x_hi) + _dot_nt(rwh_ref[...], x_lo) + _dot_nt(rwl_ref[...], x_hi)) + rb_ref[...]
            for x_hi, x_lo in zip(x_his, x_los)]
    row = lax.broadcasted_iota(jnp.int32, (E, T), 0).astype(F32)
    sels = [[] for _ in tiles]
    vals = [[] for _ in tiles]
    for _ in range(TOP_K):
        for s in range(len(tiles)):
            m = jnp.max(curs[s], axis=0, keepdims=True)
            first = jnp.min(jnp.where(curs[s] == m, row, float(E)), axis=0, keepdims=True)
            sel = row == first
            sels[s].append(sel)
            vals[s].append(m)
            curs[s] = jnp.where(sel, -jnp.inf, curs[s])
    for s in range(len(tiles)):
        es = [jnp.exp(vk - vals[s][0]) for vk in vals[s]]
        den = es[0] + es[1] + es[2] + es[3]
        tw_ref[s] = _pack_rows([e / den for e in es], SUBLANES)

        onehot = jnp.zeros((E, T), F32)
        for sel in sels[s]:
            onehot = onehot + jnp.where(sel, 1.0, 0.0)
        lrank = _dot(onehot.astype(BF16), tri_ref[...])
        cnt = jnp.sum(onehot, axis=1, keepdims=True)
        cnt = jnp.floor((cnt + (RUN_ALIGN - 1)) * (1.0 / RUN_ALIGN)) * RUN_ALIGN
        off = _dot(lowe_ref[...], jnp.broadcast_to(cnt, (E, LANES)).astype(BF16))
        base = jnp.concatenate([off] * (T // LANES), axis=1) + lrank
        lpos_ref[s] = _pack_rows(
            [jnp.sum(jnp.where(sel, base, 0.0), axis=0, keepdims=True) for sel in sels[s]],
            SUBLANES).astype(jnp.int32)
        cnt_ref[s] = cnt.astype(jnp.int32)


def _mix_route(yna, o_f, o_b, hg, x2d, wout_bf16, hgn, ffg, router_w, router_b):
    n = x2d.shape[0]
    T = TOK_TILE
    nt = n // T
    rw_t = router_w.T
    rwh = rw_t.astype(BF16)
    rwl = (rw_t - rwh.astype(F32)).astype(BF16)
    router_b = router_b.reshape(N_EXPERTS, 1)
    tri = jnp.asarray(np.triu(np.ones((T, T), np.float32), 1), BF16)
    lowe = jnp.asarray(np.tril(np.ones((N_EXPERTS, N_EXPERTS), np.float32), -1), BF16)
    row = lambda i: (i, 0)
    const = lambda i: (0, 0)
    tile3 = lambda i: (i, 0, 0)
    K = MIX_TILES_PER_STEP
    R = K * T
    return pl.pallas_call(
        _mix_route_body,
        grid=(nt // K,),
        in_specs=[
            pl.BlockSpec((R, NA_WIDTH), row),
            pl.BlockSpec((R, HG_WIDTH), row),
            pl.BlockSpec((R, HG_WIDTH), row),
            pl.BlockSpec((R, HG_WIDTH), lambda i: (i, 2)),
            pl.BlockSpec((R, D_MODEL), row),
            pl.BlockSpec((D_MODEL, D_MODEL), const),
            pl.BlockSpec((1, HG_WIDTH), const),
            pl.BlockSpec((1, D_MODEL), const),
            pl.BlockSpec((N_EXPERTS, D_MODEL), const),
            pl.BlockSpec((N_EXPERTS, D_MODEL), const),
            pl.BlockSpec((N_EXPERTS, 1), const),
            pl.BlockSpec((T, T), const),
            pl.BlockSpec((N_EXPERTS, N_EXPERTS), const),
        ],
        out_specs=[
            pl.BlockSpec((R, D_MODEL), row),
            pl.BlockSpec((R, D_MODEL), row),
            pl.BlockSpec((K, SUBLANES, T), tile3),
            pl.BlockSpec((K, SUBLANES, T), tile3),
            pl.BlockSpec((K, N_EXPERTS, 1), tile3),
        ],
        out_shape=[
            jax.ShapeDtypeStruct((n, D_MODEL), F32),
            jax.ShapeDtypeStruct((n, D_MODEL), BF16),
            jax.ShapeDtypeStruct((nt, SUBLANES, T), jnp.int32),
            jax.ShapeDtypeStruct((nt, SUBLANES, T), F32),
            jax.ShapeDtypeStruct((nt, N_EXPERTS, 1), jnp.int32),
        ],
        compiler_params=_params("arbitrary"),
        name="mix_route",
    )(yna, o_f, o_b, hg, x2d, wout_bf16, hgn, ffg, rwh, rwl, router_b, tri, lowe)


TILES_PER_STEP = 4


def _wait_rows(make_copy, rows):
    make_copy(0, 0, pl.multiple_of(rows, RUN_ALIGN)).wait()


def _copy_runs(i, cnt_ref, off_ref, goff_ref, make_copy):
    def per_expert(e, carry):
        n = pl.multiple_of(cnt_ref[i * N_EXPERTS + e], RUN_ALIGN)
        lo = pl.multiple_of(off_ref[i * N_EXPERTS + e], RUN_ALIGN)
        go = pl.multiple_of(goff_ref[i * N_EXPERTS + e], RUN_ALIGN)

        @pl.when(n > 0)
        def _start():
            make_copy(lo, go, n).start()

        return carry

    lax.fori_loop(0, N_EXPERTS, per_expert, 0)


def _slot_matrix(lpos, weights=None):
    row = lax.broadcasted_iota(jnp.int32, (STAGE_ROWS, TOK_TILE), 0)
    acc = jnp.zeros((STAGE_ROWS, TOK_TILE), F32)
    for k in range(TOP_K):
        hit = lpos[k:k + 1, :] == row
        val = 1.0 if weights is None else weights[k:k + 1, :]
        acc = jnp.where(hit, val, acc)
    return acc


def _dispatch_body(cnt_ref, off_ref, goff_ref, tot_ref, zoff_ref, zlen_ref, blkrange_ref,
                   xn_ref, lpos_ref, xbuf_ref, stage_ref, zero_ref, sems, zsem):
    i = pl.program_id(0)
    last = pl.num_programs(0) - 1

    def zero_fill(act):
        def tail(e, carry):
            n = pl.multiple_of(zlen_ref[e], RUN_ALIGN)

            @pl.when(n > 0)
            def _():
                go = pl.multiple_of(zoff_ref[e], RUN_ALIGN)
                act(pltpu.make_async_copy(zero_ref.at[pl.ds(0, n)], xbuf_ref.at[pl.ds(go, n)], zsem))

            return carry

        def unused(j, carry):
            go = pl.multiple_of(j * EXPERT_BLOCK, EXPERT_BLOCK)
            act(pltpu.make_async_copy(zero_ref, xbuf_ref.at[pl.ds(go, EXPERT_BLOCK)], zsem))
            return carry

        lax.fori_loop(0, N_EXPERTS, tail, 0)
        lax.fori_loop(blkrange_ref[0], blkrange_ref[1], unused, 0)

    @pl.when(i == 0)
    def _zero_start():
        zero_ref[...] = jnp.zeros_like(zero_ref)
        zero_fill(lambda c: c.start())

    for s in range(TILES_PER_STEP):
        t = i * TILES_PER_STEP + s

        def make_copy(lo, go, size, s=s):
            return pltpu.make_async_copy(stage_ref.at[s, pl.ds(lo, size)], xbuf_ref.at[pl.ds(go, size)], sems.at[s])

        @pl.when(i > 0)
        def _drain():
            _wait_rows(make_copy, tot_ref[t - TILES_PER_STEP])

        rows = slice(s * TOK_TILE, (s + 1) * TOK_TILE)
        perm = _slot_matrix(lpos_ref[s]).astype(BF16)
        stage_ref[s] = _dot(perm, xn_ref[rows, :])
        _copy_runs(t, cnt_ref, off_ref, goff_ref, make_copy)

        @pl.when(i == last)
        def _finish():
            _wait_rows(make_copy, tot_ref[t])

    @pl.when(i == last)
    def _zero_finish():
        zero_fill(lambda c: c.wait())


def _dispatch(cnt, off, goff, tot, zoff, zlen, blkrange, xn, lpos, n_rows):
    n = xn.shape[0]
    rows = TILES_PER_STEP * TOK_TILE
    return pl.pallas_call(
        _dispatch_body,
        grid_spec=pltpu.PrefetchScalarGridSpec(
            num_scalar_prefetch=7,
            grid=(n // rows,),
            in_specs=[
                pl.BlockSpec((rows, D_MODEL), lambda i, *_: (i, 0)),
                pl.BlockSpec((TILES_PER_STEP, SUBLANES, TOK_TILE), lambda i, *_: (i, 0, 0)),
            ],
            out_specs=pl.BlockSpec(memory_space=pl.ANY),
            scratch_shapes=[
                pltpu.VMEM((TILES_PER_STEP, STAGE_ROWS, D_MODEL), F32),
                pltpu.VMEM((EXPERT_BLOCK, D_MODEL), F32),
                pltpu.SemaphoreType.DMA((TILES_PER_STEP,)),
                pltpu.SemaphoreType.DMA(()),
            ],
        ),
        out_shape=jax.ShapeDtypeStruct((n_rows, D_MODEL), F32),
        compiler_params=_params("arbitrary"),
        name="moe_dispatch",
    )(cnt, off, goff, tot, zoff, zlen, blkrange, xn, lpos)


CAST_ROWS = 128
EXPERT_ROW_STEP = 128


def _experts_body(blk_e_ref, first_ref, slot_ref, next_e_ref, parts_ref, nused_ref,
                  x_ref, bgu_ref, bd_ref, wgu_hbm, wd_hbm, y_ref,
                  wgu_f32, wd_f32, wgu_bf, wd_bf, sem_gu, sem_d):
    j = pl.program_id(0)

    def weight_copies(e, slot):
        return (pltpu.make_async_copy(wgu_hbm.at[e], wgu_f32.at[slot], sem_gu.at[slot]),
                pltpu.make_async_copy(wd_hbm.at[e], wd_f32.at[slot], sem_d.at[slot]))

    @pl.when(j < nused_ref[0])
    def _():
        e = blk_e_ref[j]
        slot = slot_ref[j]

        @pl.when(first_ref[j] == 1)
        def _new_expert():
            @pl.when(j == 0)
            def _():
                for c in weight_copies(e, slot):
                    c.start()

            for c in weight_copies(e, slot):
                c.wait()
            nxt = next_e_ref[j]

            @pl.when(nxt >= 0)
            def _():
                for c in weight_copies(nxt, 1 - slot):
                    c.start(priority=1)

            def cast(src, dst):
                def rows(r, carry):
                    sl = pl.ds(pl.multiple_of(r * CAST_ROWS, CAST_ROWS), CAST_ROWS)
                    dst[sl, :] = src[slot, sl, :].astype(BF16)
                    return carry
                lax.fori_loop(0, src.shape[1] // CAST_ROWS, rows, 0)

            cast(wgu_f32, wgu_bf)
            cast(wd_f32, wd_bf)

        def ffn(rows):
            x = x_ref[0:rows, :].astype(BF16)
            gu = _dot(x, wgu_bf[...]) + bgu_ref[0]
            gate = jnp.minimum(gu[:, :D_FF], SWIGLU_LIMIT)
            up = jnp.clip(gu[:, D_FF:], -SWIGLU_LIMIT, SWIGLU_LIMIT)
            act = (up + 1.0) * gate * jax.nn.sigmoid(SWIGLU_ALPHA * gate)
            y_ref[0:rows, :] = _dot(act.astype(BF16), wd_bf[...]) + bd_ref[0]
            if rows < EXPERT_BLOCK:
                y_ref[rows:, :] = jnp.zeros((EXPERT_BLOCK - rows, D_MODEL), F32)

        for parts in range(1, EXPERT_BLOCK // EXPERT_ROW_STEP + 1):
            pl.when(parts_ref[j] == parts)(functools.partial(ffn, parts * EXPERT_ROW_STEP))


def _experts(blk_e, first, slot, next_e, parts, nused, xbuf, wgu, bgu, wd, bd):
    n_rows = xbuf.shape[0]
    nblk = n_rows // EXPERT_BLOCK

    def rowblk(j, be, fi, sl, ne, pa, nu):
        return (jnp.minimum(j, nu[0] - 1), 0)

    def expert(j, be, fi, sl, ne, pa, nu):
        return (be[jnp.minimum(j, nu[0] - 1)], 0, 0)

    return pl.pallas_call(
        _experts_body,
        grid_spec=pltpu.PrefetchScalarGridSpec(
            num_scalar_prefetch=6,
            grid=(nblk,),
            in_specs=[
                pl.BlockSpec((EXPERT_BLOCK, D_MODEL), rowblk),
                pl.BlockSpec((1, 1, 2 * D_FF), expert),
                pl.BlockSpec((1, 1, D_MODEL), expert),
                pl.BlockSpec(memory_space=pl.ANY),
                pl.BlockSpec(memory_space=pl.ANY),
            ],
            out_specs=pl.BlockSpec((EXPERT_BLOCK, D_MODEL), rowblk),
            scratch_shapes=[
                pltpu.VMEM((2, D_MODEL, 2 * D_FF), F32),
                pltpu.VMEM((2, D_FF, D_MODEL), F32),
                pltpu.VMEM((D_MODEL, 2 * D_FF), BF16),
                pltpu.VMEM((D_FF, D_MODEL), BF16),
                pltpu.SemaphoreType.DMA((2,)),
                pltpu.SemaphoreType.DMA((2,)),
            ],
        ),
        out_shape=jax.ShapeDtypeStruct((n_rows, D_MODEL), F32),
        input_output_aliases={6: 0},
        compiler_params=_params("arbitrary"),
        name="moe_experts",
    )(blk_e, first, slot, next_e, parts, nused, xbuf, bgu, bd, wgu, wd)


def _combine_body(cnt_ref, off_ref, goff_ref, tot_ref, ybuf_ref, lpos_ref, tw_ref, h1_ref, fg_ref, out_ref,
                  stage_ref, sems):
    i = pl.program_id(0)
    n_tiles = pl.num_programs(0) * TILES_PER_STEP

    def copier(s):
        def make_copy(lo, go, size):
            return pltpu.make_async_copy(ybuf_ref.at[pl.ds(go, size)], stage_ref.at[s, pl.ds(lo, size)], sems.at[s])
        return make_copy

    ahead = TILES_PER_STEP - 1

    @pl.when(i == 0)
    def _init():
        stage_ref[...] = jnp.zeros_like(stage_ref)
        for s in range(ahead):
            _copy_runs(s, cnt_ref, off_ref, goff_ref, copier(s))

    for s in range(TILES_PER_STEP):
        t = i * TILES_PER_STEP + s

        @pl.when(t + ahead < n_tiles)
        def _prefetch():
            _copy_runs(t + ahead, cnt_ref, off_ref, goff_ref, copier((s + ahead) % TILES_PER_STEP))

        rows = slice(s * TOK_TILE, (s + 1) * TOK_TILE)
        w = _slot_matrix(lpos_ref[s], tw_ref[s]).astype(BF16)
        _wait_rows(copier(s), tot_ref[t])
        h2 = h1_ref[rows, :] + _dot_tn(w, stage_ref[s].astype(BF16))
        ms = jnp.mean(h2 * h2, axis=-1, keepdims=True)
        out_ref[rows, :] = h2 * lax.rsqrt(ms + RMS_EPS) * fg_ref[...]


def _combine(cnt, off, goff, tot, ybuf, lpos, tw, h1, final_g):
    n = h1.shape[0]
    rows = TILES_PER_STEP * TOK_TILE
    row = lambda i, *_: (i, 0)
    return pl.pallas_call(
        _combine_body,
        grid_spec=pltpu.PrefetchScalarGridSpec(
            num_scalar_prefetch=4,
            grid=(n // rows,),
            in_specs=[
                pl.BlockSpec(memory_space=pl.ANY),
                pl.BlockSpec((TILES_PER_STEP, SUBLANES, TOK_TILE), lambda i, *_: (i, 0, 0)),
                pl.BlockSpec((TILES_PER_STEP, SUBLANES, TOK_TILE), lambda i, *_: (i, 0, 0)),
                pl.BlockSpec((rows, D_MODEL), row),
                pl.BlockSpec((1, D_MODEL), lambda i, *_: (0, 0)),
            ],
            out_specs=pl.BlockSpec((rows, D_MODEL), row),
            scratch_shapes=[pltpu.VMEM((TILES_PER_STEP, STAGE_ROWS, D_MODEL), F32),
                            pltpu.SemaphoreType.DMA((TILES_PER_STEP,))],
        ),
        out_shape=jax.ShapeDtypeStruct((n, D_MODEL), F32),
        compiler_params=_params("arbitrary"),
        name="moe_combine",
    )(cnt, off, goff, tot, ybuf, lpos, tw, h1, final_g)


def _moe_layout(cnt_tiles):
    nt = cnt_tiles.shape[0]
    total = jnp.sum(cnt_tiles, axis=0)
    nblk_e = (total + EXPERT_BLOCK - 1) // EXPERT_BLOCK
    blk_end = jnp.cumsum(nblk_e)
    pstart = (blk_end - nblk_e) * EXPERT_BLOCK
    prefix = jnp.cumsum(cnt_tiles, axis=0) - cnt_tiles
    goff = pstart[None, :] + prefix
    off = jnp.cumsum(cnt_tiles, axis=1) - cnt_tiles
    tot = jnp.sum(cnt_tiles, axis=1)
    max_rows = nt * (TOK_TILE * TOP_K + N_EXPERTS * (RUN_ALIGN - 1)) + N_EXPERTS * (EXPERT_BLOCK - 1)
    max_blocks = (max_rows + EXPERT_BLOCK - 1) // EXPERT_BLOCK
    blk_id = jnp.arange(max_blocks, dtype=blk_end.dtype)
    blk_e = jnp.minimum(jnp.sum(blk_end[None, :] <= blk_id[:, None], axis=1), N_EXPERTS - 1).astype(jnp.int32)
    nused = blk_end[-1:].astype(jnp.int32)
    flat = lambda a: a.reshape(-1).astype(jnp.int32)
    zoff = pstart + total
    zlen = nblk_e * EXPERT_BLOCK - total
    blkrange = jnp.stack([nused[0], jnp.int32(max_blocks)])
    used = blk_id < nused[0]
    first = used & jnp.concatenate([jnp.ones((1,), bool), blk_e[1:] != blk_e[:-1]])
    slot = (jnp.cumsum(first.astype(jnp.int32)) - 1) % 2
    eid = jnp.arange(N_EXPERTS)
    later = (eid[None, :] > eid[:, None]) & (nblk_e > 0)[None, :]
    next_of = jnp.min(jnp.where(later, eid[None, :], N_EXPERTS), axis=1)
    next_of = jnp.where(next_of < N_EXPERTS, next_of, -1)
    is_e = blk_e[:, None] == eid[None, :]
    pick = lambda per_expert: jnp.sum(jnp.where(is_e, per_expert[None, :], 0), axis=1)
    rows_left = jnp.clip(pick(total) - (blk_id - pick(blk_end - nblk_e)) * EXPERT_BLOCK, 0, EXPERT_BLOCK)
    parts = (rows_left + EXPERT_ROW_STEP - 1) // EXPERT_ROW_STEP
    experts_plan = (blk_e, first.astype(jnp.int32), slot.astype(jnp.int32), pick(next_of).astype(jnp.int32),
                    parts.astype(jnp.int32), nused)
    dispatch_plan = (flat(cnt_tiles), flat(off), flat(goff), flat(tot))
    zero_plan = (flat(zoff), flat(zlen), blkrange.astype(jnp.int32))
    return dispatch_plan, zero_plan, experts_plan, max_blocks * EXPERT_BLOCK


def kernel(x, meta_tokens, attn_norm_g, w_in, na_rpb, na_norm_g, hgrn_lb_logits, hgrn_norm_g, w_out, ffn_norm_g,
           router_w, router_b, expert_w_gu, expert_b_gu, expert_w_down, expert_b_down, final_norm_g):
    B, T, D = x.shape
    x2d = x.reshape(B * T, D)
    w_in_b = w_in[0].astype(BF16)
    ng = attn_norm_g[0].reshape(1, D)
    lbl = hgrn_lb_logits[:, :, :].reshape(4, HG_WIDTH)

    att, hga, hgg = _inproj(x2d, ng, w_in_b, lbl, 512)
    att_m, hga_m, hgg_m = _inproj(meta_tokens.astype(F32), ng, w_in_b, lbl, N_META)
    att_m = jnp.pad(att_m, ((0, META_PAD - N_META), (0, 0)))
    hga_m = jnp.pad(hga_m, ((0, HG_CHUNK - N_META), (0, 0)))
    hgg_m = jnp.pad(hgg_m, ((0, HG_CHUNK - N_META), (0, 0)))

    yna = _natten(att, att_m, _natten_bias_tables(na_rpb[0]), na_norm_g[0].reshape(1, NA_WIDTH), B, T)
    o_f, o_b = _hgrn(hga, hgg, hga_m, hgg_m, B, T)

    h1, xn, lpos, tw, cnt_tiles = _mix_route(
        yna.reshape(B * T, NA_WIDTH), o_f.reshape(B * T, HG_WIDTH), o_b.reshape(B * T, HG_WIDTH), hga, x2d,
        w_out[0].astype(BF16), hgrn_norm_g[0].reshape(1, HG_WIDTH), ffn_norm_g[0].reshape(1, D),
        router_w[0], router_b[0].reshape(1, N_EXPERTS))

    dispatch_plan, zero_plan, experts_plan, n_rows = _moe_layout(cnt_tiles[:, :, 0])
    xbuf = _dispatch(*dispatch_plan, *zero_plan, xn, lpos, n_rows)
    ybuf = _experts(*experts_plan, xbuf, expert_w_gu[0], expert_b_gu[0][:, None, :],
                    expert_w_down[0], expert_b_down[0][:, None, :])
    out = _combine(*dispatch_plan, ybuf, lpos, tw, h1, final_norm_g.reshape(1, D))
    return out.reshape(B, T, D)
```

```python
import functools

import numpy as np
import jax
import jax.numpy as jnp
from jax import lax
from jax.experimental import pallas as pl
from jax.experimental.pallas import tpu as pltpu

F32 = jnp.float32
BF16 = jnp.bfloat16

D_MODEL = 1024
N_META = 16
GRID_W = 64
NA_WIDTH = 512
NA_HEAD_DIM = 64
NA_HEADS = 8
NA_KH = 8
NA_KW = 16
HG_WIDTH = 512
HG_HEAD_DIM = 128
HG_HEADS = 4
HG_CHUNK = 64
IN_COLS = 3 * NA_WIDTH + 5 * HG_WIDTH
N_EXPERTS = 32
TOP_K = 4
D_FF = 1024
SWIGLU_LIMIT = 7.0
SWIGLU_ALPHA = 1.702
RMS_EPS = 1e-6
NEG_BIG = -1e30
LOG2_E = 1.4426950408889634

LANES = 128
VMEM_LIMIT_BYTES = 56 * 1024 * 1024

TOK_TILE = 256
EXPERT_BLOCK = 512
SUBLANES = 8
RUN_ALIGN = SUBLANES
STAGE_ROWS = TOK_TILE * TOP_K + N_EXPERTS * RUN_ALIGN
N_HG_LEVELS = 6


def _dot(a, b):
    return jnp.dot(a, b, preferred_element_type=F32)


def _dot_nt(a, b):
    return lax.dot_general(a, b, (((1,), (1,)), ((), ())), preferred_element_type=F32)


def _dot_tn(a, b):
    return lax.dot_general(a, b, (((0,), (0,)), ((), ())), preferred_element_type=F32)


def _split_bf16(x):
    hi = x.astype(BF16)
    lo = (x - hi.astype(F32)).astype(BF16)
    return hi, lo


def _pack_rows(rows, n_rows):
    idx = lax.broadcasted_iota(jnp.int32, (n_rows, rows[0].shape[1]), 0)
    out = jnp.zeros(idx.shape, rows[0].dtype)
    for k, r in enumerate(rows):
        out = jnp.where(idx == k, r, out)
    return out


def _params(*sem):
    return pltpu.CompilerParams(dimension_semantics=sem, vmem_limit_bytes=VMEM_LIMIT_BYTES)


def _inproj_body(x_ref, ng_ref, w_ref, lbl_ref, att_ref, hga_ref, hgg_ref):
    x = x_ref[...]
    ms = jnp.mean(x * x, axis=-1, keepdims=True)
    n = (x * lax.rsqrt(ms + RMS_EPS) * ng_ref[...]).astype(BF16)

    def proj(lo, hi):
        return _dot(n, w_ref[:, lo:hi])

    pq = proj(0, NA_WIDTH)
    att_ref[:, 0:NA_WIDTH] = (pq * (NA_HEAD_DIM ** -0.5 * LOG2_E)).astype(BF16)
    att_ref[:, NA_WIDTH:3 * NA_WIDTH] = proj(NA_WIDTH, 3 * NA_WIDTH).astype(BF16)

    base = 3 * NA_WIDTH
    W = HG_WIDTH
    qh = proj(base, base + W)
    hga_ref[:, 0:W] = (qh * jax.nn.sigmoid(qh)).astype(BF16)
    hga_ref[:, W:2 * W] = proj(base + W, base + 2 * W).astype(BF16)
    lbl = lbl_ref[...]
    for d in range(2):
        a0 = lbl[2 * d:2 * d + 1, :]
        a1 = lbl[2 * d + 1:2 * d + 2, :]
        m = jnp.maximum(a0, a1)
        e0 = jnp.exp(a0 - m)
        e1 = jnp.exp(a1 - m)
        lb = e0 / (e0 + e1)
        raw = proj(base + (2 + d) * W, base + (3 + d) * W)
        f = lb + (1.0 - lb) * jax.nn.sigmoid(raw)
        hgg_ref[:, d * W:(d + 1) * W] = jnp.log(f)
    gt = proj(base + 4 * W, base + 5 * W)
    hga_ref[:, 2 * W:3 * W] = (gt * jax.nn.sigmoid(gt)).astype(BF16)


def _inproj(x2d, norm_g, w_bf16, lb_logits4, tm):
    n = x2d.shape[0]
    return pl.pallas_call(
        _inproj_body,
        grid=(n // tm,),
        in_specs=[
            pl.BlockSpec((tm, D_MODEL), lambda i: (i, 0)),
            pl.BlockSpec((1, D_MODEL), lambda i: (0, 0)),
            pl.BlockSpec((D_MODEL, IN_COLS), lambda i: (0, 0)),
            pl.BlockSpec((4, HG_WIDTH), lambda i: (0, 0)),
        ],
        out_specs=[
            pl.BlockSpec((tm, 3 * NA_WIDTH), lambda i: (i, 0)),
            pl.BlockSpec((tm, 3 * HG_WIDTH), lambda i: (i, 0)),
            pl.BlockSpec((tm, 2 * HG_WIDTH), lambda i: (i, 0)),
        ],
        out_shape=[
            jax.ShapeDtypeStruct((n, 3 * NA_WIDTH), BF16),
            jax.ShapeDtypeStruct((n, 3 * HG_WIDTH), BF16),
            jax.ShapeDtypeStruct((n, 2 * HG_WIDTH), F32),
        ],
        compiler_params=_params("arbitrary"),
        name="inproj",
    )(x2d, norm_g, w_bf16, lb_logits4)


N_WIN_KEYS = NA_KH * GRID_W
META_PAD = LANES


NA_GROUP = 4
GROUP_LANES = NA_GROUP * NA_HEAD_DIM


NA_ROWS_PER_STEP = 16


def _natten_row(rr, r, q_ref, k_ref, v_ref, km_ref, vm_ref, bias_ref, *, rows):
    rs = jnp.clip(r - NA_KH // 2, 0, rows - NA_KH)
    d0 = rs - r + (NA_KH - 1)
    start = pl.multiple_of(rs * GRID_W, GRID_W)
    lane_head = lax.broadcasted_iota(jnp.int32, (GRID_W, GROUP_LANES), 1) // NA_HEAD_DIM
    meta_col = lax.broadcasted_iota(jnp.int32, (1, META_PAD), 1)
    meta_bias = jnp.where(meta_col < N_META, 0.0, NEG_BIG).astype(F32)
    folded = []
    for g in range(NA_HEADS // NA_GROUP):
        sl = slice(g * GROUP_LANES, (g + 1) * GROUP_LANES)
        q4 = q_ref[0, rr * GRID_W:(rr + 1) * GRID_W, sl]
        zero = jnp.zeros_like(q4)
        qm = jnp.concatenate([jnp.where(lane_head == h, q4, zero) for h in range(NA_GROUP)], axis=0)
        kw = k_ref[0, pl.ds(start, N_WIN_KEYS), sl]
        vw = v_ref[0, pl.ds(start, N_WIN_KEYS), sl]
        hs = slice(g * NA_GROUP, (g + 1) * NA_GROUP)
        bias = jnp.concatenate(
            [bias_ref[d0 + 2 * j, hs].reshape(NA_GROUP * GRID_W, 2 * GRID_W) for j in range(NA_KH // 2)], axis=-1)
        s = _dot_nt(qm, kw) + bias
        sm = _dot_nt(qm, km_ref[:, sl]) + meta_bias
        m = jnp.maximum(jnp.max(s, axis=-1, keepdims=True), jnp.max(sm, axis=-1, keepdims=True))
        e = jnp.exp2(s - m)
        em = jnp.exp2(sm - m)
        den = jnp.sum(e, axis=-1, keepdims=True) + jnp.sum(em, axis=-1, keepdims=True)
        o = (_dot(e.astype(BF16), vw) + _dot(em.astype(BF16), vm_ref[:, sl])) * (1.0 / den)
        acc = jnp.where(lane_head == 0, o[0:GRID_W], 0.0)
        for h in range(1, NA_GROUP):
            acc = jnp.where(lane_head == h, o[h * GRID_W:(h + 1) * GRID_W], acc)
        folded.append(acc)
    return jnp.concatenate(folded, axis=-1)


def _natten_body(q_ref, k_ref, v_ref, km_ref, vm_ref, bias_ref, ng_ref, bd_ref, o_ref, *, rows):
    r0 = pl.program_id(1) * NA_ROWS_PER_STEP
    o2 = jnp.concatenate(
        [_natten_row(rr, r0 + rr, q_ref, k_ref, v_ref, km_ref, vm_ref, bias_ref, rows=rows)
         for rr in range(NA_ROWS_PER_STEP)], axis=0)
    sq_hi, sq_lo = _split_bf16(o2 * o2)
    ms = (_dot(sq_hi, bd_ref[...]) + _dot(sq_lo, bd_ref[...])) * (1.0 / NA_HEAD_DIM)
    o_ref[0] = (o2 * lax.rsqrt(ms + RMS_EPS) * ng_ref[...]).astype(o_ref.dtype)


def _natten(att, att_meta, bias_tabs, norm_g, batch, seq):
    rows = seq // GRID_W
    att3 = att.reshape(batch, seq, 3 * NA_WIDTH)
    head_of = np.arange(NA_WIDTH) // NA_HEAD_DIM
    same_head = jnp.asarray(head_of[:, None] == head_of[None, :], BF16)

    qrows = NA_ROWS_PER_STEP * GRID_W
    return pl.pallas_call(
        functools.partial(_natten_body, rows=rows),
        grid=(batch, rows // NA_ROWS_PER_STEP),
        in_specs=[
            pl.BlockSpec((1, qrows, NA_WIDTH), lambda b, r: (b, r, 0)),
            pl.BlockSpec((1, seq, NA_WIDTH), lambda b, r: (b, 0, 1)),
            pl.BlockSpec((1, seq, NA_WIDTH), lambda b, r: (b, 0, 2)),
            pl.BlockSpec((META_PAD, NA_WIDTH), lambda b, r: (0, 1)),
            pl.BlockSpec((META_PAD, NA_WIDTH), lambda b, r: (0, 2)),
            pl.BlockSpec(bias_tabs.shape, lambda b, r: (0, 0, 0, 0), pipeline_mode=pl.Buffered(1)),
            pl.BlockSpec((1, NA_WIDTH), lambda b, r: (0, 0)),
            pl.BlockSpec((NA_WIDTH, NA_WIDTH), lambda b, r: (0, 0)),
        ],
        out_specs=pl.BlockSpec((1, qrows, NA_WIDTH), lambda b, r: (b, r, 0)),
        out_shape=jax.ShapeDtypeStruct((batch, seq, NA_WIDTH), BF16),
        compiler_params=_params("arbitrary", "arbitrary"),
        name="natten",
    )(att3, att3, att3, att_meta, att_meta, bias_tabs, norm_g, same_head)


def _natten_bias_tables(rpb):
    c = np.arange(GRID_W)[:, None]
    kc = np.arange(GRID_W)[None, :]
    cs = np.clip(c - NA_KW // 2, 0, GRID_W - NA_KW)
    valid = (kc >= cs) & (kc < cs + NA_KW)
    dc = kc - c + (NA_KW - 1)
    pick = np.asarray(dc[None] == np.arange(2 * NA_KW - 1)[:, None, None], np.float32)
    t1 = jnp.einsum("hdj,jck->hdck", rpb.astype(F32) * LOG2_E, pick, precision=lax.Precision.HIGHEST)
    t1 = jnp.where(valid[None, None], t1, NEG_BIG)
    pairs = jnp.concatenate([t1[:, :-1], t1[:, 1:]], axis=-1)
    return pairs.transpose(1, 0, 2, 3)


HG_CHUNKS_PER_STEP = 16
HG_CHUNKS_PER_GROUP = 1
ROW_B, ROW_LV = 0, 1
N_EXP_BLOCKS = 1 + N_HG_LEVELS


def _hgrn_constants():
    C = HG_CHUNK
    t = np.arange(C)
    u = t[None, :]
    mats = [u <= t[:, None]]
    masks = [np.eye(C, dtype=bool)]
    for lv in range(N_HG_LEVELS):
        m = 1 << lv
        blk = t // (2 * m)
        upper = (t // m) % 2 == 1
        p = blk * 2 * m + m - 1
        q_rows = upper[:, None] & (u > p[:, None]) & (u <= t[:, None])
        k_rows = (~upper)[:, None] & (u > t[:, None]) & (u <= p[:, None])
        mats.append(q_rows | k_rows)
        masks.append((blk[:, None] == blk[None, :]) & upper[:, None] & (~upper)[None, :])
    fwd = np.concatenate(mats, axis=0).astype(np.float32)
    fwd_mask = np.stack(masks).astype(np.float32)
    blocks = fwd.reshape(N_EXP_BLOCKS, C, C)
    bwd = blocks[:, ::-1, ::-1].reshape(N_EXP_BLOCKS * C, C)
    bwd_mask = fwd_mask[:, ::-1, ::-1]
    both = np.stack([fwd, bwd])
    both = np.concatenate([both, both], axis=-1)
    both_mask = np.stack([fwd_mask, bwd_mask])
    both_mask = np.concatenate([both_mask, both_mask], axis=-1)
    return (jnp.asarray(both, BF16), jnp.asarray(both_mask, F32))


def _hgrn_exponents(mat, g):
    g_hi, g_lo = _split_bf16(g)
    return _dot(mat, jnp.concatenate([g_hi, g_lo], axis=0))


def _hgrn_init(meta_a_ref, meta_g_ref, mat_ref, stf_ref, stb_ref):
    C, W = HG_CHUNK, HG_WIDTH
    v = meta_a_ref[:, W:2 * W]
    prefix = mat_ref[0, ROW_B * C:(ROW_B + 1) * C, :]
    for d, st_ref in ((0, stf_ref), (1, stb_ref)):
        g = meta_g_ref[:, d * W:(d + 1) * W]
        b = _hgrn_exponents(prefix, g)
        kb = ((1.0 - jnp.exp(g)) * jnp.exp(b[C - 1:C] - b)).astype(BF16)
        for h in range(HG_HEADS):
            sl = slice(h * HG_HEAD_DIM, (h + 1) * HG_HEAD_DIM)
            st_ref[h] = _dot_tn(v[:, sl], kb[:, sl])


def _hgrn_group_tasks(j0):
    js = range(j0, j0 + HG_CHUNKS_PER_GROUP)
    return ([(0, j * HG_CHUNK) for j in js] +
            [(1, (HG_CHUNKS_PER_STEP - 1 - j) * HG_CHUNK) for j in js])


def _hgrn_tasks(tasks, dir_refs, states, mat_ref, mask_ref):
    C, W = HG_CHUNK, HG_WIDTH
    heads = [slice(h * HG_HEAD_DIM, (h + 1) * HG_HEAD_DIM) for h in range(HG_HEADS)]
    qs, ks, vs, ex_bs, ex_lvs, ex_ends = {}, {}, {}, {}, {}, {}
    for d, r0 in tasks:
        q_ref, v_ref, g_ref, total_row = dir_refs[d][0], dir_refs[d][1], dir_refs[d][2], dir_refs[d][3]
        g = g_ref[0, r0:r0 + C, :]
        sums = _hgrn_exponents(mat_ref[d], g)
        ex_bs[d, r0] = jnp.exp(sums[0:C])
        ex_lvs[d, r0] = jnp.exp(sums[C:]).astype(BF16)
        ex_ends[d, r0] = jnp.exp(sums[total_row:total_row + 1] - sums[0:C])
        qs[d, r0] = q_ref[0, r0:r0 + C, :]
        ks[d, r0] = 1.0 - jnp.exp(g)
        vs[d, r0] = v_ref[0, r0:r0 + C, :]

    D2 = 2 * HG_HEAD_DIM
    pairs = [slice(p * D2, (p + 1) * D2) for p in range(HG_HEADS // 2)]

    def block_diag(x):
        zero = jnp.zeros((C, HG_HEAD_DIM), x.dtype)
        return jnp.concatenate([jnp.concatenate([x[:, :HG_HEAD_DIM], zero], axis=1),
                                jnp.concatenate([zero, x[:, HG_HEAD_DIM:]], axis=1)], axis=0)

    prods = {}
    for task in tasks:
        for p, sl in enumerate(pairs):
            q2, k2 = qs[task][:, sl], ks[task][:, sl].astype(BF16)
            ps = [_dot_nt(q2, block_diag(k2))]
            for lv in range(N_HG_LEVELS):
                scale = ex_lvs[task][lv * C:(lv + 1) * C, sl]
                ps.append(_dot_nt(q2 * scale, block_diag(k2 * scale)))
            prods[task, p] = ps
    attn, qbs, kbs = {}, {}, {}
    for task in tasks:
        d = task[0]
        for p, sl in enumerate(pairs):
            a = prods[task, p][0] * mask_ref[d, 0]
            for lv in range(N_HG_LEVELS):
                a = a + prods[task, p][1 + lv] * mask_ref[d, 1 + lv]
            o2 = _dot(a.astype(BF16), block_diag(vs[task][:, sl]))
            attn[task, 2 * p] = o2[:, :HG_HEAD_DIM]
            attn[task, 2 * p + 1] = o2[:, HG_HEAD_DIM:]
        for h, sl in enumerate(heads):
            qbs[task, h] = (qs[task][:, sl].astype(F32) * ex_bs[task][:, sl]).astype(BF16)
            kbs[task, h] = _dot_tn(vs[task][:, sl], (ks[task][:, sl] * ex_ends[task][:, sl]).astype(BF16))
    for task in tasks:
        d, r0 = task
        total_row, o_ref = dir_refs[d][3], dir_refs[d][4]
        outs = []
        for h, sl in enumerate(heads):
            st = states[d, h]
            outs.append(attn[task, h] + _dot_nt(qbs[task, h], st.astype(BF16)))
            decay = ex_bs[task][total_row:total_row + 1, sl]
            states[d, h] = decay * st + kbs[task, h]
        o_ref[0, r0:r0 + C, :] = jnp.concatenate(outs, axis=-1).astype(o_ref.dtype)


def _hgrn_body(hqf_ref, hvf_ref, hgf_ref, hqb_ref, hvb_ref, hgb_ref, meta_a_ref, meta_g_ref, mat_ref, mask_ref,
               of_ref, ob_ref, stf_ref, stb_ref):
    @pl.when(pl.program_id(1) == 0)
    def _init():
        _hgrn_init(meta_a_ref, meta_g_ref, mat_ref, stf_ref, stb_ref)

    dir_refs = ((hqf_ref, hvf_ref, hgf_ref, HG_CHUNK - 1, of_ref), (hqb_ref, hvb_ref, hgb_ref, 0, ob_ref))
    st_refs = (stf_ref, stb_ref)
    states = {(d, h): st_refs[d][h] for d in range(2) for h in range(HG_HEADS)}
    for j0 in range(0, HG_CHUNKS_PER_STEP, HG_CHUNKS_PER_GROUP):
        _hgrn_tasks(_hgrn_group_tasks(j0), dir_refs, states, mat_ref, mask_ref)
    for d in range(2):
        for h in range(HG_HEADS):
            st_refs[d][h] = states[d, h]


def _hgrn(hga, hgg, hga_meta_pad, hgg_meta_pad, batch, seq):
    steps = seq // (HG_CHUNK * HG_CHUNKS_PER_STEP)
    a3 = hga.reshape(batch, seq, 3 * HG_WIDTH)
    g3 = hgg.reshape(batch, seq, 2 * HG_WIDTH)
    mats, masks = _hgrn_constants()
    hblk = (1, HG_CHUNK * HG_CHUNKS_PER_STEP, HG_WIDTH)

    def fwd(col):
        return pl.BlockSpec(hblk, lambda b, s: (b, s, col))

    def bwd(col):
        return pl.BlockSpec(hblk, lambda b, s: (b, steps - 1 - s, col))

    return pl.pallas_call(
        _hgrn_body,
        grid=(batch, steps),
        in_specs=[
            fwd(0), fwd(1), fwd(0),
            bwd(0), bwd(1), bwd(1),
            pl.BlockSpec((HG_CHUNK, 3 * HG_WIDTH), lambda b, s: (0, 0)),
            pl.BlockSpec((HG_CHUNK, 2 * HG_WIDTH), lambda b, s: (0, 0)),
            pl.BlockSpec(mats.shape, lambda b, s: (0, 0, 0)),
            pl.BlockSpec(masks.shape, lambda b, s: (0, 0, 0, 0)),
        ],
        out_specs=[fwd(0), bwd(0)],
        out_shape=[jax.ShapeDtypeStruct((batch, seq, HG_WIDTH), BF16)] * 2,
        scratch_shapes=[pltpu.VMEM((HG_HEADS, HG_HEAD_DIM, HG_HEAD_DIM), F32)] * 2,
        compiler_params=_params("arbitrary", "arbitrary"),
        name="hgrn",
    )(a3, a3, g3, a3, a3, g3, hga_meta_pad, hgg_meta_pad, mats, masks)


MIX_TILES_PER_STEP = 4


def _mix_route_body(yna_ref, of_ref, ob_ref, gate_ref, x_ref, wout_ref, hgn_ref, ffg_ref,
                    rwh_ref, rwl_ref, rb_ref, tri_ref, lowe_ref,
                    h1_ref, xn_ref, lpos_ref, tw_ref, cnt_ref):
    T, E = TOK_TILE, N_EXPERTS
    tiles = [slice(s * T, (s + 1) * T) for s in range(MIX_TILES_PER_STEP)]
    x_his, x_los = [], []
    for rows in tiles:
        o = of_ref[rows, :].astype(F32) + ob_ref[rows, :].astype(F32)
        parts = []
        for h in range(HG_HEADS):
            seg = o[:, h * HG_HEAD_DIM:(h + 1) * HG_HEAD_DIM]
            ms = jnp.mean(seg * seg, axis=-1, keepdims=True)
            parts.append(seg * lax.rsqrt(ms + RMS_EPS))
        yhg = jnp.concatenate(parts, axis=-1) * hgn_ref[...] * gate_ref[rows, :]
        mix = _dot(yna_ref[rows, :], wout_ref[0:NA_WIDTH, :]) + _dot(yhg.astype(BF16), wout_ref[NA_WIDTH:, :])
        h1 = x_ref[rows, :] + mix
        h1_ref[rows, :] = h1
        ms = jnp.mean(h1 * h1, axis=-1, keepdims=True)
        xn = h1 * lax.rsqrt(ms + RMS_EPS) * ffg_ref[...]
        x_hi, x_lo = _split_bf16(xn)
        xn_ref[rows, :] = x_hi
        x_his.append(x_hi)
        x_los.append(x_lo)

    curs = [(_dot_nt(rwh_ref[...], x_hi) + _dot_nt(rwh_ref[...], x_lo) + _dot_nt(rwl_ref[...], x_hi)) + rb_ref[...]
            for x_hi, x_lo in zip(x_his, x_los)]
    row = lax.broadcasted_iota(jnp.int32, (E, T), 0).astype(F32)
    sels = [[] for _ in tiles]
    vals = [[] for _ in tiles]
    for _ in range(TOP_K):
        for s in range(len(tiles)):
            m = jnp.max(curs[s], axis=0, keepdims=True)
            first = jnp.min(jnp.where(curs[s] == m, row, float(E)), axis=0, keepdims=True)
            sel = row == first
            sels[s].append(sel)
            vals[s].append(m)
            curs[s] = jnp.where(sel, -jnp.inf, curs[s])
    for s in range(len(tiles)):
        es = [jnp.exp(vk - vals[s][0]) for vk in vals[s]]
        den = es[0] + es[1] + es[2] + es[3]
        tw_ref[s] = _pack_rows([e / den for e in es], SUBLANES)

        onehot = jnp.zeros((E, T), F32)
        for sel in sels[s]:
            onehot = onehot + jnp.where(sel, 1.0, 0.0)
        lrank = _dot(onehot.astype(BF16), tri_ref[...])
        cnt = jnp.sum(onehot, axis=1, keepdims=True)
        cnt = jnp.floor((cnt + (RUN_ALIGN - 1)) * (1.0 / RUN_ALIGN)) * RUN_ALIGN
        off = _dot(lowe_ref[...], jnp.broadcast_to(cnt, (E, LANES)).astype(BF16))
        base = jnp.concatenate([off] * (T // LANES), axis=1) + lrank
        lpos_ref[s] = _pack_rows(
            [jnp.sum(jnp.where(sel, base, 0.0), axis=0, keepdims=True) for sel in sels[s]],
            SUBLANES).astype(jnp.int32)
        cnt_ref[s] = cnt.astype(jnp.int32)


def _mix_route(yna, o_f, o_b, hg, x2d, wout_bf16, hgn, ffg, router_w, router_b):
    n = x2d.shape[0]
    T = TOK_TILE
    nt = n // T
    rw_t = router_w.T
    rwh = rw_t.astype(BF16)
    rwl = (rw_t - rwh.astype(F32)).astype(BF16)
    router_b = router_b.reshape(N_EXPERTS, 1)
    tri = jnp.asarray(np.triu(np.ones((T, T), np.float32), 1), BF16)
    lowe = jnp.asarray(np.tril(np.ones((N_EXPERTS, N_EXPERTS), np.float32), -1), BF16)
    row = lambda i: (i, 0)
    const = lambda i: (0, 0)
    tile3 = lambda i: (i, 0, 0)
    K = MIX_TILES_PER_STEP
    R = K * T
    return pl.pallas_call(
        _mix_route_body,
        grid=(nt // K,),
        in_specs=[
            pl.BlockSpec((R, NA_WIDTH), row),
            pl.BlockSpec((R, HG_WIDTH), row),
            pl.BlockSpec((R, HG_WIDTH), row),
            pl.BlockSpec((R, HG_WIDTH), lambda i: (i, 2)),
            pl.BlockSpec((R, D_MODEL), row),
            pl.BlockSpec((D_MODEL, D_MODEL), const),
            pl.BlockSpec((1, HG_WIDTH), const),
            pl.BlockSpec((1, D_MODEL), const),
            pl.BlockSpec((N_EXPERTS, D_MODEL), const),
            pl.BlockSpec((N_EXPERTS, D_MODEL), const),
            pl.BlockSpec((N_EXPERTS, 1), const),
            pl.BlockSpec((T, T), const),
            pl.BlockSpec((N_EXPERTS, N_EXPERTS), const),
        ],
        out_specs=[
            pl.BlockSpec((R, D_MODEL), row),
            pl.BlockSpec((R, D_MODEL), row),
            pl.BlockSpec((K, SUBLANES, T), tile3),
            pl.BlockSpec((K, SUBLANES, T), tile3),
            pl.BlockSpec((K, N_EXPERTS, 1), tile3),
        ],
        out_shape=[
            jax.ShapeDtypeStruct((n, D_MODEL), F32),
            jax.ShapeDtypeStruct((n, D_MODEL), BF16),
            jax.ShapeDtypeStruct((nt, SUBLANES, T), jnp.int32),
            jax.ShapeDtypeStruct((nt, SUBLANES, T), F32),
            jax.ShapeDtypeStruct((nt, N_EXPERTS, 1), jnp.int32),
        ],
        compiler_params=_params("arbitrary"),
        name="mix_route",
    )(yna, o_f, o_b, hg, x2d, wout_bf16, hgn, ffg, rwh, rwl, router_b, tri, lowe)


TILES_PER_STEP = 4


def _wait_rows(make_copy, rows):
    make_copy(0, 0, pl.multiple_of(rows, RUN_ALIGN)).wait()


def _copy_runs(i, cnt_ref, off_ref, goff_ref, make_copy):
    def per_expert(e, carry):
        n = pl.multiple_of(cnt_ref[i * N_EXPERTS + e], RUN_ALIGN)
        lo = pl.multiple_of(off_ref[i * N_EXPERTS + e], RUN_ALIGN)
        go = pl.multiple_of(goff_ref[i * N_EXPERTS + e], RUN_ALIGN)

        @pl.when(n > 0)
        def _start():
            make_copy(lo, go, n).start()

        return carry

    lax.fori_loop(0, N_EXPERTS, per_expert, 0)


def _slot_matrix(lpos, weights=None):
    row = lax.broadcasted_iota(jnp.int32, (STAGE_ROWS, TOK_TILE), 0)
    acc = jnp.zeros((STAGE_ROWS, TOK_TILE), F32)
    for k in range(TOP_K):
        hit = lpos[k:k + 1, :] == row
        val = 1.0 if weights is None else weights[k:k + 1, :]
        acc = jnp.where(hit, val, acc)
    return acc


def _dispatch_body(cnt_ref, off_ref, goff_ref, tot_ref, zoff_ref, zlen_ref, blkrange_ref,
                   xn_ref, lpos_ref, xbuf_ref, stage_ref, zero_ref, sems, zsem):
    i = pl.program_id(0)
    last = pl.num_programs(0) - 1

    def zero_fill(act):
        def tail(e, carry):
            n = pl.multiple_of(zlen_ref[e], RUN_ALIGN)

            @pl.when(n > 0)
            def _():
                go = pl.multiple_of(zoff_ref[e], RUN_ALIGN)
                act(pltpu.make_async_copy(zero_ref.at[pl.ds(0, n)], xbuf_ref.at[pl.ds(go, n)], zsem))

            return carry

        def unused(j, carry):
            go = pl.multiple_of(j * EXPERT_BLOCK, EXPERT_BLOCK)
            act(pltpu.make_async_copy(zero_ref, xbuf_ref.at[pl.ds(go, EXPERT_BLOCK)], zsem))
            return carry

        lax.fori_loop(0, N_EXPERTS, tail, 0)
        lax.fori_loop(blkrange_ref[0], blkrange_ref[1], unused, 0)

    @pl.when(i == 0)
    def _zero_start():
        zero_ref[...] = jnp.zeros_like(zero_ref)
        zero_fill(lambda c: c.start())

    for s in range(TILES_PER_STEP):
        t = i * TILES_PER_STEP + s

        def make_copy(lo, go, size, s=s):
            return pltpu.make_async_copy(stage_ref.at[s, pl.ds(lo, size)], xbuf_ref.at[pl.ds(go, size)], sems.at[s])

        @pl.when(i > 0)
        def _drain():
            _wait_rows(make_copy, tot_ref[t - TILES_PER_STEP])

        rows = slice(s * TOK_TILE, (s + 1) * TOK_TILE)
        perm = _slot_matrix(lpos_ref[s]).astype(BF16)
        stage_ref[s] = _dot(perm, xn_ref[rows, :])
        _copy_runs(t, cnt_ref, off_ref, goff_ref, make_copy)

        @pl.when(i == last)
        def _finish():
            _wait_rows(make_copy, tot_ref[t])

    @pl.when(i == last)
    def _zero_finish():
        zero_fill(lambda c: c.wait())


def _dispatch(cnt, off, goff, tot, zoff, zlen, blkrange, xn, lpos, n_rows):
    n = xn.shape[0]
    rows = TILES_PER_STEP * TOK_TILE
    return pl.pallas_call(
        _dispatch_body,
        grid_spec=pltpu.PrefetchScalarGridSpec(
            num_scalar_prefetch=7,
            grid=(n // rows,),
            in_specs=[
                pl.BlockSpec((rows, D_MODEL), lambda i, *_: (i, 0)),
                pl.BlockSpec((TILES_PER_STEP, SUBLANES, TOK_TILE), lambda i, *_: (i, 0, 0)),
            ],
            out_specs=pl.BlockSpec(memory_space=pl.ANY),
            scratch_shapes=[
                pltpu.VMEM((TILES_PER_STEP, STAGE_ROWS, D_MODEL), F32),
                pltpu.VMEM((EXPERT_BLOCK, D_MODEL), F32),
                pltpu.SemaphoreType.DMA((TILES_PER_STEP,)),
                pltpu.SemaphoreType.DMA(()),
            ],
        ),
        out_shape=jax.ShapeDtypeStruct((n_rows, D_MODEL), F32),
        compiler_params=_params("arbitrary"),
        name="moe_dispatch",
    )(cnt, off, goff, tot, zoff, zlen, blkrange, xn, lpos)


CAST_ROWS = 128
EXPERT_ROW_STEP = 128


def _experts_body(blk_e_ref, first_ref, slot_ref, next_e_ref, parts_ref, nused_ref,
                  x_ref, bgu_ref, bd_ref, wgu_hbm, wd_hbm, y_ref,
                  wgu_f32, wd_f32, wgu_bf, wd_bf, sem_gu, sem_d):
    j = pl.program_id(0)

    def weight_copies(e, slot):
        return (pltpu.make_async_copy(wgu_hbm.at[e], wgu_f32.at[slot], sem_gu.at[slot]),
                pltpu.make_async_copy(wd_hbm.at[e], wd_f32.at[slot], sem_d.at[slot]))

    @pl.when(j < nused_ref[0])
    def _():
        e = blk_e_ref[j]
        slot = slot_ref[j]

        @pl.when(first_ref[j] == 1)
        def _new_expert():
            @pl.when(j == 0)
            def _():
                for c in weight_copies(e, slot):
                    c.start()

            for c in weight_copies(e, slot):
                c.wait()
            nxt = next_e_ref[j]

            @pl.when(nxt >= 0)
            def _():
                for c in weight_copies(nxt, 1 - slot):
                    c.start(priority=1)

            def cast(src, dst):
                def rows(r, carry):
                    sl = pl.ds(pl.multiple_of(r * CAST_ROWS, CAST_ROWS), CAST_ROWS)
                    dst[sl, :] = src[slot, sl, :].astype(BF16)
                    return carry
                lax.fori_loop(0, src.shape[1] // CAST_ROWS, rows, 0)

            cast(wgu_f32, wgu_bf)
            cast(wd_f32, wd_bf)

        def ffn(rows):
            x = x_ref[0:rows, :].astype(BF16)
            gu = _dot(x, wgu_bf[...]) + bgu_ref[0]
            gate = jnp.minimum(gu[:, :D_FF], SWIGLU_LIMIT)
            up = jnp.clip(gu[:, D_FF:], -SWIGLU_LIMIT, SWIGLU_LIMIT)
            act = (up + 1.0) * gate * jax.nn.sigmoid(SWIGLU_ALPHA * gate)
            y_ref[0:rows, :] = _dot(act.astype(BF16), wd_bf[...]) + bd_ref[0]
            if rows < EXPERT_BLOCK:
                y_ref[rows:, :] = jnp.zeros((EXPERT_BLOCK - rows, D_MODEL), F32)

        for parts in range(1, EXPERT_BLOCK // EXPERT_ROW_STEP + 1):
            pl.when(parts_ref[j] == parts)(functools.partial(ffn, parts * EXPERT_ROW_STEP))


def _experts(blk_e, first, slot, next_e, parts, nused, xbuf, wgu, bgu, wd, bd):
    n_rows = xbuf.shape[0]
    nblk = n_rows // EXPERT_BLOCK

    def rowblk(j, be, fi, sl, ne, pa, nu):
        return (jnp.minimum(j, nu[0] - 1), 0)

    def expert(j, be, fi, sl, ne, pa, nu):
        return (be[jnp.minimum(j, nu[0] - 1)], 0, 0)

    return pl.pallas_call(
        _experts_body,
        grid_spec=pltpu.PrefetchScalarGridSpec(
            num_scalar_prefetch=6,
            grid=(nblk,),
            in_specs=[
                pl.BlockSpec((EXPERT_BLOCK, D_MODEL), rowblk),
                pl.BlockSpec((1, 1, 2 * D_FF), expert),
                pl.BlockSpec((1, 1, D_MODEL), expert),
                pl.BlockSpec(memory_space=pl.ANY),
                pl.BlockSpec(memory_space=pl.ANY),
            ],
            out_specs=pl.BlockSpec((EXPERT_BLOCK, D_MODEL), rowblk),
            scratch_shapes=[
                pltpu.VMEM((2, D_MODEL, 2 * D_FF), F32),
                pltpu.VMEM((2, D_FF, D_MODEL), F32),
                pltpu.VMEM((D_MODEL, 2 * D_FF), BF16),
                pltpu.VMEM((D_FF, D_MODEL), BF16),
                pltpu.SemaphoreType.DMA((2,)),
                pltpu.SemaphoreType.DMA((2,)),
            ],
        ),
        out_shape=jax.ShapeDtypeStruct((n_rows, D_MODEL), F32),
        input_output_aliases={6: 0},
        compiler_params=_params("arbitrary"),
        name="moe_experts",
    )(blk_e, first, slot, next_e, parts, nused, xbuf, bgu, bd, wgu, wd)


def _combine_body(cnt_ref, off_ref, goff_ref, tot_ref, ybuf_ref, lpos_ref, tw_ref, h1_ref, fg_ref, out_ref,
                  stage_ref, sems):
    i = pl.program_id(0)
    n_tiles = pl.num_programs(0) * TILES_PER_STEP

    def copier(s):
        def make_copy(lo, go, size):
            return pltpu.make_async_copy(ybuf_ref.at[pl.ds(go, size)], stage_ref.at[s, pl.ds(lo, size)], sems.at[s])
        return make_copy

    ahead = TILES_PER_STEP - 1

    @pl.when(i == 0)
    def _init():
        stage_ref[...] = jnp.zeros_like(stage_ref)
        for s in range(ahead):
            _copy_runs(s, cnt_ref, off_ref, goff_ref, copier(s))

    for s in range(TILES_PER_STEP):
        t = i * TILES_PER_STEP + s

        @pl.when(t + ahead < n_tiles)
        def _prefetch():
            _copy_runs(t + ahead, cnt_ref, off_ref, goff_ref, copier((s + ahead) % TILES_PER_STEP))

        rows = slice(s * TOK_TILE, (s + 1) * TOK_TILE)
        w = _slot_matrix(lpos_ref[s], tw_ref[s]).astype(BF16)
        _wait_rows(copier(s), tot_ref[t])
        h2 = h1_ref[rows, :] + _dot_tn(w, stage_ref[s].astype(BF16))
        ms = jnp.mean(h2 * h2, axis=-1, keepdims=True)
        out_ref[rows, :] = h2 * lax.rsqrt(ms + RMS_EPS) * fg_ref[...]


def _combine(cnt, off, goff, tot, ybuf, lpos, tw, h1, final_g):
    n = h1.shape[0]
    rows = TILES_PER_STEP * TOK_TILE
    row = lambda i, *_: (i, 0)
    return pl.pallas_call(
        _combine_body,
        grid_spec=pltpu.PrefetchScalarGridSpec(
            num_scalar_prefetch=4,
            grid=(n // rows,),
            in_specs=[
                pl.BlockSpec(memory_space=pl.ANY),
                pl.BlockSpec((TILES_PER_STEP, SUBLANES, TOK_TILE), lambda i, *_: (i, 0, 0)),
                pl.BlockSpec((TILES_PER_STEP, SUBLANES, TOK_TILE), lambda i, *_: (i, 0, 0)),
                pl.BlockSpec((rows, D_MODEL), row),
                pl.BlockSpec((1, D_MODEL), lambda i, *_: (0, 0)),
            ],
            out_specs=pl.BlockSpec((rows, D_MODEL), row),
            scratch_shapes=[pltpu.VMEM((TILES_PER_STEP, STAGE_ROWS, D_MODEL), F32),
                            pltpu.SemaphoreType.DMA((TILES_PER_STEP,))],
        ),
        out_shape=jax.ShapeDtypeStruct((n, D_MODEL), F32),
        compiler_params=_params("arbitrary"),
        name="moe_combine",
    )(cnt, off, goff, tot, ybuf, lpos, tw, h1, final_g)


def _moe_layout(cnt_tiles):
    nt = cnt_tiles.shape[0]
    total = jnp.sum(cnt_tiles, axis=0)
    nblk_e = (total + EXPERT_BLOCK - 1) // EXPERT_BLOCK
    blk_end = jnp.cumsum(nblk_e)
    pstart = (blk_end - nblk_e) * EXPERT_BLOCK
    prefix = jnp.cumsum(cnt_tiles, axis=0) - cnt_tiles
    goff = pstart[None, :] + prefix
    off = jnp.cumsum(cnt_tiles, axis=1) - cnt_tiles
    tot = jnp.sum(cnt_tiles, axis=1)
    max_rows = nt * (TOK_TILE * TOP_K + N_EXPERTS * (RUN_ALIGN - 1)) + N_EXPERTS * (EXPERT_BLOCK - 1)
    max_blocks = (max_rows + EXPERT_BLOCK - 1) // EXPERT_BLOCK
    blk_id = jnp.arange(max_blocks, dtype=blk_end.dtype)
    blk_e = jnp.minimum(jnp.sum(blk_end[None, :] <= blk_id[:, None], axis=1), N_EXPERTS - 1).astype(jnp.int32)
    nused = blk_end[-1:].astype(jnp.int32)
    flat = lambda a: a.reshape(-1).astype(jnp.int32)
    zoff = pstart + total
    zlen = nblk_e * EXPERT_BLOCK - total
    blkrange = jnp.stack([nused[0], jnp.int32(max_blocks)])
    used = blk_id < nused[0]
    first = used & jnp.concatenate([jnp.ones((1,), bool), blk_e[1:] != blk_e[:-1]])
    slot = (jnp.cumsum(first.astype(jnp.int32)) - 1) % 2
    eid = jnp.arange(N_EXPERTS)
    later = (eid[None, :] > eid[:, None]) & (nblk_e > 0)[None, :]
    next_of = jnp.min(jnp.where(later, eid[None, :], N_EXPERTS), axis=1)
    next_of = jnp.where(next_of < N_EXPERTS, next_of, -1)
    is_e = blk_e[:, None] == eid[None, :]
    pick = lambda per_expert: jnp.sum(jnp.where(is_e, per_expert[None, :], 0), axis=1)
    rows_left = jnp.clip(pick(total) - (blk_id - pick(blk_end - nblk_e)) * EXPERT_BLOCK, 0, EXPERT_BLOCK)
    parts = (rows_left + EXPERT_ROW_STEP - 1) // EXPERT_ROW_STEP
    experts_plan = (blk_e, first.astype(jnp.int32), slot.astype(jnp.int32), pick(next_of).astype(jnp.int32),
                    parts.astype(jnp.int32), nused)
    dispatch_plan = (flat(cnt_tiles), flat(off), flat(goff), flat(tot))
    zero_plan = (flat(zoff), flat(zlen), blkrange.astype(jnp.int32))
    return dispatch_plan, zero_plan, experts_plan, max_blocks * EXPERT_BLOCK


def kernel(x, meta_tokens, attn_norm_g, w_in, na_rpb, na_norm_g, hgrn_lb_logits, hgrn_norm_g, w_out, ffn_norm_g,
           router_w, router_b, expert_w_gu, expert_b_gu, expert_w_down, expert_b_down, final_norm_g):
    B, T, D = x.shape
    x2d = x.reshape(B * T, D)
    w_in_b = w_in[0].astype(BF16)
    ng = attn_norm_g[0].reshape(1, D)
    lbl = hgrn_lb_logits[:, :, :].reshape(4, HG_WIDTH)

    att, hga, hgg = _inproj(x2d, ng, w_in_b, lbl, 512)
    att_m, hga_m, hgg_m = _inproj(meta_tokens.astype(F32), ng, w_in_b, lbl, N_META)
    att_m = jnp.pad(att_m, ((0, META_PAD - N_META), (0, 0)))
    hga_m = jnp.pad(hga_m, ((0, HG_CHUNK - N_META), (0, 0)))
    hgg_m = jnp.pad(hgg_m, ((0, HG_CHUNK - N_META), (0, 0)))

    yna = _natten(att, att_m, _natten_bias_tables(na_rpb[0]), na_norm_g[0].reshape(1, NA_WIDTH), B, T)
    o_f, o_b = _hgrn(hga, hgg, hga_m, hgg_m, B, T)

    h1, xn, lpos, tw, cnt_tiles = _mix_route(
        yna.reshape(B * T, NA_WIDTH), o_f.reshape(B * T, HG_WIDTH), o_b.reshape(B * T, HG_WIDTH), hga, x2d,
        w_out[0].astype(BF16), hgrn_norm_g[0].reshape(1, HG_WIDTH), ffn_norm_g[0].reshape(1, D),
        router_w[0], router_b[0].reshape(1, N_EXPERTS))

    dispatch_plan, zero_plan, experts_plan, n_rows = _moe_layout(cnt_tiles[:, :, 0])
    xbuf = _dispatch(*dispatch_plan, *zero_plan, xn, lpos, n_rows)
    ybuf = _experts(*experts_plan, xbuf, expert_w_gu[0], expert_b_gu[0][:, None, :],
                    expert_w_down[0], expert_b_down[0][:, None, :])
    out = _combine(*dispatch_plan, ybuf, lpos, tw, h1, final_norm_g.reshape(1, D))
    return out.reshape(B, T, D)
```

```python
import functools

import numpy as np
import jax
import jax.numpy as jnp
from jax import lax
from jax.experimental import pallas as pl
from jax.experimental.pallas import tpu as pltpu

F32 = jnp.float32
BF16 = jnp.bfloat16

D_MODEL = 1024
N_META = 16
GRID_W = 64
NA_WIDTH = 512
NA_HEAD_DIM = 64
NA_HEADS = 8
NA_KH = 8
NA_KW = 16
HG_WIDTH = 512
HG_HEAD_DIM = 128
HG_HEADS = 4
HG_CHUNK = 64
IN_COLS = 3 * NA_WIDTH + 5 * HG_WIDTH
N_EXPERTS = 32
TOP_K = 4
D_FF = 1024
SWIGLU_LIMIT = 7.0
SWIGLU_ALPHA = 1.702
RMS_EPS = 1e-6
NEG_BIG = -1e30
LOG2_E = 1.4426950408889634

LANES = 128
VMEM_LIMIT_BYTES = 56 * 1024 * 1024

TOK_TILE = 256
EXPERT_BLOCK = 512
SUBLANES = 8
RUN_ALIGN = SUBLANES
STAGE_ROWS = TOK_TILE * TOP_K + N_EXPERTS * RUN_ALIGN
N_HG_LEVELS = 6


def _dot(a, b):
    return jnp.dot(a, b, preferred_element_type=F32)


def _dot_nt(a, b):
    return lax.dot_general(a, b, (((1,), (1,)), ((), ())), preferred_element_type=F32)


def _dot_tn(a, b):
    return lax.dot_general(a, b, (((0,), (0,)), ((), ())), preferred_element_type=F32)


def _split_bf16(x):
    hi = x.astype(BF16)
    lo = (x - hi.astype(F32)).astype(BF16)
    return hi, lo


def _pack_rows(rows, n_rows):
    idx = lax.broadcasted_iota(jnp.int32, (n_rows, rows[0].shape[1]), 0)
    out = jnp.zeros(idx.shape, rows[0].dtype)
    for k, r in enumerate(rows):
        out = jnp.where(idx == k, r, out)
    return out


def _params(*sem):
    return pltpu.CompilerParams(dimension_semantics=sem, vmem_limit_bytes=VMEM_LIMIT_BYTES)


def _inproj_body(x_ref, ng_ref, w_ref, lbl_ref, att_ref, hga_ref, hgg_ref):
    x = x_ref[...]
    ms = jnp.mean(x * x, axis=-1, keepdims=True)
    n = (x * lax.rsqrt(ms + RMS_EPS) * ng_ref[...]).astype(BF16)

    def proj(lo, hi):
        return _dot(n, w_ref[:, lo:hi])

    pq = proj(0, NA_WIDTH)
    att_ref[:, 0:NA_WIDTH] = (pq * (NA_HEAD_DIM ** -0.5 * LOG2_E)).astype(BF16)
    att_ref[:, NA_WIDTH:3 * NA_WIDTH] = proj(NA_WIDTH, 3 * NA_WIDTH).astype(BF16)

    base = 3 * NA_WIDTH
    W = HG_WIDTH
    qh = proj(base, base + W)
    hga_ref[:, 0:W] = (qh * jax.nn.sigmoid(qh)).astype(BF16)
    hga_ref[:, W:2 * W] = proj(base + W, base + 2 * W).astype(BF16)
    lbl = lbl_ref[...]
    for d in range(2):
        a0 = lbl[2 * d:2 * d + 1, :]
        a1 = lbl[2 * d + 1:2 * d + 2, :]
        m = jnp.maximum(a0, a1)
        e0 = jnp.exp(a0 - m)
        e1 = jnp.exp(a1 - m)
        lb = e0 / (e0 + e1)
        raw = proj(base + (2 + d) * W, base + (3 + d) * W)
        f = lb + (1.0 - lb) * jax.nn.sigmoid(raw)
        hgg_ref[:, d * W:(d + 1) * W] = jnp.log(f)
    gt = proj(base + 4 * W, base + 5 * W)
    hga_ref[:, 2 * W:3 * W] = (gt * jax.nn.sigmoid(gt)).astype(BF16)


def _inproj(x2d, norm_g, w_bf16, lb_logits4, tm):
    n = x2d.shape[0]
    return pl.pallas_call(
        _inproj_body,
        grid=(n // tm,),
        in_specs=[
            pl.BlockSpec((tm, D_MODEL), lambda i: (i, 0)),
            pl.BlockSpec((1, D_MODEL), lambda i: (0, 0)),
            pl.BlockSpec((D_MODEL, IN_COLS), lambda i: (0, 0)),
            pl.BlockSpec((4, HG_WIDTH), lambda i: (0, 0)),
        ],
        out_specs=[
            pl.BlockSpec((tm, 3 * NA_WIDTH), lambda i: (i, 0)),
            pl.BlockSpec((tm, 3 * HG_WIDTH), lambda i: (i, 0)),
            pl.BlockSpec((tm, 2 * HG_WIDTH), lambda i: (i, 0)),
        ],
        out_shape=[
            jax.ShapeDtypeStruct((n, 3 * NA_WIDTH), BF16),
            jax.ShapeDtypeStruct((n, 3 * HG_WIDTH), BF16),
            jax.ShapeDtypeStruct((n, 2 * HG_WIDTH), F32),
        ],
        compiler_params=_params("arbitrary"),
        name="inproj",
    )(x2d, norm_g, w_bf16, lb_logits4)


N_WIN_KEYS = NA_KH * GRID_W
META_PAD = LANES


NA_GROUP = 4
GROUP_LANES = NA_GROUP * NA_HEAD_DIM


NA_ROWS_PER_STEP = 16


def _natten_row(rr, r, q_ref, k_ref, v_ref, km_ref, vm_ref, bias_ref, *, rows):
    rs = jnp.clip(r - NA_KH // 2, 0, rows - NA_KH)
    d0 = rs - r + (NA_KH - 1)
    start = pl.multiple_of(rs * GRID_W, GRID_W)
    lane_head = lax.broadcasted_iota(jnp.int32, (GRID_W, GROUP_LANES), 1) // NA_HEAD_DIM
    meta_col = lax.broadcasted_iota(jnp.int32, (1, META_PAD), 1)
    meta_bias = jnp.where(meta_col < N_META, 0.0, NEG_BIG).astype(F32)
    folded = []
    for g in range(NA_HEADS // NA_GROUP):
        sl = slice(g * GROUP_LANES, (g + 1) * GROUP_LANES)
        q4 = q_ref[0, rr * GRID_W:(rr + 1) * GRID_W, sl]
        zero = jnp.zeros_like(q4)
        qm = jnp.concatenate([jnp.where(lane_head == h, q4, zero) for h in range(NA_GROUP)], axis=0)
        kw = k_ref[0, pl.ds(start, N_WIN_KEYS), sl]
        vw = v_ref[0, pl.ds(start, N_WIN_KEYS), sl]
        hs = slice(g * NA_GROUP, (g + 1) * NA_GROUP)
        bias = jnp.concatenate(
            [bias_ref[d0 + 2 * j, hs].reshape(NA_GROUP * GRID_W, 2 * GRID_W) for j in range(NA_KH // 2)], axis=-1)
        s = _dot_nt(qm, kw) + bias
        sm = _dot_nt(qm, km_ref[:, sl]) + meta_bias
        m = jnp.maximum(jnp.max(s, axis=-1, keepdims=True), jnp.max(sm, axis=-1, keepdims=True))
        e = jnp.exp2(s - m)
        em = jnp.exp2(sm - m)
        den = jnp.sum(e, axis=-1, keepdims=True) + jnp.sum(em, axis=-1, keepdims=True)
        o = (_dot(e.astype(BF16), vw) + _dot(em.astype(BF16), vm_ref[:, sl])) * (1.0 / den)
        acc = jnp.where(lane_head == 0, o[0:GRID_W], 0.0)
        for h in range(1, NA_GROUP):
            acc = jnp.where(lane_head == h, o[h * GRID_W:(h + 1) * GRID_W], acc)
        folded.append(acc)
    return jnp.concatenate(folded, axis=-1)


def _natten_body(q_ref, k_ref, v_ref, km_ref, vm_ref, bias_ref, ng_ref, bd_ref, o_ref, *, rows):
    r0 = pl.program_id(1) * NA_ROWS_PER_STEP
    o2 = jnp.concatenate(
        [_natten_row(rr, r0 + rr, q_ref, k_ref, v_ref, km_ref, vm_ref, bias_ref, rows=rows)
         for rr in range(NA_ROWS_PER_STEP)], axis=0)
    sq_hi, sq_lo = _split_bf16(o2 * o2)
    ms = (_dot(sq_hi, bd_ref[...]) + _dot(sq_lo, bd_ref[...])) * (1.0 / NA_HEAD_DIM)
    o_ref[0] = (o2 * lax.rsqrt(ms + RMS_EPS) * ng_ref[...]).astype(o_ref.dtype)


def _natten(att, att_meta, bias_tabs, norm_g, batch, seq):
    rows = seq // GRID_W
    att3 = att.reshape(batch, seq, 3 * NA_WIDTH)
    head_of = np.arange(NA_WIDTH) // NA_HEAD_DIM
    same_head = jnp.asarray(head_of[:, None] == head_of[None, :], BF16)

    qrows = NA_ROWS_PER_STEP * GRID_W
    return pl.pallas_call(
        functools.partial(_natten_body, rows=rows),
        grid=(batch, rows // NA_ROWS_PER_STEP),
        in_specs=[
            pl.BlockSpec((1, qrows, NA_WIDTH), lambda b, r: (b, r, 0)),
            pl.BlockSpec((1, seq, NA_WIDTH), lambda b, r: (b, 0, 1)),
            pl.BlockSpec((1, seq, NA_WIDTH), lambda b, r: (b, 0, 2)),
            pl.BlockSpec((META_PAD, NA_WIDTH), lambda b, r: (0, 1)),
            pl.BlockSpec((META_PAD, NA_WIDTH), lambda b, r: (0, 2)),
            pl.BlockSpec(bias_tabs.shape, lambda b, r: (0, 0, 0, 0), pipeline_mode=pl.Buffered(1)),
            pl.BlockSpec((1, NA_WIDTH), lambda b, r: (0, 0)),
            pl.BlockSpec((NA_WIDTH, NA_WIDTH), lambda b, r: (0, 0)),
        ],
        out_specs=pl.BlockSpec((1, qrows, NA_WIDTH), lambda b, r: (b, r, 0)),
        out_shape=jax.ShapeDtypeStruct((batch, seq, NA_WIDTH), BF16),
        compiler_params=_params("arbitrary", "arbitrary"),
        name="natten",
    )(att3, att3, att3, att_meta, att_meta, bias_tabs, norm_g, same_head)


def _natten_bias_tables(rpb):
    c = np.arange(GRID_W)[:, None]
    kc = np.arange(GRID_W)[None, :]
    cs = np.clip(c - NA_KW // 2, 0, GRID_W - NA_KW)
    valid = (kc >= cs) & (kc < cs + NA_KW)
    dc = kc - c + (NA_KW - 1)
    pick = np.asarray(dc[None] == np.arange(2 * NA_KW - 1)[:, None, None], np.float32)
    t1 = jnp.einsum("hdj,jck->hdck", rpb.astype(F32) * LOG2_E, pick, precision=lax.Precision.HIGHEST)
    t1 = jnp.where(valid[None, None], t1, NEG_BIG)
    pairs = jnp.concatenate([t1[:, :-1], t1[:, 1:]], axis=-1)
    return pairs.transpose(1, 0, 2, 3)


HG_CHUNKS_PER_STEP = 16
HG_CHUNKS_PER_GROUP = 1
ROW_B, ROW_LV = 0, 1
N_EXP_BLOCKS = 1 + N_HG_LEVELS


def _hgrn_constants():
    C = HG_CHUNK
    t = np.arange(C)
    u = t[None, :]
    mats = [u <= t[:, None]]
    masks = [np.eye(C, dtype=bool)]
    for lv in range(N_HG_LEVELS):
        m = 1 << lv
        blk = t // (2 * m)
        upper = (t // m) % 2 == 1
        p = blk * 2 * m + m - 1
        q_rows = upper[:, None] & (u > p[:, None]) & (u <= t[:, None])
        k_rows = (~upper)[:, None] & (u > t[:, None]) & (u <= p[:, None])
        mats.append(q_rows | k_rows)
        masks.append((blk[:, None] == blk[None, :]) & upper[:, None] & (~upper)[None, :])
    fwd = np.concatenate(mats, axis=0).astype(np.float32)
    fwd_mask = np.stack(masks).astype(np.float32)
    blocks = fwd.reshape(N_EXP_BLOCKS, C, C)
    bwd = blocks[:, ::-1, ::-1].reshape(N_EXP_BLOCKS * C, C)
    bwd_mask = fwd_mask[:, ::-1, ::-1]
    both = np.stack([fwd, bwd])
    both = np.concatenate([both, both], axis=-1)
    both_mask = np.stack([fwd_mask, bwd_mask])
    both_mask = np.concatenate([both_mask, both_mask], axis=-1)
    return (jnp.asarray(both, BF16), jnp.asarray(both_mask, F32))


def _hgrn_exponents(mat, g):
    g_hi, g_lo = _split_bf16(g)
    return _dot(mat, jnp.concatenate([g_hi, g_lo], axis=0))


def _hgrn_init(meta_a_ref, meta_g_ref, mat_ref, stf_ref, stb_ref):
    C, W = HG_CHUNK, HG_WIDTH
    v = meta_a_ref[:, W:2 * W]
    prefix = mat_ref[0, ROW_B * C:(ROW_B + 1) * C, :]
    for d, st_ref in ((0, stf_ref), (1, stb_ref)):
        g = meta_g_ref[:, d * W:(d + 1) * W]
        b = _hgrn_exponents(prefix, g)
        kb = ((1.0 - jnp.exp(g)) * jnp.exp(b[C - 1:C] - b)).astype(BF16)
        for h in range(HG_HEADS):
            sl = slice(h * HG_HEAD_DIM, (h + 1) * HG_HEAD_DIM)
            st_ref[h] = _dot_tn(v[:, sl], kb[:, sl])


def _hgrn_group_tasks(j0):
    js = range(j0, j0 + HG_CHUNKS_PER_GROUP)
    return ([(0, j * HG_CHUNK) for j in js] +
            [(1, (HG_CHUNKS_PER_STEP - 1 - j) * HG_CHUNK) for j in js])


def _hgrn_tasks(tasks, dir_refs, states, mat_ref, mask_ref):
    C, W = HG_CHUNK, HG_WIDTH
    heads = [slice(h * HG_HEAD_DIM, (h + 1) * HG_HEAD_DIM) for h in range(HG_HEADS)]
    qs, ks, vs, ex_bs, ex_lvs, ex_ends = {}, {}, {}, {}, {}, {}
    for d, r0 in tasks:
        q_ref, v_ref, g_ref, total_row = dir_refs[d][0], dir_refs[d][1], dir_refs[d][2], dir_refs[d][3]
        g = g_ref[0, r0:r0 + C, :]
        sums = _hgrn_exponents(mat_ref[d], g)
        ex_bs[d, r0] = jnp.exp(sums[0:C])
        ex_lvs[d, r0] = jnp.exp(sums[C:]).astype(BF16)
        ex_ends[d, r0] = jnp.exp(sums[total_row:total_row + 1] - sums[0:C])
        qs[d, r0] = q_ref[0, r0:r0 + C, :]
        ks[d, r0] = 1.0 - jnp.exp(g)
        vs[d, r0] = v_ref[0, r0:r0 + C, :]

    D2 = 2 * HG_HEAD_DIM
    pairs = [slice(p * D2, (p + 1) * D2) for p in range(HG_HEADS // 2)]

    def block_diag(x):
        zero = jnp.zeros((C, HG_HEAD_DIM), x.dtype)
        return jnp.concatenate([jnp.concatenate([x[:, :HG_HEAD_DIM], zero], axis=1),
                                jnp.concatenate([zero, x[:, HG_HEAD_DIM:]], axis=1)], axis=0)

    prods = {}
    for task in tasks:
        for p, sl in enumerate(pairs):
            q2, k2 = qs[task][:, sl], ks[task][:, sl].astype(BF16)
            ps = [_dot_nt(q2, block_diag(k2))]
            for lv in range(N_HG_LEVELS):
                scale = ex_lvs[task][lv * C:(lv + 1) * C, sl]
                ps.append(_dot_nt(q2 * scale, block_diag(k2 * scale)))
            prods[task, p] = ps
    attn, qbs, kbs = {}, {}, {}
    for task in tasks:
        d = task[0]
        for p, sl in enumerate(pairs):
            a = prods[task, p][0] * mask_ref[d, 0]
            for lv in range(N_HG_LEVELS):
                a = a + prods[task, p][1 + lv] * mask_ref[d, 1 + lv]
            o2 = _dot(a.astype(BF16), block_diag(vs[task][:, sl]))
            attn[task, 2 * p] = o2[:, :HG_HEAD_DIM]
            attn[task, 2 * p + 1] = o2[:, HG_HEAD_DIM:]
        for h, sl in enumerate(heads):
            qbs[task, h] = (qs[task][:, sl].astype(F32) * ex_bs[task][:, sl]).astype(BF16)
            kbs[task, h] = _dot_tn(vs[task][:, sl], (ks[task][:, sl] * ex_ends[task][:, sl]).astype(BF16))
    for task in tasks:
        d, r0 = task
        total_row, o_ref = dir_refs[d][3], dir_refs[d][4]
        outs = []
        for h, sl in enumerate(heads):
            st = states[d, h]
            outs.append(attn[task, h] + _dot_nt(qbs[task, h], st.astype(BF16)))
            decay = ex_bs[task][total_row:total_row + 1, sl]
            states[d, h] = decay * st + kbs[task, h]
        o_ref[0, r0:r0 + C, :] = jnp.concatenate(outs, axis=-1).astype(o_ref.dtype)


def _hgrn_body(hqf_ref, hvf_ref, hgf_ref, hqb_ref, hvb_ref, hgb_ref, meta_a_ref, meta_g_ref, mat_ref, mask_ref,
               of_ref, ob_ref, stf_ref, stb_ref):
    @pl.when(pl.program_id(1) == 0)
    def _init():
        _hgrn_init(meta_a_ref, meta_g_ref, mat_ref, stf_ref, stb_ref)

    dir_refs = ((hqf_ref, hvf_ref, hgf_ref, HG_CHUNK - 1, of_ref), (hqb_ref, hvb_ref, hgb_ref, 0, ob_ref))
    st_refs = (stf_ref, stb_ref)
    states = {(d, h): st_refs[d][h] for d in range(2) for h in range(HG_HEADS)}
    for j0 in range(0, HG_CHUNKS_PER_STEP, HG_CHUNKS_PER_GROUP):
        _hgrn_tasks(_hgrn_group_tasks(j0), dir_refs, states, mat_ref, mask_ref)
    for d in range(2):
        for h in range(HG_HEADS):
            st_refs[d][h] = states[d, h]


def _hgrn(hga, hgg, hga_meta_pad, hgg_meta_pad, batch, seq):
    steps = seq // (HG_CHUNK * HG_CHUNKS_PER_STEP)
    a3 = hga.reshape(batch, seq, 3 * HG_WIDTH)
    g3 = hgg.reshape(batch, seq, 2 * HG_WIDTH)
    mats, masks = _hgrn_constants()
    hblk = (1, HG_CHUNK * HG_CHUNKS_PER_STEP, HG_WIDTH)

    def fwd(col):
        return pl.BlockSpec(hblk, lambda b, s: (b, s, col))

    def bwd(col):
        return pl.BlockSpec(hblk, lambda b, s: (b, steps - 1 - s, col))

    return pl.pallas_call(
        _hgrn_body,
        grid=(batch, steps),
        in_specs=[
            fwd(0), fwd(1), fwd(0),
            bwd(0), bwd(1), bwd(1),
            pl.BlockSpec((HG_CHUNK, 3 * HG_WIDTH), lambda b, s: (0, 0)),
            pl.BlockSpec((HG_CHUNK, 2 * HG_WIDTH), lambda b, s: (0, 0)),
            pl.BlockSpec(mats.shape, lambda b, s: (0, 0, 0)),
            pl.BlockSpec(masks.shape, lambda b, s: (0, 0, 0, 0)),
        ],
        out_specs=[fwd(0), bwd(0)],
        out_shape=[jax.ShapeDtypeStruct((batch, seq, HG_WIDTH), BF16)] * 2,
        scratch_shapes=[pltpu.VMEM((HG_HEADS, HG_HEAD_DIM, HG_HEAD_DIM), F32)] * 2,
        compiler_params=_params("arbitrary", "arbitrary"),
        name="hgrn",
    )(a3, a3, g3, a3, a3, g3, hga_meta_pad, hgg_meta_pad, mats, masks)


MIX_TILES_PER_STEP = 4


def _mix_route_body(yna_ref, of_ref, ob_ref, gate_ref, x_ref, wout_ref, hgn_ref, ffg_ref,
                    rwh_ref, rwl_ref, rb_ref, tri_ref, lowe_ref,
                    h1_ref, xn_ref, lpos_ref, tw_ref, cnt_ref):
    T, E = TOK_TILE, N_EXPERTS
    tiles = [slice(s * T, (s + 1) * T) for s in range(MIX_TILES_PER_STEP)]
    x_his, x_los = [], []
    for rows in tiles:
        o = of_ref[rows, :].astype(F32) + ob_ref[rows, :].astype(F32)
        parts = []
        for h in range(HG_HEADS):
            seg = o[:, h * HG_HEAD_DIM:(h + 1) * HG_HEAD_DIM]
            ms = jnp.mean(seg * seg, axis=-1, keepdims=True)
            parts.append(seg * lax.rsqrt(ms + RMS_EPS))
        yhg = jnp.concatenate(parts, axis=-1) * hgn_ref[...] * gate_ref[rows, :]
        mix = _dot(yna_ref[rows, :], wout_ref[0:NA_WIDTH, :]) + _dot(yhg.astype(BF16), wout_ref[NA_WIDTH:, :])
        h1 = x_ref[rows, :] + mix
        h1_ref[rows, :] = h1
        ms = jnp.mean(h1 * h1, axis=-1, keepdims=True)
        xn = h1 * lax.rsqrt(ms + RMS_EPS) * ffg_ref[...]
        x_hi, x_lo = _split_bf16(xn)
        xn_ref[rows, :] = x_hi
        x_his.append(x_hi)
        x_los.append(x_lo)

    curs = [(_dot_nt(rwh_ref[...], x_hi) + _dot_nt(rwh_ref[...], x_lo) + _dot_nt(rwl_ref[...], x_hi)) + rb_ref[...]
            for x_hi, x_lo in zip(x_his, x_los)]
    row = lax.broadcasted_iota(jnp.int32, (E, T), 0).astype(F32)
    sels = [[] for _ in tiles]
    vals = [[] for _ in tiles]
    for _ in range(TOP_K):
        for s in range(len(tiles)):
            m = jnp.max(curs[s], axis=0, keepdims=True)
            first = jnp.min(jnp.where(curs[s] == m, row, float(E)), axis=0, keepdims=True)
            sel = row == first
            sels[s].append(sel)
            vals[s].append(m)
            curs[s] = jnp.where(sel, -jnp.inf, curs[s])
    for s in range(len(tiles)):
        es = [jnp.exp(vk - vals[s][0]) for vk in vals[s]]
        den = es[0] + es[1] + es[2] + es[3]
        tw_ref[s] = _pack_rows([e / den for e in es], SUBLANES)

        onehot = jnp.zeros((E, T), F32)
        for sel in sels[s]:
            onehot = onehot + jnp.where(sel, 1.0, 0.0)
        lrank = _dot(onehot.astype(BF16), tri_ref[...])
        cnt = jnp.sum(onehot, axis=1, keepdims=True)
        cnt = jnp.floor((cnt + (RUN_ALIGN - 1)) * (1.0 / RUN_ALIGN)) * RUN_ALIGN
        off = _dot(lowe_ref[...], jnp.broadcast_to(cnt, (E, LANES)).astype(BF16))
        base = jnp.concatenate([off] * (T // LANES), axis=1) + lrank
        lpos_ref[s] = _pack_rows(
            [jnp.sum(jnp.where(sel, base, 0.0), axis=0, keepdims=True) for sel in sels[s]],
            SUBLANES).astype(jnp.int32)
        cnt_ref[s] = cnt.astype(jnp.int32)


def _mix_route(yna, o_f, o_b, hg, x2d, wout_bf16, hgn, ffg, router_w, router_b):
    n = x2d.shape[0]
    T = TOK_TILE
    nt = n // T
    rw_t = router_w.T
    rwh = rw_t.astype(BF16)
    rwl = (rw_t - rwh.astype(F32)).astype(BF16)
    router_b = router_b.reshape(N_EXPERTS, 1)
    tri = jnp.asarray(np.triu(np.ones((T, T), np.float32), 1), BF16)
    lowe = jnp.asarray(np.tril(np.ones((N_EXPERTS, N_EXPERTS), np.float32), -1), BF16)
    row = lambda i: (i, 0)
    const = lambda i: (0, 0)
    tile3 = lambda i: (i, 0, 0)
    K = MIX_TILES_PER_STEP
    R = K * T
    return pl.pallas_call(
        _mix_route_body,
        grid=(nt // K,),
        in_specs=[
            pl.BlockSpec((R, NA_WIDTH), row),
            pl.BlockSpec((R, HG_WIDTH), row),
            pl.BlockSpec((R, HG_WIDTH), row),
            pl.BlockSpec((R, HG_WIDTH), lambda i: (i, 2)),
            pl.BlockSpec((R, D_MODEL), row),
            pl.BlockSpec((D_MODEL, D_MODEL), const),
            pl.BlockSpec((1, HG_WIDTH), const),
            pl.BlockSpec((1, D_MODEL), const),
            pl.BlockSpec((N_EXPERTS, D_MODEL), const),
            pl.BlockSpec((N_EXPERTS, D_MODEL), const),
            pl.BlockSpec((N_EXPERTS, 1), const),
            pl.BlockSpec((T, T), const),
            pl.BlockSpec((N_EXPERTS, N_EXPERTS), const),
        ],
        out_specs=[
            pl.BlockSpec((R, D_MODEL), row),
            pl.BlockSpec((R, D_MODEL), row),
            pl.BlockSpec((K, SUBLANES, T), tile3),
            pl.BlockSpec((K, SUBLANES, T), tile3),
            pl.BlockSpec((K, N_EXPERTS, 1), tile3),
        ],
        out_shape=[
            jax.ShapeDtypeStruct((n, D_MODEL), F32),
            jax.ShapeDtypeStruct((n, D_MODEL), BF16),
            jax.ShapeDtypeStruct((nt, SUBLANES, T), jnp.int32),
            jax.ShapeDtypeStruct((nt, SUBLANES, T), F32),
            jax.ShapeDtypeStruct((nt, N_EXPERTS, 1), jnp.int32),
        ],
        compiler_params=_params("arbitrary"),
        name="mix_route",
    )(yna, o_f, o_b, hg, x2d, wout_bf16, hgn, ffg, rwh, rwl, router_b, tri, lowe)


TILES_PER_STEP = 4


def _wait_rows(make_copy, rows):
    make_copy(0, 0, pl.multiple_of(rows, RUN_ALIGN)).wait()


def _copy_runs(i, cnt_ref, off_ref, goff_ref, make_copy):
    def per_expert(e, carry):
        n = pl.multiple_of(cnt_ref[i * N_EXPERTS + e], RUN_ALIGN)
        lo = pl.multiple_of(off_ref[i * N_EXPERTS + e], RUN_ALIGN)
        go = pl.multiple_of(goff_ref[i * N_EXPERTS + e], RUN_ALIGN)

        @pl.when(n > 0)
        def _start():
            make_copy(lo, go, n).start()

        return carry

    lax.fori_loop(0, N_EXPERTS, per_expert, 0)


def _slot_matrix(lpos, weights=None):
    row = lax.broadcasted_iota(jnp.int32, (STAGE_ROWS, TOK_TILE), 0)
    acc = jnp.zeros((STAGE_ROWS, TOK_TILE), F32)
    for k in range(TOP_K):
        hit = lpos[k:k + 1, :] == row
        val = 1.0 if weights is None else weights[k:k + 1, :]
        acc = jnp.where(hit, val, acc)
    return acc


def _dispatch_body(cnt_ref, off_ref, goff_ref, tot_ref, zoff_ref, zlen_ref, blkrange_ref,
                   xn_ref, lpos_ref, xbuf_ref, stage_ref, zero_ref, sems, zsem):
    i = pl.program_id(0)
    last = pl.num_programs(0) - 1

    def zero_fill(act):
        def tail(e, carry):
            n = pl.multiple_of(zlen_ref[e], RUN_ALIGN)

            @pl.when(n > 0)
            def _():
                go = pl.multiple_of(zoff_ref[e], RUN_ALIGN)
                act(pltpu.make_async_copy(zero_ref.at[pl.ds(0, n)], xbuf_ref.at[pl.ds(go, n)], zsem))

            return carry

        def unused(j, carry):
            go = pl.multiple_of(j * EXPERT_BLOCK, EXPERT_BLOCK)
            act(pltpu.make_async_copy(zero_ref, xbuf_ref.at[pl.ds(go, EXPERT_BLOCK)], zsem))
            return carry

        lax.fori_loop(0, N_EXPERTS, tail, 0)
        lax.fori_loop(blkrange_ref[0], blkrange_ref[1], unused, 0)

    @pl.when(i == 0)
    def _zero_start():
        zero_ref[...] = jnp.zeros_like(zero_ref)
        zero_fill(lambda c: c.start())

    for s in range(TILES_PER_STEP):
        t = i * TILES_PER_STEP + s

        def make_copy(lo, go, size, s=s):
            return pltpu.make_async_copy(stage_ref.at[s, pl.ds(lo, size)], xbuf_ref.at[pl.ds(go, size)], sems.at[s])

        @pl.when(i > 0)
        def _drain():
            _wait_rows(make_copy, tot_ref[t - TILES_PER_STEP])

        rows = slice(s * TOK_TILE, (s + 1) * TOK_TILE)
        perm = _slot_matrix(lpos_ref[s]).astype(BF16)
        stage_ref[s] = _dot(perm, xn_ref[rows, :])
        _copy_runs(t, cnt_ref, off_ref, goff_ref, make_copy)

        @pl.when(i == last)
        def _finish():
            _wait_rows(make_copy, tot_ref[t])

    @pl.when(i == last)
    def _zero_finish():
        zero_fill(lambda c: c.wait())


def _dispatch(cnt, off, goff, tot, zoff, zlen, blkrange, xn, lpos, n_rows):
    n = xn.shape[0]
    rows = TILES_PER_STEP * TOK_TILE
    return pl.pallas_call(
        _dispatch_body,
        grid_spec=pltpu.PrefetchScalarGridSpec(
            num_scalar_prefetch=7,
            grid=(n // rows,),
            in_specs=[
                pl.BlockSpec((rows, D_MODEL), lambda i, *_: (i, 0)),
                pl.BlockSpec((TILES_PER_STEP, SUBLANES, TOK_TILE), lambda i, *_: (i, 0, 0)),
            ],
            out_specs=pl.BlockSpec(memory_space=pl.ANY),
            scratch_shapes=[
                pltpu.VMEM((TILES_PER_STEP, STAGE_ROWS, D_MODEL), F32),
                pltpu.VMEM((EXPERT_BLOCK, D_MODEL), F32),
                pltpu.SemaphoreType.DMA((TILES_PER_STEP,)),
                pltpu.SemaphoreType.DMA(()),
            ],
        ),
        out_shape=jax.ShapeDtypeStruct((n_rows, D_MODEL), F32),
        compiler_params=_params("arbitrary"),
        name="moe_dispatch",
    )(cnt, off, goff, tot, zoff, zlen, blkrange, xn, lpos)


CAST_ROWS = 128
EXPERT_ROW_STEP = 64


def _experts_body(blk_e_ref, first_ref, slot_ref, next_e_ref, parts_ref, nused_ref,
                  x_ref, bgu_ref, bd_ref, wgu_hbm, wd_hbm, y_ref,
                  wgu_f32, wd_f32, wgu_bf, wd_bf, sem_gu, sem_d):
    j = pl.program_id(0)

    def weight_copies(e, slot):
        return (pltpu.make_async_copy(wgu_hbm.at[e], wgu_f32.at[slot], sem_gu.at[slot]),
                pltpu.make_async_copy(wd_hbm.at[e], wd_f32.at[slot], sem_d.at[slot]))

    @pl.when(j < nused_ref[0])
    def _():
        e = blk_e_ref[j]
        slot = slot_ref[j]

        @pl.when(first_ref[j] == 1)
        def _new_expert():
            @pl.when(j == 0)
            def _():
                for c in weight_copies(e, slot):
                    c.start()

            for c in weight_copies(e, slot):
                c.wait()
            nxt = next_e_ref[j]

            @pl.when(nxt >= 0)
            def _():
                for c in weight_copies(nxt, 1 - slot):
                    c.start(priority=1)

            def cast(src, dst):
                def rows(r, carry):
                    sl = pl.ds(pl.multiple_of(r * CAST_ROWS, CAST_ROWS), CAST_ROWS)
                    dst[sl, :] = src[slot, sl, :].astype(BF16)
                    return carry
                lax.fori_loop(0, src.shape[1] // CAST_ROWS, rows, 0)

            cast(wgu_f32, wgu_bf)
            cast(wd_f32, wd_bf)

        def ffn(rows):
            x = x_ref[0:rows, :].astype(BF16)
            gu = _dot(x, wgu_bf[...]) + bgu_ref[0]
            gate = jnp.minimum(gu[:, :D_FF], SWIGLU_LIMIT)
            up = jnp.clip(gu[:, D_FF:], -SWIGLU_LIMIT, SWIGLU_LIMIT)
            act = (up + 1.0) * gate * jax.nn.sigmoid(SWIGLU_ALPHA * gate)
            y_ref[0:rows, :] = _dot(act.astype(BF16), wd_bf[...]) + bd_ref[0]
            if rows < EXPERT_BLOCK:
                y_ref[rows:, :] = jnp.zeros((EXPERT_BLOCK - rows, D_MODEL), F32)

        for parts in range(1, EXPERT_BLOCK // EXPERT_ROW_STEP + 1):
            pl.when(parts_ref[j] == parts)(functools.partial(ffn, parts * EXPERT_ROW_STEP))


def _experts(blk_e, first, slot, next_e, parts, nused, xbuf, wgu, bgu, wd, bd):
    n_rows = xbuf.shape[0]
    nblk = n_rows // EXPERT_BLOCK

    def rowblk(j, be, fi, sl, ne, pa, nu):
        return (jnp.minimum(j, nu[0] - 1), 0)

    def expert(j, be, fi, sl, ne, pa, nu):
        return (be[jnp.minimum(j, nu[0] - 1)], 0, 0)

    return pl.pallas_call(
        _experts_body,
        grid_spec=pltpu.PrefetchScalarGridSpec(
            num_scalar_prefetch=6,
            grid=(nblk,),
            in_specs=[
                pl.BlockSpec((EXPERT_BLOCK, D_MODEL), rowblk),
                pl.BlockSpec((1, 1, 2 * D_FF), expert),
                pl.BlockSpec((1, 1, D_MODEL), expert),
                pl.BlockSpec(memory_space=pl.ANY),
                pl.BlockSpec(memory_space=pl.ANY),
            ],
            out_specs=pl.BlockSpec((EXPERT_BLOCK, D_MODEL), rowblk),
            scratch_shapes=[
                pltpu.VMEM((2, D_MODEL, 2 * D_FF), F32),
                pltpu.VMEM((2, D_FF, D_MODEL), F32),
                pltpu.VMEM((D_MODEL, 2 * D_FF), BF16),
                pltpu.VMEM((D_FF, D_MODEL), BF16),
                pltpu.SemaphoreType.DMA((2,)),
                pltpu.SemaphoreType.DMA((2,)),
            ],
        ),
        out_shape=jax.ShapeDtypeStruct((n_rows, D_MODEL), F32),
        input_output_aliases={6: 0},
        compiler_params=_params("arbitrary"),
        name="moe_experts",
    )(blk_e, first, slot, next_e, parts, nused, xbuf, bgu, bd, wgu, wd)


def _combine_body(cnt_ref, off_ref, goff_ref, tot_ref, ybuf_ref, lpos_ref, tw_ref, h1_ref, fg_ref, out_ref,
                  stage_ref, sems):
    i = pl.program_id(0)
    n_tiles = pl.num_programs(0) * TILES_PER_STEP

    def copier(s):
        def make_copy(lo, go, size):
            return pltpu.make_async_copy(ybuf_ref.at[pl.ds(go, size)], stage_ref.at[s, pl.ds(lo, size)], sems.at[s])
        return make_copy

    ahead = TILES_PER_STEP - 1

    @pl.when(i == 0)
    def _init():
        stage_ref[...] = jnp.zeros_like(stage_ref)
        for s in range(ahead):
            _copy_runs(s, cnt_ref, off_ref, goff_ref, copier(s))

    for s in range(TILES_PER_STEP):
        t = i * TILES_PER_STEP + s

        @pl.when(t + ahead < n_tiles)
        def _prefetch():
            _copy_runs(t + ahead, cnt_ref, off_ref, goff_ref, copier((s + ahead) % TILES_PER_STEP))

        rows = slice(s * TOK_TILE, (s + 1) * TOK_TILE)
        w = _slot_matrix(lpos_ref[s], tw_ref[s]).astype(BF16)
        _wait_rows(copier(s), tot_ref[t])
        h2 = h1_ref[rows, :] + _dot_tn(w, stage_ref[s].astype(BF16))
        ms = jnp.mean(h2 * h2, axis=-1, keepdims=True)
        out_ref[rows, :] = h2 * lax.rsqrt(ms + RMS_EPS) * fg_ref[...]


def _combine(cnt, off, goff, tot, ybuf, lpos, tw, h1, final_g):
    n = h1.shape[0]
    rows = TILES_PER_STEP * TOK_TILE
    row = lambda i, *_: (i, 0)
    return pl.pallas_call(
        _combine_body,
        grid_spec=pltpu.PrefetchScalarGridSpec(
            num_scalar_prefetch=4,
            grid=(n // rows,),
            in_specs=[
                pl.BlockSpec(memory_space=pl.ANY),
                pl.BlockSpec((TILES_PER_STEP, SUBLANES, TOK_TILE), lambda i, *_: (i, 0, 0)),
                pl.BlockSpec((TILES_PER_STEP, SUBLANES, TOK_TILE), lambda i, *_: (i, 0, 0)),
                pl.BlockSpec((rows, D_MODEL), row),
                pl.BlockSpec((1, D_MODEL), lambda i, *_: (0, 0)),
            ],
            out_specs=pl.BlockSpec((rows, D_MODEL), row),
            scratch_shapes=[pltpu.VMEM((TILES_PER_STEP, STAGE_ROWS, D_MODEL), F32),
                            pltpu.SemaphoreType.DMA((TILES_PER_STEP,))],
        ),
        out_shape=jax.ShapeDtypeStruct((n, D_MODEL), F32),
        compiler_params=_params("arbitrary"),
        name="moe_combine",
    )(cnt, off, goff, tot, ybuf, lpos, tw, h1, final_g)


def _moe_layout(cnt_tiles):
    nt = cnt_tiles.shape[0]
    total = jnp.sum(cnt_tiles, axis=0)
    nblk_e = (total + EXPERT_BLOCK - 1) // EXPERT_BLOCK
    blk_end = jnp.cumsum(nblk_e)
    pstart = (blk_end - nblk_e) * EXPERT_BLOCK
    prefix = jnp.cumsum(cnt_tiles, axis=0) - cnt_tiles
    goff = pstart[None, :] + prefix
    off = jnp.cumsum(cnt_tiles, axis=1) - cnt_tiles
    tot = jnp.sum(cnt_tiles, axis=1)
    max_rows = nt * (TOK_TILE * TOP_K + N_EXPERTS * (RUN_ALIGN - 1)) + N_EXPERTS * (EXPERT_BLOCK - 1)
    max_blocks = (max_rows + EXPERT_BLOCK - 1) // EXPERT_BLOCK
    blk_id = jnp.arange(max_blocks, dtype=blk_end.dtype)
    blk_e = jnp.minimum(jnp.sum(blk_end[None, :] <= blk_id[:, None], axis=1), N_EXPERTS - 1).astype(jnp.int32)
    nused = blk_end[-1:].astype(jnp.int32)
    flat = lambda a: a.reshape(-1).astype(jnp.int32)
    zoff = pstart + total
    zlen = nblk_e * EXPERT_BLOCK - total
    blkrange = jnp.stack([nused[0], jnp.int32(max_blocks)])
    used = blk_id < nused[0]
    first = used & jnp.concatenate([jnp.ones((1,), bool), blk_e[1:] != blk_e[:-1]])
    slot = (jnp.cumsum(first.astype(jnp.int32)) - 1) % 2
    eid = jnp.arange(N_EXPERTS)
    later = (eid[None, :] > eid[:, None]) & (nblk_e > 0)[None, :]
    next_of = jnp.min(jnp.where(later, eid[None, :], N_EXPERTS), axis=1)
    next_of = jnp.where(next_of < N_EXPERTS, next_of, -1)
    is_e = blk_e[:, None] == eid[None, :]
    pick = lambda per_expert: jnp.sum(jnp.where(is_e, per_expert[None, :], 0), axis=1)
    rows_left = jnp.clip(pick(total) - (blk_id - pick(blk_end - nblk_e)) * EXPERT_BLOCK, 0, EXPERT_BLOCK)
    parts = (rows_left + EXPERT_ROW_STEP - 1) // EXPERT_ROW_STEP
    experts_plan = (blk_e, first.astype(jnp.int32), slot.astype(jnp.int32), pick(next_of).astype(jnp.int32),
                    parts.astype(jnp.int32), nused)
    dispatch_plan = (flat(cnt_tiles), flat(off), flat(goff), flat(tot))
    zero_plan = (flat(zoff), flat(zlen), blkrange.astype(jnp.int32))
    return dispatch_plan, zero_plan, experts_plan, max_blocks * EXPERT_BLOCK


def kernel(x, meta_tokens, attn_norm_g, w_in, na_rpb, na_norm_g, hgrn_lb_logits, hgrn_norm_g, w_out, ffn_norm_g,
           router_w, router_b, expert_w_gu, expert_b_gu, expert_w_down, expert_b_down, final_norm_g):
    B, T, D = x.shape
    x2d = x.reshape(B * T, D)
    w_in_b = w_in[0].astype(BF16)
    ng = attn_norm_g[0].reshape(1, D)
    lbl = hgrn_lb_logits[:, :, :].reshape(4, HG_WIDTH)

    att, hga, hgg = _inproj(x2d, ng, w_in_b, lbl, 1024)
    att_m, hga_m, hgg_m = _inproj(meta_tokens.astype(F32), ng, w_in_b, lbl, N_META)
    att_m = jnp.pad(att_m, ((0, META_PAD - N_META), (0, 0)))
    hga_m = jnp.pad(hga_m, ((0, HG_CHUNK - N_META), (0, 0)))
    hgg_m = jnp.pad(hgg_m, ((0, HG_CHUNK - N_META), (0, 0)))

    yna = _natten(att, att_m, _natten_bias_tables(na_rpb[0]), na_norm_g[0].reshape(1, NA_WIDTH), B, T)
    o_f, o_b = _hgrn(hga, hgg, hga_m, hgg_m, B, T)

    h1, xn, lpos, tw, cnt_tiles = _mix_route(
        yna.reshape(B * T, NA_WIDTH), o_f.reshape(B * T, HG_WIDTH), o_b.reshape(B * T, HG_WIDTH), hga, x2d,
        w_out[0].astype(BF16), hgrn_norm_g[0].reshape(1, HG_WIDTH), ffn_norm_g[0].reshape(1, D),
        router_w[0], router_b[0].reshape(1, N_EXPERTS))

    dispatch_plan, zero_plan, experts_plan, n_rows = _moe_layout(cnt_tiles[:, :, 0])
    xbuf = _dispatch(*dispatch_plan, *zero_plan, xn, lpos, n_rows)
    ybuf = _experts(*experts_plan, xbuf, expert_w_gu[0], expert_b_gu[0][:, None, :],
                    expert_w_down[0], expert_b_down[0][:, None, :])
    out = _combine(*dispatch_plan, ybuf, lpos, tw, h1, final_norm_g.reshape(1, D))
    return out.reshape(B, T, D)
```

```python
import functools

import numpy as np
import jax
import jax.numpy as jnp
from jax import lax
from jax.experimental import pallas as pl
from jax.experimental.pallas import tpu as pltpu

F32 = jnp.float32
BF16 = jnp.bfloat16

D_MODEL = 1024
N_META = 16
GRID_W = 64
NA_WIDTH = 512
NA_HEAD_DIM = 64
NA_HEADS = 8
NA_KH = 8
NA_KW = 16
HG_WIDTH = 512
HG_HEAD_DIM = 128
HG_HEADS = 4
HG_CHUNK = 64
IN_COLS = 3 * NA_WIDTH + 5 * HG_WIDTH
N_EXPERTS = 32
TOP_K = 4
D_FF = 1024
SWIGLU_LIMIT = 7.0
SWIGLU_ALPHA = 1.702
RMS_EPS = 1e-6
NEG_BIG = -1e30
LOG2_E = 1.4426950408889634

LANES = 128
VMEM_LIMIT_BYTES = 56 * 1024 * 1024

TOK_TILE = 256
EXPERT_BLOCK = 512
SUBLANES = 8
RUN_ALIGN = SUBLANES
STAGE_ROWS = TOK_TILE * TOP_K + N_EXPERTS * RUN_ALIGN
N_HG_LEVELS = 6


def _dot(a, b):
    return jnp.dot(a, b, preferred_element_type=F32)


def _dot_nt(a, b):
    return lax.dot_general(a, b, (((1,), (1,)), ((), ())), preferred_element_type=F32)


def _dot_tn(a, b):
    return lax.dot_general(a, b, (((0,), (0,)), ((), ())), preferred_element_type=F32)


def _split_bf16(x):
    hi = x.astype(BF16)
    lo = (x - hi.astype(F32)).astype(BF16)
    return hi, lo


def _pack_rows(rows, n_rows):
    idx = lax.broadcasted_iota(jnp.int32, (n_rows, rows[0].shape[1]), 0)
    out = jnp.zeros(idx.shape, rows[0].dtype)
    for k, r in enumerate(rows):
        out = jnp.where(idx == k, r, out)
    return out


def _params(*sem):
    return pltpu.CompilerParams(dimension_semantics=sem, vmem_limit_bytes=VMEM_LIMIT_BYTES)


def _inproj_body(x_ref, ng_ref, w_ref, lbl_ref, att_ref, hga_ref, hgg_ref):
    x = x_ref[...]
    ms = jnp.mean(x * x, axis=-1, keepdims=True)
    n = (x * lax.rsqrt(ms + RMS_EPS) * ng_ref[...]).astype(BF16)

    def proj(lo, hi):
        return _dot(n, w_ref[:, lo:hi])

    pq = proj(0, NA_WIDTH)
    att_ref[:, 0:NA_WIDTH] = (pq * (NA_HEAD_DIM ** -0.5 * LOG2_E)).astype(BF16)
    att_ref[:, NA_WIDTH:3 * NA_WIDTH] = proj(NA_WIDTH, 3 * NA_WIDTH).astype(BF16)

    base = 3 * NA_WIDTH
    W = HG_WIDTH
    qh = proj(base, base + W)
    hga_ref[:, 0:W] = (qh * jax.nn.sigmoid(qh)).astype(BF16)
    hga_ref[:, W:2 * W] = proj(base + W, base + 2 * W).astype(BF16)
    lbl = lbl_ref[...]
    for d in range(2):
        a0 = lbl[2 * d:2 * d + 1, :]
        a1 = lbl[2 * d + 1:2 * d + 2, :]
        m = jnp.maximum(a0, a1)
        e0 = jnp.exp(a0 - m)
        e1 = jnp.exp(a1 - m)
        lb = e0 / (e0 + e1)
        raw = proj(base + (2 + d) * W, base + (3 + d) * W)
        f = lb + (1.0 - lb) * jax.nn.sigmoid(raw)
        hgg_ref[:, d * W:(d + 1) * W] = jnp.log(f)
    gt = proj(base + 4 * W, base + 5 * W)
    hga_ref[:, 2 * W:3 * W] = (gt * jax.nn.sigmoid(gt)).astype(BF16)


def _inproj(x2d, norm_g, w_bf16, lb_logits4, tm):
    n = x2d.shape[0]
    return pl.pallas_call(
        _inproj_body,
        grid=(n // tm,),
        in_specs=[
            pl.BlockSpec((tm, D_MODEL), lambda i: (i, 0)),
            pl.BlockSpec((1, D_MODEL), lambda i: (0, 0)),
            pl.BlockSpec((D_MODEL, IN_COLS), lambda i: (0, 0)),
            pl.BlockSpec((4, HG_WIDTH), lambda i: (0, 0)),
        ],
        out_specs=[
            pl.BlockSpec((tm, 3 * NA_WIDTH), lambda i: (i, 0)),
            pl.BlockSpec((tm, 3 * HG_WIDTH), lambda i: (i, 0)),
            pl.BlockSpec((tm, 2 * HG_WIDTH), lambda i: (i, 0)),
        ],
        out_shape=[
            jax.ShapeDtypeStruct((n, 3 * NA_WIDTH), BF16),
            jax.ShapeDtypeStruct((n, 3 * HG_WIDTH), BF16),
            jax.ShapeDtypeStruct((n, 2 * HG_WIDTH), F32),
        ],
        compiler_params=_params("arbitrary"),
        name="inproj",
    )(x2d, norm_g, w_bf16, lb_logits4)


N_WIN_KEYS = NA_KH * GRID_W
META_PAD = LANES


NA_GROUP = 4
GROUP_LANES = NA_GROUP * NA_HEAD_DIM


NA_ROWS_PER_STEP = 16


def _natten_row(rr, r, q_ref, k_ref, v_ref, km_ref, vm_ref, bias_ref, *, rows):
    rs = jnp.clip(r - NA_KH // 2, 0, rows - NA_KH)
    d0 = rs - r + (NA_KH - 1)
    start = pl.multiple_of(rs * GRID_W, GRID_W)
    lane_head = lax.broadcasted_iota(jnp.int32, (GRID_W, GROUP_LANES), 1) // NA_HEAD_DIM
    meta_col = lax.broadcasted_iota(jnp.int32, (1, META_PAD), 1)
    meta_bias = jnp.where(meta_col < N_META, 0.0, NEG_BIG).astype(F32)
    folded = []
    for g in range(NA_HEADS // NA_GROUP):
        sl = slice(g * GROUP_LANES, (g + 1) * GROUP_LANES)
        q4 = q_ref[0, rr * GRID_W:(rr + 1) * GRID_W, sl]
        zero = jnp.zeros_like(q4)
        qm = jnp.concatenate([jnp.where(lane_head == h, q4, zero) for h in range(NA_GROUP)], axis=0)
        kw = k_ref[0, pl.ds(start, N_WIN_KEYS), sl]
        vw = v_ref[0, pl.ds(start, N_WIN_KEYS), sl]
        hs = slice(g * NA_GROUP, (g + 1) * NA_GROUP)
        bias = jnp.concatenate(
            [bias_ref[d0 + 2 * j, hs].reshape(NA_GROUP * GRID_W, 2 * GRID_W) for j in range(NA_KH // 2)], axis=-1)
        s = _dot_nt(qm, kw) + bias
        sm = _dot_nt(qm, km_ref[:, sl]) + meta_bias
        m = jnp.maximum(jnp.max(s, axis=-1, keepdims=True), jnp.max(sm, axis=-1, keepdims=True))
        e = jnp.exp2(s - m)
        em = jnp.exp2(sm - m)
        den = jnp.sum(e, axis=-1, keepdims=True) + jnp.sum(em, axis=-1, keepdims=True)
        o = (_dot(e.astype(BF16), vw) + _dot(em.astype(BF16), vm_ref[:, sl])) * (1.0 / den)
        acc = jnp.where(lane_head == 0, o[0:GRID_W], 0.0)
        for h in range(1, NA_GROUP):
            acc = jnp.where(lane_head == h, o[h * GRID_W:(h + 1) * GRID_W], acc)
        folded.append(acc)
    return jnp.concatenate(folded, axis=-1)


def _natten_body(q_ref, k_ref, v_ref, km_ref, vm_ref, bias_ref, ng_ref, bd_ref, o_ref, *, rows):
    r0 = pl.program_id(1) * NA_ROWS_PER_STEP
    o2 = jnp.concatenate(
        [_natten_row(rr, r0 + rr, q_ref, k_ref, v_ref, km_ref, vm_ref, bias_ref, rows=rows)
         for rr in range(NA_ROWS_PER_STEP)], axis=0)
    sq_hi, sq_lo = _split_bf16(o2 * o2)
    ms = (_dot(sq_hi, bd_ref[...]) + _dot(sq_lo, bd_ref[...])) * (1.0 / NA_HEAD_DIM)
    o_ref[0] = (o2 * lax.rsqrt(ms + RMS_EPS) * ng_ref[...]).astype(o_ref.dtype)


def _natten(att, att_meta, bias_tabs, norm_g, batch, seq):
    rows = seq // GRID_W
    att3 = att.reshape(batch, seq, 3 * NA_WIDTH)
    head_of = np.arange(NA_WIDTH) // NA_HEAD_DIM
    same_head = jnp.asarray(head_of[:, None] == head_of[None, :], BF16)

    qrows = NA_ROWS_PER_STEP * GRID_W
    return pl.pallas_call(
        functools.partial(_natten_body, rows=rows),
        grid=(batch, rows // NA_ROWS_PER_STEP),
        in_specs=[
            pl.BlockSpec((1, qrows, NA_WIDTH), lambda b, r: (b, r, 0)),
            pl.BlockSpec((1, seq, NA_WIDTH), lambda b, r: (b, 0, 1)),
            pl.BlockSpec((1, seq, NA_WIDTH), lambda b, r: (b, 0, 2)),
            pl.BlockSpec((META_PAD, NA_WIDTH), lambda b, r: (0, 1)),
            pl.BlockSpec((META_PAD, NA_WIDTH), lambda b, r: (0, 2)),
            pl.BlockSpec(bias_tabs.shape, lambda b, r: (0, 0, 0, 0), pipeline_mode=pl.Buffered(1)),
            pl.BlockSpec((1, NA_WIDTH), lambda b, r: (0, 0)),
            pl.BlockSpec((NA_WIDTH, NA_WIDTH), lambda b, r: (0, 0)),
        ],
        out_specs=pl.BlockSpec((1, qrows, NA_WIDTH), lambda b, r: (b, r, 0)),
        out_shape=jax.ShapeDtypeStruct((batch, seq, NA_WIDTH), BF16),
        compiler_params=_params("arbitrary", "arbitrary"),
        name="natten",
    )(att3, att3, att3, att_meta, att_meta, bias_tabs, norm_g, same_head)


def _natten_bias_tables(rpb):
    c = np.arange(GRID_W)[:, None]
    kc = np.arange(GRID_W)[None, :]
    cs = np.clip(c - NA_KW // 2, 0, GRID_W - NA_KW)
    valid = (kc >= cs) & (kc < cs + NA_KW)
    dc = kc - c + (NA_KW - 1)
    pick = np.asarray(dc[None] == np.arange(2 * NA_KW - 1)[:, None, None], np.float32)
    t1 = jnp.einsum("hdj,jck->hdck", rpb.astype(F32) * LOG2_E, pick, precision=lax.Precision.HIGHEST)
    t1 = jnp.where(valid[None, None], t1, NEG_BIG)
    pairs = jnp.concatenate([t1[:, :-1], t1[:, 1:]], axis=-1)
    return pairs.transpose(1, 0, 2, 3)


HG_CHUNKS_PER_STEP = 16
HG_CHUNKS_PER_GROUP = 1
ROW_B, ROW_LV = 0, 1
N_MXU_LEVELS = 2
N_EXP_BLOCKS = 1 + N_MXU_LEVELS


def _hgrn_constants():
    C = HG_CHUNK
    t = np.arange(C)
    u = t[None, :]
    mats = [u <= t[:, None]]
    masks = [np.eye(C, dtype=bool)]
    for lv in range(N_HG_LEVELS):
        m = 1 << lv
        blk = t // (2 * m)
        upper = (t // m) % 2 == 1
        p = blk * 2 * m + m - 1
        q_rows = upper[:, None] & (u > p[:, None]) & (u <= t[:, None])
        k_rows = (~upper)[:, None] & (u > t[:, None]) & (u <= p[:, None])
        if lv < N_MXU_LEVELS:
            mats.append(q_rows | k_rows)
        masks.append((blk[:, None] == blk[None, :]) & upper[:, None] & (~upper)[None, :])
    fwd = np.concatenate(mats, axis=0).astype(np.float32)
    fwd_mask = np.stack(masks).astype(np.float32)
    blocks = fwd.reshape(N_EXP_BLOCKS, C, C)
    bwd = blocks[:, ::-1, ::-1].reshape(N_EXP_BLOCKS * C, C)
    bwd_mask = fwd_mask[:, ::-1, ::-1]
    both = np.stack([fwd, bwd])
    both = np.concatenate([both, both], axis=-1)
    both_mask = np.stack([fwd_mask, bwd_mask])
    both_mask = np.concatenate([both_mask, both_mask], axis=-1)
    return (jnp.asarray(both, BF16), jnp.asarray(both_mask, F32))


def _hgrn_exponents(mat, g):
    g_hi, g_lo = _split_bf16(g)
    return _dot(mat, jnp.concatenate([g_hi, g_lo], axis=0))


def _hgrn_init(meta_a_ref, meta_g_ref, mat_ref, stf_ref, stb_ref):
    C, W = HG_CHUNK, HG_WIDTH
    v = meta_a_ref[:, W:2 * W]
    prefix = mat_ref[0, ROW_B * C:(ROW_B + 1) * C, :]
    for d, st_ref in ((0, stf_ref), (1, stb_ref)):
        g = meta_g_ref[:, d * W:(d + 1) * W]
        b = _hgrn_exponents(prefix, g)
        kb = ((1.0 - jnp.exp(g)) * jnp.exp(b[C - 1:C] - b)).astype(BF16)
        for h in range(HG_HEADS):
            sl = slice(h * HG_HEAD_DIM, (h + 1) * HG_HEAD_DIM)
            st_ref[h] = _dot_tn(v[:, sl], kb[:, sl])


def _hgrn_group_tasks(j0):
    js = range(j0, j0 + HG_CHUNKS_PER_GROUP)
    return ([(0, j * HG_CHUNK) for j in js] +
            [(1, (HG_CHUNKS_PER_STEP - 1 - j) * HG_CHUNK) for j in js])


def _hgrn_tasks(tasks, dir_refs, states, mat_ref, mask_ref):
    C, W = HG_CHUNK, HG_WIDTH
    heads = [slice(h * HG_HEAD_DIM, (h + 1) * HG_HEAD_DIM) for h in range(HG_HEADS)]
    qs, ks, vs, ex_bs, ex_lvs, ex_ends = {}, {}, {}, {}, {}, {}
    for d, r0 in tasks:
        q_ref, v_ref, g_ref, total_row = dir_refs[d][0], dir_refs[d][1], dir_refs[d][2], dir_refs[d][3]
        g = g_ref[0, r0:r0 + C, :]
        sums = _hgrn_exponents(mat_ref[d], g)
        b = sums[0:C]
        ex_bs[d, r0] = jnp.exp(b)
        level_sums = [sums[(ROW_LV + lv) * C:(ROW_LV + lv + 1) * C] for lv in range(N_MXU_LEVELS)]
        for lv in range(N_MXU_LEVELS, N_HG_LEVELS):
            m = 1 << lv
            b3 = b.reshape(C // (2 * m), 2 * m, W)
            row = lax.broadcasted_iota(jnp.int32, b3.shape, 1)
            pivot, is_query = (m - 1, row >= m) if d == 0 else (m, row < m)
            diff = b3 - b3[:, pivot:pivot + 1, :]
            level_sums.append(jnp.where(is_query, diff, -diff).reshape(C, W))
        ex_lvs[d, r0] = [jnp.exp(s).astype(BF16) for s in level_sums]
        ex_ends[d, r0] = jnp.exp(sums[total_row:total_row + 1] - sums[0:C])
        qs[d, r0] = q_ref[0, r0:r0 + C, :]
        ks[d, r0] = 1.0 - jnp.exp(g)
        vs[d, r0] = v_ref[0, r0:r0 + C, :]

    D2 = 2 * HG_HEAD_DIM
    pairs = [slice(p * D2, (p + 1) * D2) for p in range(HG_HEADS // 2)]

    def block_diag(x):
        zero = jnp.zeros((C, HG_HEAD_DIM), x.dtype)
        return jnp.concatenate([jnp.concatenate([x[:, :HG_HEAD_DIM], zero], axis=1),
                                jnp.concatenate([zero, x[:, HG_HEAD_DIM:]], axis=1)], axis=0)

    prods = {}
    for task in tasks:
        for p, sl in enumerate(pairs):
            q2, k2 = qs[task][:, sl], ks[task][:, sl].astype(BF16)
            ps = [_dot_nt(q2, block_diag(k2))]
            for lv in range(N_HG_LEVELS):
                scale = ex_lvs[task][lv][:, sl]
                ps.append(_dot_nt(q2 * scale, block_diag(k2 * scale)))
            prods[task, p] = ps
    attn, qbs, kbs = {}, {}, {}
    for task in tasks:
        d = task[0]
        for p, sl in enumerate(pairs):
            a = prods[task, p][0] * mask_ref[d, 0]
            for lv in range(N_HG_LEVELS):
                a = a + prods[task, p][1 + lv] * mask_ref[d, 1 + lv]
            o2 = _dot(a.astype(BF16), block_diag(vs[task][:, sl]))
            attn[task, 2 * p] = o2[:, :HG_HEAD_DIM]
            attn[task, 2 * p + 1] = o2[:, HG_HEAD_DIM:]
        for h, sl in enumerate(heads):
            qbs[task, h] = (qs[task][:, sl].astype(F32) * ex_bs[task][:, sl]).astype(BF16)
            kbs[task, h] = _dot_tn(vs[task][:, sl], (ks[task][:, sl] * ex_ends[task][:, sl]).astype(BF16))
    for task in tasks:
        d, r0 = task
        total_row, o_ref = dir_refs[d][3], dir_refs[d][4]
        outs = []
        for h, sl in enumerate(heads):
            st = states[d, h]
            outs.append(attn[task, h] + _dot_nt(qbs[task, h], st.astype(BF16)))
            decay = ex_bs[task][total_row:total_row + 1, sl]
            states[d, h] = decay * st + kbs[task, h]
        o_ref[0, r0:r0 + C, :] = jnp.concatenate(outs, axis=-1).astype(o_ref.dtype)


def _hgrn_body(hqf_ref, hvf_ref, hgf_ref, hqb_ref, hvb_ref, hgb_ref, meta_a_ref, meta_g_ref, mat_ref, mask_ref,
               of_ref, ob_ref, stf_ref, stb_ref):
    @pl.when(pl.program_id(1) == 0)
    def _init():
        _hgrn_init(meta_a_ref, meta_g_ref, mat_ref, stf_ref, stb_ref)

    dir_refs = ((hqf_ref, hvf_ref, hgf_ref, HG_CHUNK - 1, of_ref), (hqb_ref, hvb_ref, hgb_ref, 0, ob_ref))
    st_refs = (stf_ref, stb_ref)
    states = {(d, h): st_refs[d][h] for d in range(2) for h in range(HG_HEADS)}
    for j0 in range(0, HG_CHUNKS_PER_STEP, HG_CHUNKS_PER_GROUP):
        _hgrn_tasks(_hgrn_group_tasks(j0), dir_refs, states, mat_ref, mask_ref)
    for d in range(2):
        for h in range(HG_HEADS):
            st_refs[d][h] = states[d, h]


def _hgrn(hga, hgg, hga_meta_pad, hgg_meta_pad, batch, seq):
    steps = seq // (HG_CHUNK * HG_CHUNKS_PER_STEP)
    a3 = hga.reshape(batch, seq, 3 * HG_WIDTH)
    g3 = hgg.reshape(batch, seq, 2 * HG_WIDTH)
    mats, masks = _hgrn_constants()
    hblk = (1, HG_CHUNK * HG_CHUNKS_PER_STEP, HG_WIDTH)

    def fwd(col):
        return pl.BlockSpec(hblk, lambda b, s: (b, s, col))

    def bwd(col):
        return pl.BlockSpec(hblk, lambda b, s: (b, steps - 1 - s, col))

    return pl.pallas_call(
        _hgrn_body,
        grid=(batch, steps),
        in_specs=[
            fwd(0), fwd(1), fwd(0),
            bwd(0), bwd(1), bwd(1),
            pl.BlockSpec((HG_CHUNK, 3 * HG_WIDTH), lambda b, s: (0, 0)),
            pl.BlockSpec((HG_CHUNK, 2 * HG_WIDTH), lambda b, s: (0, 0)),
            pl.BlockSpec(mats.shape, lambda b, s: (0, 0, 0)),
            pl.BlockSpec(masks.shape, lambda b, s: (0, 0, 0, 0)),
        ],
        out_specs=[fwd(0), bwd(0)],
        out_shape=[jax.ShapeDtypeStruct((batch, seq, HG_WIDTH), BF16)] * 2,
        scratch_shapes=[pltpu.VMEM((HG_HEADS, HG_HEAD_DIM, HG_HEAD_DIM), F32)] * 2,
        compiler_params=_params("arbitrary", "arbitrary"),
        name="hgrn",
    )(a3, a3, g3, a3, a3, g3, hga_meta_pad, hgg_meta_pad, mats, masks)


MIX_TILES_PER_STEP = 4


def _mix_route_body(yna_ref, of_ref, ob_ref, gate_ref, x_ref, wout_ref, hgn_ref, ffg_ref,
                    rwh_ref, rwl_ref, rb_ref, tri_ref, lowe_ref,
                    h1_ref, xn_ref, lpos_ref, tw_ref, cnt_ref):
    T, E = TOK_TILE, N_EXPERTS
    tiles = [slice(s * T, (s + 1) * T) for s in range(MIX_TILES_PER_STEP)]
    x_his, x_los = [], []
    for rows in tiles:
        o = of_ref[rows, :].astype(F32) + ob_ref[rows, :].astype(F32)
        parts = []
        for h in range(HG_HEADS):
            seg = o[:, h * HG_HEAD_DIM:(h + 1) * HG_HEAD_DIM]
            ms = jnp.mean(seg * seg, axis=-1, keepdims=True)
            parts.append(seg * lax.rsqrt(ms + RMS_EPS))
        yhg = jnp.concatenate(parts, axis=-1) * hgn_ref[...] * gate_ref[rows, :]
        mix = _dot(yna_ref[rows, :], wout_ref[0:NA_WIDTH, :]) + _dot(yhg.astype(BF16), wout_ref[NA_WIDTH:, :])
        h1 = x_ref[rows, :] + mix
        h1_ref[rows, :] = h1
        ms = jnp.mean(h1 * h1, axis=-1, keepdims=True)
        xn = h1 * lax.rsqrt(ms + RMS_EPS) * ffg_ref[...]
        x_hi, x_lo = _split_bf16(xn)
        xn_ref[rows, :] = x_hi
        x_his.append(x_hi)
        x_los.append(x_lo)

    curs = [(_dot_nt(rwh_ref[...], x_hi) + _dot_nt(rwh_ref[...], x_lo) + _dot_nt(rwl_ref[...], x_hi)) + rb_ref[...]
            for x_hi, x_lo in zip(x_his, x_los)]
    row = lax.broadcasted_iota(jnp.int32, (E, T), 0).astype(F32)
    sels = [[] for _ in tiles]
    vals = [[] for _ in tiles]
    for _ in range(TOP_K):
        for s in range(len(tiles)):
            m = jnp.max(curs[s], axis=0, keepdims=True)
            first = jnp.min(jnp.where(curs[s] == m, row, float(E)), axis=0, keepdims=True)
            sel = row == first
            sels[s].append(sel)
            vals[s].append(m)
            curs[s] = jnp.where(sel, -jnp.inf, curs[s])
    for s in range(len(tiles)):
        es = [jnp.exp(vk - vals[s][0]) for vk in vals[s]]
        den = es[0] + es[1] + es[2] + es[3]
        tw_ref[s] = _pack_rows([e / den for e in es], SUBLANES)

        onehot = jnp.zeros((E, T), F32)
        for sel in sels[s]:
            onehot = onehot + jnp.where(sel, 1.0, 0.0)
        lrank = _dot(onehot.astype(BF16), tri_ref[...])
        cnt = jnp.sum(onehot, axis=1, keepdims=True)
        cnt = jnp.floor((cnt + (RUN_ALIGN - 1)) * (1.0 / RUN_ALIGN)) * RUN_ALIGN
        off = _dot(lowe_ref[...], jnp.broadcast_to(cnt, (E, LANES)).astype(BF16))
        base = jnp.concatenate([off] * (T // LANES), axis=1) + lrank
        lpos_ref[s] = _pack_rows(
            [jnp.sum(jnp.where(sel, base, 0.0), axis=0, keepdims=True) for sel in sels[s]],
            SUBLANES).astype(jnp.int32)
        cnt_ref[s] = cnt.astype(jnp.int32)


def _mix_route(yna, o_f, o_b, hg, x2d, wout_bf16, hgn, ffg, router_w, router_b):
    n = x2d.shape[0]
    T = TOK_TILE
    nt = n // T
    rw_t = router_w.T
    rwh = rw_t.astype(BF16)
    rwl = (rw_t - rwh.astype(F32)).astype(BF16)
    router_b = router_b.reshape(N_EXPERTS, 1)
    tri = jnp.asarray(np.triu(np.ones((T, T), np.float32), 1), BF16)
    lowe = jnp.asarray(np.tril(np.ones((N_EXPERTS, N_EXPERTS), np.float32), -1), BF16)
    row = lambda i: (i, 0)
    const = lambda i: (0, 0)
    tile3 = lambda i: (i, 0, 0)
    K = MIX_TILES_PER_STEP
    R = K * T
    return pl.pallas_call(
        _mix_route_body,
        grid=(nt // K,),
        in_specs=[
            pl.BlockSpec((R, NA_WIDTH), row),
            pl.BlockSpec((R, HG_WIDTH), row),
            pl.BlockSpec((R, HG_WIDTH), row),
            pl.BlockSpec((R, HG_WIDTH), lambda i: (i, 2)),
            pl.BlockSpec((R, D_MODEL), row),
            pl.BlockSpec((D_MODEL, D_MODEL), const),
            pl.BlockSpec((1, HG_WIDTH), const),
            pl.BlockSpec((1, D_MODEL), const),
            pl.BlockSpec((N_EXPERTS, D_MODEL), const),
            pl.BlockSpec((N_EXPERTS, D_MODEL), const),
            pl.BlockSpec((N_EXPERTS, 1), const),
            pl.BlockSpec((T, T), const),
            pl.BlockSpec((N_EXPERTS, N_EXPERTS), const),
        ],
        out_specs=[
            pl.BlockSpec((R, D_MODEL), row),
            pl.BlockSpec((R, D_MODEL), row),
            pl.BlockSpec((K, SUBLANES, T), tile3),
            pl.BlockSpec((K, SUBLANES, T), tile3),
            pl.BlockSpec((K, N_EXPERTS, 1), tile3),
        ],
        out_shape=[
            jax.ShapeDtypeStruct((n, D_MODEL), F32),
            jax.ShapeDtypeStruct((n, D_MODEL), BF16),
            jax.ShapeDtypeStruct((nt, SUBLANES, T), jnp.int32),
            jax.ShapeDtypeStruct((nt, SUBLANES, T), F32),
            jax.ShapeDtypeStruct((nt, N_EXPERTS, 1), jnp.int32),
        ],
        compiler_params=_params("arbitrary"),
        name="mix_route",
    )(yna, o_f, o_b, hg, x2d, wout_bf16, hgn, ffg, rwh, rwl, router_b, tri, lowe)


TILES_PER_STEP = 4


def _wait_rows(make_copy, rows):
    make_copy(0, 0, pl.multiple_of(rows, RUN_ALIGN)).wait()


def _copy_runs(i, cnt_ref, off_ref, goff_ref, make_copy):
    def per_expert(e, carry):
        n = pl.multiple_of(cnt_ref[i * N_EXPERTS + e], RUN_ALIGN)
        lo = pl.multiple_of(off_ref[i * N_EXPERTS + e], RUN_ALIGN)
        go = pl.multiple_of(goff_ref[i * N_EXPERTS + e], RUN_ALIGN)

        @pl.when(n > 0)
        def _start():
            make_copy(lo, go, n).start()

        return carry

    lax.fori_loop(0, N_EXPERTS, per_expert, 0)


def _slot_matrix(lpos, weights=None):
    row = lax.broadcasted_iota(jnp.int32, (STAGE_ROWS, TOK_TILE), 0)
    acc = jnp.zeros((STAGE_ROWS, TOK_TILE), F32)
    for k in range(TOP_K):
        hit = lpos[k:k + 1, :] == row
        val = 1.0 if weights is None else weights[k:k + 1, :]
        acc = jnp.where(hit, val, acc)
    return acc


def _dispatch_body(cnt_ref, off_ref, goff_ref, tot_ref, zoff_ref, zlen_ref, blkrange_ref,
                   xn_ref, lpos_ref, xbuf_ref, stage_ref, zero_ref, sems, zsem):
    i = pl.program_id(0)
    last = pl.num_programs(0) - 1

    def zero_fill(act):
        def tail(e, carry):
            n = pl.multiple_of(zlen_ref[e], RUN_ALIGN)

            @pl.when(n > 0)
            def _():
                go = pl.multiple_of(zoff_ref[e], RUN_ALIGN)
                act(pltpu.make_async_copy(zero_ref.at[pl.ds(0, n)], xbuf_ref.at[pl.ds(go, n)], zsem))

            return carry

        def unused(j, carry):
            go = pl.multiple_of(j * EXPERT_BLOCK, EXPERT_BLOCK)
            act(pltpu.make_async_copy(zero_ref, xbuf_ref.at[pl.ds(go, EXPERT_BLOCK)], zsem))
            return carry

        lax.fori_loop(0, N_EXPERTS, tail, 0)
        lax.fori_loop(blkrange_ref[0], blkrange_ref[1], unused, 0)

    @pl.when(i == 0)
    def _zero_start():
        zero_ref[...] = jnp.zeros_like(zero_ref)
        zero_fill(lambda c: c.start())

    for s in range(TILES_PER_STEP):
        t = i * TILES_PER_STEP + s

        def make_copy(lo, go, size, s=s):
            return pltpu.make_async_copy(stage_ref.at[s, pl.ds(lo, size)], xbuf_ref.at[pl.ds(go, size)], sems.at[s])

        @pl.when(i > 0)
        def _drain():
            _wait_rows(make_copy, tot_ref[t - TILES_PER_STEP])

        rows = slice(s * TOK_TILE, (s + 1) * TOK_TILE)
        perm = _slot_matrix(lpos_ref[s]).astype(BF16)
        stage_ref[s] = _dot(perm, xn_ref[rows, :])
        _copy_runs(t, cnt_ref, off_ref, goff_ref, make_copy)

        @pl.when(i == last)
        def _finish():
            _wait_rows(make_copy, tot_ref[t])

    @pl.when(i == last)
    def _zero_finish():
        zero_fill(lambda c: c.wait())


def _dispatch(cnt, off, goff, tot, zoff, zlen, blkrange, xn, lpos, n_rows):
    n = xn.shape[0]
    rows = TILES_PER_STEP * TOK_TILE
    return pl.pallas_call(
        _dispatch_body,
        grid_spec=pltpu.PrefetchScalarGridSpec(
            num_scalar_prefetch=7,
            grid=(n // rows,),
            in_specs=[
                pl.BlockSpec((rows, D_MODEL), lambda i, *_: (i, 0)),
                pl.BlockSpec((TILES_PER_STEP, SUBLANES, TOK_TILE), lambda i, *_: (i, 0, 0)),
            ],
            out_specs=pl.BlockSpec(memory_space=pl.ANY),
            scratch_shapes=[
                pltpu.VMEM((TILES_PER_STEP, STAGE_ROWS, D_MODEL), F32),
                pltpu.VMEM((EXPERT_BLOCK, D_MODEL), F32),
                pltpu.SemaphoreType.DMA((TILES_PER_STEP,)),
                pltpu.SemaphoreType.DMA(()),
            ],
        ),
        out_shape=jax.ShapeDtypeStruct((n_rows, D_MODEL), F32),
        compiler_params=_params("arbitrary"),
        name="moe_dispatch",
    )(cnt, off, goff, tot, zoff, zlen, blkrange, xn, lpos)


CAST_ROWS = 128
EXPERT_ROW_STEP = 128


def _experts_body(blk_e_ref, first_ref, slot_ref, next_e_ref, parts_ref, nused_ref,
                  x_ref, bgu_ref, bd_ref, wgu_hbm, wd_hbm, y_ref,
                  wgu_f32, wd_f32, wgu_bf, wd_bf, sem_gu, sem_d):
    j = pl.program_id(0)

    def weight_copies(e, slot):
        return (pltpu.make_async_copy(wgu_hbm.at[e], wgu_f32.at[slot], sem_gu.at[slot]),
                pltpu.make_async_copy(wd_hbm.at[e], wd_f32.at[slot], sem_d.at[slot]))

    @pl.when(j < nused_ref[0])
    def _():
        e = blk_e_ref[j]
        slot = slot_ref[j]

        @pl.when(first_ref[j] == 1)
        def _new_expert():
            @pl.when(j == 0)
            def _():
                for c in weight_copies(e, slot):
                    c.start()

            for c in weight_copies(e, slot):
                c.wait()
            nxt = next_e_ref[j]

            @pl.when(nxt >= 0)
            def _():
                for c in weight_copies(nxt, 1 - slot):
                    c.start(priority=1)

            def cast(src, dst):
                def rows(r, carry):
                    sl = pl.ds(pl.multiple_of(r * CAST_ROWS, CAST_ROWS), CAST_ROWS)
                    dst[sl, :] = src[slot, sl, :].astype(BF16)
                    return carry
                lax.fori_loop(0, src.shape[1] // CAST_ROWS, rows, 0)

            cast(wgu_f32, wgu_bf)
            cast(wd_f32, wd_bf)

        def ffn(rows):
            x = x_ref[0:rows, :].astype(BF16)
            gu = _dot(x, wgu_bf[...]) + bgu_ref[0]
            gate = jnp.minimum(gu[:, :D_FF], SWIGLU_LIMIT)
            up = jnp.clip(gu[:, D_FF:], -SWIGLU_LIMIT, SWIGLU_LIMIT)
            act = (up + 1.0) * gate * jax.nn.sigmoid(SWIGLU_ALPHA * gate)
            y_ref[0:rows, :] = _dot(act.astype(BF16), wd_bf[...]) + bd_ref[0]
            if rows < EXPERT_BLOCK:
                y_ref[rows:, :] = jnp.zeros((EXPERT_BLOCK - rows, D_MODEL), F32)

        for parts in range(1, EXPERT_BLOCK // EXPERT_ROW_STEP + 1):
            pl.when(parts_ref[j] == parts)(functools.partial(ffn, parts * EXPERT_ROW_STEP))


def _experts(blk_e, first, slot, next_e, parts, nused, xbuf, wgu, bgu, wd, bd):
    n_rows = xbuf.shape[0]
    nblk = n_rows // EXPERT_BLOCK

    def rowblk(j, be, fi, sl, ne, pa, nu):
        return (jnp.minimum(j, nu[0] - 1), 0)

    def expert(j, be, fi, sl, ne, pa, nu):
        return (be[jnp.minimum(j, nu[0] - 1)], 0, 0)

    return pl.pallas_call(
        _experts_body,
        grid_spec=pltpu.PrefetchScalarGridSpec(
            num_scalar_prefetch=6,
            grid=(nblk,),
            in_specs=[
                pl.BlockSpec((EXPERT_BLOCK, D_MODEL), rowblk),
                pl.BlockSpec((1, 1, 2 * D_FF), expert),
                pl.BlockSpec((1, 1, D_MODEL), expert),
                pl.BlockSpec(memory_space=pl.ANY),
                pl.BlockSpec(memory_space=pl.ANY),
            ],
            out_specs=pl.BlockSpec((EXPERT_BLOCK, D_MODEL), rowblk),
            scratch_shapes=[
                pltpu.VMEM((2, D_MODEL, 2 * D_FF), F32),
                pltpu.VMEM((2, D_FF, D_MODEL), F32),
                pltpu.VMEM((D_MODEL, 2 * D_FF), BF16),
                pltpu.VMEM((D_FF, D_MODEL), BF16),
                pltpu.SemaphoreType.DMA((2,)),
                pltpu.SemaphoreType.DMA((2,)),
            ],
        ),
        out_shape=jax.ShapeDtypeStruct((n_rows, D_MODEL), F32),
        input_output_aliases={6: 0},
        compiler_params=_params("arbitrary"),
        name="moe_experts",
    )(blk_e, first, slot, next_e, parts, nused, xbuf, bgu, bd, wgu, wd)


def _combine_body(cnt_ref, off_ref, goff_ref, tot_ref, ybuf_ref, lpos_ref, tw_ref, h1_ref, fg_ref, out_ref,
                  stage_ref, sems):
    i = pl.program_id(0)
    n_tiles = pl.num_programs(0) * TILES_PER_STEP

    def copier(s):
        def make_copy(lo, go, size):
            return pltpu.make_async_copy(ybuf_ref.at[pl.ds(go, size)], stage_ref.at[s, pl.ds(lo, size)], sems.at[s])
        return make_copy

    ahead = TILES_PER_STEP - 1

    @pl.when(i == 0)
    def _init():
        stage_ref[...] = jnp.zeros_like(stage_ref)
        for s in range(ahead):
            _copy_runs(s, cnt_ref, off_ref, goff_ref, copier(s))

    for s in range(TILES_PER_STEP):
        t = i * TILES_PER_STEP + s

        @pl.when(t + ahead < n_tiles)
        def _prefetch():
            _copy_runs(t + ahead, cnt_ref, off_ref, goff_ref, copier((s + ahead) % TILES_PER_STEP))

        rows = slice(s * TOK_TILE, (s + 1) * TOK_TILE)
        w = _slot_matrix(lpos_ref[s], tw_ref[s]).astype(BF16)
        _wait_rows(copier(s), tot_ref[t])
        h2 = h1_ref[rows, :] + _dot_tn(w, stage_ref[s].astype(BF16))
        ms = jnp.mean(h2 * h2, axis=-1, keepdims=True)
        out_ref[rows, :] = h2 * lax.rsqrt(ms + RMS_EPS) * fg_ref[...]


def _combine(cnt, off, goff, tot, ybuf, lpos, tw, h1, final_g):
    n = h1.shape[0]
    rows = TILES_PER_STEP * TOK_TILE
    row = lambda i, *_: (i, 0)
    return pl.pallas_call(
        _combine_body,
        grid_spec=pltpu.PrefetchScalarGridSpec(
            num_scalar_prefetch=4,
            grid=(n // rows,),
            in_specs=[
                pl.BlockSpec(memory_space=pl.ANY),
                pl.BlockSpec((TILES_PER_STEP, SUBLANES, TOK_TILE), lambda i, *_: (i, 0, 0)),
                pl.BlockSpec((TILES_PER_STEP, SUBLANES, TOK_TILE), lambda i, *_: (i, 0, 0)),
                pl.BlockSpec((rows, D_MODEL), row),
                pl.BlockSpec((1, D_MODEL), lambda i, *_: (0, 0)),
            ],
            out_specs=pl.BlockSpec((rows, D_MODEL), row),
            scratch_shapes=[pltpu.VMEM((TILES_PER_STEP, STAGE_ROWS, D_MODEL), F32),
                            pltpu.SemaphoreType.DMA((TILES_PER_STEP,))],
        ),
        out_shape=jax.ShapeDtypeStruct((n, D_MODEL), F32),
        compiler_params=_params("arbitrary"),
        name="moe_combine",
    )(cnt, off, goff, tot, ybuf, lpos, tw, h1, final_g)


def _moe_layout(cnt_tiles):
    nt = cnt_tiles.shape[0]
    total = jnp.sum(cnt_tiles, axis=0)
    nblk_e = (total + EXPERT_BLOCK - 1) // EXPERT_BLOCK
    blk_end = jnp.cumsum(nblk_e)
    pstart = (blk_end - nblk_e) * EXPERT_BLOCK
    prefix = jnp.cumsum(cnt_tiles, axis=0) - cnt_tiles
    goff = pstart[None, :] + prefix
    off = jnp.cumsum(cnt_tiles, axis=1) - cnt_tiles
    tot = jnp.sum(cnt_tiles, axis=1)
    max_rows = nt * (TOK_TILE * TOP_K + N_EXPERTS * (RUN_ALIGN - 1)) + N_EXPERTS * (EXPERT_BLOCK - 1)
    max_blocks = (max_rows + EXPERT_BLOCK - 1) // EXPERT_BLOCK
    blk_id = jnp.arange(max_blocks, dtype=blk_end.dtype)
    blk_e = jnp.minimum(jnp.sum(blk_end[None, :] <= blk_id[:, None], axis=1), N_EXPERTS - 1).astype(jnp.int32)
    nused = blk_end[-1:].astype(jnp.int32)
    flat = lambda a: a.reshape(-1).astype(jnp.int32)
    zoff = pstart + total
    zlen = nblk_e * EXPERT_BLOCK - total
    blkrange = jnp.stack([nused[0], jnp.int32(max_blocks)])
    used = blk_id < nused[0]
    first = used & jnp.concatenate([jnp.ones((1,), bool), blk_e[1:] != blk_e[:-1]])
    slot = (jnp.cumsum(first.astype(jnp.int32)) - 1) % 2
    eid = jnp.arange(N_EXPERTS)
    later = (eid[None, :] > eid[:, None]) & (nblk_e > 0)[None, :]
    next_of = jnp.min(jnp.where(later, eid[None, :], N_EXPERTS), axis=1)
    next_of = jnp.where(next_of < N_EXPERTS, next_of, -1)
    is_e = blk_e[:, None] == eid[None, :]
    pick = lambda per_expert: jnp.sum(jnp.where(is_e, per_expert[None, :], 0), axis=1)
    rows_left = jnp.clip(pick(total) - (blk_id - pick(blk_end - nblk_e)) * EXPERT_BLOCK, 0, EXPERT_BLOCK)
    parts = (rows_left + EXPERT_ROW_STEP - 1) // EXPERT_ROW_STEP
    experts_plan = (blk_e, first.astype(jnp.int32), slot.astype(jnp.int32), pick(next_of).astype(jnp.int32),
                    parts.astype(jnp.int32), nused)
    dispatch_plan = (flat(cnt_tiles), flat(off), flat(goff), flat(tot))
    zero_plan = (flat(zoff), flat(zlen), blkrange.astype(jnp.int32))
    return dispatch_plan, zero_plan, experts_plan, max_blocks * EXPERT_BLOCK


def kernel(x, meta_tokens, attn_norm_g, w_in, na_rpb, na_norm_g, hgrn_lb_logits, hgrn_norm_g, w_out, ffn_norm_g,
           router_w, router_b, expert_w_gu, expert_b_gu, expert_w_down, expert_b_down, final_norm_g):
    B, T, D = x.shape
    x2d = x.reshape(B * T, D)
    w_in_b = w_in[0].astype(BF16)
    ng = attn_norm_g[0].reshape(1, D)
    lbl = hgrn_lb_logits[:, :, :].reshape(4, HG_WIDTH)

    att, hga, hgg = _inproj(x2d, ng, w_in_b, lbl, 512)
    att_m, hga_m, hgg_m = _inproj(meta_tokens.astype(F32), ng, w_in_b, lbl, N_META)
    att_m = jnp.pad(att_m, ((0, META_PAD - N_META), (0, 0)))
    hga_m = jnp.pad(hga_m, ((0, HG_CHUNK - N_META), (0, 0)))
    hgg_m = jnp.pad(hgg_m, ((0, HG_CHUNK - N_META), (0, 0)))

    yna = _natten(att, att_m, _natten_bias_tables(na_rpb[0]), na_norm_g[0].reshape(1, NA_WIDTH), B, T)
    o_f, o_b = _hgrn(hga, hgg, hga_m, hgg_m, B, T)

    h1, xn, lpos, tw, cnt_tiles = _mix_route(
        yna.reshape(B * T, NA_WIDTH), o_f.reshape(B * T, HG_WIDTH), o_b.reshape(B * T, HG_WIDTH), hga, x2d,
        w_out[0].astype(BF16), hgrn_norm_g[0].reshape(1, HG_WIDTH), ffn_norm_g[0].reshape(1, D),
        router_w[0], router_b[0].reshape(1, N_EXPERTS))

    dispatch_plan, zero_plan, experts_plan, n_rows = _moe_layout(cnt_tiles[:, :, 0])
    xbuf = _dispatch(*dispatch_plan, *zero_plan, xn, lpos, n_rows)
    ybuf = _experts(*experts_plan, xbuf, expert_w_gu[0], expert_b_gu[0][:, None, :],
                    expert_w_down[0], expert_b_down[0][:, None, :])
    out = _combine(*dispatch_plan, ybuf, lpos, tw, h1, final_norm_g.reshape(1, D))
    return out.reshape(B, T, D)
```

```python
import functools

import numpy as np
import jax
import jax.numpy as jnp
from jax import lax
from jax.experimental import pallas as pl
from jax.experimental.pallas import tpu as pltpu

F32 = jnp.float32
BF16 = jnp.bfloat16

D_MODEL = 1024
N_META = 16
GRID_W = 64
NA_WIDTH = 512
NA_HEAD_DIM = 64
NA_HEADS = 8
NA_KH = 8
NA_KW = 16
HG_WIDTH = 512
HG_HEAD_DIM = 128
HG_HEADS = 4
HG_CHUNK = 64
IN_COLS = 3 * NA_WIDTH + 5 * HG_WIDTH
N_EXPERTS = 32
TOP_K = 4
D_FF = 1024
SWIGLU_LIMIT = 7.0
SWIGLU_ALPHA = 1.702
RMS_EPS = 1e-6
NEG_BIG = -1e30
LOG2_E = 1.4426950408889634

LANES = 128
VMEM_LIMIT_BYTES = 56 * 1024 * 1024

TOK_TILE = 256
EXPERT_BLOCK = 512
SUBLANES = 8
RUN_ALIGN = SUBLANES
STAGE_ROWS = TOK_TILE * TOP_K + N_EXPERTS * RUN_ALIGN
N_HG_LEVELS = 6


def _dot(a, b):
    return jnp.dot(a, b, preferred_element_type=F32)


def _dot_nt(a, b):
    return lax.dot_general(a, b, (((1,), (1,)), ((), ())), preferred_element_type=F32)


def _dot_tn(a, b):
    return lax.dot_general(a, b, (((0,), (0,)), ((), ())), preferred_element_type=F32)


def _split_bf16(x):
    hi = x.astype(BF16)
    lo = (x - hi.astype(F32)).astype(BF16)
    return hi, lo


def _pack_rows(rows, n_rows):
    idx = lax.broadcasted_iota(jnp.int32, (n_rows, rows[0].shape[1]), 0)
    out = jnp.zeros(idx.shape, rows[0].dtype)
    for k, r in enumerate(rows):
        out = jnp.where(idx == k, r, out)
    return out


def _params(*sem):
    return pltpu.CompilerParams(dimension_semantics=sem, vmem_limit_bytes=VMEM_LIMIT_BYTES)


def _inproj_body(x_ref, ng_ref, w_ref, lbl_ref, att_ref, hga_ref, hgg_ref):
    x = x_ref[...]
    ms = jnp.mean(x * x, axis=-1, keepdims=True)
    n = (x * lax.rsqrt(ms + RMS_EPS) * ng_ref[...]).astype(BF16)

    def proj(lo, hi):
        return _dot(n, w_ref[:, lo:hi])

    pq = proj(0, NA_WIDTH)
    att_ref[:, 0:NA_WIDTH] = (pq * (NA_HEAD_DIM ** -0.5 * LOG2_E)).astype(BF16)
    att_ref[:, NA_WIDTH:3 * NA_WIDTH] = proj(NA_WIDTH, 3 * NA_WIDTH).astype(BF16)

    base = 3 * NA_WIDTH
    W = HG_WIDTH
    qh = proj(base, base + W)
    hga_ref[:, 0:W] = (qh * jax.nn.sigmoid(qh)).astype(BF16)
    hga_ref[:, W:2 * W] = proj(base + W, base + 2 * W).astype(BF16)
    lbl = lbl_ref[...]
    for d in range(2):
        a0 = lbl[2 * d:2 * d + 1, :]
        a1 = lbl[2 * d + 1:2 * d + 2, :]
        m = jnp.maximum(a0, a1)
        e0 = jnp.exp(a0 - m)
        e1 = jnp.exp(a1 - m)
        lb = e0 / (e0 + e1)
        raw = proj(base + (2 + d) * W, base + (3 + d) * W)
        f = lb + (1.0 - lb) * jax.nn.sigmoid(raw)
        hgg_ref[:, d * W:(d + 1) * W] = jnp.log(f)
    gt = proj(base + 4 * W, base + 5 * W)
    hga_ref[:, 2 * W:3 * W] = (gt * jax.nn.sigmoid(gt)).astype(BF16)


def _inproj(x2d, norm_g, w_bf16, lb_logits4, tm):
    n = x2d.shape[0]
    return pl.pallas_call(
        _inproj_body,
        grid=(n // tm,),
        in_specs=[
            pl.BlockSpec((tm, D_MODEL), lambda i: (i, 0)),
            pl.BlockSpec((1, D_MODEL), lambda i: (0, 0)),
            pl.BlockSpec((D_MODEL, IN_COLS), lambda i: (0, 0)),
            pl.BlockSpec((4, HG_WIDTH), lambda i: (0, 0)),
        ],
        out_specs=[
            pl.BlockSpec((tm, 3 * NA_WIDTH), lambda i: (i, 0)),
            pl.BlockSpec((tm, 3 * HG_WIDTH), lambda i: (i, 0)),
            pl.BlockSpec((tm, 2 * HG_WIDTH), lambda i: (i, 0)),
        ],
        out_shape=[
            jax.ShapeDtypeStruct((n, 3 * NA_WIDTH), BF16),
            jax.ShapeDtypeStruct((n, 3 * HG_WIDTH), BF16),
            jax.ShapeDtypeStruct((n, 2 * HG_WIDTH), F32),
        ],
        compiler_params=_params("arbitrary"),
        name="inproj",
    )(x2d, norm_g, w_bf16, lb_logits4)


N_WIN_KEYS = NA_KH * GRID_W
META_PAD = LANES


NA_GROUP = 4
GROUP_LANES = NA_GROUP * NA_HEAD_DIM


NA_ROWS_PER_STEP = 16


def _natten_row(rr, r, q_ref, k_ref, v_ref, km_ref, vm_ref, bias_ref, *, rows):
    rs = jnp.clip(r - NA_KH // 2, 0, rows - NA_KH)
    d0 = rs - r + (NA_KH - 1)
    start = pl.multiple_of(rs * GRID_W, GRID_W)
    lane_head = lax.broadcasted_iota(jnp.int32, (GRID_W, GROUP_LANES), 1) // NA_HEAD_DIM
    meta_col = lax.broadcasted_iota(jnp.int32, (1, META_PAD), 1)
    meta_bias = jnp.where(meta_col < N_META, 0.0, NEG_BIG).astype(F32)
    folded = []
    for g in range(NA_HEADS // NA_GROUP):
        sl = slice(g * GROUP_LANES, (g + 1) * GROUP_LANES)
        q4 = q_ref[0, rr * GRID_W:(rr + 1) * GRID_W, sl]
        zero = jnp.zeros_like(q4)
        qm = jnp.concatenate([jnp.where(lane_head == h, q4, zero) for h in range(NA_GROUP)], axis=0)
        kw = k_ref[0, pl.ds(start, N_WIN_KEYS), sl]
        vw = v_ref[0, pl.ds(start, N_WIN_KEYS), sl]
        hs = slice(g * NA_GROUP, (g + 1) * NA_GROUP)
        bias = jnp.concatenate(
            [bias_ref[d0 + 2 * j, hs].reshape(NA_GROUP * GRID_W, 2 * GRID_W) for j in range(NA_KH // 2)], axis=-1)
        s = _dot_nt(qm, kw) + bias
        sm = _dot_nt(qm, km_ref[:, sl]) + meta_bias
        m = jnp.maximum(jnp.max(s, axis=-1, keepdims=True), jnp.max(sm, axis=-1, keepdims=True))
        e = jnp.exp2(s - m)
        em = jnp.exp2(sm - m)
        den = jnp.sum(e, axis=-1, keepdims=True) + jnp.sum(em, axis=-1, keepdims=True)
        o = (_dot(e.astype(BF16), vw) + _dot(em.astype(BF16), vm_ref[:, sl])) * (1.0 / den)
        acc = jnp.where(lane_head == 0, o[0:GRID_W], 0.0)
        for h in range(1, NA_GROUP):
            acc = jnp.where(lane_head == h, o[h * GRID_W:(h + 1) * GRID_W], acc)
        folded.append(acc)
    return jnp.concatenate(folded, axis=-1)


def _natten_body(q_ref, k_ref, v_ref, km_ref, vm_ref, bias_ref, ng_ref, bd_ref, o_ref, *, rows):
    r0 = pl.program_id(1) * NA_ROWS_PER_STEP
    o2 = jnp.concatenate(
        [_natten_row(rr, r0 + rr, q_ref, k_ref, v_ref, km_ref, vm_ref, bias_ref, rows=rows)
         for rr in range(NA_ROWS_PER_STEP)], axis=0)
    sq_hi, sq_lo = _split_bf16(o2 * o2)
    ms = (_dot(sq_hi, bd_ref[...]) + _dot(sq_lo, bd_ref[...])) * (1.0 / NA_HEAD_DIM)
    o_ref[0] = (o2 * lax.rsqrt(ms + RMS_EPS) * ng_ref[...]).astype(o_ref.dtype)


def _natten(att, att_meta, bias_tabs, norm_g, batch, seq):
    rows = seq // GRID_W
    att3 = att.reshape(batch, seq, 3 * NA_WIDTH)
    head_of = np.arange(NA_WIDTH) // NA_HEAD_DIM
    same_head = jnp.asarray(head_of[:, None] == head_of[None, :], BF16)

    qrows = NA_ROWS_PER_STEP * GRID_W
    return pl.pallas_call(
        functools.partial(_natten_body, rows=rows),
        grid=(batch, rows // NA_ROWS_PER_STEP),
        in_specs=[
            pl.BlockSpec((1, qrows, NA_WIDTH), lambda b, r: (b, r, 0)),
            pl.BlockSpec((1, seq, NA_WIDTH), lambda b, r: (b, 0, 1)),
            pl.BlockSpec((1, seq, NA_WIDTH), lambda b, r: (b, 0, 2)),
            pl.BlockSpec((META_PAD, NA_WIDTH), lambda b, r: (0, 1)),
            pl.BlockSpec((META_PAD, NA_WIDTH), lambda b, r: (0, 2)),
            pl.BlockSpec(bias_tabs.shape, lambda b, r: (0, 0, 0, 0), pipeline_mode=pl.Buffered(1)),
            pl.BlockSpec((1, NA_WIDTH), lambda b, r: (0, 0)),
            pl.BlockSpec((NA_WIDTH, NA_WIDTH), lambda b, r: (0, 0)),
        ],
        out_specs=pl.BlockSpec((1, qrows, NA_WIDTH), lambda b, r: (b, r, 0)),
        out_shape=jax.ShapeDtypeStruct((batch, seq, NA_WIDTH), BF16),
        compiler_params=_params("arbitrary", "arbitrary"),
        name="natten",
    )(att3, att3, att3, att_meta, att_meta, bias_tabs, norm_g, same_head)


def _natten_bias_tables(rpb):
    c = np.arange(GRID_W)[:, None]
    kc = np.arange(GRID_W)[None, :]
    cs = np.clip(c - NA_KW // 2, 0, GRID_W - NA_KW)
    valid = (kc >= cs) & (kc < cs + NA_KW)
    dc = kc - c + (NA_KW - 1)
    pick = np.asarray(dc[None] == np.arange(2 * NA_KW - 1)[:, None, None], np.float32)
    t1 = jnp.einsum("hdj,jck->hdck", rpb.astype(F32) * LOG2_E, pick, precision=lax.Precision.HIGHEST)
    t1 = jnp.where(valid[None, None], t1, NEG_BIG)
    pairs = jnp.concatenate([t1[:, :-1], t1[:, 1:]], axis=-1)
    return pairs.transpose(1, 0, 2, 3)


HG_CHUNKS_PER_STEP = 16
HG_CHUNKS_PER_GROUP = 1
ROW_B, ROW_LV = 0, 1
N_EXP_BLOCKS = 1


def _hgrn_constants():
    C = HG_CHUNK
    t = np.arange(C)
    u = t[None, :]
    mats = [u <= t[:, None]]
    masks = [np.eye(C, dtype=bool)]
    for lv in range(N_HG_LEVELS):
        m = 1 << lv
        blk = t // (2 * m)
        upper = (t // m) % 2 == 1
        masks.append((blk[:, None] == blk[None, :]) & upper[:, None] & (~upper)[None, :])
    fwd = np.concatenate(mats, axis=0).astype(np.float32)
    fwd_mask = np.stack(masks).astype(np.float32)
    blocks = fwd.reshape(N_EXP_BLOCKS, C, C)
    bwd = blocks[:, ::-1, ::-1].reshape(N_EXP_BLOCKS * C, C)
    bwd_mask = fwd_mask[:, ::-1, ::-1]
    both = np.stack([fwd, bwd])
    both = np.concatenate([both, both], axis=-1)
    both_mask = np.stack([fwd_mask, bwd_mask])
    both_mask = np.concatenate([both_mask, both_mask], axis=-1)
    return (jnp.asarray(both, BF16), jnp.asarray(both_mask, F32))


def _hgrn_exponents(mat, g):
    g_hi, g_lo = _split_bf16(g)
    return _dot(mat, jnp.concatenate([g_hi, g_lo], axis=0))


def _hgrn_init(meta_a_ref, meta_g_ref, mat_ref, stf_ref, stb_ref):
    C, W = HG_CHUNK, HG_WIDTH
    v = meta_a_ref[:, W:2 * W]
    prefix = mat_ref[0, ROW_B * C:(ROW_B + 1) * C, :]
    for d, st_ref in ((0, stf_ref), (1, stb_ref)):
        g = meta_g_ref[:, d * W:(d + 1) * W]
        b = _hgrn_exponents(prefix, g)
        kb = ((1.0 - jnp.exp(g)) * jnp.exp(b[C - 1:C] - b)).astype(BF16)
        for h in range(HG_HEADS):
            sl = slice(h * HG_HEAD_DIM, (h + 1) * HG_HEAD_DIM)
            st_ref[h] = _dot_tn(v[:, sl], kb[:, sl])


def _hgrn_group_tasks(j0):
    js = range(j0, j0 + HG_CHUNKS_PER_GROUP)
    return ([(0, j * HG_CHUNK) for j in js] +
            [(1, (HG_CHUNKS_PER_STEP - 1 - j) * HG_CHUNK) for j in js])


def _hgrn_tasks(tasks, dir_refs, states, mat_ref, mask_ref):
    C, W = HG_CHUNK, HG_WIDTH
    heads = [slice(h * HG_HEAD_DIM, (h + 1) * HG_HEAD_DIM) for h in range(HG_HEADS)]
    qs, ks, vs, ex_bs, ex_lvs, ex_ends = {}, {}, {}, {}, {}, {}
    for d, r0 in tasks:
        q_ref, v_ref, g_ref, total_row = dir_refs[d][0], dir_refs[d][1], dir_refs[d][2], dir_refs[d][3]
        g = g_ref[0, r0:r0 + C, :]
        sums = _hgrn_exponents(mat_ref[d], g)
        b = sums[0:C]
        ex_bs[d, r0] = jnp.exp(b)
        level_sums = []
        for lv in range(N_HG_LEVELS):
            m = 1 << lv
            rows_per = max(2 * m, SUBLANES)
            b3 = b.reshape(C // rows_per, rows_per, W)
            row = lax.broadcasted_iota(jnp.int32, b3.shape, 1)
            in_blk = row % (2 * m)
            pivot_in_blk, is_query = (m - 1, in_blk >= m) if d == 0 else (m, in_blk < m)
            if m == 1:
                level_sums.append(jnp.where(is_query, g.reshape(b3.shape), 0.0).reshape(C, W))
                continue
            pivots = b3[:, pivot_in_blk:pivot_in_blk + 1, :]
            for blk in range(1, rows_per // (2 * m)):
                p = blk * 2 * m + pivot_in_blk
                pivots = jnp.where(row >= blk * 2 * m, b3[:, p:p + 1, :], pivots)
            diff = b3 - pivots
            level_sums.append(jnp.where(is_query, diff, -diff).reshape(C, W))
        ex_lvs[d, r0] = [jnp.exp(s).astype(BF16) for s in level_sums]
        ex_ends[d, r0] = jnp.exp(sums[total_row:total_row + 1] - sums[0:C])
        qs[d, r0] = q_ref[0, r0:r0 + C, :]
        ks[d, r0] = 1.0 - jnp.exp(g)
        vs[d, r0] = v_ref[0, r0:r0 + C, :]

    D2 = 2 * HG_HEAD_DIM
    pairs = [slice(p * D2, (p + 1) * D2) for p in range(HG_HEADS // 2)]

    def block_diag(x):
        zero = jnp.zeros((C, HG_HEAD_DIM), x.dtype)
        return jnp.concatenate([jnp.concatenate([x[:, :HG_HEAD_DIM], zero], axis=1),
                                jnp.concatenate([zero, x[:, HG_HEAD_DIM:]], axis=1)], axis=0)

    prods = {}
    for task in tasks:
        for p, sl in enumerate(pairs):
            q2, k2 = qs[task][:, sl], ks[task][:, sl].astype(BF16)
            ps = [_dot_nt(q2, block_diag(k2))]
            for lv in range(N_HG_LEVELS):
                scale = ex_lvs[task][lv][:, sl]
                ps.append(_dot_nt(q2 * scale, block_diag(k2 * scale)))
            prods[task, p] = ps
    attn, qbs, kbs = {}, {}, {}
    for task in tasks:
        d = task[0]
        for p, sl in enumerate(pairs):
            a = prods[task, p][0] * mask_ref[d, 0]
            for lv in range(N_HG_LEVELS):
                a = a + prods[task, p][1 + lv] * mask_ref[d, 1 + lv]
            o2 = _dot(a.astype(BF16), block_diag(vs[task][:, sl]))
            attn[task, 2 * p] = o2[:, :HG_HEAD_DIM]
            attn[task, 2 * p + 1] = o2[:, HG_HEAD_DIM:]
        for h, sl in enumerate(heads):
            qbs[task, h] = (qs[task][:, sl].astype(F32) * ex_bs[task][:, sl]).astype(BF16)
            kbs[task, h] = _dot_tn(vs[task][:, sl], (ks[task][:, sl] * ex_ends[task][:, sl]).astype(BF16))
    for task in tasks:
        d, r0 = task
        total_row, o_ref = dir_refs[d][3], dir_refs[d][4]
        outs = []
        for h, sl in enumerate(heads):
            st = states[d, h]
            outs.append(attn[task, h] + _dot_nt(qbs[task, h], st.astype(BF16)))
            decay = ex_bs[task][total_row:total_row + 1, sl]
            states[d, h] = decay * st + kbs[task, h]
        o_ref[0, r0:r0 + C, :] = jnp.concatenate(outs, axis=-1).astype(o_ref.dtype)


def _hgrn_body(hqf_ref, hvf_ref, hgf_ref, hqb_ref, hvb_ref, hgb_ref, meta_a_ref, meta_g_ref, mat_ref, mask_ref,
               of_ref, ob_ref, stf_ref, stb_ref):
    @pl.when(pl.program_id(1) == 0)
    def _init():
        _hgrn_init(meta_a_ref, meta_g_ref, mat_ref, stf_ref, stb_ref)

    dir_refs = ((hqf_ref, hvf_ref, hgf_ref, HG_CHUNK - 1, of_ref), (hqb_ref, hvb_ref, hgb_ref, 0, ob_ref))
    st_refs = (stf_ref, stb_ref)
    states = {(d, h): st_refs[d][h] for d in range(2) for h in range(HG_HEADS)}
    for j0 in range(0, HG_CHUNKS_PER_STEP, HG_CHUNKS_PER_GROUP):
        _hgrn_tasks(_hgrn_group_tasks(j0), dir_refs, states, mat_ref, mask_ref)
    for d in range(2):
        for h in range(HG_HEADS):
            st_refs[d][h] = states[d, h]


def _hgrn(hga, hgg, hga_meta_pad, hgg_meta_pad, batch, seq):
    steps = seq // (HG_CHUNK * HG_CHUNKS_PER_STEP)
    a3 = hga.reshape(batch, seq, 3 * HG_WIDTH)
    g3 = hgg.reshape(batch, seq, 2 * HG_WIDTH)
    mats, masks = _hgrn_constants()
    hblk = (1, HG_CHUNK * HG_CHUNKS_PER_STEP, HG_WIDTH)

    def fwd(col):
        return pl.BlockSpec(hblk, lambda b, s: (b, s, col))

    def bwd(col):
        return pl.BlockSpec(hblk, lambda b, s: (b, steps - 1 - s, col))

    return pl.pallas_call(
        _hgrn_body,
        grid=(batch, steps),
        in_specs=[
            fwd(0), fwd(1), fwd(0),
            bwd(0), bwd(1), bwd(1),
            pl.BlockSpec((HG_CHUNK, 3 * HG_WIDTH), lambda b, s: (0, 0)),
            pl.BlockSpec((HG_CHUNK, 2 * HG_WIDTH), lambda b, s: (0, 0)),
            pl.BlockSpec(mats.shape, lambda b, s: (0, 0, 0)),
            pl.BlockSpec(masks.shape, lambda b, s: (0, 0, 0, 0)),
        ],
        out_specs=[fwd(0), bwd(0)],
        out_shape=[jax.ShapeDtypeStruct((batch, seq, HG_WIDTH), BF16)] * 2,
        scratch_shapes=[pltpu.VMEM((HG_HEADS, HG_HEAD_DIM, HG_HEAD_DIM), F32)] * 2,
        compiler_params=_params("arbitrary", "arbitrary"),
        name="hgrn",
    )(a3, a3, g3, a3, a3, g3, hga_meta_pad, hgg_meta_pad, mats, masks)


MIX_TILES_PER_STEP = 4


def _mix_route_body(yna_ref, of_ref, ob_ref, gate_ref, x_ref, wout_ref, hgn_ref, ffg_ref,
                    rwh_ref, rwl_ref, rb_ref, tri_ref, lowe_ref,
                    h1_ref, xn_ref, lpos_ref, tw_ref, cnt_ref):
    T, E = TOK_TILE, N_EXPERTS
    tiles = [slice(s * T, (s + 1) * T) for s in range(MIX_TILES_PER_STEP)]
    x_his, x_los = [], []
    for rows in tiles:
        o = of_ref[rows, :].astype(F32) + ob_ref[rows, :].astype(F32)
        parts = []
        for h in range(HG_HEADS):
            seg = o[:, h * HG_HEAD_DIM:(h + 1) * HG_HEAD_DIM]
            ms = jnp.mean(seg * seg, axis=-1, keepdims=True)
            parts.append(seg * lax.rsqrt(ms + RMS_EPS))
        yhg = jnp.concatenate(parts, axis=-1) * hgn_ref[...] * gate_ref[rows, :]
        mix = _dot(yna_ref[rows, :], wout_ref[0:NA_WIDTH, :]) + _dot(yhg.astype(BF16), wout_ref[NA_WIDTH:, :])
        h1 = x_ref[rows, :] + mix
        h1_ref[rows, :] = h1
        ms = jnp.mean(h1 * h1, axis=-1, keepdims=True)
        xn = h1 * lax.rsqrt(ms + RMS_EPS) * ffg_ref[...]
        x_hi, x_lo = _split_bf16(xn)
        xn_ref[rows, :] = x_hi
        x_his.append(x_hi)
        x_los.append(x_lo)

    curs = [(_dot_nt(rwh_ref[...], x_hi) + _dot_nt(rwh_ref[...], x_lo) + _dot_nt(rwl_ref[...], x_hi)) + rb_ref[...]
            for x_hi, x_lo in zip(x_his, x_los)]
    row = lax.broadcasted_iota(jnp.int32, (E, T), 0).astype(F32)
    sels = [[] for _ in tiles]
    vals = [[] for _ in tiles]
    for _ in range(TOP_K):
        for s in range(len(tiles)):
            m = jnp.max(curs[s], axis=0, keepdims=True)
            first = jnp.min(jnp.where(curs[s] == m, row, float(E)), axis=0, keepdims=True)
            sel = row == first
            sels[s].append(sel)
            vals[s].append(m)
            curs[s] = jnp.where(sel, -jnp.inf, curs[s])
    for s in range(len(tiles)):
        es = [jnp.exp(vk - vals[s][0]) for vk in vals[s]]
        den = es[0] + es[1] + es[2] + es[3]
        tw_ref[s] = _pack_rows([e / den for e in es], SUBLANES)

        onehot = jnp.zeros((E, T), F32)
        for sel in sels[s]:
            onehot = onehot + jnp.where(sel, 1.0, 0.0)
        lrank = _dot(onehot.astype(BF16), tri_ref[...])
        cnt = jnp.sum(onehot, axis=1, keepdims=True)
        cnt = jnp.floor((cnt + (RUN_ALIGN - 1)) * (1.0 / RUN_ALIGN)) * RUN_ALIGN
        off = _dot(lowe_ref[...], jnp.broadcast_to(cnt, (E, LANES)).astype(BF16))
        base = jnp.concatenate([off] * (T // LANES), axis=1) + lrank
        lpos_ref[s] = _pack_rows(
            [jnp.sum(jnp.where(sel, base, 0.0), axis=0, keepdims=True) for sel in sels[s]],
            SUBLANES).astype(jnp.int32)
        cnt_ref[s] = cnt.astype(jnp.int32)


def _mix_route(yna, o_f, o_b, hg, x2d, wout_bf16, hgn, ffg, router_w, router_b):
    n = x2d.shape[0]
    T = TOK_TILE
    nt = n // T
    rw_t = router_w.T
    rwh = rw_t.astype(BF16)
    rwl = (rw_t - rwh.astype(F32)).astype(BF16)
    router_b = router_b.reshape(N_EXPERTS, 1)
    tri = jnp.asarray(np.triu(np.ones((T, T), np.float32), 1), BF16)
    lowe = jnp.asarray(np.tril(np.ones((N_EXPERTS, N_EXPERTS), np.float32), -1), BF16)
    row = lambda i: (i, 0)
    const = lambda i: (0, 0)
    tile3 = lambda i: (i, 0, 0)
    K = MIX_TILES_PER_STEP
    R = K * T
    return pl.pallas_call(
        _mix_route_body,
        grid=(nt // K,),
        in_specs=[
            pl.BlockSpec((R, NA_WIDTH), row),
            pl.BlockSpec((R, HG_WIDTH), row),
            pl.BlockSpec((R, HG_WIDTH), row),
            pl.BlockSpec((R, HG_WIDTH), lambda i: (i, 2)),
            pl.BlockSpec((R, D_MODEL), row),
            pl.BlockSpec((D_MODEL, D_MODEL), const),
            pl.BlockSpec((1, HG_WIDTH), const),
            pl.BlockSpec((1, D_MODEL), const),
            pl.BlockSpec((N_EXPERTS, D_MODEL), const),
            pl.BlockSpec((N_EXPERTS, D_MODEL), const),
            pl.BlockSpec((N_EXPERTS, 1), const),
            pl.BlockSpec((T, T), const),
            pl.BlockSpec((N_EXPERTS, N_EXPERTS), const),
        ],
        out_specs=[
            pl.BlockSpec((R, D_MODEL), row),
            pl.BlockSpec((R, D_MODEL), row),
            pl.BlockSpec((K, SUBLANES, T), tile3),
            pl.BlockSpec((K, SUBLANES, T), tile3),
            pl.BlockSpec((K, N_EXPERTS, 1), tile3),
        ],
        out_shape=[
            jax.ShapeDtypeStruct((n, D_MODEL), F32),
            jax.ShapeDtypeStruct((n, D_MODEL), BF16),
            jax.ShapeDtypeStruct((nt, SUBLANES, T), jnp.int32),
            jax.ShapeDtypeStruct((nt, SUBLANES, T), F32),
            jax.ShapeDtypeStruct((nt, N_EXPERTS, 1), jnp.int32),
        ],
        compiler_params=_params("arbitrary"),
        name="mix_route",
    )(yna, o_f, o_b, hg, x2d, wout_bf16, hgn, ffg, rwh, rwl, router_b, tri, lowe)


TILES_PER_STEP = 4


def _wait_rows(make_copy, rows):
    make_copy(0, 0, pl.multiple_of(rows, RUN_ALIGN)).wait()


def _copy_runs(i, cnt_ref, off_ref, goff_ref, make_copy):
    def per_expert(e, carry):
        n = pl.multiple_of(cnt_ref[i * N_EXPERTS + e], RUN_ALIGN)
        lo = pl.multiple_of(off_ref[i * N_EXPERTS + e], RUN_ALIGN)
        go = pl.multiple_of(goff_ref[i * N_EXPERTS + e], RUN_ALIGN)

        @pl.when(n > 0)
        def _start():
            make_copy(lo, go, n).start()

        return carry

    lax.fori_loop(0, N_EXPERTS, per_expert, 0)


def _slot_matrix(lpos, weights=None):
    row = lax.broadcasted_iota(jnp.int32, (STAGE_ROWS, TOK_TILE), 0)
    acc = jnp.zeros((STAGE_ROWS, TOK_TILE), F32)
    for k in range(TOP_K):
        hit = lpos[k:k + 1, :] == row
        val = 1.0 if weights is None else weights[k:k + 1, :]
        acc = jnp.where(hit, val, acc)
    return acc


def _dispatch_body(cnt_ref, off_ref, goff_ref, tot_ref, zoff_ref, zlen_ref, blkrange_ref,
                   xn_ref, lpos_ref, xbuf_ref, stage_ref, zero_ref, sems, zsem):
    i = pl.program_id(0)
    last = pl.num_programs(0) - 1

    def zero_fill(act):
        def tail(e, carry):
            n = pl.multiple_of(zlen_ref[e], RUN_ALIGN)

            @pl.when(n > 0)
            def _():
                go = pl.multiple_of(zoff_ref[e], RUN_ALIGN)
                act(pltpu.make_async_copy(zero_ref.at[pl.ds(0, n)], xbuf_ref.at[pl.ds(go, n)], zsem))

            return carry

        def unused(j, carry):
            go = pl.multiple_of(j * EXPERT_BLOCK, EXPERT_BLOCK)
            act(pltpu.make_async_copy(zero_ref, xbuf_ref.at[pl.ds(go, EXPERT_BLOCK)], zsem))
            return carry

        lax.fori_loop(0, N_EXPERTS, tail, 0)
        lax.fori_loop(blkrange_ref[0], blkrange_ref[1], unused, 0)

    @pl.when(i == 0)
    def _zero_start():
        zero_ref[...] = jnp.zeros_like(zero_ref)
        zero_fill(lambda c: c.start())

    for s in range(TILES_PER_STEP):
        t = i * TILES_PER_STEP + s

        def make_copy(lo, go, size, s=s):
            return pltpu.make_async_copy(stage_ref.at[s, pl.ds(lo, size)], xbuf_ref.at[pl.ds(go, size)], sems.at[s])

        @pl.when(i > 0)
        def _drain():
            _wait_rows(make_copy, tot_ref[t - TILES_PER_STEP])

        rows = slice(s * TOK_TILE, (s + 1) * TOK_TILE)
        perm = _slot_matrix(lpos_ref[s]).astype(BF16)
        stage_ref[s] = _dot(perm, xn_ref[rows, :])
        _copy_runs(t, cnt_ref, off_ref, goff_ref, make_copy)

        @pl.when(i == last)
        def _finish():
            _wait_rows(make_copy, tot_ref[t])

    @pl.when(i == last)
    def _zero_finish():
        zero_fill(lambda c: c.wait())


def _dispatch(cnt, off, goff, tot, zoff, zlen, blkrange, xn, lpos, n_rows):
    n = xn.shape[0]
    rows = TILES_PER_STEP * TOK_TILE
    return pl.pallas_call(
        _dispatch_body,
        grid_spec=pltpu.PrefetchScalarGridSpec(
            num_scalar_prefetch=7,
            grid=(n // rows,),
            in_specs=[
                pl.BlockSpec((rows, D_MODEL), lambda i, *_: (i, 0)),
                pl.BlockSpec((TILES_PER_STEP, SUBLANES, TOK_TILE), lambda i, *_: (i, 0, 0)),
            ],
            out_specs=pl.BlockSpec(memory_space=pl.ANY),
            scratch_shapes=[
                pltpu.VMEM((TILES_PER_STEP, STAGE_ROWS, D_MODEL), F32),
                pltpu.VMEM((EXPERT_BLOCK, D_MODEL), F32),
                pltpu.SemaphoreType.DMA((TILES_PER_STEP,)),
                pltpu.SemaphoreType.DMA(()),
            ],
        ),
        out_shape=jax.ShapeDtypeStruct((n_rows, D_MODEL), F32),
        compiler_params=_params("arbitrary"),
        name="moe_dispatch",
    )(cnt, off, goff, tot, zoff, zlen, blkrange, xn, lpos)


CAST_ROWS = 128
EXPERT_ROW_STEP = 128


def _experts_body(blk_e_ref, first_ref, slot_ref, next_e_ref, parts_ref, nused_ref,
                  x_ref, bgu_ref, bd_ref, wgu_hbm, wd_hbm, y_ref,
                  wgu_f32, wd_f32, wgu_bf, wd_bf, sem_gu, sem_d):
    j = pl.program_id(0)

    def weight_copies(e, slot):
        return (pltpu.make_async_copy(wgu_hbm.at[e], wgu_f32.at[slot], sem_gu.at[slot]),
                pltpu.make_async_copy(wd_hbm.at[e], wd_f32.at[slot], sem_d.at[slot]))

    @pl.when(j < nused_ref[0])
    def _():
        e = blk_e_ref[j]
        slot = slot_ref[j]

        @pl.when(first_ref[j] == 1)
        def _new_expert():
            @pl.when(j == 0)
            def _():
                for c in weight_copies(e, slot):
                    c.start()

            for c in weight_copies(e, slot):
                c.wait()
            nxt = next_e_ref[j]

            @pl.when(nxt >= 0)
            def _():
                for c in weight_copies(nxt, 1 - slot):
                    c.start(priority=1)

            def cast(src, dst):
                def rows(r, carry):
                    sl = pl.ds(pl.multiple_of(r * CAST_ROWS, CAST_ROWS), CAST_ROWS)
                    dst[sl, :] = src[slot, sl, :].astype(BF16)
                    return carry
                lax.fori_loop(0, src.shape[1] // CAST_ROWS, rows, 0)

            cast(wgu_f32, wgu_bf)
            cast(wd_f32, wd_bf)

        def ffn(rows):
            x = x_ref[0:rows, :].astype(BF16)
            gu = _dot(x, wgu_bf[...]) + bgu_ref[0]
            gate = jnp.minimum(gu[:, :D_FF], SWIGLU_LIMIT)
            up = jnp.clip(gu[:, D_FF:], -SWIGLU_LIMIT, SWIGLU_LIMIT)
            act = (up + 1.0) * gate * jax.nn.sigmoid(SWIGLU_ALPHA * gate)
            y_ref[0:rows, :] = _dot(act.astype(BF16), wd_bf[...]) + bd_ref[0]
            if rows < EXPERT_BLOCK:
                y_ref[rows:, :] = jnp.zeros((EXPERT_BLOCK - rows, D_MODEL), F32)

        for parts in range(1, EXPERT_BLOCK // EXPERT_ROW_STEP + 1):
            pl.when(parts_ref[j] == parts)(functools.partial(ffn, parts * EXPERT_ROW_STEP))


def _experts(blk_e, first, slot, next_e, parts, nused, xbuf, wgu, bgu, wd, bd):
    n_rows = xbuf.shape[0]
    nblk = n_rows // EXPERT_BLOCK

    def rowblk(j, be, fi, sl, ne, pa, nu):
        return (jnp.minimum(j, nu[0] - 1), 0)

    def expert(j, be, fi, sl, ne, pa, nu):
        return (be[jnp.minimum(j, nu[0] - 1)], 0, 0)

    return pl.pallas_call(
        _experts_body,
        grid_spec=pltpu.PrefetchScalarGridSpec(
            num_scalar_prefetch=6,
            grid=(nblk,),
            in_specs=[
                pl.BlockSpec((EXPERT_BLOCK, D_MODEL), rowblk),
                pl.BlockSpec((1, 1, 2 * D_FF), expert),
                pl.BlockSpec((1, 1, D_MODEL), expert),
                pl.BlockSpec(memory_space=pl.ANY),
                pl.BlockSpec(memory_space=pl.ANY),
            ],
            out_specs=pl.BlockSpec((EXPERT_BLOCK, D_MODEL), rowblk),
            scratch_shapes=[
                pltpu.VMEM((2, D_MODEL, 2 * D_FF), F32),
                pltpu.VMEM((2, D_FF, D_MODEL), F32),
                pltpu.VMEM((D_MODEL, 2 * D_FF), BF16),
                pltpu.VMEM((D_FF, D_MODEL), BF16),
                pltpu.SemaphoreType.DMA((2,)),
                pltpu.SemaphoreType.DMA((2,)),
            ],
        ),
        out_shape=jax.ShapeDtypeStruct((n_rows, D_MODEL), F32),
        input_output_aliases={6: 0},
        compiler_params=_params("arbitrary"),
        name="moe_experts",
    )(blk_e, first, slot, next_e, parts, nused, xbuf, bgu, bd, wgu, wd)


def _combine_body(cnt_ref, off_ref, goff_ref, tot_ref, ybuf_ref, lpos_ref, tw_ref, h1_ref, fg_ref, out_ref,
                  stage_ref, sems):
    i = pl.program_id(0)
    n_tiles = pl.num_programs(0) * TILES_PER_STEP

    def copier(s):
        def make_copy(lo, go, size):
            return pltpu.make_async_copy(ybuf_ref.at[pl.ds(go, size)], stage_ref.at[s, pl.ds(lo, size)], sems.at[s])
        return make_copy

    ahead = TILES_PER_STEP - 1

    @pl.when(i == 0)
    def _init():
        stage_ref[...] = jnp.zeros_like(stage_ref)
        for s in range(ahead):
            _copy_runs(s, cnt_ref, off_ref, goff_ref, copier(s))

    for s in range(TILES_PER_STEP):
        t = i * TILES_PER_STEP + s

        @pl.when(t + ahead < n_tiles)
        def _prefetch():
            _copy_runs(t + ahead, cnt_ref, off_ref, goff_ref, copier((s + ahead) % TILES_PER_STEP))

        rows = slice(s * TOK_TILE, (s + 1) * TOK_TILE)
        w = _slot_matrix(lpos_ref[s], tw_ref[s]).astype(BF16)
        _wait_rows(copier(s), tot_ref[t])
        h2 = h1_ref[rows, :] + _dot_tn(w, stage_ref[s].astype(BF16))
        ms = jnp.mean(h2 * h2, axis=-1, keepdims=True)
        out_ref[rows, :] = h2 * lax.rsqrt(ms + RMS_EPS) * fg_ref[...]


def _combine(cnt, off, goff, tot, ybuf, lpos, tw, h1, final_g):
    n = h1.shape[0]
    rows = TILES_PER_STEP * TOK_TILE
    row = lambda i, *_: (i, 0)
    return pl.pallas_call(
        _combine_body,
        grid_spec=pltpu.PrefetchScalarGridSpec(
            num_scalar_prefetch=4,
            grid=(n // rows,),
            in_specs=[
                pl.BlockSpec(memory_space=pl.ANY),
                pl.BlockSpec((TILES_PER_STEP, SUBLANES, TOK_TILE), lambda i, *_: (i, 0, 0)),
                pl.BlockSpec((TILES_PER_STEP, SUBLANES, TOK_TILE), lambda i, *_: (i, 0, 0)),
                pl.BlockSpec((rows, D_MODEL), row),
                pl.BlockSpec((1, D_MODEL), lambda i, *_: (0, 0)),
            ],
            out_specs=pl.BlockSpec((rows, D_MODEL), row),
            scratch_shapes=[pltpu.VMEM((TILES_PER_STEP, STAGE_ROWS, D_MODEL), F32),
                            pltpu.SemaphoreType.DMA((TILES_PER_STEP,))],
        ),
        out_shape=jax.ShapeDtypeStruct((n, D_MODEL), F32),
        compiler_params=_params("arbitrary"),
        name="moe_combine",
    )(cnt, off, goff, tot, ybuf, lpos, tw, h1, final_g)


def _moe_layout(cnt_tiles):
    nt = cnt_tiles.shape[0]
    total = jnp.sum(cnt_tiles, axis=0)
    nblk_e = (total + EXPERT_BLOCK - 1) // EXPERT_BLOCK
    blk_end = jnp.cumsum(nblk_e)
    pstart = (blk_end - nblk_e) * EXPERT_BLOCK
    prefix = jnp.cumsum(cnt_tiles, axis=0) - cnt_tiles
    goff = pstart[None, :] + prefix
    off = jnp.cumsum(cnt_tiles, axis=1) - cnt_tiles
    tot = jnp.sum(cnt_tiles, axis=1)
    max_rows = nt * (TOK_TILE * TOP_K + N_EXPERTS * (RUN_ALIGN - 1)) + N_EXPERTS * (EXPERT_BLOCK - 1)
    max_blocks = (max_rows + EXPERT_BLOCK - 1) // EXPERT_BLOCK
    blk_id = jnp.arange(max_blocks, dtype=blk_end.dtype)
    blk_e = jnp.minimum(jnp.sum(blk_end[None, :] <= blk_id[:, None], axis=1), N_EXPERTS - 1).astype(jnp.int32)
    nused = blk_end[-1:].astype(jnp.int32)
    flat = lambda a: a.reshape(-1).astype(jnp.int32)
    zoff = pstart + total
    zlen = nblk_e * EXPERT_BLOCK - total
    blkrange = jnp.stack([nused[0], jnp.int32(max_blocks)])
    used = blk_id < nused[0]
    first = used & jnp.concatenate([jnp.ones((1,), bool), blk_e[1:] != blk_e[:-1]])
    slot = (jnp.cumsum(first.astype(jnp.int32)) - 1) % 2
    eid = jnp.arange(N_EXPERTS)
    later = (eid[None, :] > eid[:, None]) & (nblk_e > 0)[None, :]
    next_of = jnp.min(jnp.where(later, eid[None, :], N_EXPERTS), axis=1)
    next_of = jnp.where(next_of < N_EXPERTS, next_of, -1)
    is_e = blk_e[:, None] == eid[None, :]
    pick = lambda per_expert: jnp.sum(jnp.where(is_e, per_expert[None, :], 0), axis=1)
    rows_left = jnp.clip(pick(total) - (blk_id - pick(blk_end - nblk_e)) * EXPERT_BLOCK, 0, EXPERT_BLOCK)
    parts = (rows_left + EXPERT_ROW_STEP - 1) // EXPERT_ROW_STEP
    experts_plan = (blk_e, first.astype(jnp.int32), slot.astype(jnp.int32), pick(next_of).astype(jnp.int32),
                    parts.astype(jnp.int32), nused)
    dispatch_plan = (flat(cnt_tiles), flat(off), flat(goff), flat(tot))
    zero_plan = (flat(zoff), flat(zlen), blkrange.astype(jnp.int32))
    return dispatch_plan, zero_plan, experts_plan, max_blocks * EXPERT_BLOCK


def kernel(x, meta_tokens, attn_norm_g, w_in, na_rpb, na_norm_g, hgrn_lb_logits, hgrn_norm_g, w_out, ffn_norm_g,
           router_w, router_b, expert_w_gu, expert_b_gu, expert_w_down, expert_b_down, final_norm_g):
    B, T, D = x.shape
    x2d = x.reshape(B * T, D)
    w_in_b = w_in[0].astype(BF16)
    ng = attn_norm_g[0].reshape(1, D)
    lbl = hgrn_lb_logits[:, :, :].reshape(4, HG_WIDTH)

    att, hga, hgg = _inproj(x2d, ng, w_in_b, lbl, 512)
    att_m, hga_m, hgg_m = _inproj(meta_tokens.astype(F32), ng, w_in_b, lbl, N_META)
    att_m = jnp.pad(att_m, ((0, META_PAD - N_META), (0, 0)))
    hga_m = jnp.pad(hga_m, ((0, HG_CHUNK - N_META), (0, 0)))
    hgg_m = jnp.pad(hgg_m, ((0, HG_CHUNK - N_META), (0, 0)))

    yna = _natten(att, att_m, _natten_bias_tables(na_rpb[0]), na_norm_g[0].reshape(1, NA_WIDTH), B, T)
    o_f, o_b = _hgrn(hga, hgg, hga_m, hgg_m, B, T)

    h1, xn, lpos, tw, cnt_tiles = _mix_route(
        yna.reshape(B * T, NA_WIDTH), o_f.reshape(B * T, HG_WIDTH), o_b.reshape(B * T, HG_WIDTH), hga, x2d,
        w_out[0].astype(BF16), hgrn_norm_g[0].reshape(1, HG_WIDTH), ffn_norm_g[0].reshape(1, D),
        router_w[0], router_b[0].reshape(1, N_EXPERTS))

    dispatch_plan, zero_plan, experts_plan, n_rows = _moe_layout(cnt_tiles[:, :, 0])
    xbuf = _dispatch(*dispatch_plan, *zero_plan, xn, lpos, n_rows)
    ybuf = _experts(*experts_plan, xbuf, expert_w_gu[0], expert_b_gu[0][:, None, :],
                    expert_w_down[0], expert_b_down[0][:, None, :])
    out = _combine(*dispatch_plan, ybuf, lpos, tw, h1, final_norm_g.reshape(1, D))
    return out.reshape(B, T, D)
```

```python
import functools

import numpy as np
import jax
import jax.numpy as jnp
from jax import lax
from jax.experimental import pallas as pl
from jax.experimental.pallas import tpu as pltpu

F32 = jnp.float32
BF16 = jnp.bfloat16

D_MODEL = 1024
N_META = 16
GRID_W = 64
NA_WIDTH = 512
NA_HEAD_DIM = 64
NA_HEADS = 8
NA_KH = 8
NA_KW = 16
HG_WIDTH = 512
HG_HEAD_DIM = 128
HG_HEADS = 4
HG_CHUNK = 64
IN_COLS = 3 * NA_WIDTH + 5 * HG_WIDTH
N_EXPERTS = 32
TOP_K = 4
D_FF = 1024
SWIGLU_LIMIT = 7.0
SWIGLU_ALPHA = 1.702
RMS_EPS = 1e-6
NEG_BIG = -1e30
LOG2_E = 1.4426950408889634

LANES = 128
VMEM_LIMIT_BYTES = 56 * 1024 * 1024

TOK_TILE = 256
EXPERT_BLOCK = 512
SUBLANES = 8
RUN_ALIGN = SUBLANES
STAGE_ROWS = TOK_TILE * TOP_K + N_EXPERTS * RUN_ALIGN
N_HG_LEVELS = 6


def _dot(a, b):
    return jnp.dot(a, b, preferred_element_type=F32)


def _dot_nt(a, b):
    return lax.dot_general(a, b, (((1,), (1,)), ((), ())), preferred_element_type=F32)


def _dot_tn(a, b):
    return lax.dot_general(a, b, (((0,), (0,)), ((), ())), preferred_element_type=F32)


def _split_bf16(x):
    hi = x.astype(BF16)
    lo = (x - hi.astype(F32)).astype(BF16)
    return hi, lo


def _pack_rows(rows, n_rows):
    idx = lax.broadcasted_iota(jnp.int32, (n_rows, rows[0].shape[1]), 0)
    out = jnp.zeros(idx.shape, rows[0].dtype)
    for k, r in enumerate(rows):
        out = jnp.where(idx == k, r, out)
    return out


def _params(*sem):
    return pltpu.CompilerParams(dimension_semantics=sem, vmem_limit_bytes=VMEM_LIMIT_BYTES)


def _inproj_body(x_ref, ng_ref, w_ref, lbl_ref, att_ref, hga_ref, hgg_ref):
    x = x_ref[...]
    ms = jnp.mean(x * x, axis=-1, keepdims=True)
    n = (x * lax.rsqrt(ms + RMS_EPS) * ng_ref[...]).astype(BF16)

    def proj(lo, hi):
        return _dot(n, w_ref[:, lo:hi])

    pq = proj(0, NA_WIDTH)
    att_ref[:, 0:NA_WIDTH] = (pq * (NA_HEAD_DIM ** -0.5 * LOG2_E)).astype(BF16)
    att_ref[:, NA_WIDTH:3 * NA_WIDTH] = proj(NA_WIDTH, 3 * NA_WIDTH).astype(BF16)

    base = 3 * NA_WIDTH
    W = HG_WIDTH
    qh = proj(base, base + W)
    hga_ref[:, 0:W] = (qh * jax.nn.sigmoid(qh)).astype(BF16)
    hga_ref[:, W:2 * W] = proj(base + W, base + 2 * W).astype(BF16)
    lbl = lbl_ref[...]
    for d in range(2):
        a0 = lbl[2 * d:2 * d + 1, :]
        a1 = lbl[2 * d + 1:2 * d + 2, :]
        m = jnp.maximum(a0, a1)
        e0 = jnp.exp(a0 - m)
        e1 = jnp.exp(a1 - m)
        lb = e0 / (e0 + e1)
        raw = proj(base + (2 + d) * W, base + (3 + d) * W)
        f = lb + (1.0 - lb) * jax.nn.sigmoid(raw)
        hgg_ref[:, d * W:(d + 1) * W] = jnp.log(f)
    gt = proj(base + 4 * W, base + 5 * W)
    hga_ref[:, 2 * W:3 * W] = (gt * jax.nn.sigmoid(gt)).astype(BF16)


def _inproj(x2d, norm_g, w_bf16, lb_logits4, tm):
    n = x2d.shape[0]
    return pl.pallas_call(
        _inproj_body,
        grid=(n // tm,),
        in_specs=[
            pl.BlockSpec((tm, D_MODEL), lambda i: (i, 0)),
            pl.BlockSpec((1, D_MODEL), lambda i: (0, 0)),
            pl.BlockSpec((D_MODEL, IN_COLS), lambda i: (0, 0)),
            pl.BlockSpec((4, HG_WIDTH), lambda i: (0, 0)),
        ],
        out_specs=[
            pl.BlockSpec((tm, 3 * NA_WIDTH), lambda i: (i, 0)),
            pl.BlockSpec((tm, 3 * HG_WIDTH), lambda i: (i, 0)),
            pl.BlockSpec((tm, 2 * HG_WIDTH), lambda i: (i, 0)),
        ],
        out_shape=[
            jax.ShapeDtypeStruct((n, 3 * NA_WIDTH), BF16),
            jax.ShapeDtypeStruct((n, 3 * HG_WIDTH), BF16),
            jax.ShapeDtypeStruct((n, 2 * HG_WIDTH), F32),
        ],
        compiler_params=_params("arbitrary"),
        name="inproj",
    )(x2d, norm_g, w_bf16, lb_logits4)


N_WIN_KEYS = NA_KH * GRID_W
META_PAD = LANES


NA_GROUP = 4
GROUP_LANES = NA_GROUP * NA_HEAD_DIM


NA_ROWS_PER_STEP = 16


def _natten_row(rr, r, q_ref, k_ref, v_ref, km_ref, vm_ref, bias_ref, *, rows):
    rs = jnp.clip(r - NA_KH // 2, 0, rows - NA_KH)
    d0 = rs - r + (NA_KH - 1)
    start = pl.multiple_of(rs * GRID_W, GRID_W)
    lane_head = lax.broadcasted_iota(jnp.int32, (GRID_W, GROUP_LANES), 1) // NA_HEAD_DIM
    meta_col = lax.broadcasted_iota(jnp.int32, (1, META_PAD), 1)
    meta_bias = jnp.where(meta_col < N_META, 0.0, NEG_BIG).astype(F32)
    folded = []
    for g in range(NA_HEADS // NA_GROUP):
        sl = slice(g * GROUP_LANES, (g + 1) * GROUP_LANES)
        q4 = q_ref[0, rr * GRID_W:(rr + 1) * GRID_W, sl]
        zero = jnp.zeros_like(q4)
        qm = jnp.concatenate([jnp.where(lane_head == h, q4, zero) for h in range(NA_GROUP)], axis=0)
        kw = k_ref[0, pl.ds(start, N_WIN_KEYS), sl]
        vw = v_ref[0, pl.ds(start, N_WIN_KEYS), sl]
        hs = slice(g * NA_GROUP, (g + 1) * NA_GROUP)
        bias = jnp.concatenate(
            [bias_ref[d0 + 2 * j, hs].reshape(NA_GROUP * GRID_W, 2 * GRID_W) for j in range(NA_KH // 2)], axis=-1)
        s = _dot_nt(qm, kw) + bias
        sm = _dot_nt(qm, km_ref[:, sl]) + meta_bias
        m = jnp.maximum(jnp.max(s, axis=-1, keepdims=True), jnp.max(sm, axis=-1, keepdims=True))
        e = jnp.exp2(s - m)
        em = jnp.exp2(sm - m)
        den = jnp.sum(e, axis=-1, keepdims=True) + jnp.sum(em, axis=-1, keepdims=True)
        o = (_dot(e.astype(BF16), vw) + _dot(em.astype(BF16), vm_ref[:, sl])) * (1.0 / den)
        acc = jnp.where(lane_head == 0, o[0:GRID_W], 0.0)
        for h in range(1, NA_GROUP):
            acc = jnp.where(lane_head == h, o[h * GRID_W:(h + 1) * GRID_W], acc)
        folded.append(acc)
    return jnp.concatenate(folded, axis=-1)


def _natten_body(q_ref, k_ref, v_ref, km_ref, vm_ref, bias_ref, ng_ref, bd_ref, o_ref, *, rows):
    r0 = pl.program_id(1) * NA_ROWS_PER_STEP
    o2 = jnp.concatenate(
        [_natten_row(rr, r0 + rr, q_ref, k_ref, v_ref, km_ref, vm_ref, bias_ref, rows=rows)
         for rr in range(NA_ROWS_PER_STEP)], axis=0)
    sq_hi, sq_lo = _split_bf16(o2 * o2)
    ms = (_dot(sq_hi, bd_ref[...]) + _dot(sq_lo, bd_ref[...])) * (1.0 / NA_HEAD_DIM)
    o_ref[0] = (o2 * lax.rsqrt(ms + RMS_EPS) * ng_ref[...]).astype(o_ref.dtype)


def _natten(att, att_meta, bias_tabs, norm_g, batch, seq):
    rows = seq // GRID_W
    att3 = att.reshape(batch, seq, 3 * NA_WIDTH)
    head_of = np.arange(NA_WIDTH) // NA_HEAD_DIM
    same_head = jnp.asarray(head_of[:, None] == head_of[None, :], BF16)

    qrows = NA_ROWS_PER_STEP * GRID_W
    return pl.pallas_call(
        functools.partial(_natten_body, rows=rows),
        grid=(batch, rows // NA_ROWS_PER_STEP),
        in_specs=[
            pl.BlockSpec((1, qrows, NA_WIDTH), lambda b, r: (b, r, 0)),
            pl.BlockSpec((1, seq, NA_WIDTH), lambda b, r: (b, 0, 1)),
            pl.BlockSpec((1, seq, NA_WIDTH), lambda b, r: (b, 0, 2)),
            pl.BlockSpec((META_PAD, NA_WIDTH), lambda b, r: (0, 1)),
            pl.BlockSpec((META_PAD, NA_WIDTH), lambda b, r: (0, 2)),
            pl.BlockSpec(bias_tabs.shape, lambda b, r: (0, 0, 0, 0), pipeline_mode=pl.Buffered(1)),
            pl.BlockSpec((1, NA_WIDTH), lambda b, r: (0, 0)),
            pl.BlockSpec((NA_WIDTH, NA_WIDTH), lambda b, r: (0, 0)),
        ],
        out_specs=pl.BlockSpec((1, qrows, NA_WIDTH), lambda b, r: (b, r, 0)),
        out_shape=jax.ShapeDtypeStruct((batch, seq, NA_WIDTH), BF16),
        compiler_params=_params("arbitrary", "arbitrary"),
        name="natten",
    )(att3, att3, att3, att_meta, att_meta, bias_tabs, norm_g, same_head)


def _natten_bias_tables(rpb):
    c = np.arange(GRID_W)[:, None]
    kc = np.arange(GRID_W)[None, :]
    cs = np.clip(c - NA_KW // 2, 0, GRID_W - NA_KW)
    valid = (kc >= cs) & (kc < cs + NA_KW)
    dc = kc - c + (NA_KW - 1)
    pick = np.asarray(dc[None] == np.arange(2 * NA_KW - 1)[:, None, None], np.float32)
    t1 = jnp.einsum("hdj,jck->hdck", rpb.astype(F32) * LOG2_E, pick, precision=lax.Precision.HIGHEST)
    t1 = jnp.where(valid[None, None], t1, NEG_BIG)
    pairs = jnp.concatenate([t1[:, :-1], t1[:, 1:]], axis=-1)
    return pairs.transpose(1, 0, 2, 3)


HG_CHUNKS_PER_STEP = 16
HG_CHUNKS_PER_GROUP = 1
ROW_B, ROW_LV = 0, 1
N_MXU_LEVELS = 2
N_EXP_BLOCKS = N_MXU_LEVELS


def _hgrn_constants():
    C = HG_CHUNK
    t = np.arange(C)
    u = t[None, :]
    mats = [u <= t[:, None]]
    masks = [np.eye(C, dtype=bool)]
    for lv in range(N_HG_LEVELS):
        m = 1 << lv
        blk = t // (2 * m)
        upper = (t // m) % 2 == 1
        p = blk * 2 * m + m - 1
        q_rows = upper[:, None] & (u > p[:, None]) & (u <= t[:, None])
        k_rows = (~upper)[:, None] & (u > t[:, None]) & (u <= p[:, None])
        if 1 <= lv < N_MXU_LEVELS:
            mats.append(q_rows | k_rows)
        masks.append((blk[:, None] == blk[None, :]) & upper[:, None] & (~upper)[None, :])
    fwd = np.concatenate(mats, axis=0).astype(np.float32)
    fwd_mask = np.stack(masks).astype(np.float32)
    blocks = fwd.reshape(N_EXP_BLOCKS, C, C)
    bwd = blocks[:, ::-1, ::-1].reshape(N_EXP_BLOCKS * C, C)
    bwd_mask = fwd_mask[:, ::-1, ::-1]
    both = np.stack([fwd, bwd])
    both = np.concatenate([both, both], axis=-1)
    both_mask = np.stack([fwd_mask, bwd_mask])
    both_mask = np.concatenate([both_mask, both_mask], axis=-1)
    return (jnp.asarray(both, BF16), jnp.asarray(both_mask, F32))


def _hgrn_exponents(mat, g):
    g_hi, g_lo = _split_bf16(g)
    return _dot(mat, jnp.concatenate([g_hi, g_lo], axis=0))


def _hgrn_init(meta_a_ref, meta_g_ref, mat_ref, stf_ref, stb_ref):
    C, W = HG_CHUNK, HG_WIDTH
    v = meta_a_ref[:, W:2 * W]
    prefix = mat_ref[0, ROW_B * C:(ROW_B + 1) * C, :]
    for d, st_ref in ((0, stf_ref), (1, stb_ref)):
        g = meta_g_ref[:, d * W:(d + 1) * W]
        b = _hgrn_exponents(prefix, g)
        kb = ((1.0 - jnp.exp(g)) * jnp.exp(b[C - 1:C] - b)).astype(BF16)
        for h in range(HG_HEADS):
            sl = slice(h * HG_HEAD_DIM, (h + 1) * HG_HEAD_DIM)
            st_ref[h] = _dot_tn(v[:, sl], kb[:, sl])


def _hgrn_group_tasks(j0):
    js = range(j0, j0 + HG_CHUNKS_PER_GROUP)
    return ([(0, j * HG_CHUNK) for j in js] +
            [(1, (HG_CHUNKS_PER_STEP - 1 - j) * HG_CHUNK) for j in js])


def _hgrn_tasks(tasks, dir_refs, states, mat_ref, mask_ref):
    C, W = HG_CHUNK, HG_WIDTH
    heads = [slice(h * HG_HEAD_DIM, (h + 1) * HG_HEAD_DIM) for h in range(HG_HEADS)]
    qs, ks, vs, ex_bs, ex_lvs, ex_ends = {}, {}, {}, {}, {}, {}
    for d, r0 in tasks:
        q_ref, v_ref, g_ref, total_row = dir_refs[d][0], dir_refs[d][1], dir_refs[d][2], dir_refs[d][3]
        g = g_ref[0, r0:r0 + C, :]
        sums = _hgrn_exponents(mat_ref[d], g)
        b = sums[0:C]
        ex_bs[d, r0] = jnp.exp(b)
        odd = (lax.broadcasted_iota(jnp.int32, (C, W), 0) & 1) == 1
        level_sums = [jnp.where(odd if d == 0 else jnp.logical_not(odd), g, 0.0)]
        level_sums += [sums[(ROW_LV + lv - 1) * C:(ROW_LV + lv) * C] for lv in range(1, N_MXU_LEVELS)]
        for lv in range(N_MXU_LEVELS, N_HG_LEVELS):
            m = 1 << lv
            b3 = b.reshape(C // (2 * m), 2 * m, W)
            row = lax.broadcasted_iota(jnp.int32, b3.shape, 1)
            pivot, is_query = (m - 1, row >= m) if d == 0 else (m, row < m)
            diff = b3 - b3[:, pivot:pivot + 1, :]
            level_sums.append(jnp.where(is_query, diff, -diff).reshape(C, W))
        ex_lvs[d, r0] = [jnp.exp(s).astype(BF16) for s in level_sums]
        ex_ends[d, r0] = jnp.exp(sums[total_row:total_row + 1] - sums[0:C])
        qs[d, r0] = q_ref[0, r0:r0 + C, :]
        ks[d, r0] = 1.0 - jnp.exp(g)
        vs[d, r0] = v_ref[0, r0:r0 + C, :]

    D2 = 2 * HG_HEAD_DIM
    pairs = [slice(p * D2, (p + 1) * D2) for p in range(HG_HEADS // 2)]

    def block_diag(x):
        zero = jnp.zeros((C, HG_HEAD_DIM), x.dtype)
        return jnp.concatenate([jnp.concatenate([x[:, :HG_HEAD_DIM], zero], axis=1),
                                jnp.concatenate([zero, x[:, HG_HEAD_DIM:]], axis=1)], axis=0)

    prods = {}
    for task in tasks:
        for p, sl in enumerate(pairs):
            q2, k2 = qs[task][:, sl], ks[task][:, sl].astype(BF16)
            ps = [_dot_nt(q2, block_diag(k2))]
            for lv in range(N_HG_LEVELS):
                scale = ex_lvs[task][lv][:, sl]
                ps.append(_dot_nt(q2 * scale, block_diag(k2 * scale)))
            prods[task, p] = ps
    attn, qbs, kbs = {}, {}, {}
    for task in tasks:
        d = task[0]
        for p, sl in enumerate(pairs):
            a = prods[task, p][0] * mask_ref[d, 0]
            for lv in range(N_HG_LEVELS):
                a = a + prods[task, p][1 + lv] * mask_ref[d, 1 + lv]
            o2 = _dot(a.astype(BF16), block_diag(vs[task][:, sl]))
            attn[task, 2 * p] = o2[:, :HG_HEAD_DIM]
            attn[task, 2 * p + 1] = o2[:, HG_HEAD_DIM:]
        for h, sl in enumerate(heads):
            qbs[task, h] = (qs[task][:, sl].astype(F32) * ex_bs[task][:, sl]).astype(BF16)
            kbs[task, h] = _dot_tn(vs[task][:, sl], (ks[task][:, sl] * ex_ends[task][:, sl]).astype(BF16))
    for task in tasks:
        d, r0 = task
        total_row, o_ref = dir_refs[d][3], dir_refs[d][4]
        outs = []
        for h, sl in enumerate(heads):
            st = states[d, h]
            outs.append(attn[task, h] + _dot_nt(qbs[task, h], st.astype(BF16)))
            decay = ex_bs[task][total_row:total_row + 1, sl]
            states[d, h] = decay * st + kbs[task, h]
        o_ref[0, r0:r0 + C, :] = jnp.concatenate(outs, axis=-1).astype(o_ref.dtype)


def _hgrn_body(hqf_ref, hvf_ref, hgf_ref, hqb_ref, hvb_ref, hgb_ref, meta_a_ref, meta_g_ref, mat_ref, mask_ref,
               of_ref, ob_ref, stf_ref, stb_ref):
    @pl.when(pl.program_id(1) == 0)
    def _init():
        _hgrn_init(meta_a_ref, meta_g_ref, mat_ref, stf_ref, stb_ref)

    dir_refs = ((hqf_ref, hvf_ref, hgf_ref, HG_CHUNK - 1, of_ref), (hqb_ref, hvb_ref, hgb_ref, 0, ob_ref))
    st_refs = (stf_ref, stb_ref)
    states = {(d, h): st_refs[d][h] for d in range(2) for h in range(HG_HEADS)}
    for j0 in range(0, HG_CHUNKS_PER_STEP, HG_CHUNKS_PER_GROUP):
        _hgrn_tasks(_hgrn_group_tasks(j0), dir_refs, states, mat_ref, mask_ref)
    for d in range(2):
        for h in range(HG_HEADS):
            st_refs[d][h] = states[d, h]


def _hgrn(hga, hgg, hga_meta_pad, hgg_meta_pad, batch, seq):
    steps = seq // (HG_CHUNK * HG_CHUNKS_PER_STEP)
    a3 = hga.reshape(batch, seq, 3 * HG_WIDTH)
    g3 = hgg.reshape(batch, seq, 2 * HG_WIDTH)
    mats, masks = _hgrn_constants()
    hblk = (1, HG_CHUNK * HG_CHUNKS_PER_STEP, HG_WIDTH)

    def fwd(col):
        return pl.BlockSpec(hblk, lambda b, s: (b, s, col))

    def bwd(col):
        return pl.BlockSpec(hblk, lambda b, s: (b, steps - 1 - s, col))

    return pl.pallas_call(
        _hgrn_body,
        grid=(batch, steps),
        in_specs=[
            fwd(0), fwd(1), fwd(0),
            bwd(0), bwd(1), bwd(1),
            pl.BlockSpec((HG_CHUNK, 3 * HG_WIDTH), lambda b, s: (0, 0)),
            pl.BlockSpec((HG_CHUNK, 2 * HG_WIDTH), lambda b, s: (0, 0)),
            pl.BlockSpec(mats.shape, lambda b, s: (0, 0, 0)),
            pl.BlockSpec(masks.shape, lambda b, s: (0, 0, 0, 0)),
        ],
        out_specs=[fwd(0), bwd(0)],
        out_shape=[jax.ShapeDtypeStruct((batch, seq, HG_WIDTH), BF16)] * 2,
        scratch_shapes=[pltpu.VMEM((HG_HEADS, HG_HEAD_DIM, HG_HEAD_DIM), F32)] * 2,
        compiler_params=_params("arbitrary", "arbitrary"),
        name="hgrn",
    )(a3, a3, g3, a3, a3, g3, hga_meta_pad, hgg_meta_pad, mats, masks)


MIX_TILES_PER_STEP = 4


def _mix_route_body(yna_ref, of_ref, ob_ref, gate_ref, x_ref, wout_ref, hgn_ref, ffg_ref,
                    rwh_ref, rwl_ref, rb_ref, tri_ref, lowe_ref,
                    h1_ref, xn_ref, lpos_ref, tw_ref, cnt_ref):
    T, E = TOK_TILE, N_EXPERTS
    tiles = [slice(s * T, (s + 1) * T) for s in range(MIX_TILES_PER_STEP)]
    x_his, x_los = [], []
    for rows in tiles:
        o = of_ref[rows, :].astype(F32) + ob_ref[rows, :].astype(F32)
        parts = []
        for h in range(HG_HEADS):
            seg = o[:, h * HG_HEAD_DIM:(h + 1) * HG_HEAD_DIM]
            ms = jnp.mean(seg * seg, axis=-1, keepdims=True)
            parts.append(seg * lax.rsqrt(ms + RMS_EPS))
        yhg = jnp.concatenate(parts, axis=-1) * hgn_ref[...] * gate_ref[rows, :]
        mix = _dot(yna_ref[rows, :], wout_ref[0:NA_WIDTH, :]) + _dot(yhg.astype(BF16), wout_ref[NA_WIDTH:, :])
        h1 = x_ref[rows, :] + mix
        h1_ref[rows, :] = h1
        ms = jnp.mean(h1 * h1, axis=-1, keepdims=True)
        xn = h1 * lax.rsqrt(ms + RMS_EPS) * ffg_ref[...]
        x_hi, x_lo = _split_bf16(xn)
        xn_ref[rows, :] = x_hi
        x_his.append(x_hi)
        x_los.append(x_lo)

    curs = [(_dot_nt(rwh_ref[...], x_hi) + _dot_nt(rwh_ref[...], x_lo) + _dot_nt(rwl_ref[...], x_hi)) + rb_ref[...]
            for x_hi, x_lo in zip(x_his, x_los)]
    row = lax.broadcasted_iota(jnp.int32, (E, T), 0).astype(F32)
    sels = [[] for _ in tiles]
    vals = [[] for _ in tiles]
    for _ in range(TOP_K):
        for s in range(len(tiles)):
            m = jnp.max(curs[s], axis=0, keepdims=True)
            first = jnp.min(jnp.where(curs[s] == m, row, float(E)), axis=0, keepdims=True)
            sel = row == first
            sels[s].append(sel)
            vals[s].append(m)
            curs[s] = jnp.where(sel, -jnp.inf, curs[s])
    for s in range(len(tiles)):
        es = [jnp.exp(vk - vals[s][0]) for vk in vals[s]]
        den = es[0] + es[1] + es[2] + es[3]
        tw_ref[s] = _pack_rows([e / den for e in es], SUBLANES)

        onehot = jnp.zeros((E, T), F32)
        for sel in sels[s]:
            onehot = onehot + jnp.where(sel, 1.0, 0.0)
        lrank = _dot(onehot.astype(BF16), tri_ref[...])
        cnt = jnp.sum(onehot, axis=1, keepdims=True)
        cnt = jnp.floor((cnt + (RUN_ALIGN - 1)) * (1.0 / RUN_ALIGN)) * RUN_ALIGN
        off = _dot(lowe_ref[...], jnp.broadcast_to(cnt, (E, LANES)).astype(BF16))
        base = jnp.concatenate([off] * (T // LANES), axis=1) + lrank
        lpos_ref[s] = _pack_rows(
            [jnp.sum(jnp.where(sel, base, 0.0), axis=0, keepdims=True) for sel in sels[s]],
            SUBLANES).astype(jnp.int32)
        cnt_ref[s] = cnt.astype(jnp.int32)


def _mix_route(yna, o_f, o_b, hg, x2d, wout_bf16, hgn, ffg, router_w, router_b):
    n = x2d.shape[0]
    T = TOK_TILE
    nt = n // T
    rw_t = router_w.T
    rwh = rw_t.astype(BF16)
    rwl = (rw_t - rwh.astype(F32)).astype(BF16)
    router_b = router_b.reshape(N_EXPERTS, 1)
    tri = jnp.asarray(np.triu(np.ones((T, T), np.float32), 1), BF16)
    lowe = jnp.asarray(np.tril(np.ones((N_EXPERTS, N_EXPERTS), np.float32), -1), BF16)
    row = lambda i: (i, 0)
    const = lambda i: (0, 0)
    tile3 = lambda i: (i, 0, 0)
    K = MIX_TILES_PER_STEP
    R = K * T
    return pl.pallas_call(
        _mix_route_body,
        grid=(nt // K,),
        in_specs=[
            pl.BlockSpec((R, NA_WIDTH), row),
            pl.BlockSpec((R, HG_WIDTH), row),
            pl.BlockSpec((R, HG_WIDTH), row),
            pl.BlockSpec((R, HG_WIDTH), lambda i: (i, 2)),
            pl.BlockSpec((R, D_MODEL), row),
            pl.BlockSpec((D_MODEL, D_MODEL), const),
            pl.BlockSpec((1, HG_WIDTH), const),
            pl.BlockSpec((1, D_MODEL), const),
            pl.BlockSpec((N_EXPERTS, D_MODEL), const),
            pl.BlockSpec((N_EXPERTS, D_MODEL), const),
            pl.BlockSpec((N_EXPERTS, 1), const),
            pl.BlockSpec((T, T), const),
            pl.BlockSpec((N_EXPERTS, N_EXPERTS), const),
        ],
        out_specs=[
            pl.BlockSpec((R, D_MODEL), row),
            pl.BlockSpec((R, D_MODEL), row),
            pl.BlockSpec((K, SUBLANES, T), tile3),
            pl.BlockSpec((K, SUBLANES, T), tile3),
            pl.BlockSpec((K, N_EXPERTS, 1), tile3),
        ],
        out_shape=[
            jax.ShapeDtypeStruct((n, D_MODEL), F32),
            jax.ShapeDtypeStruct((n, D_MODEL), BF16),
            jax.ShapeDtypeStruct((nt, SUBLANES, T), jnp.int32),
            jax.ShapeDtypeStruct((nt, SUBLANES, T), F32),
            jax.ShapeDtypeStruct((nt, N_EXPERTS, 1), jnp.int32),
        ],
        compiler_params=_params("arbitrary"),
        name="mix_route",
    )(yna, o_f, o_b, hg, x2d, wout_bf16, hgn, ffg, rwh, rwl, router_b, tri, lowe)


TILES_PER_STEP = 4


def _wait_rows(make_copy, rows):
    make_copy(0, 0, pl.multiple_of(rows, RUN_ALIGN)).wait()


def _copy_runs(i, cnt_ref, off_ref, goff_ref, make_copy):
    def per_expert(e, carry):
        n = pl.multiple_of(cnt_ref[i * N_EXPERTS + e], RUN_ALIGN)
        lo = pl.multiple_of(off_ref[i * N_EXPERTS + e], RUN_ALIGN)
        go = pl.multiple_of(goff_ref[i * N_EXPERTS + e], RUN_ALIGN)

        @pl.when(n > 0)
        def _start():
            make_copy(lo, go, n).start()

        return carry

    lax.fori_loop(0, N_EXPERTS, per_expert, 0)


def _slot_matrix(lpos, weights=None):
    row = lax.broadcasted_iota(jnp.int32, (STAGE_ROWS, TOK_TILE), 0)
    acc = jnp.zeros((STAGE_ROWS, TOK_TILE), F32)
    for k in range(TOP_K):
        hit = lpos[k:k + 1, :] == row
        val = 1.0 if weights is None else weights[k:k + 1, :]
        acc = jnp.where(hit, val, acc)
    return acc


def _dispatch_body(cnt_ref, off_ref, goff_ref, tot_ref, zoff_ref, zlen_ref, blkrange_ref,
                   xn_ref, lpos_ref, xbuf_ref, stage_ref, zero_ref, sems, zsem):
    i = pl.program_id(0)
    last = pl.num_programs(0) - 1

    def zero_fill(act):
        def tail(e, carry):
            n = pl.multiple_of(zlen_ref[e], RUN_ALIGN)

            @pl.when(n > 0)
            def _():
                go = pl.multiple_of(zoff_ref[e], RUN_ALIGN)
                act(pltpu.make_async_copy(zero_ref.at[pl.ds(0, n)], xbuf_ref.at[pl.ds(go, n)], zsem))

            return carry

        def unused(j, carry):
            go = pl.multiple_of(j * EXPERT_BLOCK, EXPERT_BLOCK)
            act(pltpu.make_async_copy(zero_ref, xbuf_ref.at[pl.ds(go, EXPERT_BLOCK)], zsem))
            return carry

        lax.fori_loop(0, N_EXPERTS, tail, 0)
        lax.fori_loop(blkrange_ref[0], blkrange_ref[1], unused, 0)

    @pl.when(i == 0)
    def _zero_start():
        zero_ref[...] = jnp.zeros_like(zero_ref)
        zero_fill(lambda c: c.start())

    for s in range(TILES_PER_STEP):
        t = i * TILES_PER_STEP + s

        def make_copy(lo, go, size, s=s):
            return pltpu.make_async_copy(stage_ref.at[s, pl.ds(lo, size)], xbuf_ref.at[pl.ds(go, size)], sems.at[s])

        @pl.when(i > 0)
        def _drain():
            _wait_rows(make_copy, tot_ref[t - TILES_PER_STEP])

        rows = slice(s * TOK_TILE, (s + 1) * TOK_TILE)
        perm = _slot_matrix(lpos_ref[s]).astype(BF16)
        stage_ref[s] = _dot(perm, xn_ref[rows, :])
        _copy_runs(t, cnt_ref, off_ref, goff_ref, make_copy)

        @pl.when(i == last)
        def _finish():
            _wait_rows(make_copy, tot_ref[t])

    @pl.when(i == last)
    def _zero_finish():
        zero_fill(lambda c: c.wait())


def _dispatch(cnt, off, goff, tot, zoff, zlen, blkrange, xn, lpos, n_rows):
    n = xn.shape[0]
    rows = TILES_PER_STEP * TOK_TILE
    return pl.pallas_call(
        _dispatch_body,
        grid_spec=pltpu.PrefetchScalarGridSpec(
            num_scalar_prefetch=7,
            grid=(n // rows,),
            in_specs=[
                pl.BlockSpec((rows, D_MODEL), lambda i, *_: (i, 0)),
                pl.BlockSpec((TILES_PER_STEP, SUBLANES, TOK_TILE), lambda i, *_: (i, 0, 0)),
            ],
            out_specs=pl.BlockSpec(memory_space=pl.ANY),
            scratch_shapes=[
                pltpu.VMEM((TILES_PER_STEP, STAGE_ROWS, D_MODEL), F32),
                pltpu.VMEM((EXPERT_BLOCK, D_MODEL), F32),
                pltpu.SemaphoreType.DMA((TILES_PER_STEP,)),
                pltpu.SemaphoreType.DMA(()),
            ],
        ),
        out_shape=jax.ShapeDtypeStruct((n_rows, D_MODEL), F32),
        compiler_params=_params("arbitrary"),
        name="moe_dispatch",
    )(cnt, off, goff, tot, zoff, zlen, blkrange, xn, lpos)


CAST_ROWS = 128
EXPERT_ROW_STEP = 128


def _experts_body(blk_e_ref, first_ref, slot_ref, next_e_ref, parts_ref, nused_ref,
                  x_ref, bgu_ref, bd_ref, wgu_hbm, wd_hbm, y_ref,
                  wgu_f32, wd_f32, wgu_bf, wd_bf, sem_gu, sem_d):
    j = pl.program_id(0)

    def weight_copies(e, slot):
        return (pltpu.make_async_copy(wgu_hbm.at[e], wgu_f32.at[slot], sem_gu.at[slot]),
                pltpu.make_async_copy(wd_hbm.at[e], wd_f32.at[slot], sem_d.at[slot]))

    @pl.when(j < nused_ref[0])
    def _():
        e = blk_e_ref[j]
        slot = slot_ref[j]

        @pl.when(first_ref[j] == 1)
        def _new_expert():
            @pl.when(j == 0)
            def _():
                for c in weight_copies(e, slot):
                    c.start()

            for c in weight_copies(e, slot):
                c.wait()
            nxt = next_e_ref[j]

            @pl.when(nxt >= 0)
            def _():
                for c in weight_copies(nxt, 1 - slot):
                    c.start(priority=1)

            def cast(src, dst):
                def rows(r, carry):
                    sl = pl.ds(pl.multiple_of(r * CAST_ROWS, CAST_ROWS), CAST_ROWS)
                    dst[sl, :] = src[slot, sl, :].astype(BF16)
                    return carry
                lax.fori_loop(0, src.shape[1] // CAST_ROWS, rows, 0)

            cast(wgu_f32, wgu_bf)
            cast(wd_f32, wd_bf)

        def ffn(rows):
            x = x_ref[0:rows, :].astype(BF16)
            gu = _dot(x, wgu_bf[...]) + bgu_ref[0]
            gate = jnp.minimum(gu[:, :D_FF], SWIGLU_LIMIT)
            up = jnp.clip(gu[:, D_FF:], -SWIGLU_LIMIT, SWIGLU_LIMIT)
            act = (up + 1.0) * gate * jax.nn.sigmoid(SWIGLU_ALPHA * gate)
            y_ref[0:rows, :] = _dot(act.astype(BF16), wd_bf[...]) + bd_ref[0]
            if rows < EXPERT_BLOCK:
                y_ref[rows:, :] = jnp.zeros((EXPERT_BLOCK - rows, D_MODEL), F32)

        for parts in range(1, EXPERT_BLOCK // EXPERT_ROW_STEP + 1):
            pl.when(parts_ref[j] == parts)(functools.partial(ffn, parts * EXPERT_ROW_STEP))


def _experts(blk_e, first, slot, next_e, parts, nused, xbuf, wgu, bgu, wd, bd):
    n_rows = xbuf.shape[0]
    nblk = n_rows // EXPERT_BLOCK

    def rowblk(j, be, fi, sl, ne, pa, nu):
        return (jnp.minimum(j, nu[0] - 1), 0)

    def expert(j, be, fi, sl, ne, pa, nu):
        return (be[jnp.minimum(j, nu[0] - 1)], 0, 0)

    return pl.pallas_call(
        _experts_body,
        grid_spec=pltpu.PrefetchScalarGridSpec(
            num_scalar_prefetch=6,
            grid=(nblk,),
            in_specs=[
                pl.BlockSpec((EXPERT_BLOCK, D_MODEL), rowblk),
                pl.BlockSpec((1, 1, 2 * D_FF), expert),
                pl.BlockSpec((1, 1, D_MODEL), expert),
                pl.BlockSpec(memory_space=pl.ANY),
                pl.BlockSpec(memory_space=pl.ANY),
            ],
            out_specs=pl.BlockSpec((EXPERT_BLOCK, D_MODEL), rowblk),
            scratch_shapes=[
                pltpu.VMEM((2, D_MODEL, 2 * D_FF), F32),
                pltpu.VMEM((2, D_FF, D_MODEL), F32),
                pltpu.VMEM((D_MODEL, 2 * D_FF), BF16),
                pltpu.VMEM((D_FF, D_MODEL), BF16),
                pltpu.SemaphoreType.DMA((2,)),
                pltpu.SemaphoreType.DMA((2,)),
            ],
        ),
        out_shape=jax.ShapeDtypeStruct((n_rows, D_MODEL), F32),
        input_output_aliases={6: 0},
        compiler_params=_params("arbitrary"),
        name="moe_experts",
    )(blk_e, first, slot, next_e, parts, nused, xbuf, bgu, bd, wgu, wd)


def _combine_body(cnt_ref, off_ref, goff_ref, tot_ref, ybuf_ref, lpos_ref, tw_ref, h1_ref, fg_ref, out_ref,
                  stage_ref, sems):
    i = pl.program_id(0)
    n_tiles = pl.num_programs(0) * TILES_PER_STEP

    def copier(s):
        def make_copy(lo, go, size):
            return pltpu.make_async_copy(ybuf_ref.at[pl.ds(go, size)], stage_ref.at[s, pl.ds(lo, size)], sems.at[s])
        return make_copy

    ahead = TILES_PER_STEP - 1

    @pl.when(i == 0)
    def _init():
        stage_ref[...] = jnp.zeros_like(stage_ref)
        for s in range(ahead):
            _copy_runs(s, cnt_ref, off_ref, goff_ref, copier(s))

    for s in range(TILES_PER_STEP):
        t = i * TILES_PER_STEP + s

        @pl.when(t + ahead < n_tiles)
        def _prefetch():
            _copy_runs(t + ahead, cnt_ref, off_ref, goff_ref, copier((s + ahead) % TILES_PER_STEP))

        rows = slice(s * TOK_TILE, (s + 1) * TOK_TILE)
        w = _slot_matrix(lpos_ref[s], tw_ref[s]).astype(BF16)
        _wait_rows(copier(s), tot_ref[t])
        h2 = h1_ref[rows, :] + _dot_tn(w, stage_ref[s].astype(BF16))
        ms = jnp.mean(h2 * h2, axis=-1, keepdims=True)
        out_ref[rows, :] = h2 * lax.rsqrt(ms + RMS_EPS) * fg_ref[...]


def _combine(cnt, off, goff, tot, ybuf, lpos, tw, h1, final_g):
    n = h1.shape[0]
    rows = TILES_PER_STEP * TOK_TILE
    row = lambda i, *_: (i, 0)
    return pl.pallas_call(
        _combine_body,
        grid_spec=pltpu.PrefetchScalarGridSpec(
            num_scalar_prefetch=4,
            grid=(n // rows,),
            in_specs=[
                pl.BlockSpec(memory_space=pl.ANY),
                pl.BlockSpec((TILES_PER_STEP, SUBLANES, TOK_TILE), lambda i, *_: (i, 0, 0)),
                pl.BlockSpec((TILES_PER_STEP, SUBLANES, TOK_TILE), lambda i, *_: (i, 0, 0)),
                pl.BlockSpec((rows, D_MODEL), row),
                pl.BlockSpec((1, D_MODEL), lambda i, *_: (0, 0)),
            ],
            out_specs=pl.BlockSpec((rows, D_MODEL), row),
            scratch_shapes=[pltpu.VMEM((TILES_PER_STEP, STAGE_ROWS, D_MODEL), F32),
                            pltpu.SemaphoreType.DMA((TILES_PER_STEP,))],
        ),
        out_shape=jax.ShapeDtypeStruct((n, D_MODEL), F32),
        compiler_params=_params("arbitrary"),
        name="moe_combine",
    )(cnt, off, goff, tot, ybuf, lpos, tw, h1, final_g)


def _moe_layout(cnt_tiles):
    nt = cnt_tiles.shape[0]
    total = jnp.sum(cnt_tiles, axis=0)
    nblk_e = (total + EXPERT_BLOCK - 1) // EXPERT_BLOCK
    blk_end = jnp.cumsum(nblk_e)
    pstart = (blk_end - nblk_e) * EXPERT_BLOCK
    prefix = jnp.cumsum(cnt_tiles, axis=0) - cnt_tiles
    goff = pstart[None, :] + prefix
    off = jnp.cumsum(cnt_tiles, axis=1) - cnt_tiles
    tot = jnp.sum(cnt_tiles, axis=1)
    max_rows = nt * (TOK_TILE * TOP_K + N_EXPERTS * (RUN_ALIGN - 1)) + N_EXPERTS * (EXPERT_BLOCK - 1)
    max_blocks = (max_rows + EXPERT_BLOCK - 1) // EXPERT_BLOCK
    blk_id = jnp.arange(max_blocks, dtype=blk_end.dtype)
    blk_e = jnp.minimum(jnp.sum(blk_end[None, :] <= blk_id[:, None], axis=1), N_EXPERTS - 1).astype(jnp.int32)
    nused = blk_end[-1:].astype(jnp.int32)
    flat = lambda a: a.reshape(-1).astype(jnp.int32)
    zoff = pstart + total
    zlen = nblk_e * EXPERT_BLOCK - total
    blkrange = jnp.stack([nused[0], jnp.int32(max_blocks)])
    used = blk_id < nused[0]
    first = used & jnp.concatenate([jnp.ones((1,), bool), blk_e[1:] != blk_e[:-1]])
    slot = (jnp.cumsum(first.astype(jnp.int32)) - 1) % 2
    eid = jnp.arange(N_EXPERTS)
    later = (eid[None, :] > eid[:, None]) & (nblk_e > 0)[None, :]
    next_of = jnp.min(jnp.where(later, eid[None, :], N_EXPERTS), axis=1)
    next_of = jnp.where(next_of < N_EXPERTS, next_of, -1)
    is_e = blk_e[:, None] == eid[None, :]
    pick = lambda per_expert: jnp.sum(jnp.where(is_e, per_expert[None, :], 0), axis=1)
    rows_left = jnp.clip(pick(total) - (blk_id - pick(blk_end - nblk_e)) * EXPERT_BLOCK, 0, EXPERT_BLOCK)
    parts = (rows_left + EXPERT_ROW_STEP - 1) // EXPERT_ROW_STEP
    experts_plan = (blk_e, first.astype(jnp.int32), slot.astype(jnp.int32), pick(next_of).astype(jnp.int32),
                    parts.astype(jnp.int32), nused)
    dispatch_plan = (flat(cnt_tiles), flat(off), flat(goff), flat(tot))
    zero_plan = (flat(zoff), flat(zlen), blkrange.astype(jnp.int32))
    return dispatch_plan, zero_plan, experts_plan, max_blocks * EXPERT_BLOCK


def kernel(x, meta_tokens, attn_norm_g, w_in, na_rpb, na_norm_g, hgrn_lb_logits, hgrn_norm_g, w_out, ffn_norm_g,
           router_w, router_b, expert_w_gu, expert_b_gu, expert_w_down, expert_b_down, final_norm_g):
    B, T, D = x.shape
    x2d = x.reshape(B * T, D)
    w_in_b = w_in[0].astype(BF16)
    ng = attn_norm_g[0].reshape(1, D)
    lbl = hgrn_lb_logits[:, :, :].reshape(4, HG_WIDTH)

    att, hga, hgg = _inproj(x2d, ng, w_in_b, lbl, 512)
    att_m, hga_m, hgg_m = _inproj(meta_tokens.astype(F32), ng, w_in_b, lbl, N_META)
    att_m = jnp.pad(att_m, ((0, META_PAD - N_META), (0, 0)))
    hga_m = jnp.pad(hga_m, ((0, HG_CHUNK - N_META), (0, 0)))
    hgg_m = jnp.pad(hgg_m, ((0, HG_CHUNK - N_META), (0, 0)))

    yna = _natten(att, att_m, _natten_bias_tables(na_rpb[0]), na_norm_g[0].reshape(1, NA_WIDTH), B, T)
    o_f, o_b = _hgrn(hga, hgg, hga_m, hgg_m, B, T)

    h1, xn, lpos, tw, cnt_tiles = _mix_route(
        yna.reshape(B * T, NA_WIDTH), o_f.reshape(B * T, HG_WIDTH), o_b.reshape(B * T, HG_WIDTH), hga, x2d,
        w_out[0].astype(BF16), hgrn_norm_g[0].reshape(1, HG_WIDTH), ffn_norm_g[0].reshape(1, D),
        router_w[0], router_b[0].reshape(1, N_EXPERTS))

    dispatch_plan, zero_plan, experts_plan, n_rows = _moe_layout(cnt_tiles[:, :, 0])
    xbuf = _dispatch(*dispatch_plan, *zero_plan, xn, lpos, n_rows)
    ybuf = _experts(*experts_plan, xbuf, expert_w_gu[0], expert_b_gu[0][:, None, :],
                    expert_w_down[0], expert_b_down[0][:, None, :])
    out = _combine(*dispatch_plan, ybuf, lpos, tw, h1, final_norm_g.reshape(1, D))
    return out.reshape(B, T, D)
```

```python
import functools

import numpy as np
import jax
import jax.numpy as jnp
from jax import lax
from jax.experimental import pallas as pl
from jax.experimental.pallas import tpu as pltpu

F32 = jnp.float32
BF16 = jnp.bfloat16

D_MODEL = 1024
N_META = 16
GRID_W = 64
NA_WIDTH = 512
NA_HEAD_DIM = 64
NA_HEADS = 8
NA_KH = 8
NA_KW = 16
HG_WIDTH = 512
HG_HEAD_DIM = 128
HG_HEADS = 4
HG_CHUNK = 64
IN_COLS = 3 * NA_WIDTH + 5 * HG_WIDTH
N_EXPERTS = 32
TOP_K = 4
D_FF = 1024
SWIGLU_LIMIT = 7.0
SWIGLU_ALPHA = 1.702
RMS_EPS = 1e-6
NEG_BIG = -1e30
LOG2_E = 1.4426950408889634

LANES = 128
VMEM_LIMIT_BYTES = 56 * 1024 * 1024

TOK_TILE = 256
EXPERT_BLOCK = 512
SUBLANES = 8
RUN_ALIGN = SUBLANES
STAGE_ROWS = TOK_TILE * TOP_K + N_EXPERTS * RUN_ALIGN
N_HG_LEVELS = 6


def _dot(a, b):
    return jnp.dot(a, b, preferred_element_type=F32)


def _dot_nt(a, b):
    return lax.dot_general(a, b, (((1,), (1,)), ((), ())), preferred_element_type=F32)


def _dot_tn(a, b):
    return lax.dot_general(a, b, (((0,), (0,)), ((), ())), preferred_element_type=F32)


def _split_bf16(x):
    hi = x.astype(BF16)
    lo = (x - hi.astype(F32)).astype(BF16)
    return hi, lo


def _pack_rows(rows, n_rows):
    idx = lax.broadcasted_iota(jnp.int32, (n_rows, rows[0].shape[1]), 0)
    out = jnp.zeros(idx.shape, rows[0].dtype)
    for k, r in enumerate(rows):
        out = jnp.where(idx == k, r, out)
    return out


def _params(*sem):
    return pltpu.CompilerParams(dimension_semantics=sem, vmem_limit_bytes=VMEM_LIMIT_BYTES)


def _inproj_body(x_ref, ng_ref, w_ref, lbl_ref, att_ref, hga_ref, hgg_ref):
    x = x_ref[...]
    ms = jnp.mean(x * x, axis=-1, keepdims=True)
    n = (x * lax.rsqrt(ms + RMS_EPS) * ng_ref[...]).astype(BF16)

    def proj(lo, hi):
        return _dot(n, w_ref[:, lo:hi])

    pq = proj(0, NA_WIDTH)
    att_ref[:, 0:NA_WIDTH] = (pq * (NA_HEAD_DIM ** -0.5 * LOG2_E)).astype(BF16)
    att_ref[:, NA_WIDTH:3 * NA_WIDTH] = proj(NA_WIDTH, 3 * NA_WIDTH).astype(BF16)

    base = 3 * NA_WIDTH
    W = HG_WIDTH
    qh = proj(base, base + W)
    hga_ref[:, 0:W] = (qh * jax.nn.sigmoid(qh)).astype(BF16)
    hga_ref[:, W:2 * W] = proj(base + W, base + 2 * W).astype(BF16)
    lbl = lbl_ref[...]
    for d in range(2):
        a0 = lbl[2 * d:2 * d + 1, :]
        a1 = lbl[2 * d + 1:2 * d + 2, :]
        m = jnp.maximum(a0, a1)
        e0 = jnp.exp(a0 - m)
        e1 = jnp.exp(a1 - m)
        lb = e0 / (e0 + e1)
        raw = proj(base + (2 + d) * W, base + (3 + d) * W)
        f = lb + (1.0 - lb) * jax.nn.sigmoid(raw)
        hgg_ref[:, d * W:(d + 1) * W] = jnp.log(f)
    gt = proj(base + 4 * W, base + 5 * W)
    hga_ref[:, 2 * W:3 * W] = (gt * jax.nn.sigmoid(gt)).astype(BF16)


def _inproj(x2d, norm_g, w_bf16, lb_logits4, tm):
    n = x2d.shape[0]
    return pl.pallas_call(
        _inproj_body,
        grid=(n // tm,),
        in_specs=[
            pl.BlockSpec((tm, D_MODEL), lambda i: (i, 0)),
            pl.BlockSpec((1, D_MODEL), lambda i: (0, 0)),
            pl.BlockSpec((D_MODEL, IN_COLS), lambda i: (0, 0)),
            pl.BlockSpec((4, HG_WIDTH), lambda i: (0, 0)),
        ],
        out_specs=[
            pl.BlockSpec((tm, 3 * NA_WIDTH), lambda i: (i, 0)),
            pl.BlockSpec((tm, 3 * HG_WIDTH), lambda i: (i, 0)),
            pl.BlockSpec((tm, 2 * HG_WIDTH), lambda i: (i, 0)),
        ],
        out_shape=[
            jax.ShapeDtypeStruct((n, 3 * NA_WIDTH), BF16),
            jax.ShapeDtypeStruct((n, 3 * HG_WIDTH), BF16),
            jax.ShapeDtypeStruct((n, 2 * HG_WIDTH), F32),
        ],
        compiler_params=_params("arbitrary"),
        name="inproj",
    )(x2d, norm_g, w_bf16, lb_logits4)


N_WIN_KEYS = NA_KH * GRID_W
META_PAD = LANES


NA_GROUP = 4
GROUP_LANES = NA_GROUP * NA_HEAD_DIM


NA_ROWS_PER_STEP = 16


def _natten_row(rr, r, q_ref, k_ref, v_ref, km_ref, vm_ref, bias_ref, *, rows):
    rs = jnp.clip(r - NA_KH // 2, 0, rows - NA_KH)
    d0 = rs - r + (NA_KH - 1)
    start = pl.multiple_of(rs * GRID_W, GRID_W)
    lane_head = lax.broadcasted_iota(jnp.int32, (GRID_W, GROUP_LANES), 1) // NA_HEAD_DIM
    meta_col = lax.broadcasted_iota(jnp.int32, (1, META_PAD), 1)
    meta_bias = jnp.where(meta_col < N_META, 0.0, NEG_BIG).astype(F32)
    folded = []
    for g in range(NA_HEADS // NA_GROUP):
        sl = slice(g * GROUP_LANES, (g + 1) * GROUP_LANES)
        q4 = q_ref[0, rr * GRID_W:(rr + 1) * GRID_W, sl]
        zero = jnp.zeros_like(q4)
        qm = jnp.concatenate([jnp.where(lane_head == h, q4, zero) for h in range(NA_GROUP)], axis=0)
        kw = k_ref[0, pl.ds(start, N_WIN_KEYS), sl]
        vw = v_ref[0, pl.ds(start, N_WIN_KEYS), sl]
        hs = slice(g * NA_GROUP, (g + 1) * NA_GROUP)
        bias = jnp.concatenate(
            [bias_ref[d0 + 2 * j, hs].reshape(NA_GROUP * GRID_W, 2 * GRID_W) for j in range(NA_KH // 2)], axis=-1)
        s = _dot_nt(qm, kw) + bias
        sm = _dot_nt(qm, km_ref[:, sl]) + meta_bias
        m = jnp.maximum(jnp.max(s, axis=-1, keepdims=True), jnp.max(sm, axis=-1, keepdims=True))
        e = jnp.exp2(s - m)
        em = jnp.exp2(sm - m)
        den = jnp.sum(e, axis=-1, keepdims=True) + jnp.sum(em, axis=-1, keepdims=True)
        o = (_dot(e.astype(BF16), vw) + _dot(em.astype(BF16), vm_ref[:, sl])) * (1.0 / den)
        acc = jnp.where(lane_head == 0, o[0:GRID_W], 0.0)
        for h in range(1, NA_GROUP):
            acc = jnp.where(lane_head == h, o[h * GRID_W:(h + 1) * GRID_W], acc)
        folded.append(acc)
    return jnp.concatenate(folded, axis=-1)


def _natten_body(q_ref, k_ref, v_ref, km_ref, vm_ref, bias_ref, ng_ref, bd_ref, o_ref, *, rows):
    r0 = pl.program_id(1) * NA_ROWS_PER_STEP
    o2 = jnp.concatenate(
        [_natten_row(rr, r0 + rr, q_ref, k_ref, v_ref, km_ref, vm_ref, bias_ref, rows=rows)
         for rr in range(NA_ROWS_PER_STEP)], axis=0)
    sq_hi, sq_lo = _split_bf16(o2 * o2)
    ms = (_dot(sq_hi, bd_ref[...]) + _dot(sq_lo, bd_ref[...])) * (1.0 / NA_HEAD_DIM)
    o_ref[0] = (o2 * lax.rsqrt(ms + RMS_EPS) * ng_ref[...]).astype(o_ref.dtype)


def _natten(att, att_meta, bias_tabs, norm_g, batch, seq):
    rows = seq // GRID_W
    att3 = att.reshape(batch, seq, 3 * NA_WIDTH)
    head_of = np.arange(NA_WIDTH) // NA_HEAD_DIM
    same_head = jnp.asarray(head_of[:, None] == head_of[None, :], BF16)

    qrows = NA_ROWS_PER_STEP * GRID_W
    return pl.pallas_call(
        functools.partial(_natten_body, rows=rows),
        grid=(batch, rows // NA_ROWS_PER_STEP),
        in_specs=[
            pl.BlockSpec((1, qrows, NA_WIDTH), lambda b, r: (b, r, 0)),
            pl.BlockSpec((1, seq, NA_WIDTH), lambda b, r: (b, 0, 1)),
            pl.BlockSpec((1, seq, NA_WIDTH), lambda b, r: (b, 0, 2)),
            pl.BlockSpec((META_PAD, NA_WIDTH), lambda b, r: (0, 1)),
            pl.BlockSpec((META_PAD, NA_WIDTH), lambda b, r: (0, 2)),
            pl.BlockSpec(bias_tabs.shape, lambda b, r: (0, 0, 0, 0), pipeline_mode=pl.Buffered(1)),
            pl.BlockSpec((1, NA_WIDTH), lambda b, r: (0, 0)),
            pl.BlockSpec((NA_WIDTH, NA_WIDTH), lambda b, r: (0, 0)),
        ],
        out_specs=pl.BlockSpec((1, qrows, NA_WIDTH), lambda b, r: (b, r, 0)),
        out_shape=jax.ShapeDtypeStruct((batch, seq, NA_WIDTH), BF16),
        compiler_params=_params("arbitrary", "arbitrary"),
        name="natten",
    )(att3, att3, att3, att_meta, att_meta, bias_tabs, norm_g, same_head)


def _natten_bias_tables(rpb):
    c = np.arange(GRID_W)[:, None]
    kc = np.arange(GRID_W)[None, :]
    cs = np.clip(c - NA_KW // 2, 0, GRID_W - NA_KW)
    valid = (kc >= cs) & (kc < cs + NA_KW)
    dc = kc - c + (NA_KW - 1)
    pick = np.asarray(dc[None] == np.arange(2 * NA_KW - 1)[:, None, None], np.float32)
    t1 = jnp.einsum("hdj,jck->hdck", rpb.astype(F32) * LOG2_E, pick, precision=lax.Precision.HIGHEST)
    t1 = jnp.where(valid[None, None], t1, NEG_BIG)
    pairs = jnp.concatenate([t1[:, :-1], t1[:, 1:]], axis=-1)
    return pairs.transpose(1, 0, 2, 3)


HG_CHUNKS_PER_STEP = 16
HG_CHUNKS_PER_GROUP = 1
ROW_B, ROW_LV = 0, 1
N_MXU_LEVELS = 2
N_EXP_BLOCKS = N_MXU_LEVELS


def _hgrn_constants():
    C = HG_CHUNK
    t = np.arange(C)
    u = t[None, :]
    mats = [u <= t[:, None]]
    masks = [np.eye(C, dtype=bool)]
    for lv in range(N_HG_LEVELS):
        m = 1 << lv
        blk = t // (2 * m)
        upper = (t // m) % 2 == 1
        p = blk * 2 * m + m - 1
        q_rows = upper[:, None] & (u > p[:, None]) & (u <= t[:, None])
        k_rows = (~upper)[:, None] & (u > t[:, None]) & (u <= p[:, None])
        if 1 <= lv < N_MXU_LEVELS:
            mats.append(q_rows | k_rows)
        masks.append((blk[:, None] == blk[None, :]) & upper[:, None] & (~upper)[None, :])
    fwd = np.concatenate(mats, axis=0).astype(np.float32)
    fwd_mask = np.stack(masks).astype(np.float32)
    blocks = fwd.reshape(N_EXP_BLOCKS, C, C)
    bwd = blocks[:, ::-1, ::-1].reshape(N_EXP_BLOCKS * C, C)
    bwd_mask = fwd_mask[:, ::-1, ::-1]
    both = np.stack([fwd, bwd])
    both = np.concatenate([both, both], axis=-1)
    both_mask = np.stack([fwd_mask, bwd_mask])
    both_mask = np.concatenate([both_mask, both_mask], axis=-1)
    return (jnp.asarray(both, BF16), jnp.asarray(both_mask, F32))


def _hgrn_exponents(mat, g):
    g_hi, g_lo = _split_bf16(g)
    return _dot(mat, jnp.concatenate([g_hi, g_lo], axis=0))


def _hgrn_init(meta_a_ref, meta_g_ref, mat_ref, stf_ref, stb_ref):
    C, W = HG_CHUNK, HG_WIDTH
    v = meta_a_ref[:, W:2 * W]
    prefix = mat_ref[0, ROW_B * C:(ROW_B + 1) * C, :]
    for d, st_ref in ((0, stf_ref), (1, stb_ref)):
        g = meta_g_ref[:, d * W:(d + 1) * W]
        b = _hgrn_exponents(prefix, g)
        kb = ((1.0 - jnp.exp(g)) * jnp.exp(b[C - 1:C] - b)).astype(BF16)
        for h in range(HG_HEADS):
            sl = slice(h * HG_HEAD_DIM, (h + 1) * HG_HEAD_DIM)
            st_ref[h] = _dot_tn(v[:, sl], kb[:, sl])


def _hgrn_group_tasks(j0):
    js = range(j0, j0 + HG_CHUNKS_PER_GROUP)
    return ([(0, j * HG_CHUNK) for j in js] +
            [(1, (HG_CHUNKS_PER_STEP - 1 - j) * HG_CHUNK) for j in js])


def _hgrn_tasks(tasks, dir_refs, states, mat_ref, mask_ref):
    C, W = HG_CHUNK, HG_WIDTH
    heads = [slice(h * HG_HEAD_DIM, (h + 1) * HG_HEAD_DIM) for h in range(HG_HEADS)]
    qs, ks, vs, ex_bs, ex_lvs, ex_ends = {}, {}, {}, {}, {}, {}
    for d, r0 in tasks:
        q_ref, v_ref, g_ref, total_row = dir_refs[d][0], dir_refs[d][1], dir_refs[d][2], dir_refs[d][3]
        g = g_ref[0, r0:r0 + C, :]
        sums = _hgrn_exponents(mat_ref[d], g)
        b = sums[0:C]
        ex_bs[d, r0] = jnp.exp(b)
        odd = (lax.broadcasted_iota(jnp.int32, (C, W), 0) & 1) == 1
        level_sums = [jnp.where(odd if d == 0 else jnp.logical_not(odd), g, 0.0)]
        level_sums += [sums[(ROW_LV + lv - 1) * C:(ROW_LV + lv) * C] for lv in range(1, N_MXU_LEVELS)]
        for lv in range(N_MXU_LEVELS, N_HG_LEVELS):
            m = 1 << lv
            b3 = b.reshape(C // (2 * m), 2 * m, W)
            row = lax.broadcasted_iota(jnp.int32, b3.shape, 1)
            pivot, is_query = (m - 1, row >= m) if d == 0 else (m, row < m)
            diff = b3 - b3[:, pivot:pivot + 1, :]
            level_sums.append(jnp.where(is_query, diff, -diff).reshape(C, W))
        ex_lvs[d, r0] = [jnp.exp(s).astype(BF16) for s in level_sums]
        ex_ends[d, r0] = jnp.exp(sums[total_row:total_row + 1] - sums[0:C])
        qs[d, r0] = q_ref[0, r0:r0 + C, :]
        ks[d, r0] = 1.0 - jnp.exp(g)
        vs[d, r0] = v_ref[0, r0:r0 + C, :]

    D2 = 2 * HG_HEAD_DIM
    pairs = [slice(p * D2, (p + 1) * D2) for p in range(HG_HEADS // 2)]

    def block_diag(x):
        zero = jnp.zeros((C, HG_HEAD_DIM), x.dtype)
        return jnp.concatenate([jnp.concatenate([x[:, :HG_HEAD_DIM], zero], axis=1),
                                jnp.concatenate([zero, x[:, HG_HEAD_DIM:]], axis=1)], axis=0)

    prods = {}
    for task in tasks:
        for p, sl in enumerate(pairs):
            q2, k2 = qs[task][:, sl], ks[task][:, sl].astype(BF16)
            ps = [_dot_nt(q2, block_diag(k2))]
            for lv in range(N_HG_LEVELS):
                scale = ex_lvs[task][lv][:, sl]
                ps.append(_dot_nt(q2 * scale, block_diag(k2 * scale)))
            prods[task, p] = ps
    attn, qbs, kbs = {}, {}, {}
    for task in tasks:
        d = task[0]
        for p, sl in enumerate(pairs):
            a = prods[task, p][0] * mask_ref[d, 0]
            for lv in range(N_HG_LEVELS):
                a = a + prods[task, p][1 + lv] * mask_ref[d, 1 + lv]
            o2 = _dot(a.astype(BF16), block_diag(vs[task][:, sl]))
            attn[task, 2 * p] = o2[:, :HG_HEAD_DIM]
            attn[task, 2 * p + 1] = o2[:, HG_HEAD_DIM:]
        for h, sl in enumerate(heads):
            qbs[task, h] = (qs[task][:, sl].astype(F32) * ex_bs[task][:, sl]).astype(BF16)
            kbs[task, h] = _dot_tn(vs[task][:, sl], (ks[task][:, sl] * ex_ends[task][:, sl]).astype(BF16))
    for task in tasks:
        d, r0 = task
        total_row, o_ref = dir_refs[d][3], dir_refs[d][4]
        outs = []
        for h, sl in enumerate(heads):
            st = states[d, h]
            outs.append(attn[task, h] + _dot_nt(qbs[task, h], st.astype(BF16)))
            decay = ex_bs[task][total_row:total_row + 1, sl]
            states[d, h] = decay * st + kbs[task, h]
        o_ref[0, r0:r0 + C, :] = jnp.concatenate(outs, axis=-1).astype(o_ref.dtype)


def _hgrn_body(hqf_ref, hvf_ref, hgf_ref, hqb_ref, hvb_ref, hgb_ref, meta_a_ref, meta_g_ref, mat_ref, mask_ref,
               of_ref, ob_ref, stf_ref, stb_ref):
    @pl.when(pl.program_id(1) == 0)
    def _init():
        _hgrn_init(meta_a_ref, meta_g_ref, mat_ref, stf_ref, stb_ref)

    dir_refs = ((hqf_ref, hvf_ref, hgf_ref, HG_CHUNK - 1, of_ref), (hqb_ref, hvb_ref, hgb_ref, 0, ob_ref))
    st_refs = (stf_ref, stb_ref)
    states = {(d, h): st_refs[d][h] for d in range(2) for h in range(HG_HEADS)}
    for j0 in range(0, HG_CHUNKS_PER_STEP, HG_CHUNKS_PER_GROUP):
        _hgrn_tasks(_hgrn_group_tasks(j0), dir_refs, states, mat_ref, mask_ref)
    for d in range(2):
        for h in range(HG_HEADS):
            st_refs[d][h] = states[d, h]


def _hgrn(hga, hgg, hga_meta_pad, hgg_meta_pad, batch, seq):
    steps = seq // (HG_CHUNK * HG_CHUNKS_PER_STEP)
    a3 = hga.reshape(batch, seq, 3 * HG_WIDTH)
    g3 = hgg.reshape(batch, seq, 2 * HG_WIDTH)
    mats, masks = _hgrn_constants()
    hblk = (1, HG_CHUNK * HG_CHUNKS_PER_STEP, HG_WIDTH)

    def fwd(col):
        return pl.BlockSpec(hblk, lambda b, s: (b, s, col))

    def bwd(col):
        return pl.BlockSpec(hblk, lambda b, s: (b, steps - 1 - s, col))

    return pl.pallas_call(
        _hgrn_body,
        grid=(batch, steps),
        in_specs=[
            fwd(0), fwd(1), fwd(0),
            bwd(0), bwd(1), bwd(1),
            pl.BlockSpec((HG_CHUNK, 3 * HG_WIDTH), lambda b, s: (0, 0)),
            pl.BlockSpec((HG_CHUNK, 2 * HG_WIDTH), lambda b, s: (0, 0)),
            pl.BlockSpec(mats.shape, lambda b, s: (0, 0, 0)),
            pl.BlockSpec(masks.shape, lambda b, s: (0, 0, 0, 0)),
        ],
        out_specs=[fwd(0), bwd(0)],
        out_shape=[jax.ShapeDtypeStruct((batch, seq, HG_WIDTH), BF16)] * 2,
        scratch_shapes=[pltpu.VMEM((HG_HEADS, HG_HEAD_DIM, HG_HEAD_DIM), F32)] * 2,
        compiler_params=_params("arbitrary", "arbitrary"),
        name="hgrn",
    )(a3, a3, g3, a3, a3, g3, hga_meta_pad, hgg_meta_pad, mats, masks)


MIX_TILES_PER_STEP = 4


def _mix_route_body(yna_ref, of_ref, ob_ref, gate_ref, x_ref, wout_ref, hgn_ref, ffg_ref,
                    rwh_ref, rwl_ref, rb_ref, tri_ref, lowe_ref,
                    h1_ref, xn_ref, lpos_ref, tw_ref, cnt_ref):
    T, E = TOK_TILE, N_EXPERTS
    tiles = [slice(s * T, (s + 1) * T) for s in range(MIX_TILES_PER_STEP)]
    x_his, x_los = [], []
    for rows in tiles:
        o = of_ref[rows, :].astype(F32) + ob_ref[rows, :].astype(F32)
        parts = []
        for h in range(HG_HEADS):
            seg = o[:, h * HG_HEAD_DIM:(h + 1) * HG_HEAD_DIM]
            ms = jnp.mean(seg * seg, axis=-1, keepdims=True)
            parts.append(seg * lax.rsqrt(ms + RMS_EPS))
        yhg = jnp.concatenate(parts, axis=-1) * hgn_ref[...] * gate_ref[rows, :]
        mix = _dot(yna_ref[rows, :], wout_ref[0:NA_WIDTH, :]) + _dot(yhg.astype(BF16), wout_ref[NA_WIDTH:, :])
        h1 = x_ref[rows, :] + mix
        h1_ref[rows, :] = h1
        ms = jnp.mean(h1 * h1, axis=-1, keepdims=True)
        xn = h1 * lax.rsqrt(ms + RMS_EPS) * ffg_ref[...]
        x_hi, x_lo = _split_bf16(xn)
        xn_ref[rows, :] = x_hi
        x_his.append(x_hi)
        x_los.append(x_lo)

    curs = [(_dot_nt(rwh_ref[...], x_hi) + _dot_nt(rwh_ref[...], x_lo) + _dot_nt(rwl_ref[...], x_hi)) + rb_ref[...]
            for x_hi, x_lo in zip(x_his, x_los)]
    row = lax.broadcasted_iota(jnp.int32, (E, T), 0).astype(F32)
    sels = [[] for _ in tiles]
    vals = [[] for _ in tiles]
    for _ in range(TOP_K):
        for s in range(len(tiles)):
            m = jnp.max(curs[s], axis=0, keepdims=True)
            first = jnp.min(jnp.where(curs[s] == m, row, float(E)), axis=0, keepdims=True)
            sel = row == first
            sels[s].append(sel)
            vals[s].append(m)
            curs[s] = jnp.where(sel, -jnp.inf, curs[s])
    for s in range(len(tiles)):
        es = [jnp.exp(vk - vals[s][0]) for vk in vals[s]]
        den = es[0] + es[1] + es[2] + es[3]
        tw_ref[s] = _pack_rows([e / den for e in es], SUBLANES)

        onehot = jnp.zeros((E, T), F32)
        for sel in sels[s]:
            onehot = onehot + jnp.where(sel, 1.0, 0.0)
        lrank = _dot(onehot.astype(BF16), tri_ref[...])
        cnt = jnp.sum(onehot, axis=1, keepdims=True)
        cnt = jnp.floor((cnt + (RUN_ALIGN - 1)) * (1.0 / RUN_ALIGN)) * RUN_ALIGN
        off = _dot(lowe_ref[...], jnp.broadcast_to(cnt, (E, LANES)).astype(BF16))
        base = jnp.concatenate([off] * (T // LANES), axis=1) + lrank
        lpos_ref[s] = _pack_rows(
            [jnp.sum(jnp.where(sel, base, 0.0), axis=0, keepdims=True) for sel in sels[s]],
            SUBLANES).astype(jnp.int32)
        cnt_ref[s] = cnt.astype(jnp.int32)


def _mix_route(yna, o_f, o_b, hg, x2d, wout_bf16, hgn, ffg, router_w, router_b):
    n = x2d.shape[0]
    T = TOK_TILE
    nt = n // T
    rw_t = router_w.T
    rwh = rw_t.astype(BF16)
    rwl = (rw_t - rwh.astype(F32)).astype(BF16)
    router_b = router_b.reshape(N_EXPERTS, 1)
    tri = jnp.asarray(np.triu(np.ones((T, T), np.float32), 1), BF16)
    lowe = jnp.asarray(np.tril(np.ones((N_EXPERTS, N_EXPERTS), np.float32), -1), BF16)
    row = lambda i: (i, 0)
    const = lambda i: (0, 0)
    tile3 = lambda i: (i, 0, 0)
    K = MIX_TILES_PER_STEP
    R = K * T
    return pl.pallas_call(
        _mix_route_body,
        grid=(nt // K,),
        in_specs=[
            pl.BlockSpec((R, NA_WIDTH), row),
            pl.BlockSpec((R, HG_WIDTH), row),
            pl.BlockSpec((R, HG_WIDTH), row),
            pl.BlockSpec((R, HG_WIDTH), lambda i: (i, 2)),
            pl.BlockSpec((R, D_MODEL), row),
            pl.BlockSpec((D_MODEL, D_MODEL), const),
            pl.BlockSpec((1, HG_WIDTH), const),
            pl.BlockSpec((1, D_MODEL), const),
            pl.BlockSpec((N_EXPERTS, D_MODEL), const),
            pl.BlockSpec((N_EXPERTS, D_MODEL), const),
            pl.BlockSpec((N_EXPERTS, 1), const),
            pl.BlockSpec((T, T), const),
            pl.BlockSpec((N_EXPERTS, N_EXPERTS), const),
        ],
        out_specs=[
            pl.BlockSpec((R, D_MODEL), row),
            pl.BlockSpec((R, D_MODEL), row),
            pl.BlockSpec((K, SUBLANES, T), tile3),
            pl.BlockSpec((K, SUBLANES, T), tile3),
            pl.BlockSpec((K, N_EXPERTS, 1), tile3),
        ],
        out_shape=[
            jax.ShapeDtypeStruct((n, D_MODEL), F32),
            jax.ShapeDtypeStruct((n, D_MODEL), BF16),
            jax.ShapeDtypeStruct((nt, SUBLANES, T), jnp.int32),
            jax.ShapeDtypeStruct((nt, SUBLANES, T), F32),
            jax.ShapeDtypeStruct((nt, N_EXPERTS, 1), jnp.int32),
        ],
        compiler_params=_params("arbitrary"),
        name="mix_route",
    )(yna, o_f, o_b, hg, x2d, wout_bf16, hgn, ffg, rwh, rwl, router_b, tri, lowe)


TILES_PER_STEP = 4


def _wait_rows(make_copy, rows):
    make_copy(0, 0, pl.multiple_of(rows, RUN_ALIGN)).wait()


def _copy_runs(i, cnt_ref, off_ref, goff_ref, make_copy):
    def per_expert(e, carry):
        n = pl.multiple_of(cnt_ref[i * N_EXPERTS + e], RUN_ALIGN)
        lo = pl.multiple_of(off_ref[i * N_EXPERTS + e], RUN_ALIGN)
        go = pl.multiple_of(goff_ref[i * N_EXPERTS + e], RUN_ALIGN)

        @pl.when(n > 0)
        def _start():
            make_copy(lo, go, n).start()

        return carry

    lax.fori_loop(0, N_EXPERTS, per_expert, 0)


def _slot_matrix(lpos, weights=None):
    row = lax.broadcasted_iota(jnp.int32, (STAGE_ROWS, TOK_TILE), 0)
    acc = jnp.zeros((STAGE_ROWS, TOK_TILE), F32)
    for k in range(TOP_K):
        hit = lpos[k:k + 1, :] == row
        val = 1.0 if weights is None else weights[k:k + 1, :]
        acc = jnp.where(hit, val, acc)
    return acc


def _dispatch_body(cnt_ref, off_ref, goff_ref, tot_ref, zoff_ref, zlen_ref, blkrange_ref,
                   xn_ref, lpos_ref, xbuf_ref, stage_ref, zero_ref, sems, zsem):
    i = pl.program_id(0)
    last = pl.num_programs(0) - 1

    def zero_fill(act):
        def tail(e, carry):
            n = pl.multiple_of(zlen_ref[e], RUN_ALIGN)

            @pl.when(n > 0)
            def _():
                go = pl.multiple_of(zoff_ref[e], RUN_ALIGN)
                act(pltpu.make_async_copy(zero_ref.at[pl.ds(0, n)], xbuf_ref.at[pl.ds(go, n)], zsem))

            return carry

        def unused(j, carry):
            go = pl.multiple_of(j * EXPERT_BLOCK, EXPERT_BLOCK)
            act(pltpu.make_async_copy(zero_ref, xbuf_ref.at[pl.ds(go, EXPERT_BLOCK)], zsem))
            return carry

        lax.fori_loop(0, N_EXPERTS, tail, 0)
        lax.fori_loop(blkrange_ref[0], blkrange_ref[1], unused, 0)

    @pl.when(i == 0)
    def _zero_start():
        zero_ref[...] = jnp.zeros_like(zero_ref)
        zero_fill(lambda c: c.start())

    for s in range(TILES_PER_STEP):
        t = i * TILES_PER_STEP + s

        def make_copy(lo, go, size, s=s):
            return pltpu.make_async_copy(stage_ref.at[s, pl.ds(lo, size)], xbuf_ref.at[pl.ds(go, size)], sems.at[s])

        @pl.when(i > 0)
        def _drain():
            _wait_rows(make_copy, tot_ref[t - TILES_PER_STEP])

        rows = slice(s * TOK_TILE, (s + 1) * TOK_TILE)
        perm = _slot_matrix(lpos_ref[s]).astype(BF16)
        stage_ref[s] = _dot(perm, xn_ref[rows, :])
        _copy_runs(t, cnt_ref, off_ref, goff_ref, make_copy)

        @pl.when(i == last)
        def _finish():
            _wait_rows(make_copy, tot_ref[t])

    @pl.when(i == last)
    def _zero_finish():
        zero_fill(lambda c: c.wait())


def _dispatch(cnt, off, goff, tot, zoff, zlen, blkrange, xn, lpos, n_rows):
    n = xn.shape[0]
    rows = TILES_PER_STEP * TOK_TILE
    return pl.pallas_call(
        _dispatch_body,
        grid_spec=pltpu.PrefetchScalarGridSpec(
            num_scalar_prefetch=7,
            grid=(n // rows,),
            in_specs=[
                pl.BlockSpec((rows, D_MODEL), lambda i, *_: (i, 0)),
                pl.BlockSpec((TILES_PER_STEP, SUBLANES, TOK_TILE), lambda i, *_: (i, 0, 0)),
            ],
            out_specs=pl.BlockSpec(memory_space=pl.ANY),
            scratch_shapes=[
                pltpu.VMEM((TILES_PER_STEP, STAGE_ROWS, D_MODEL), F32),
                pltpu.VMEM((EXPERT_BLOCK, D_MODEL), F32),
                pltpu.SemaphoreType.DMA((TILES_PER_STEP,)),
                pltpu.SemaphoreType.DMA(()),
            ],
        ),
        out_shape=jax.ShapeDtypeStruct((n_rows, D_MODEL), F32),
        compiler_params=_params("arbitrary"),
        name="moe_dispatch",
    )(cnt, off, goff, tot, zoff, zlen, blkrange, xn, lpos)


CAST_ROWS = 128
EXPERT_ROW_STEP = 128


def _experts_body(blk_e_ref, first_ref, slot_ref, next_e_ref, parts_ref, nused_ref,
                  x_ref, bgu_ref, bd_ref, wgu_hbm, wd_hbm, y_ref,
                  wgu_f32, wd_f32, wgu_bf, wd_bf, sem_gu, sem_d):
    j = pl.program_id(0)

    def weight_copies(e, slot):
        return (pltpu.make_async_copy(wgu_hbm.at[e], wgu_f32.at[slot], sem_gu.at[slot]),
                pltpu.make_async_copy(wd_hbm.at[e], wd_f32.at[slot], sem_d.at[slot]))

    @pl.when(j < nused_ref[0])
    def _():
        e = blk_e_ref[j]
        slot = slot_ref[j]

        @pl.when(first_ref[j] == 1)
        def _new_expert():
            @pl.when(j == 0)
            def _():
                for c in weight_copies(e, slot):
                    c.start()

            for c in weight_copies(e, slot):
                c.wait()
            nxt = next_e_ref[j]

            @pl.when(nxt >= 0)
            def _():
                for c in weight_copies(nxt, 1 - slot):
                    c.start(priority=1)

            def cast(src, dst):
                def rows(r, carry):
                    sl = pl.ds(pl.multiple_of(r * CAST_ROWS, CAST_ROWS), CAST_ROWS)
                    dst[sl, :] = src[slot, sl, :].astype(BF16)
                    return carry
                lax.fori_loop(0, src.shape[1] // CAST_ROWS, rows, 0)

            cast(wgu_f32, wgu_bf)
            cast(wd_f32, wd_bf)

        def ffn(rows):
            x = x_ref[0:rows, :].astype(BF16)
            gu = _dot(x, wgu_bf[...]) + bgu_ref[0]
            gate = jnp.minimum(gu[:, :D_FF], SWIGLU_LIMIT)
            up = jnp.clip(gu[:, D_FF:], -SWIGLU_LIMIT, SWIGLU_LIMIT)
            act = (up + 1.0) * gate * jax.nn.sigmoid(SWIGLU_ALPHA * gate)
            y_ref[0:rows, :] = _dot(act.astype(BF16), wd_bf[...]) + bd_ref[0]
            if rows < EXPERT_BLOCK:
                y_ref[rows:, :] = jnp.zeros((EXPERT_BLOCK - rows, D_MODEL), F32)

        for parts in range(1, EXPERT_BLOCK // EXPERT_ROW_STEP + 1):
            pl.when(parts_ref[j] == parts)(functools.partial(ffn, parts * EXPERT_ROW_STEP))


def _experts(blk_e, first, slot, next_e, parts, nused, xbuf, wgu, bgu, wd, bd):
    n_rows = xbuf.shape[0]
    nblk = n_rows // EXPERT_BLOCK

    def rowblk(j, be, fi, sl, ne, pa, nu):
        return (jnp.minimum(j, nu[0] - 1), 0)

    def expert(j, be, fi, sl, ne, pa, nu):
        return (be[jnp.minimum(j, nu[0] - 1)], 0, 0)

    return pl.pallas_call(
        _experts_body,
        grid_spec=pltpu.PrefetchScalarGridSpec(
            num_scalar_prefetch=6,
            grid=(nblk,),
            in_specs=[
                pl.BlockSpec((EXPERT_BLOCK, D_MODEL), rowblk),
                pl.BlockSpec((1, 1, 2 * D_FF), expert),
                pl.BlockSpec((1, 1, D_MODEL), expert),
                pl.BlockSpec(memory_space=pl.ANY),
                pl.BlockSpec(memory_space=pl.ANY),
            ],
            out_specs=pl.BlockSpec((EXPERT_BLOCK, D_MODEL), rowblk),
            scratch_shapes=[
                pltpu.VMEM((2, D_MODEL, 2 * D_FF), F32),
                pltpu.VMEM((2, D_FF, D_MODEL), F32),
                pltpu.VMEM((D_MODEL, 2 * D_FF), BF16),
                pltpu.VMEM((D_FF, D_MODEL), BF16),
                pltpu.SemaphoreType.DMA((2,)),
                pltpu.SemaphoreType.DMA((2,)),
            ],
        ),
        out_shape=jax.ShapeDtypeStruct((n_rows, D_MODEL), F32),
        input_output_aliases={6: 0},
        compiler_params=_params("arbitrary"),
        name="moe_experts",
    )(blk_e, first, slot, next_e, parts, nused, xbuf, bgu, bd, wgu, wd)


def _combine_body(cnt_ref, off_ref, goff_ref, tot_ref, ybuf_ref, lpos_ref, tw_ref, h1_ref, fg_ref, out_ref,
                  stage_ref, sems):
    i = pl.program_id(0)
    n_tiles = pl.num_programs(0) * TILES_PER_STEP

    def copier(s):
        def make_copy(lo, go, size):
            return pltpu.make_async_copy(ybuf_ref.at[pl.ds(go, size)], stage_ref.at[s, pl.ds(lo, size)], sems.at[s])
        return make_copy

    ahead = TILES_PER_STEP - 1

    @pl.when(i == 0)
    def _init():
        stage_ref[...] = jnp.zeros_like(stage_ref)
        for s in range(ahead):
            _copy_runs(s, cnt_ref, off_ref, goff_ref, copier(s))

    for s in range(TILES_PER_STEP):
        t = i * TILES_PER_STEP + s

        @pl.when(t + ahead < n_tiles)
        def _prefetch():
            _copy_runs(t + ahead, cnt_ref, off_ref, goff_ref, copier((s + ahead) % TILES_PER_STEP))

        rows = slice(s * TOK_TILE, (s + 1) * TOK_TILE)
        w = _slot_matrix(lpos_ref[s], tw_ref[s]).astype(BF16)
        _wait_rows(copier(s), tot_ref[t])
        h2 = h1_ref[rows, :] + _dot_tn(w, stage_ref[s].astype(BF16))
        ms = jnp.mean(h2 * h2, axis=-1, keepdims=True)
        out_ref[rows, :] = h2 * lax.rsqrt(ms + RMS_EPS) * fg_ref[...]


def _combine(cnt, off, goff, tot, ybuf, lpos, tw, h1, final_g):
    n = h1.shape[0]
    rows = TILES_PER_STEP * TOK_TILE
    row = lambda i, *_: (i, 0)
    return pl.pallas_call(
        _combine_body,
        grid_spec=pltpu.PrefetchScalarGridSpec(
            num_scalar_prefetch=4,
            grid=(n // rows,),
            in_specs=[
                pl.BlockSpec(memory_space=pl.ANY),
                pl.BlockSpec((TILES_PER_STEP, SUBLANES, TOK_TILE), lambda i, *_: (i, 0, 0)),
                pl.BlockSpec((TILES_PER_STEP, SUBLANES, TOK_TILE), lambda i, *_: (i, 0, 0)),
                pl.BlockSpec((rows, D_MODEL), row),
                pl.BlockSpec((1, D_MODEL), lambda i, *_: (0, 0)),
            ],
            out_specs=pl.BlockSpec((rows, D_MODEL), row),
            scratch_shapes=[pltpu.VMEM((TILES_PER_STEP, STAGE_ROWS, D_MODEL), F32),
                            pltpu.SemaphoreType.DMA((TILES_PER_STEP,))],
        ),
        out_shape=jax.ShapeDtypeStruct((n, D_MODEL), F32),
        compiler_params=_params("arbitrary"),
        name="moe_combine",
    )(cnt, off, goff, tot, ybuf, lpos, tw, h1, final_g)


def _moe_layout(cnt_tiles):
    nt = cnt_tiles.shape[0]
    total = jnp.sum(cnt_tiles, axis=0)
    nblk_e = (total + EXPERT_BLOCK - 1) // EXPERT_BLOCK
    blk_end = jnp.cumsum(nblk_e)
    pstart = (blk_end - nblk_e) * EXPERT_BLOCK
    prefix = jnp.cumsum(cnt_tiles, axis=0) - cnt_tiles
    goff = pstart[None, :] + prefix
    off = jnp.cumsum(cnt_tiles, axis=1) - cnt_tiles
    tot = jnp.sum(cnt_tiles, axis=1)
    max_rows = nt * (TOK_TILE * TOP_K + N_EXPERTS * (RUN_ALIGN - 1)) + N_EXPERTS * (EXPERT_BLOCK - 1)
    max_blocks = (max_rows + EXPERT_BLOCK - 1) // EXPERT_BLOCK
    blk_id = jnp.arange(max_blocks, dtype=blk_end.dtype)
    blk_e = jnp.minimum(jnp.sum(blk_end[None, :] <= blk_id[:, None], axis=1), N_EXPERTS - 1).astype(jnp.int32)
    nused = blk_end[-1:].astype(jnp.int32)
    flat = lambda a: a.reshape(-1).astype(jnp.int32)
    zoff = pstart + total
    zlen = nblk_e * EXPERT_BLOCK - total
    blkrange = jnp.stack([nused[0], jnp.int32(max_blocks)])
    used = blk_id < nused[0]
    first = used & jnp.concatenate([jnp.ones((1,), bool), blk_e[1:] != blk_e[:-1]])
    slot = (jnp.cumsum(first.astype(jnp.int32)) - 1) % 2
    eid = jnp.arange(N_EXPERTS)
    later = (eid[None, :] > eid[:, None]) & (nblk_e > 0)[None, :]
    next_of = jnp.min(jnp.where(later, eid[None, :], N_EXPERTS), axis=1)
    next_of = jnp.where(next_of < N_EXPERTS, next_of, -1)
    is_e = blk_e[:, None] == eid[None, :]
    pick = lambda per_expert: jnp.sum(jnp.where(is_e, per_expert[None, :], 0), axis=1)
    rows_left = jnp.clip(pick(total) - (blk_id - pick(blk_end - nblk_e)) * EXPERT_BLOCK, 0, EXPERT_BLOCK)
    parts = (rows_left + EXPERT_ROW_STEP - 1) // EXPERT_ROW_STEP
    experts_plan = (blk_e, first.astype(jnp.int32), slot.astype(jnp.int32), pick(next_of).astype(jnp.int32),
                    parts.astype(jnp.int32), nused)
    dispatch_plan = (flat(cnt_tiles), flat(off), flat(goff), flat(tot))
    zero_plan = (flat(zoff), flat(zlen), blkrange.astype(jnp.int32))
    return dispatch_plan, zero_plan, experts_plan, max_blocks * EXPERT_BLOCK


def kernel(x, meta_tokens, attn_norm_g, w_in, na_rpb, na_norm_g, hgrn_lb_logits, hgrn_norm_g, w_out, ffn_norm_g,
           router_w, router_b, expert_w_gu, expert_b_gu, expert_w_down, expert_b_down, final_norm_g):
    B, T, D = x.shape
    x2d = x.reshape(B * T, D)
    w_in_b = w_in[0].astype(BF16)
    ng = attn_norm_g[0].reshape(1, D)
    lbl = hgrn_lb_logits[:, :, :].reshape(4, HG_WIDTH)

    att, hga, hgg = _inproj(x2d, ng, w_in_b, lbl, 1024)
    att_m, hga_m, hgg_m = _inproj(meta_tokens.astype(F32), ng, w_in_b, lbl, N_META)
    att_m = jnp.pad(att_m, ((0, META_PAD - N_META), (0, 0)))
    hga_m = jnp.pad(hga_m, ((0, HG_CHUNK - N_META), (0, 0)))
    hgg_m = jnp.pad(hgg_m, ((0, HG_CHUNK - N_META), (0, 0)))

    yna = _natten(att, att_m, _natten_bias_tables(na_rpb[0]), na_norm_g[0].reshape(1, NA_WIDTH), B, T)
    o_f, o_b = _hgrn(hga, hgg, hga_m, hgg_m, B, T)

    h1, xn, lpos, tw, cnt_tiles = _mix_route(
        yna.reshape(B * T, NA_WIDTH), o_f.reshape(B * T, HG_WIDTH), o_b.reshape(B * T, HG_WIDTH), hga, x2d,
        w_out[0].astype(BF16), hgrn_norm_g[0].reshape(1, HG_WIDTH), ffn_norm_g[0].reshape(1, D),
        router_w[0], router_b[0].reshape(1, N_EXPERTS))

    dispatch_plan, zero_plan, experts_plan, n_rows = _moe_layout(cnt_tiles[:, :, 0])
    xbuf = _dispatch(*dispatch_plan, *zero_plan, xn, lpos, n_rows)
    ybuf = _experts(*experts_plan, xbuf, expert_w_gu[0], expert_b_gu[0][:, None, :],
                    expert_w_down[0], expert_b_down[0][:, None, :])
    out = _combine(*dispatch_plan, ybuf, lpos, tw, h1, final_norm_g.reshape(1, D))
    return out.reshape(B, T, D)
```
